```python
import math
import jax, jax.numpy as jnp
from jax import lax
import numpy as np

D_MODEL = 1024
BATCH = 8
SEQ = 2048
DEPTH = 1
DEC_BATCH = 128
DEC_SEQ = 4
PAST_LEN = 16384
PAGE_SIZE = 128

RET_HEADS = 4
RET_DK = 128
RET_DV = 128
RET_W = RET_HEADS * RET_DV
RET_CHUNK = 64
HG_HEADS = 4
HG_DK = 128
HG_DV = 128
HG_W = HG_HEADS * HG_DV
HG_CHUNK = 32
MIX_W = RET_W + HG_W
IN_W = 2 * RET_HEADS * RET_DK + 2 * RET_W + 2 * HG_HEADS * HG_DK + 2 * HG_W
N_MEM = 256
XA_HEADS = 4
XA_HD = D_MODEL // XA_HEADS
D_FF = -(-8 * D_MODEL // (3 * 256)) * 256
ROPE_BASE = 10000.0
EPS = 1e-6

kernel_name = 'hymba_retention_hgrn2_memxattn_step'


def _rmsnorm(x, g):
    xf = x.astype(jnp.float32)
    y = xf * lax.rsqrt(jnp.mean(xf * xf, axis=-1, keepdims=True) + EPS)
    return (y * g.astype(jnp.float32)).astype(x.dtype)


def _split_heads(x, n_heads):
    b, t, _ = x.shape
    return x.reshape(b, t, n_heads, -1).transpose(0, 2, 1, 3).astype(jnp.float32)


def _merge_heads(x):
    b, h, t, d = x.shape
    return x.transpose(0, 2, 1, 3).reshape(b, t, h * d)


def _head_rmsnorm(o):
    return o * lax.rsqrt(jnp.mean(o * o, axis=-1, keepdims=True) + EPS)


def _rotary(x, pos):
    half = x.shape[-1] // 2
    inv_freq = ROPE_BASE ** (-jnp.arange(half, dtype=jnp.float32) / half)
    ang = pos[:, None] * inv_freq[None, :]
    cos, sin = jnp.cos(ang), jnp.sin(ang)
    x1, x2 = x[..., :half], x[..., half:]
    return jnp.concatenate([x1 * cos - x2 * sin, x1 * sin + x2 * cos], axis=-1)


def _to_chunks(x, L):
    b, h, t, d = x.shape
    return x.reshape(b, h, t // L, L, d).transpose(2, 0, 1, 3, 4)


def _from_chunks(x):
    n, b, h, L, d = x.shape
    return x.transpose(1, 2, 0, 3, 4).reshape(b, h, n * L, d)


def _retention(q, k, v, s0):
    t = q.shape[2]
    L = math.gcd(t, RET_CHUNK)
    log_g = jnp.log(1.0 - 2.0 ** (-5.0 - jnp.arange(RET_HEADS, dtype=jnp.float32)))
    idx = jnp.arange(L, dtype=jnp.float32)
    rel = idx[:, None] - idx[None, :]
    causal = rel >= 0
    decay = jnp.where(causal, jnp.exp(log_g[:, None, None] * jnp.where(causal, rel, 0.0)), 0.0)
    q_dec = jnp.exp(log_g[:, None] * (idx + 1.0))[..., None]
    k_dec = jnp.exp(log_g[:, None] * (L - 1.0 - idx))[..., None]
    s_dec = jnp.exp(log_g * L)[:, None, None]

    def step(s, chunk):
        qc, kc, vc = chunk
        att = jnp.einsum('bhld,bhmd->bhlm', qc, kc) * decay
        o = jnp.einsum('bhlm,bhme->bhle', att, vc) + jnp.einsum('bhld,bhde->bhle', qc * q_dec, s)
        s = s_dec * s + jnp.einsum('bhld,bhle->bhde', kc * k_dec, vc)
        return s, o

    s, o = lax.scan(step, s0, (_to_chunks(q, L), _to_chunks(k, L), _to_chunks(v, L)))
    return _from_chunks(o), s


def _hgrn2(q, k, v, log_f, s0):
    t = q.shape[2]
    L = math.gcd(t, HG_CHUNK)
    causal = (jnp.arange(L)[:, None] >= jnp.arange(L)[None, :])[:, :, None]

    def step(s, chunk):
        qc, kc, vc, lf = chunk
        b = jnp.cumsum(lf, axis=2)
        diff = b[:, :, :, None, :] - b[:, :, None, :, :]
        dec = jnp.where(causal, jnp.exp(jnp.where(causal, diff, 0.0)), 0.0)
        att = jnp.einsum('bhtc,bhsc,bhtsc->bhts', qc, kc, dec)
        o = jnp.einsum('bhts,bhse->bhte', att, vc) + jnp.einsum('bhtc,bhce->bhte', qc * jnp.exp(b), s)
        b_last = b[:, :, -1:, :]
        s = jnp.exp(b_last[:, :, 0, :])[..., None] * s + jnp.einsum('bhsc,bhse->bhce', kc * jnp.exp(b_last - b), vc)
        return s, o

    s, o = lax.scan(step, s0, (_to_chunks(q, L), _to_chunks(k, L), _to_chunks(v, L), _to_chunks(log_f, L)))
    return _from_chunks(o), s


def _mixer(h, offset, s_ret0, s_hg0, w_in, ret_gain, hg_gain, lower, w_out):
    b, t, _ = h.shape
    sizes = [RET_HEADS * RET_DK, RET_HEADS * RET_DK, RET_W, RET_W, HG_HEADS * HG_DK, HG_HEADS * HG_DK, HG_W, HG_W]
    points = [int(p) for p in np.cumsum(sizes)[:-1]]
    rq, rk, rv, rg, hq, hf, hi, hgate = jnp.split(h @ w_in, points, axis=-1)
    pos = jnp.arange(t, dtype=jnp.float32) + offset
    q_r = _rotary(_split_heads(rq, RET_HEADS), pos)
    k_r = _rotary(_split_heads(rk, RET_HEADS), pos) * (RET_DK ** -0.5)
    v_r = _split_heads(rv, RET_HEADS)
    o_r, s_r = _retention(q_r, k_r, v_r, s_ret0.astype(jnp.float32))
    o_r = _merge_heads(_head_rmsnorm(o_r)) * ret_gain.astype(jnp.float32) * jax.nn.silu(rg.astype(jnp.float32))
    f = lower + (1.0 - lower) * jax.nn.sigmoid(hf.astype(jnp.float32))
    q_h = _split_heads(jax.nn.silu(hq.astype(jnp.float32)), HG_HEADS)
    k_h = _split_heads(1.0 - f, HG_HEADS)
    v_h = _split_heads(hi, HG_HEADS)
    o_h, s_h = _hgrn2(q_h, k_h, v_h, _split_heads(jnp.log(f), HG_HEADS), s_hg0.astype(jnp.float32))
    o_h = _merge_heads(_head_rmsnorm(o_h)) * hg_gain.astype(jnp.float32) * jax.nn.silu(hgate.astype(jnp.float32))
    out = jnp.concatenate([o_r, o_h], axis=-1).astype(h.dtype) @ w_out
    return out, s_r, s_h


def _mem_kv(mem, g_mem, w_xk, w_xv):
    b = mem.shape[0]
    m = _rmsnorm(mem, g_mem)
    k = (m @ w_xk).reshape(b, N_MEM, XA_HEADS, XA_HD)
    v = (m @ w_xv).reshape(b, N_MEM, XA_HEADS, XA_HD)
    return k, v


def _cross_attn(h, mk, mv, w_xq, w_xo):
    b, t, _ = h.shape
    q = (h @ w_xq).reshape(b, t, XA_HEADS, XA_HD)
    s = jnp.einsum('bthd,bmhd->bhtm', q, mk).astype(jnp.float32) * (XA_HD ** -0.5)
    p = jax.nn.softmax(s, axis=-1).astype(mv.dtype)
    o = jnp.einsum('bhtm,bmhd->bthd', p, mv).reshape(b, t, XA_HEADS * XA_HD)
    return o @ w_xo


def _swiglu(h, w_gate, w_up, w_down):
    return (jax.nn.silu(h @ w_gate) * (h @ w_up)) @ w_down


def _forward(x, offset, s_ret, s_hg, mem_k, mem_v, g_mix, w_in, ret_gain, hg_gain, hg_lb, w_out,
             g_xa, w_xq, w_xo, g_ffn, w_gate, w_up, w_down, g_final):
    lower_all = jnp.cumsum(jax.nn.softmax(hg_lb.astype(jnp.float32), axis=0), axis=0)
    new_ret, new_hg = [], []
    for l in range(DEPTH):
        mix, s_r, s_h = _mixer(_rmsnorm(x, g_mix[l]), offset, s_ret[l], s_hg[l], w_in[l],
                               ret_gain[l], hg_gain[l], lower_all[l], w_out[l])
        x = x + mix
        x = x + _cross_attn(_rmsnorm(x, g_xa[l]), mem_k[l], mem_v[l], w_xq[l], w_xo[l])
        x = x + _swiglu(_rmsnorm(x, g_ffn[l]), w_gate[l], w_up[l], w_down[l])
        new_ret.append(s_r)
        new_hg.append(s_h)
    return _rmsnorm(x, g_final), jnp.stack(new_ret).astype(x.dtype), jnp.stack(new_hg).astype(x.dtype)


def setup_inputs(seed: int = 0) -> dict:
    key = jax.random.key(seed)
    ks = jax.random.split(key, 24)

    def nrm(k, shape, scale):
        return jax.random.normal(k, shape, jnp.float32) * scale

    def gain(k, shape):
        return 1.0 + 0.01 * jax.random.normal(k, shape, jnp.float32)

    d = D_MODEL
    return {
        'x_prompt': nrm(ks[0], (BATCH, SEQ, d), 1.0),
        'x_sample': nrm(ks[1], (DEC_BATCH, DEC_SEQ, d), 1.0),
        'mem_prompt': nrm(ks[2], (BATCH, N_MEM, d), 1.0),
        'state_ret': nrm(ks[3], (DEPTH, DEC_BATCH, RET_HEADS, RET_DK, RET_DV), 0.5),
        'state_hgrn': nrm(ks[4], (DEPTH, DEC_BATCH, HG_HEADS, HG_DK, HG_DV), 0.5),
        'cache_mem_k': nrm(ks[5], (DEPTH, DEC_BATCH, N_MEM, XA_HEADS, XA_HD), 1.0),
        'cache_mem_v': nrm(ks[6], (DEPTH, DEC_BATCH, N_MEM, XA_HEADS, XA_HD), 1.0),
        'g_mix': gain(ks[7], (DEPTH, d)),
        'w_in': nrm(ks[8], (DEPTH, d, IN_W), d ** -0.5),
        'ret_gain': gain(ks[9], (DEPTH, RET_W)),
        'hg_gain': gain(ks[10], (DEPTH, HG_W)),
        'hg_lb': nrm(ks[11], (DEPTH + 1, HG_HEADS * HG_DK), 0.1),
        'w_out': nrm(ks[12], (DEPTH, MIX_W, d), MIX_W ** -0.5),
        'g_xa': gain(ks[13], (DEPTH, d)),
        'g_mem': gain(ks[14], (DEPTH, d)),
        'w_xq': nrm(ks[15], (DEPTH, d, XA_HEADS * XA_HD), d ** -0.5),
        'w_xk': nrm(ks[16], (DEPTH, d, XA_HEADS * XA_HD), d ** -0.5),
        'w_xv': nrm(ks[17], (DEPTH, d, XA_HEADS * XA_HD), d ** -0.5),
        'w_xo': nrm(ks[18], (DEPTH, XA_HEADS * XA_HD, d), (XA_HEADS * XA_HD) ** -0.5),
        'g_ffn': gain(ks[19], (DEPTH, d)),
        'w_gate': nrm(ks[20], (DEPTH, d, D_FF), d ** -0.5),
        'w_up': nrm(ks[21], (DEPTH, d, D_FF), d ** -0.5),
        'w_down': nrm(ks[22], (DEPTH, D_FF, d), D_FF ** -0.5),
        'g_final': gain(ks[23], (d,)),
    }


def reference(x_prompt, x_sample, mem_prompt, state_ret, state_hgrn, cache_mem_k, cache_mem_v,
              g_mix, w_in, ret_gain, hg_gain, hg_lb, w_out, g_xa, g_mem, w_xq, w_xk, w_xv, w_xo,
              g_ffn, w_gate, w_up, w_down, g_final):
    b = x_prompt.shape[0]
    mk, mv = [], []
    for l in range(DEPTH):
        k_l, v_l = _mem_kv(mem_prompt, g_mem[l], w_xk[l], w_xv[l])
        mk.append(k_l)
        mv.append(v_l)
    mem_k_prompt = jnp.stack(mk)
    mem_v_prompt = jnp.stack(mv)
    zero_ret = jnp.zeros((DEPTH, b, RET_HEADS, RET_DK, RET_DV), jnp.float32)
    zero_hg = jnp.zeros((DEPTH, b, HG_HEADS, HG_DK, HG_DV), jnp.float32)
    y_prompt, ret_p, hg_p = _forward(x_prompt, 0, zero_ret, zero_hg, mem_k_prompt, mem_v_prompt,
                                     g_mix, w_in, ret_gain, hg_gain, hg_lb, w_out, g_xa, w_xq, w_xo,
                                     g_ffn, w_gate, w_up, w_down, g_final)
    y_sample, ret_s, hg_s = _forward(x_sample, PAST_LEN, state_ret, state_hgrn, cache_mem_k, cache_mem_v,
                                     g_mix, w_in, ret_gain, hg_gain, hg_lb, w_out, g_xa, w_xq, w_xo,
                                     g_ffn, w_gate, w_up, w_down, g_final)
    return (y_prompt, y_sample, ret_p, hg_p, mem_k_prompt, mem_v_prompt, ret_s, hg_s)
```

```python
import functools
import math

import jax
import jax.numpy as jnp
import numpy as np
from jax import lax
from jax.experimental import pallas as pl
from jax.experimental.pallas import tpu as pltpu

D_MODEL = 1024
RET_HEADS = 4
RET_DK = 128
RET_DV = 128
RET_W = RET_HEADS * RET_DV
HG_HEADS = 4
HG_DK = 128
HG_DV = 128
HG_W = HG_HEADS * HG_DV
MIX_W = RET_W + HG_W
IN_W = 2 * RET_HEADS * RET_DK + 2 * RET_W + 2 * HG_HEADS * HG_DK + 2 * HG_W
N_MEM = 256
XA_HEADS = 4
XA_HD = D_MODEL // XA_HEADS
PAST_LEN = 16384
ROPE_BASE = 10000.0
EPS = 1e-6

_RQ, _RK, _RV, _RG = 0, 512, 1024, 1536
_HQ, _HF, _HI, _HGATE = 2048, 2560, 3072, 3584

LANE = 128
MIX_TOKENS = 256
RET_CHUNK_LEN = 128
HG_CHUNK_LEN = 64
HG_SUB = 16
TAIL_TOKENS = 256
MEMKV_ROWS = 512
STATE_SEQS = 8
XATTN_SEQS = 4
VMEM_LIMIT_BYTES = 56 * 1024 * 1024

F32 = jnp.float32
BF16 = jnp.bfloat16


def _dot(a, b):
    return jnp.dot(a, b, preferred_element_type=F32)


def _dot_nt(a, b):
    return lax.dot_general(a, b, (((1,), (1,)), ((), ())), preferred_element_type=F32)


def _dot_tn(a, b):
    return lax.dot_general(a, b, (((0,), (0,)), ((), ())), preferred_element_type=F32)


def _rms(x, g):
    ms = jnp.mean(x * x, axis=-1, keepdims=True)
    return x * lax.rsqrt(ms + EPS) * g


def _sigmoid(x):
    return 1.0 / (1.0 + jnp.exp(-x))


def _silu(x):
    return x * _sigmoid(x)


def _head_norm(o):
    return o * lax.rsqrt(jnp.mean(o * o, axis=-1, keepdims=True) + EPS)


def _rope(x, cos2, sin2):
    return x * cos2 + pltpu.roll(x, x.shape[-1] // 2, 1) * sin2


def _lower_bound(hglb):
    m = jnp.max(hglb, axis=0, keepdims=True)
    e = jnp.exp(hglb - m)
    return e[0:1, :] / jnp.sum(e, axis=0, keepdims=True)


def _cumsum_rows(x, period):
    row = lax.broadcasted_iota(jnp.int32, x.shape, 0) & (period - 1)
    s = 1
    while s < period:
        x = x + jnp.where(row >= s, pltpu.roll(x, s, 0), 0.0)
        s *= 2
    return x


def _memkv_kernel(mem_ref, g_ref, wk_ref, wv_ref, k_ref, v_ref, kb_ref, vb_ref):
    m = _rms(mem_ref[...], g_ref[...]).astype(BF16)
    k = _dot(m, wk_ref[...])
    v = _dot(m, wv_ref[...])
    k_ref[...] = k
    v_ref[...] = v
    kb_ref[...] = k.astype(BF16)
    vb_ref[...] = v.astype(BF16)


def _memkv(mem2d, g_mem, w_xk, w_xv):
    n = mem2d.shape[0]
    full = lambda i: (0, 0)
    row = lambda i: (i, 0)
    blk = pl.BlockSpec((MEMKV_ROWS, D_MODEL), row)
    return pl.pallas_call(
        _memkv_kernel,
        grid=(n // MEMKV_ROWS,),
        in_specs=[blk, pl.BlockSpec((1, D_MODEL), full),
                  pl.BlockSpec((D_MODEL, D_MODEL), full), pl.BlockSpec((D_MODEL, D_MODEL), full)],
        out_specs=[blk, blk, blk, blk],
        out_shape=[jax.ShapeDtypeStruct((n, D_MODEL), F32), jax.ShapeDtypeStruct((n, D_MODEL), F32),
                   jax.ShapeDtypeStruct((n, D_MODEL), BF16), jax.ShapeDtypeStruct((n, D_MODEL), BF16)],
        compiler_params=pltpu.CompilerParams(dimension_semantics=("arbitrary",),
                                             vmem_limit_bytes=VMEM_LIMIT_BYTES),
        name="memkv",
    )(mem2d, g_mem, w_xk, w_xv)


def _gate_store(o, gain, gate, out_ref, rows, cols):
    out_ref[rows, cols] = (_head_norm(o) * gain * _silu(gate)).astype(BF16)


def _mix_kernel(x_ref, gmix_ref, win_ref, rgain_ref, hgain_ref, hglb_ref, wout_ref,
                cos_ref, sin_ref, dec_ref, qdec_ref, kdec_ref, sdec_ref,
                x1_ref, sret_ref, shg_ref, shgt_scr, omix_scr):
    t = pl.program_id(1)

    @pl.when(t == 0)
    def _():
        sret_ref[...] = jnp.zeros_like(sret_ref)
        shgt_scr[...] = jnp.zeros_like(shgt_scr)

    x = x_ref[...]
    h = _rms(x, gmix_ref[...]).astype(BF16)
    proj = _dot(h, win_ref[...])

    for c in range(MIX_TOKENS // RET_CHUNK_LEN):
        rows = slice(c * RET_CHUNK_LEN, (c + 1) * RET_CHUNK_LEN)
        cos2 = cos_ref[rows, :]
        sin2 = sin_ref[rows, :]
        for hd in range(RET_HEADS):
            cols = slice(hd * LANE, (hd + 1) * LANE)
            q = _rope(proj[rows, _RQ + hd * LANE:_RQ + (hd + 1) * LANE], cos2, sin2)
            k = _rope(proj[rows, _RK + hd * LANE:_RK + (hd + 1) * LANE], cos2, sin2) * (RET_DK ** -0.5)
            vb = proj[rows, _RV + hd * LANE:_RV + (hd + 1) * LANE].astype(BF16)
            gate = proj[rows, _RG + hd * LANE:_RG + (hd + 1) * LANE]
            qb = q.astype(BF16)
            kb = k.astype(BF16)
            s0 = sret_ref[0, hd]
            att = _dot_nt(qb, kb) * dec_ref[hd]
            o = _dot(att.astype(BF16), vb) + qdec_ref[hd] * _dot(qb, s0.astype(BF16))
            kd = (k * kdec_ref[hd]).astype(BF16)
            sret_ref[0, hd] = sdec_ref[hd] * s0 + _dot_tn(kd, vb)
            _gate_store(o, rgain_ref[:, cols], gate, omix_scr, rows, cols)

    lower = _lower_bound(hglb_ref[...])
    n_sub = HG_CHUNK_LEN // HG_SUB
    crow = lax.broadcasted_iota(jnp.int32, (HG_CHUNK_LEN, HG_CHUNK_LEN), 0)
    ccol = lax.broadcasted_iota(jnp.int32, (HG_CHUNK_LEN, HG_CHUNK_LEN), 1)
    causal = crow >= ccol
    for c in range(MIX_TOKENS // HG_CHUNK_LEN):
        rows = slice(c * HG_CHUNK_LEN, (c + 1) * HG_CHUNK_LEN)
        f = lower + (1.0 - lower) * _sigmoid(proj[rows, _HF:_HF + HG_W])
        kk = 1.0 - f
        qq = _silu(proj[rows, _HQ:_HQ + HG_W])
        b = _cumsum_rows(jnp.log(f), HG_CHUNK_LEN)
        b_last = b[HG_CHUNK_LEN - 1:HG_CHUNK_LEN, :]
        q_inter = (qq * jnp.exp(b)).astype(BF16)
        k_upd = (kk * jnp.exp(b_last - b)).astype(BF16)
        d_last = jnp.exp(b_last)
        for hd in range(HG_HEADS):
            cols = slice(hd * LANE, (hd + 1) * LANE)
            bh = b[:, cols]
            qh = qq[:, cols]
            kh = kk[:, cols]
            vb = proj[rows, _HI + hd * LANE:_HI + (hd + 1) * LANE].astype(BF16)
            gate = proj[rows, _HGATE + hd * LANE:_HGATE + (hd + 1) * LANE]
            q_parts, k_parts = [], []
            for j in range(n_sub):
                lo, hi = j * HG_SUB, (j + 1) * HG_SUB
                ref = bh[lo + HG_SUB // 2 - 1:lo + HG_SUB // 2, :]
                qt = qh[lo:, :] * jnp.exp(bh[lo:, :] - ref)
                kt = kh[lo:hi, :] * jnp.exp(ref - bh[lo:hi, :])
                if lo:
                    qt = jnp.concatenate([jnp.zeros((lo, LANE), F32), qt], axis=0)
                    kt = jnp.concatenate([jnp.zeros((lo, LANE), F32), kt], axis=0)
                if hi < HG_CHUNK_LEN:
                    kt = jnp.concatenate([kt, jnp.zeros((HG_CHUNK_LEN - hi, LANE), F32)], axis=0)
                q_parts.append(qt.astype(BF16))
                k_parts.append(kt.astype(BF16))
            qcat = jnp.concatenate(q_parts, axis=1)
            kcat = jnp.concatenate(k_parts, axis=1)
            att = jnp.where(causal, _dot_nt(qcat, kcat), 0.0)
            st = shgt_scr[hd]
            o = _dot(att.astype(BF16), vb) + _dot_nt(q_inter[:, cols], st.astype(BF16))
            shgt_scr[hd] = st * d_last[:, cols] + _dot_tn(vb, k_upd[:, cols])
            _gate_store(o, hgain_ref[:, cols], gate, omix_scr, rows,
                        slice(RET_W + hd * LANE, RET_W + (hd + 1) * LANE))

    x1_ref[...] = x + _dot(omix_scr[...], wout_ref[...])

    @pl.when(t == pl.num_programs(1) - 1)
    def _():
        for hd in range(HG_HEADS):
            shg_ref[0, hd] = shgt_scr[hd].T


def _ret_tables(length, period):
    log_g = np.log(1.0 - 2.0 ** (-5.0 - np.arange(RET_HEADS, dtype=np.float64)))
    idx = np.arange(length)
    pos = idx % period
    rel = (idx[:, None] - idx[None, :]).astype(np.float64)
    same = (idx[:, None] // period) == (idx[None, :] // period)
    valid = (rel >= 0) & same
    dec = np.where(valid[None], np.exp(log_g[:, None, None] * np.where(valid, rel, 0.0)[None]), 0.0)
    qdec = np.exp(log_g[:, None] * (pos + 1.0))[:, :, None] * np.ones((1, 1, LANE))
    kdec = np.exp(log_g[:, None] * (period - 1.0 - pos))[:, :, None] * np.ones((1, 1, LANE))
    sdec = np.exp(log_g * period)[:, None, None] * np.ones((1, 1, LANE))
    as32 = lambda a: jnp.asarray(a, dtype=F32)
    return as32(dec), as32(qdec), as32(kdec), as32(sdec)


def _rope_tables(pos):
    half = RET_DK // 2
    inv_freq = ROPE_BASE ** (-jnp.arange(half, dtype=F32) / half)
    ang = pos[:, None] * inv_freq[None, :]
    cos, sin = jnp.cos(ang), jnp.sin(ang)
    return jnp.concatenate([cos, cos], axis=-1), jnp.concatenate([-sin, sin], axis=-1)


def _mix_prompt(x2d, batch, seq, g_mix, w_in, ret_gain, hg_gain, hg_lb, w_out):
    nt = seq // MIX_TOKENS
    cos2, sin2 = _rope_tables(jnp.arange(seq, dtype=F32))
    dec, qdec, kdec, sdec = _ret_tables(RET_CHUNK_LEN, RET_CHUNK_LEN)
    c2 = lambda b, t: (0, 0)
    c3 = lambda b, t: (0, 0, 0)
    tok = pl.BlockSpec((MIX_TOKENS, D_MODEL), lambda b, t: (b * nt + t, 0))
    state = pl.BlockSpec((1, RET_HEADS, RET_DK, RET_DV), lambda b, t: (b, 0, 0, 0))
    return pl.pallas_call(
        _mix_kernel,
        grid=(batch, nt),
        in_specs=[tok, pl.BlockSpec((1, D_MODEL), c2), pl.BlockSpec((D_MODEL, IN_W), c2),
                  pl.BlockSpec((1, RET_W), c2), pl.BlockSpec((1, HG_W), c2),
                  pl.BlockSpec(hg_lb.shape, c2), pl.BlockSpec((MIX_W, D_MODEL), c2),
                  pl.BlockSpec((MIX_TOKENS, LANE), lambda b, t: (t, 0)),
                  pl.BlockSpec((MIX_TOKENS, LANE), lambda b, t: (t, 0)),
                  pl.BlockSpec(dec.shape, c3), pl.BlockSpec(qdec.shape, c3),
                  pl.BlockSpec(kdec.shape, c3), pl.BlockSpec(sdec.shape, c3)],
        out_specs=[tok, state, state],
        out_shape=[jax.ShapeDtypeStruct(x2d.shape, F32),
                   jax.ShapeDtypeStruct((batch, RET_HEADS, RET_DK, RET_DV), F32),
                   jax.ShapeDtypeStruct((batch, HG_HEADS, HG_DK, HG_DV), F32)],
        scratch_shapes=[pltpu.VMEM((HG_HEADS, HG_DV, HG_DK), F32),
                        pltpu.VMEM((MIX_TOKENS, MIX_W), BF16)],
        compiler_params=pltpu.CompilerParams(dimension_semantics=("arbitrary", "arbitrary"),
                                             vmem_limit_bytes=VMEM_LIMIT_BYTES),
        name="mix_prompt",
    )(x2d, g_mix, w_in, ret_gain, hg_gain, hg_lb, w_out, cos2, sin2, dec, qdec, kdec, sdec)


def _query(x1, gxa, wxq):
    return _dot(_rms(x1, gxa).astype(BF16), wxq) * (XA_HD ** -0.5)


def _softmax_rows(s):
    p = jnp.exp(s - jnp.max(s, axis=-1, keepdims=True))
    return p / jnp.sum(p, axis=-1, keepdims=True)


def _ffn_final(x1, ox, wxo, gffn, wgate, wup, wdown, gfinal):
    x2 = x1 + _dot(ox, wxo)
    hn = _rms(x2, gffn).astype(BF16)
    a = (_silu(_dot(hn, wgate)) * _dot(hn, wup)).astype(BF16)
    x3 = x2 + _dot(a, wdown)
    return _rms(x3, gfinal)


def _tail_kernel(x1_ref, mk_ref, mv_ref, gxa_ref, wxq_ref, wxo_ref, gffn_ref, wgate_ref, wup_ref,
                 wdown_ref, gfinal_ref, y_ref, ox_scr):
    x1 = x1_ref[...]
    q = _query(x1, gxa_ref[...], wxq_ref[...]).astype(BF16)
    for hd in range(XA_HEADS):
        cols = slice(hd * XA_HD, (hd + 1) * XA_HD)
        p = _softmax_rows(_dot_nt(q[:, cols], mk_ref[0, :, cols]))
        ox_scr[:, cols] = _dot(p.astype(BF16), mv_ref[0, :, cols]).astype(BF16)
    y_ref[...] = _ffn_final(x1, ox_scr[...], wxo_ref[...], gffn_ref[...], wgate_ref[...],
                            wup_ref[...], wdown_ref[...], gfinal_ref[...])


def _resident(shape):
    zeros = (0,) * len(shape)
    return pl.BlockSpec(shape, lambda *_: zeros, pipeline_mode=pl.Buffered(1))


def _tail_prompt(x1, batch, seq, mkb, mvb, g_xa, w_xq, w_xo, g_ffn, w_gate, w_up, w_down, g_final):
    nt = seq // TAIL_TOKENS
    d_ff = w_gate.shape[1]
    tok = pl.BlockSpec((TAIL_TOKENS, D_MODEL), lambda b, t: (b * nt + t, 0))
    mem = pl.BlockSpec((1, N_MEM, D_MODEL), lambda b, t: (b, 0, 0))
    return pl.pallas_call(
        _tail_kernel,
        grid=(batch, nt),
        in_specs=[tok, mem, mem, _resident((1, D_MODEL)), _resident((D_MODEL, D_MODEL)),
                  _resident((D_MODEL, D_MODEL)), _resident((1, D_MODEL)),
                  _resident((D_MODEL, d_ff)), _resident((D_MODEL, d_ff)),
                  _resident((d_ff, D_MODEL)), _resident((1, D_MODEL))],
        out_specs=tok,
        out_shape=jax.ShapeDtypeStruct(x1.shape, F32),
        scratch_shapes=[pltpu.VMEM((TAIL_TOKENS, D_MODEL), BF16)],
        compiler_params=pltpu.CompilerParams(dimension_semantics=("arbitrary", "arbitrary"),
                                             vmem_limit_bytes=VMEM_LIMIT_BYTES),
        name="tail_prompt",
    )(x1, mkb, mvb, g_xa, w_xq, w_xo, g_ffn, w_gate, w_up, w_down, g_final)


def _proj_kernel(x_ref, g_ref, w_ref, o_ref):
    o_ref[...] = _dot(_rms(x_ref[...], g_ref[...]).astype(BF16), w_ref[...])


def _proj_sample(x2d, g_mix, w_in):
    n = x2d.shape[0]
    nb = IN_W // D_MODEL
    return pl.pallas_call(
        _proj_kernel,
        grid=(nb,),
        in_specs=[pl.BlockSpec((n, D_MODEL), lambda j: (0, 0)),
                  pl.BlockSpec((1, D_MODEL), lambda j: (0, 0)),
                  pl.BlockSpec((D_MODEL, D_MODEL), lambda j: (0, j))],
        out_specs=pl.BlockSpec((n, D_MODEL), lambda j: (0, j)),
        out_shape=jax.ShapeDtypeStruct((n, IN_W), F32),
        compiler_params=pltpu.CompilerParams(dimension_semantics=("arbitrary",),
                                             vmem_limit_bytes=VMEM_LIMIT_BYTES),
        name="proj_sample",
    )(x2d, g_mix, w_in)


def _state_kernel(dec_len, proj_ref, sret_ref, shg_ref, rgain_ref, hgain_ref, hglb_ref,
                  cos_ref, sin_ref, dec_ref, qdec_ref, kdec_ref, sdec_ref,
                  omix_ref, nret_ref, nhg_ref):
    n_rows = STATE_SEQS * dec_len
    pair_rows = 8
    seqs_per_pair = pair_rows // dec_len
    row8 = lax.broadcasted_iota(jnp.int32, (pair_rows, LANE), 0)
    rown = lax.broadcasted_iota(jnp.int32, (n_rows, n_rows), 0)
    coln = lax.broadcasted_iota(jnp.int32, (n_rows, n_rows), 1)
    same_seq_causal = (rown >= coln) & ((rown // dec_len) == (coln // dec_len))
    proj = proj_ref[...]
    cos2 = cos_ref[...]
    sin2 = sin_ref[...]

    def per_sequence(q_all, kd_all, vb_all, state_ref, new_ref, hd, scale_of):
        outs = []
        for p in range(n_rows // pair_rows):
            prow = slice(p * pair_rows, (p + 1) * pair_rows)
            q8, kd8, v8 = q_all[prow], kd_all[prow], vb_all[prow]
            acc = jnp.zeros((pair_rows, LANE), F32)
            for j in range(seqs_per_pair):
                seq = p * seqs_per_pair + j
                mine = (row8 >= j * dec_len) & (row8 < (j + 1) * dec_len)
                s0 = state_ref[seq, hd]
                acc = jnp.where(mine, _dot(q8, s0.astype(BF16)), acc)
                kz = jnp.where(mine, kd8, jnp.zeros_like(kd8))
                new_ref[seq, hd] = scale_of(seq) * s0 + _dot_tn(kz, v8)
            outs.append(acc)
        return jnp.concatenate(outs, axis=0)

    for hd in range(RET_HEADS):
        cols = slice(hd * LANE, (hd + 1) * LANE)
        q = _rope(proj[:, _RQ + hd * LANE:_RQ + (hd + 1) * LANE], cos2, sin2)
        k = _rope(proj[:, _RK + hd * LANE:_RK + (hd + 1) * LANE], cos2, sin2) * (RET_DK ** -0.5)
        vb = proj[:, _RV + hd * LANE:_RV + (hd + 1) * LANE].astype(BF16)
        gate = proj[:, _RG + hd * LANE:_RG + (hd + 1) * LANE]
        qb = q.astype(BF16)
        att = _dot_nt(qb, k.astype(BF16)) * dec_ref[hd]
        kd = (k * kdec_ref[hd]).astype(BF16)
        sdec = sdec_ref[hd]
        qs = per_sequence(qb, kd, vb, sret_ref, nret_ref, hd, lambda seq: sdec)
        o = _dot(att.astype(BF16), vb) + qdec_ref[hd] * qs
        _gate_store(o, rgain_ref[:, cols], gate, omix_ref, slice(None), cols)

    lower = _lower_bound(hglb_ref[...])
    f = lower + (1.0 - lower) * _sigmoid(proj[:, _HF:_HF + HG_W])
    kk = 1.0 - f
    qq = _silu(proj[:, _HQ:_HQ + HG_W])
    b = _cumsum_rows(jnp.log(f), dec_len)
    pos = lax.broadcasted_iota(jnp.int32, b.shape, 0) & (dec_len - 1)

    def spread(row_in_seq):
        picked = jnp.where(pos == row_in_seq, b, 0.0)
        out = picked
        for s in range(1, dec_len):
            out = out + jnp.where(pos == (row_in_seq + s), pltpu.roll(picked, s, 0), 0.0)
            out = out + jnp.where(pos == (row_in_seq - s), pltpu.roll(picked, n_rows - s, 0), 0.0)
        return out

    ref = spread(dec_len // 2 - 1)
    b_last = spread(dec_len - 1)
    q_intra = (qq * jnp.exp(b - ref)).astype(BF16)
    k_intra = (kk * jnp.exp(ref - b)).astype(BF16)
    q_inter = (qq * jnp.exp(b)).astype(BF16)
    k_upd = (kk * jnp.exp(b_last - b)).astype(BF16)
    d_last = jnp.exp(b_last)
    for hd in range(HG_HEADS):
        cols = slice(hd * LANE, (hd + 1) * LANE)
        vb = proj[:, _HI + hd * LANE:_HI + (hd + 1) * LANE].astype(BF16)
        gate = proj[:, _HGATE + hd * LANE:_HGATE + (hd + 1) * LANE]
        att = jnp.where(same_seq_causal, _dot_nt(q_intra[:, cols], k_intra[:, cols]), 0.0)
        dpad = jnp.concatenate([d_last[:, cols], jnp.zeros((LANE - n_rows, LANE), F32)], axis=0)
        dcol = dpad.T
        scale_of = lambda seq: dcol[:, seq * dec_len:seq * dec_len + 1]
        qs = per_sequence(q_inter[:, cols], k_upd[:, cols], vb, shg_ref, nhg_ref, hd, scale_of)
        o = _dot(att.astype(BF16), vb) + qs
        _gate_store(o, hgain_ref[:, cols], gate, omix_ref, slice(None),
                    slice(RET_W + hd * LANE, RET_W + (hd + 1) * LANE))


def _state_sample(proj, state_ret, state_hgrn, dec_len, ret_gain, hg_gain, hg_lb):
    n_seq = state_ret.shape[0]
    n_rows = STATE_SEQS * dec_len
    pos = jnp.tile(jnp.arange(dec_len, dtype=F32) + PAST_LEN, STATE_SEQS)
    cos2, sin2 = _rope_tables(pos)
    dec, qdec, kdec, sdec = _ret_tables(n_rows, dec_len)
    c2 = lambda i: (0, 0)
    c3 = lambda i: (0, 0, 0)
    state = pl.BlockSpec((STATE_SEQS, RET_HEADS, RET_DK, RET_DV), lambda i: (i, 0, 0, 0))
    return pl.pallas_call(
        functools.partial(_state_kernel, dec_len),
        grid=(n_seq // STATE_SEQS,),
        in_specs=[pl.BlockSpec((n_rows, IN_W), lambda i: (i, 0)), state, state,
                  pl.BlockSpec((1, RET_W), c2), pl.BlockSpec((1, HG_W), c2),
                  pl.BlockSpec(hg_lb.shape, c2),
                  pl.BlockSpec((n_rows, LANE), c2), pl.BlockSpec((n_rows, LANE), c2),
                  pl.BlockSpec(dec.shape, c3), pl.BlockSpec(qdec.shape, c3),
                  pl.BlockSpec(kdec.shape, c3), pl.BlockSpec(sdec.shape, c3)],
        out_specs=[pl.BlockSpec((n_rows, MIX_W), lambda i: (i, 0)), state, state],
        out_shape=[jax.ShapeDtypeStruct((proj.shape[0], MIX_W), BF16),
                   jax.ShapeDtypeStruct(state_ret.shape, F32),
                   jax.ShapeDtypeStruct(state_hgrn.shape, F32)],
        compiler_params=pltpu.CompilerParams(dimension_semantics=("arbitrary",),
                                             vmem_limit_bytes=VMEM_LIMIT_BYTES),
        name="state_sample",
    )(proj, state_ret, state_hgrn, ret_gain, hg_gain, hg_lb, cos2, sin2, dec, qdec, kdec, sdec)


def _outq_kernel(x_ref, omix_ref, wout_ref, gxa_ref, wxq_ref, x1_ref, q_ref):
    x1 = x_ref[...] + _dot(omix_ref[...], wout_ref[...])
    x1_ref[...] = x1
    q_ref[...] = _query(x1, gxa_ref[...], wxq_ref[...])


def _outq_sample(x2d, omix, w_out, g_xa, w_xq):
    n = x2d.shape[0]
    z = lambda i: (0, 0)
    return pl.pallas_call(
        _outq_kernel,
        grid=(1,),
        in_specs=[pl.BlockSpec((n, D_MODEL), z), pl.BlockSpec((n, MIX_W), z),
                  pl.BlockSpec((MIX_W, D_MODEL), z), pl.BlockSpec((1, D_MODEL), z),
                  pl.BlockSpec((D_MODEL, D_MODEL), z)],
        out_specs=[pl.BlockSpec((n, D_MODEL), z), pl.BlockSpec((n, D_MODEL), z)],
        out_shape=[jax.ShapeDtypeStruct((n, D_MODEL), F32), jax.ShapeDtypeStruct((n, D_MODEL), F32)],
        compiler_params=pltpu.CompilerParams(dimension_semantics=("arbitrary",),
                                             vmem_limit_bytes=VMEM_LIMIT_BYTES),
        name="outq_sample",
    )(x2d, omix, w_out, g_xa, w_xq)


def _xattn_kernel(dec_len, q_ref, k_ref, v_ref, o_ref):
    pair_rows = 8
    seqs_per_pair = pair_rows // dec_len
    row8 = lax.broadcasted_iota(jnp.int32, (pair_rows, XA_HD), 0)
    for p in range(XATTN_SEQS // seqs_per_pair):
        prow = slice(p * pair_rows, (p + 1) * pair_rows)
        for hd in range(XA_HEADS):
            cols = slice(hd * XA_HD, (hd + 1) * XA_HD)
            q8 = q_ref[prow, cols].astype(BF16)
            s = jnp.zeros((pair_rows, N_MEM), F32)
            for j in range(seqs_per_pair):
                mine = (row8 >= j * dec_len) & (row8 < (j + 1) * dec_len)
                kb = k_ref[p * seqs_per_pair + j, :, cols].astype(BF16)
                s = jnp.where(mine, _dot_nt(q8, kb), s)
            pb = _softmax_rows(s).astype(BF16)
            o = jnp.zeros((pair_rows, XA_HD), F32)
            for j in range(seqs_per_pair):
                mine = (row8 >= j * dec_len) & (row8 < (j + 1) * dec_len)
                vb = v_ref[p * seqs_per_pair + j, :, cols].astype(BF16)
                o = jnp.where(mine, _dot(pb, vb), o)
            o_ref[prow, cols] = o


def _xattn_sample(q, cache_k, cache_v, dec_len):
    n_seq = cache_k.shape[0]
    rows = XATTN_SEQS * dec_len
    qblk = pl.BlockSpec((rows, D_MODEL), lambda i: (i, 0))
    cblk = pl.BlockSpec((XATTN_SEQS, N_MEM, D_MODEL), lambda i: (i, 0, 0))
    return pl.pallas_call(
        functools.partial(_xattn_kernel, dec_len),
        grid=(n_seq // XATTN_SEQS,),
        in_specs=[qblk, cblk, cblk],
        out_specs=qblk,
        out_shape=jax.ShapeDtypeStruct(q.shape, F32),
        compiler_params=pltpu.CompilerParams(dimension_semantics=("arbitrary",),
                                             vmem_limit_bytes=VMEM_LIMIT_BYTES),
        name="xattn_sample",
    )(q, cache_k, cache_v)


def _post_kernel(x1_ref, ox_ref, wxo_ref, gffn_ref, wgate_ref, wup_ref, wdown_ref, gfinal_ref, y_ref):
    y_ref[...] = _ffn_final(x1_ref[...], ox_ref[...].astype(BF16), wxo_ref[...], gffn_ref[...],
                            wgate_ref[...], wup_ref[...], wdown_ref[...], gfinal_ref[...])


def _post_sample(x1, ox, w_xo, g_ffn, w_gate, w_up, w_down, g_final):
    n = x1.shape[0]
    d_ff = w_gate.shape[1]
    return pl.pallas_call(
        _post_kernel,
        grid=(1,),
        in_specs=[_resident((n, D_MODEL)), _resident((n, D_MODEL)), _resident((D_MODEL, D_MODEL)),
                  _resident((1, D_MODEL)), _resident((D_MODEL, d_ff)), _resident((D_MODEL, d_ff)),
                  _resident((d_ff, D_MODEL)), _resident((1, D_MODEL))],
        out_specs=pl.BlockSpec((n, D_MODEL), lambda i: (0, 0)),
        out_shape=jax.ShapeDtypeStruct((n, D_MODEL), F32),
        compiler_params=pltpu.CompilerParams(dimension_semantics=("arbitrary",),
                                             vmem_limit_bytes=VMEM_LIMIT_BYTES),
        name="post_sample",
    )(x1, ox, w_xo, g_ffn, w_gate, w_up, w_down, g_final)


def kernel(x_prompt, x_sample, mem_prompt, state_ret, state_hgrn, cache_mem_k, cache_mem_v, g_mix, w_in,
           ret_gain, hg_gain, hg_lb, w_out, g_xa, g_mem, w_xq, w_xk, w_xv, w_xo, g_ffn, w_gate, w_up,
           w_down, g_final):
    depth = w_in.shape[0]
    assert depth == 1, "single-layer step"
    batch, seq, d = x_prompt.shape
    dec_batch, dec_len, _ = x_sample.shape
    assert d == D_MODEL and seq % MIX_TOKENS == 0 and seq % TAIL_TOKENS == 0
    assert 8 % dec_len == 0 and dec_batch % STATE_SEQS == 0 and dec_batch % XATTN_SEQS == 0

    bf = lambda w: w[0].astype(BF16)
    w_in_b, w_out_b, w_xq_b, w_xk_b, w_xv_b, w_xo_b = map(bf, (w_in, w_out, w_xq, w_xk, w_xv, w_xo))
    w_gate_b, w_up_b, w_down_b = map(bf, (w_gate, w_up, w_down))
    g_final2 = g_final.reshape(1, D_MODEL)

    mk, mv, mkb, mvb = _memkv(mem_prompt.reshape(batch * N_MEM, D_MODEL), g_mem, w_xk_b, w_xv_b)
    x1_p, ret_p, hg_p = _mix_prompt(x_prompt.reshape(batch * seq, D_MODEL), batch, seq, g_mix, w_in_b,
                                    ret_gain, hg_gain, hg_lb, w_out_b)
    y_p = _tail_prompt(x1_p, batch, seq, mkb.reshape(batch, N_MEM, D_MODEL),
                       mvb.reshape(batch, N_MEM, D_MODEL), g_xa, w_xq_b, w_xo_b, g_ffn,
                       w_gate_b, w_up_b, w_down_b, g_final2)

    xs = x_sample.reshape(dec_batch * dec_len, D_MODEL)
    proj_s = _proj_sample(xs, g_mix, w_in_b)
    omix_s, ret_s, hg_s = _state_sample(proj_s, state_ret[0], state_hgrn[0], dec_len,
                                        ret_gain, hg_gain, hg_lb)
    x1_s, q_s = _outq_sample(xs, omix_s, w_out_b, g_xa, w_xq_b)
    ox_s = _xattn_sample(q_s, cache_mem_k[0].reshape(dec_batch, N_MEM, D_MODEL),
                         cache_mem_v[0].reshape(dec_batch, N_MEM, D_MODEL), dec_len)
    y_s = _post_sample(x1_s, ox_s, w_xo_b, g_ffn, w_gate_b, w_up_b, w_down_b, g_final2)

    kv_shape = (depth, batch, N_MEM, XA_HEADS, XA_HD)
    return (y_p.reshape(batch, seq, D_MODEL), y_s.reshape(dec_batch, dec_len, D_MODEL),
            ret_p[None], hg_p[None], mk.reshape(kv_shape), mv.reshape(kv_shape),
            ret_s[None], hg_s[None])
```

```python
import functools
import math

import jax
import jax.numpy as jnp
import numpy as np
from jax import lax
from jax.experimental import pallas as pl
from jax.experimental.pallas import tpu as pltpu

D_MODEL = 1024
RET_HEADS = 4
RET_DK = 128
RET_DV = 128
RET_W = RET_HEADS * RET_DV
HG_HEADS = 4
HG_DK = 128
HG_DV = 128
HG_W = HG_HEADS * HG_DV
MIX_W = RET_W + HG_W
IN_W = 2 * RET_HEADS * RET_DK + 2 * RET_W + 2 * HG_HEADS * HG_DK + 2 * HG_W
N_MEM = 256
XA_HEADS = 4
XA_HD = D_MODEL // XA_HEADS
PAST_LEN = 16384
ROPE_BASE = 10000.0
EPS = 1e-6

_RQ, _RK, _RV, _RG = 0, 512, 1024, 1536
_HQ, _HF, _HI, _HGATE = 2048, 2560, 3072, 3584

LANE = 128
MIX_TOKENS = 256
RET_CHUNK_LEN = 128
HG_CHUNK_LEN = 64
HG_SUB = 16
TAIL_TOKENS = 512
MEMKV_ROWS = 512
STATE_SEQS = 8
XATTN_SEQS = 4
VMEM_LIMIT_BYTES = 56 * 1024 * 1024

F32 = jnp.float32
BF16 = jnp.bfloat16


def _dot(a, b):
    return jnp.dot(a, b, preferred_element_type=F32)


def _dot_nt(a, b):
    return lax.dot_general(a, b, (((1,), (1,)), ((), ())), preferred_element_type=F32)


def _dot_tn(a, b):
    return lax.dot_general(a, b, (((0,), (0,)), ((), ())), preferred_element_type=F32)


def _rms(x, g):
    ms = jnp.mean(x * x, axis=-1, keepdims=True)
    return x * lax.rsqrt(ms + EPS) * g


def _sigmoid(x):
    return 1.0 / (1.0 + jnp.exp(-x))


def _silu(x):
    return x * _sigmoid(x)


def _head_norm(o):
    return o * lax.rsqrt(jnp.mean(o * o, axis=-1, keepdims=True) + EPS)


def _rope(x, cos2, sin2):
    return x * cos2 + pltpu.roll(x, x.shape[-1] // 2, 1) * sin2


def _lower_bound(hglb):
    m = jnp.max(hglb, axis=0, keepdims=True)
    e = jnp.exp(hglb - m)
    return e[0:1, :] / jnp.sum(e, axis=0, keepdims=True)


def _cumsum_rows(x, period):
    row = lax.broadcasted_iota(jnp.int32, x.shape, 0) & (period - 1)
    s = 1
    while s < period:
        x = x + jnp.where(row >= s, pltpu.roll(x, s, 0), 0.0)
        s *= 2
    return x


def _memkv_kernel(mem_ref, g_ref, wk_ref, wv_ref, k_ref, v_ref, kb_ref, vb_ref):
    m = _rms(mem_ref[...], g_ref[...]).astype(BF16)
    k = _dot(m, wk_ref[...])
    v = _dot(m, wv_ref[...])
    k_ref[...] = k
    v_ref[...] = v
    kb_ref[...] = k.astype(BF16)
    vb_ref[...] = v.astype(BF16)


def _memkv(mem2d, g_mem, w_xk, w_xv):
    n = mem2d.shape[0]
    full = lambda i: (0, 0)
    row = lambda i: (i, 0)
    blk = pl.BlockSpec((MEMKV_ROWS, D_MODEL), row)
    return pl.pallas_call(
        _memkv_kernel,
        grid=(n // MEMKV_ROWS,),
        in_specs=[blk, pl.BlockSpec((1, D_MODEL), full),
                  pl.BlockSpec((D_MODEL, D_MODEL), full), pl.BlockSpec((D_MODEL, D_MODEL), full)],
        out_specs=[blk, blk, blk, blk],
        out_shape=[jax.ShapeDtypeStruct((n, D_MODEL), F32), jax.ShapeDtypeStruct((n, D_MODEL), F32),
                   jax.ShapeDtypeStruct((n, D_MODEL), BF16), jax.ShapeDtypeStruct((n, D_MODEL), BF16)],
        compiler_params=pltpu.CompilerParams(dimension_semantics=("arbitrary",),
                                             vmem_limit_bytes=VMEM_LIMIT_BYTES),
        name="memkv",
    )(mem2d, g_mem, w_xk, w_xv)


def _gate_store(o, gain, gate, out_ref, rows, cols):
    out_ref[rows, cols] = (_head_norm(o) * gain * _silu(gate)).astype(BF16)


def _mix_kernel(x_ref, gmix_ref, win_ref, rgain_ref, hgain_ref, hglb_ref, wout_ref,
                cos_ref, sin_ref, dec_ref, qdec_ref, kdec_ref, sdec_ref,
                x1_ref, sret_ref, shg_ref, shgt_scr, omix_scr):
    t = pl.program_id(1)

    @pl.when(t == 0)
    def _():
        sret_ref[...] = jnp.zeros_like(sret_ref)
        shgt_scr[...] = jnp.zeros_like(shgt_scr)

    x = x_ref[...]
    h = _rms(x, gmix_ref[...]).astype(BF16)
    proj = _dot(h, win_ref[...])

    ret_units = []
    for c in range(MIX_TOKENS // RET_CHUNK_LEN):
        rows = slice(c * RET_CHUNK_LEN, (c + 1) * RET_CHUNK_LEN)
        cos2 = cos_ref[rows, :]
        sin2 = sin_ref[rows, :]
        for hd in range(RET_HEADS):
            q = _rope(proj[rows, _RQ + hd * LANE:_RQ + (hd + 1) * LANE], cos2, sin2)
            k = _rope(proj[rows, _RK + hd * LANE:_RK + (hd + 1) * LANE], cos2, sin2) * (RET_DK ** -0.5)
            vb = proj[rows, _RV + hd * LANE:_RV + (hd + 1) * LANE].astype(BF16)
            qb = q.astype(BF16)
            att = (_dot_nt(qb, k.astype(BF16)) * dec_ref[hd]).astype(BF16)
            kv = _dot_tn((k * kdec_ref[hd]).astype(BF16), vb)
            ret_units.append((rows, hd, qb, att, vb, kv))

    lower = _lower_bound(hglb_ref[...])
    n_sub = HG_CHUNK_LEN // HG_SUB
    crow = lax.broadcasted_iota(jnp.int32, (HG_CHUNK_LEN, HG_CHUNK_LEN), 0)
    ccol = lax.broadcasted_iota(jnp.int32, (HG_CHUNK_LEN, HG_CHUNK_LEN), 1)
    causal = crow >= ccol
    hg_units = []
    for c in range(MIX_TOKENS // HG_CHUNK_LEN):
        rows = slice(c * HG_CHUNK_LEN, (c + 1) * HG_CHUNK_LEN)
        f = lower + (1.0 - lower) * _sigmoid(proj[rows, _HF:_HF + HG_W])
        kk = 1.0 - f
        qq = _silu(proj[rows, _HQ:_HQ + HG_W])
        b = _cumsum_rows(jnp.log(f), HG_CHUNK_LEN)
        b_last = b[HG_CHUNK_LEN - 1:HG_CHUNK_LEN, :]
        q_inter = (qq * jnp.exp(b)).astype(BF16)
        k_upd = (kk * jnp.exp(b_last - b)).astype(BF16)
        d_last = jnp.exp(b_last)
        for hd in range(HG_HEADS):
            cols = slice(hd * LANE, (hd + 1) * LANE)
            bh = b[:, cols]
            qh = qq[:, cols]
            kh = kk[:, cols]
            vb = proj[rows, _HI + hd * LANE:_HI + (hd + 1) * LANE].astype(BF16)
            q_parts, k_parts = [], []
            for j in range(n_sub):
                lo, hi = j * HG_SUB, (j + 1) * HG_SUB
                ref = bh[lo + HG_SUB // 2 - 1:lo + HG_SUB // 2, :]
                qt = qh[lo:, :] * jnp.exp(bh[lo:, :] - ref)
                kt = kh[lo:hi, :] * jnp.exp(ref - bh[lo:hi, :])
                if lo:
                    qt = jnp.concatenate([jnp.zeros((lo, LANE), F32), qt], axis=0)
                    kt = jnp.concatenate([jnp.zeros((lo, LANE), F32), kt], axis=0)
                if hi < HG_CHUNK_LEN:
                    kt = jnp.concatenate([kt, jnp.zeros((HG_CHUNK_LEN - hi, LANE), F32)], axis=0)
                q_parts.append(qt.astype(BF16))
                k_parts.append(kt.astype(BF16))
            qcat = jnp.concatenate(q_parts, axis=1)
            kcat = jnp.concatenate(k_parts, axis=1)
            att = jnp.where(causal, _dot_nt(qcat, kcat), 0.0).astype(BF16)
            kv = _dot_tn(vb, k_upd[:, cols])
            hg_units.append((rows, hd, q_inter[:, cols], att, vb, kv, d_last[:, cols]))

    for hd in range(RET_HEADS):
        cols = slice(hd * LANE, (hd + 1) * LANE)
        s = sret_ref[0, hd]
        for rows, uh, qb, att, vb, kv in ret_units:
            if uh != hd:
                continue
            o = _dot(att, vb) + qdec_ref[hd] * _dot(qb, s.astype(BF16))
            s = sdec_ref[hd] * s + kv
            gate = proj[rows, _RG + hd * LANE:_RG + (hd + 1) * LANE]
            _gate_store(o, rgain_ref[:, cols], gate, omix_scr, rows, cols)
        sret_ref[0, hd] = s

    for hd in range(HG_HEADS):
        cols = slice(hd * LANE, (hd + 1) * LANE)
        st = shgt_scr[hd]
        for rows, uh, qi, att, vb, kv, dl in hg_units:
            if uh != hd:
                continue
            o = _dot(att, vb) + _dot_nt(qi, st.astype(BF16))
            st = st * dl + kv
            gate = proj[rows, _HGATE + hd * LANE:_HGATE + (hd + 1) * LANE]
            _gate_store(o, hgain_ref[:, cols], gate, omix_scr, rows,
                        slice(RET_W + hd * LANE, RET_W + (hd + 1) * LANE))
        shgt_scr[hd] = st

    x1_ref[...] = x + _dot(omix_scr[...], wout_ref[...])

    @pl.when(t == pl.num_programs(1) - 1)
    def _():
        for hd in range(HG_HEADS):
            shg_ref[0, hd] = shgt_scr[hd].T


def _ret_tables(length, period):
    log_g = np.log(1.0 - 2.0 ** (-5.0 - np.arange(RET_HEADS, dtype=np.float64)))
    idx = np.arange(length)
    pos = idx % period
    rel = (idx[:, None] - idx[None, :]).astype(np.float64)
    same = (idx[:, None] // period) == (idx[None, :] // period)
    valid = (rel >= 0) & same
    dec = np.where(valid[None], np.exp(log_g[:, None, None] * np.where(valid, rel, 0.0)[None]), 0.0)
    qdec = np.exp(log_g[:, None] * (pos + 1.0))[:, :, None] * np.ones((1, 1, LANE))
    kdec = np.exp(log_g[:, None] * (period - 1.0 - pos))[:, :, None] * np.ones((1, 1, LANE))
    sdec = np.exp(log_g * period)[:, None, None] * np.ones((1, 1, LANE))
    as32 = lambda a: jnp.asarray(a, dtype=F32)
    return as32(dec), as32(qdec), as32(kdec), as32(sdec)


def _rope_tables(pos):
    half = RET_DK // 2
    inv_freq = ROPE_BASE ** (-jnp.arange(half, dtype=F32) / half)
    ang = pos[:, None] * inv_freq[None, :]
    cos, sin = jnp.cos(ang), jnp.sin(ang)
    return jnp.concatenate([cos, cos], axis=-1), jnp.concatenate([-sin, sin], axis=-1)


def _mix_prompt(x2d, batch, seq, g_mix, w_in, ret_gain, hg_gain, hg_lb, w_out):
    nt = seq // MIX_TOKENS
    cos2, sin2 = _rope_tables(jnp.arange(seq, dtype=F32))
    dec, qdec, kdec, sdec = _ret_tables(RET_CHUNK_LEN, RET_CHUNK_LEN)
    c2 = lambda b, t: (0, 0)
    c3 = lambda b, t: (0, 0, 0)
    tok = pl.BlockSpec((MIX_TOKENS, D_MODEL), lambda b, t: (b * nt + t, 0))
    state = pl.BlockSpec((1, RET_HEADS, RET_DK, RET_DV), lambda b, t: (b, 0, 0, 0))
    return pl.pallas_call(
        _mix_kernel,
        grid=(batch, nt),
        in_specs=[tok, pl.BlockSpec((1, D_MODEL), c2), pl.BlockSpec((D_MODEL, IN_W), c2),
                  pl.BlockSpec((1, RET_W), c2), pl.BlockSpec((1, HG_W), c2),
                  pl.BlockSpec(hg_lb.shape, c2), pl.BlockSpec((MIX_W, D_MODEL), c2),
                  pl.BlockSpec((MIX_TOKENS, LANE), lambda b, t: (t, 0)),
                  pl.BlockSpec((MIX_TOKENS, LANE), lambda b, t: (t, 0)),
                  pl.BlockSpec(dec.shape, c3), pl.BlockSpec(qdec.shape, c3),
                  pl.BlockSpec(kdec.shape, c3), pl.BlockSpec(sdec.shape, c3)],
        out_specs=[tok, state, state],
        out_shape=[jax.ShapeDtypeStruct(x2d.shape, F32),
                   jax.ShapeDtypeStruct((batch, RET_HEADS, RET_DK, RET_DV), F32),
                   jax.ShapeDtypeStruct((batch, HG_HEADS, HG_DK, HG_DV), F32)],
        scratch_shapes=[pltpu.VMEM((HG_HEADS, HG_DV, HG_DK), F32),
                        pltpu.VMEM((MIX_TOKENS, MIX_W), BF16)],
        compiler_params=pltpu.CompilerParams(dimension_semantics=("arbitrary", "arbitrary"),
                                             vmem_limit_bytes=VMEM_LIMIT_BYTES),
        name="mix_prompt",
    )(x2d, g_mix, w_in, ret_gain, hg_gain, hg_lb, w_out, cos2, sin2, dec, qdec, kdec, sdec)


def _query(x1, gxa, wxq):
    return _dot(_rms(x1, gxa).astype(BF16), wxq) * (XA_HD ** -0.5)


def _softmax_rows(s):
    p = jnp.exp(s - jnp.max(s, axis=-1, keepdims=True))
    return p / jnp.sum(p, axis=-1, keepdims=True)


def _ffn_final(x1, ox, wxo, gffn, wgate, wup, wdown, gfinal):
    x2 = x1 + _dot(ox, wxo)
    hn = _rms(x2, gffn).astype(BF16)
    a = (_silu(_dot(hn, wgate)) * _dot(hn, wup)).astype(BF16)
    x3 = x2 + _dot(a, wdown)
    return _rms(x3, gfinal)


def _tail_kernel(x1_ref, mk_ref, mv_ref, gxa_ref, wxq_ref, wxo_ref, gffn_ref, wgate_ref, wup_ref,
                 wdown_ref, gfinal_ref, y_ref, ox_scr):
    x1 = x1_ref[...]
    q = _query(x1, gxa_ref[...], wxq_ref[...]).astype(BF16)
    for hd in range(XA_HEADS):
        cols = slice(hd * XA_HD, (hd + 1) * XA_HD)
        p = _softmax_rows(_dot_nt(q[:, cols], mk_ref[0, :, cols]))
        ox_scr[:, cols] = _dot(p.astype(BF16), mv_ref[0, :, cols]).astype(BF16)
    y_ref[...] = _ffn_final(x1, ox_scr[...], wxo_ref[...], gffn_ref[...], wgate_ref[...],
                            wup_ref[...], wdown_ref[...], gfinal_ref[...])


def _resident(shape):
    zeros = (0,) * len(shape)
    return pl.BlockSpec(shape, lambda *_: zeros, pipeline_mode=pl.Buffered(1))


def _tail_prompt(x1, batch, seq, mkb, mvb, g_xa, w_xq, w_xo, g_ffn, w_gate, w_up, w_down, g_final):
    nt = seq // TAIL_TOKENS
    d_ff = w_gate.shape[1]
    tok = pl.BlockSpec((TAIL_TOKENS, D_MODEL), lambda b, t: (b * nt + t, 0))
    mem = pl.BlockSpec((1, N_MEM, D_MODEL), lambda b, t: (b, 0, 0))
    return pl.pallas_call(
        _tail_kernel,
        grid=(batch, nt),
        in_specs=[tok, mem, mem, _resident((1, D_MODEL)), _resident((D_MODEL, D_MODEL)),
                  _resident((D_MODEL, D_MODEL)), _resident((1, D_MODEL)),
                  _resident((D_MODEL, d_ff)), _resident((D_MODEL, d_ff)),
                  _resident((d_ff, D_MODEL)), _resident((1, D_MODEL))],
        out_specs=tok,
        out_shape=jax.ShapeDtypeStruct(x1.shape, F32),
        scratch_shapes=[pltpu.VMEM((TAIL_TOKENS, D_MODEL), BF16)],
        compiler_params=pltpu.CompilerParams(dimension_semantics=("arbitrary", "arbitrary"),
                                             vmem_limit_bytes=VMEM_LIMIT_BYTES),
        name="tail_prompt",
    )(x1, mkb, mvb, g_xa, w_xq, w_xo, g_ffn, w_gate, w_up, w_down, g_final)


def _proj_kernel(x_ref, g_ref, w_ref, o_ref):
    o_ref[...] = _dot(_rms(x_ref[...], g_ref[...]).astype(BF16), w_ref[...])


def _proj_sample(x2d, g_mix, w_in):
    n = x2d.shape[0]
    nb = IN_W // D_MODEL
    return pl.pallas_call(
        _proj_kernel,
        grid=(nb,),
        in_specs=[pl.BlockSpec((n, D_MODEL), lambda j: (0, 0)),
                  pl.BlockSpec((1, D_MODEL), lambda j: (0, 0)),
                  pl.BlockSpec((D_MODEL, D_MODEL), lambda j: (0, j))],
        out_specs=pl.BlockSpec((n, D_MODEL), lambda j: (0, j)),
        out_shape=jax.ShapeDtypeStruct((n, IN_W), F32),
        compiler_params=pltpu.CompilerParams(dimension_semantics=("arbitrary",),
                                             vmem_limit_bytes=VMEM_LIMIT_BYTES),
        name="proj_sample",
    )(x2d, g_mix, w_in)


def _state_kernel(dec_len, proj_ref, sret_ref, shg_ref, rgain_ref, hgain_ref, hglb_ref,
                  cos_ref, sin_ref, dec_ref, qdec_ref, kdec_ref, sdec_ref,
                  omix_ref, nret_ref, nhg_ref):
    n_rows = STATE_SEQS * dec_len
    pair_rows = 8
    seqs_per_pair = pair_rows // dec_len
    row8 = lax.broadcasted_iota(jnp.int32, (pair_rows, LANE), 0)
    rown = lax.broadcasted_iota(jnp.int32, (n_rows, n_rows), 0)
    coln = lax.broadcasted_iota(jnp.int32, (n_rows, n_rows), 1)
    same_seq_causal = (rown >= coln) & ((rown // dec_len) == (coln // dec_len))
    proj = proj_ref[...]
    cos2 = cos_ref[...]
    sin2 = sin_ref[...]

    def per_sequence(q_all, kd_all, vb_all, state_ref, new_ref, hd, scale_of):
        outs = []
        for p in range(n_rows // pair_rows):
            prow = slice(p * pair_rows, (p + 1) * pair_rows)
            q8, kd8, v8 = q_all[prow], kd_all[prow], vb_all[prow]
            acc = jnp.zeros((pair_rows, LANE), F32)
            for j in range(seqs_per_pair):
                seq = p * seqs_per_pair + j
                mine = (row8 >= j * dec_len) & (row8 < (j + 1) * dec_len)
                s0 = state_ref[seq, hd]
                acc = jnp.where(mine, _dot(q8, s0.astype(BF16)), acc)
                kz = jnp.where(mine, kd8, jnp.zeros_like(kd8))
                new_ref[seq, hd] = scale_of(seq) * s0 + _dot_tn(kz, v8)
            outs.append(acc)
        return jnp.concatenate(outs, axis=0)

    for hd in range(RET_HEADS):
        cols = slice(hd * LANE, (hd + 1) * LANE)
        q = _rope(proj[:, _RQ + hd * LANE:_RQ + (hd + 1) * LANE], cos2, sin2)
        k = _rope(proj[:, _RK + hd * LANE:_RK + (hd + 1) * LANE], cos2, sin2) * (RET_DK ** -0.5)
        vb = proj[:, _RV + hd * LANE:_RV + (hd + 1) * LANE].astype(BF16)
        gate = proj[:, _RG + hd * LANE:_RG + (hd + 1) * LANE]
        qb = q.astype(BF16)
        att = _dot_nt(qb, k.astype(BF16)) * dec_ref[hd]
        kd = (k * kdec_ref[hd]).astype(BF16)
        sdec = sdec_ref[hd]
        qs = per_sequence(qb, kd, vb, sret_ref, nret_ref, hd, lambda seq: sdec)
        o = _dot(att.astype(BF16), vb) + qdec_ref[hd] * qs
        _gate_store(o, rgain_ref[:, cols], gate, omix_ref, slice(None), cols)

    lower = _lower_bound(hglb_ref[...])
    f = lower + (1.0 - lower) * _sigmoid(proj[:, _HF:_HF + HG_W])
    kk = 1.0 - f
    qq = _silu(proj[:, _HQ:_HQ + HG_W])
    b = _cumsum_rows(jnp.log(f), dec_len)
    pos = lax.broadcasted_iota(jnp.int32, b.shape, 0) & (dec_len - 1)

    def spread(row_in_seq):
        picked = jnp.where(pos == row_in_seq, b, 0.0)
        out = picked
        for s in range(1, dec_len):
            out = out + jnp.where(pos == (row_in_seq + s), pltpu.roll(picked, s, 0), 0.0)
            out = out + jnp.where(pos == (row_in_seq - s), pltpu.roll(picked, n_rows - s, 0), 0.0)
        return out

    ref = spread(dec_len // 2 - 1)
    b_last = spread(dec_len - 1)
    q_intra = (qq * jnp.exp(b - ref)).astype(BF16)
    k_intra = (kk * jnp.exp(ref - b)).astype(BF16)
    q_inter = (qq * jnp.exp(b)).astype(BF16)
    k_upd = (kk * jnp.exp(b_last - b)).astype(BF16)
    d_last = jnp.exp(b_last)
    for hd in range(HG_HEADS):
        cols = slice(hd * LANE, (hd + 1) * LANE)
        vb = proj[:, _HI + hd * LANE:_HI + (hd + 1) * LANE].astype(BF16)
        gate = proj[:, _HGATE + hd * LANE:_HGATE + (hd + 1) * LANE]
        att = jnp.where(same_seq_causal, _dot_nt(q_intra[:, cols], k_intra[:, cols]), 0.0)
        dpad = jnp.concatenate([d_last[:, cols], jnp.zeros((LANE - n_rows, LANE), F32)], axis=0)
        dcol = dpad.T
        scale_of = lambda seq: dcol[:, seq * dec_len:seq * dec_len + 1]
        qs = per_sequence(q_inter[:, cols], k_upd[:, cols], vb, shg_ref, nhg_ref, hd, scale_of)
        o = _dot(att.astype(BF16), vb) + qs
        _gate_store(o, hgain_ref[:, cols], gate, omix_ref, slice(None),
                    slice(RET_W + hd * LANE, RET_W + (hd + 1) * LANE))


def _state_sample(proj, state_ret, state_hgrn, dec_len, ret_gain, hg_gain, hg_lb):
    n_seq = state_ret.shape[0]
    n_rows = STATE_SEQS * dec_len
    pos = jnp.tile(jnp.arange(dec_len, dtype=F32) + PAST_LEN, STATE_SEQS)
    cos2, sin2 = _rope_tables(pos)
    dec, qdec, kdec, sdec = _ret_tables(n_rows, dec_len)
    c2 = lambda i: (0, 0)
    c3 = lambda i: (0, 0, 0)
    state = pl.BlockSpec((STATE_SEQS, RET_HEADS, RET_DK, RET_DV), lambda i: (i, 0, 0, 0))
    return pl.pallas_call(
        functools.partial(_state_kernel, dec_len),
        grid=(n_seq // STATE_SEQS,),
        in_specs=[pl.BlockSpec((n_rows, IN_W), lambda i: (i, 0)), state, state,
                  pl.BlockSpec((1, RET_W), c2), pl.BlockSpec((1, HG_W), c2),
                  pl.BlockSpec(hg_lb.shape, c2),
                  pl.BlockSpec((n_rows, LANE), c2), pl.BlockSpec((n_rows, LANE), c2),
                  pl.BlockSpec(dec.shape, c3), pl.BlockSpec(qdec.shape, c3),
                  pl.BlockSpec(kdec.shape, c3), pl.BlockSpec(sdec.shape, c3)],
        out_specs=[pl.BlockSpec((n_rows, MIX_W), lambda i: (i, 0)), state, state],
        out_shape=[jax.ShapeDtypeStruct((proj.shape[0], MIX_W), BF16),
                   jax.ShapeDtypeStruct(state_ret.shape, F32),
                   jax.ShapeDtypeStruct(state_hgrn.shape, F32)],
        compiler_params=pltpu.CompilerParams(dimension_semantics=("arbitrary",),
                                             vmem_limit_bytes=VMEM_LIMIT_BYTES),
        name="state_sample",
    )(proj, state_ret, state_hgrn, ret_gain, hg_gain, hg_lb, cos2, sin2, dec, qdec, kdec, sdec)


def _outq_kernel(x_ref, omix_ref, wout_ref, gxa_ref, wxq_ref, x1_ref, q_ref):
    x1 = x_ref[...] + _dot(omix_ref[...], wout_ref[...])
    x1_ref[...] = x1
    q_ref[...] = _query(x1, gxa_ref[...], wxq_ref[...])


def _outq_sample(x2d, omix, w_out, g_xa, w_xq):
    n = x2d.shape[0]
    z = lambda i: (0, 0)
    return pl.pallas_call(
        _outq_kernel,
        grid=(1,),
        in_specs=[pl.BlockSpec((n, D_MODEL), z), pl.BlockSpec((n, MIX_W), z),
                  pl.BlockSpec((MIX_W, D_MODEL), z), pl.BlockSpec((1, D_MODEL), z),
                  pl.BlockSpec((D_MODEL, D_MODEL), z)],
        out_specs=[pl.BlockSpec((n, D_MODEL), z), pl.BlockSpec((n, D_MODEL), z)],
        out_shape=[jax.ShapeDtypeStruct((n, D_MODEL), F32), jax.ShapeDtypeStruct((n, D_MODEL), F32)],
        compiler_params=pltpu.CompilerParams(dimension_semantics=("arbitrary",),
                                             vmem_limit_bytes=VMEM_LIMIT_BYTES),
        name="outq_sample",
    )(x2d, omix, w_out, g_xa, w_xq)


def _xattn_kernel(dec_len, q_ref, k_ref, v_ref, o_ref):
    pair_rows = 8
    seqs_per_pair = pair_rows // dec_len
    row8 = lax.broadcasted_iota(jnp.int32, (pair_rows, XA_HD), 0)
    for p in range(XATTN_SEQS // seqs_per_pair):
        prow = slice(p * pair_rows, (p + 1) * pair_rows)
        for hd in range(XA_HEADS):
            cols = slice(hd * XA_HD, (hd + 1) * XA_HD)
            q8 = q_ref[prow, cols].astype(BF16)
            s = jnp.zeros((pair_rows, N_MEM), F32)
            for j in range(seqs_per_pair):
                mine = (row8 >= j * dec_len) & (row8 < (j + 1) * dec_len)
                kb = k_ref[p * seqs_per_pair + j, :, cols]
                s = jnp.where(mine, _dot_nt(q8, kb), s)
            pb = _softmax_rows(s).astype(BF16)
            o = jnp.zeros((pair_rows, XA_HD), F32)
            for j in range(seqs_per_pair):
                mine = (row8 >= j * dec_len) & (row8 < (j + 1) * dec_len)
                vb = v_ref[p * seqs_per_pair + j, :, cols]
                o = jnp.where(mine, _dot(pb, vb), o)
            o_ref[prow, cols] = o


def _xattn_sample(q, cache_k, cache_v, dec_len):
    n_seq = cache_k.shape[0]
    rows = XATTN_SEQS * dec_len
    qblk = pl.BlockSpec((rows, D_MODEL), lambda i: (i, 0))
    cblk = pl.BlockSpec((XATTN_SEQS, N_MEM, D_MODEL), lambda i: (i, 0, 0))
    return pl.pallas_call(
        functools.partial(_xattn_kernel, dec_len),
        grid=(n_seq // XATTN_SEQS,),
        in_specs=[qblk, cblk, cblk],
        out_specs=qblk,
        out_shape=jax.ShapeDtypeStruct(q.shape, F32),
        compiler_params=pltpu.CompilerParams(dimension_semantics=("arbitrary",),
                                             vmem_limit_bytes=VMEM_LIMIT_BYTES),
        name="xattn_sample",
    )(q, cache_k, cache_v)


def _post_kernel(x1_ref, ox_ref, wxo_ref, gffn_ref, wgate_ref, wup_ref, wdown_ref, gfinal_ref, y_ref):
    y_ref[...] = _ffn_final(x1_ref[...], ox_ref[...].astype(BF16), wxo_ref[...], gffn_ref[...],
                            wgate_ref[...], wup_ref[...], wdown_ref[...], gfinal_ref[...])


def _post_sample(x1, ox, w_xo, g_ffn, w_gate, w_up, w_down, g_final):
    n = x1.shape[0]
    d_ff = w_gate.shape[1]
    return pl.pallas_call(
        _post_kernel,
        grid=(1,),
        in_specs=[_resident((n, D_MODEL)), _resident((n, D_MODEL)), _resident((D_MODEL, D_MODEL)),
                  _resident((1, D_MODEL)), _resident((D_MODEL, d_ff)), _resident((D_MODEL, d_ff)),
                  _resident((d_ff, D_MODEL)), _resident((1, D_MODEL))],
        out_specs=pl.BlockSpec((n, D_MODEL), lambda i: (0, 0)),
        out_shape=jax.ShapeDtypeStruct((n, D_MODEL), F32),
        compiler_params=pltpu.CompilerParams(dimension_semantics=("arbitrary",),
                                             vmem_limit_bytes=VMEM_LIMIT_BYTES),
        name="post_sample",
    )(x1, ox, w_xo, g_ffn, w_gate, w_up, w_down, g_final)


def kernel(x_prompt, x_sample, mem_prompt, state_ret, state_hgrn, cache_mem_k, cache_mem_v, g_mix, w_in,
           ret_gain, hg_gain, hg_lb, w_out, g_xa, g_mem, w_xq, w_xk, w_xv, w_xo, g_ffn, w_gate, w_up,
           w_down, g_final):
    depth = w_in.shape[0]
    assert depth == 1, "single-layer step"
    batch, seq, d = x_prompt.shape
    dec_batch, dec_len, _ = x_sample.shape
    assert d == D_MODEL and seq % MIX_TOKENS == 0 and seq % TAIL_TOKENS == 0
    assert 8 % dec_len == 0 and dec_batch % STATE_SEQS == 0 and dec_batch % XATTN_SEQS == 0

    bf = lambda w: w[0].astype(BF16)
    w_in_b, w_out_b, w_xq_b, w_xk_b, w_xv_b, w_xo_b = map(bf, (w_in, w_out, w_xq, w_xk, w_xv, w_xo))
    w_gate_b, w_up_b, w_down_b = map(bf, (w_gate, w_up, w_down))
    g_final2 = g_final.reshape(1, D_MODEL)

    mk, mv, mkb, mvb = _memkv(mem_prompt.reshape(batch * N_MEM, D_MODEL), g_mem, w_xk_b, w_xv_b)
    x1_p, ret_p, hg_p = _mix_prompt(x_prompt.reshape(batch * seq, D_MODEL), batch, seq, g_mix, w_in_b,
                                    ret_gain, hg_gain, hg_lb, w_out_b)
    y_p = _tail_prompt(x1_p, batch, seq, mkb.reshape(batch, N_MEM, D_MODEL),
                       mvb.reshape(batch, N_MEM, D_MODEL), g_xa, w_xq_b, w_xo_b, g_ffn,
                       w_gate_b, w_up_b, w_down_b, g_final2)

    xs = x_sample.reshape(dec_batch * dec_len, D_MODEL)
    proj_s = _proj_sample(xs, g_mix, w_in_b)
    omix_s, ret_s, hg_s = _state_sample(proj_s, state_ret[0], state_hgrn[0], dec_len,
                                        ret_gain, hg_gain, hg_lb)
    x1_s, q_s = _outq_sample(xs, omix_s, w_out_b, g_xa, w_xq_b)
    cache_b = lambda c: c[0].reshape(dec_batch, N_MEM, D_MODEL).astype(BF16)
    ox_s = _xattn_sample(q_s, cache_b(cache_mem_k), cache_b(cache_mem_v), dec_len)
    y_s = _post_sample(x1_s, ox_s, w_xo_b, g_ffn, w_gate_b, w_up_b, w_down_b, g_final2)

    kv_shape = (depth, batch, N_MEM, XA_HEADS, XA_HD)
    return (y_p.reshape(batch, seq, D_MODEL), y_s.reshape(dec_batch, dec_len, D_MODEL),
            ret_p[None], hg_p[None], mk.reshape(kv_shape), mv.reshape(kv_shape),
            ret_s[None], hg_s[None])
```

```python
import functools
import math

import jax
import jax.numpy as jnp
import numpy as np
from jax import lax
from jax.experimental import pallas as pl
from jax.experimental.pallas import tpu as pltpu

D_MODEL = 1024
RET_HEADS = 4
RET_DK = 128
RET_DV = 128
RET_W = RET_HEADS * RET_DV
HG_HEADS = 4
HG_DK = 128
HG_DV = 128
HG_W = HG_HEADS * HG_DV
MIX_W = RET_W + HG_W
IN_W = 2 * RET_HEADS * RET_DK + 2 * RET_W + 2 * HG_HEADS * HG_DK + 2 * HG_W
N_MEM = 256
XA_HEADS = 4
XA_HD = D_MODEL // XA_HEADS
PAST_LEN = 16384
ROPE_BASE = 10000.0
EPS = 1e-6

_RQ, _RK, _RV, _RG = 0, 512, 1024, 1536
_HQ, _HF, _HI, _HGATE = 2048, 2560, 3072, 3584

LANE = 128
MIX_TOKENS = 256
RET_CHUNK_LEN = 128
HG_CHUNK_LEN = 64
HG_SUB = 16
TAIL_TOKENS = 512
MEMKV_ROWS = 512
STATE_SEQS = 8
XATTN_SEQS = 4
VMEM_LIMIT_BYTES = 56 * 1024 * 1024

F32 = jnp.float32
BF16 = jnp.bfloat16


def _dot(a, b):
    return jnp.dot(a, b, preferred_element_type=F32)


def _dot_nt(a, b):
    return lax.dot_general(a, b, (((1,), (1,)), ((), ())), preferred_element_type=F32)


def _dot_tn(a, b):
    return lax.dot_general(a, b, (((0,), (0,)), ((), ())), preferred_element_type=F32)


def _rms(x, g):
    ms = jnp.mean(x * x, axis=-1, keepdims=True)
    return x * lax.rsqrt(ms + EPS) * g


def _sigmoid(x):
    return 1.0 / (1.0 + jnp.exp(-x))


def _silu(x):
    return x * _sigmoid(x)


def _head_norm(o):
    return o * lax.rsqrt(jnp.mean(o * o, axis=-1, keepdims=True) + EPS)


def _rope(x, cos2, sin2):
    return x * cos2 + pltpu.roll(x, x.shape[-1] // 2, 1) * sin2


def _lower_bound(hglb):
    m = jnp.max(hglb, axis=0, keepdims=True)
    e = jnp.exp(hglb - m)
    return e[0:1, :] / jnp.sum(e, axis=0, keepdims=True)


def _cumsum_rows(x, period):
    row = lax.broadcasted_iota(jnp.int32, x.shape, 0) & (period - 1)
    s = 1
    while s < period:
        x = x + jnp.where(row >= s, pltpu.roll(x, s, 0), 0.0)
        s *= 2
    return x


def _memkv_kernel(mem_ref, g_ref, wk_ref, wv_ref, k_ref, v_ref, kb_ref, vb_ref):
    m = _rms(mem_ref[...], g_ref[...]).astype(BF16)
    k = _dot(m, wk_ref[...])
    v = _dot(m, wv_ref[...])
    k_ref[...] = k
    v_ref[...] = v
    kb_ref[...] = k.astype(BF16)
    vb_ref[...] = v.astype(BF16)


def _memkv(mem2d, g_mem, w_xk, w_xv):
    n = mem2d.shape[0]
    full = lambda i: (0, 0)
    row = lambda i: (i, 0)
    blk = pl.BlockSpec((MEMKV_ROWS, D_MODEL), row)
    return pl.pallas_call(
        _memkv_kernel,
        grid=(n // MEMKV_ROWS,),
        in_specs=[blk, pl.BlockSpec((1, D_MODEL), full),
                  pl.BlockSpec((D_MODEL, D_MODEL), full), pl.BlockSpec((D_MODEL, D_MODEL), full)],
        out_specs=[blk, blk, blk, blk],
        out_shape=[jax.ShapeDtypeStruct((n, D_MODEL), F32), jax.ShapeDtypeStruct((n, D_MODEL), F32),
                   jax.ShapeDtypeStruct((n, D_MODEL), BF16), jax.ShapeDtypeStruct((n, D_MODEL), BF16)],
        compiler_params=pltpu.CompilerParams(dimension_semantics=("arbitrary",),
                                             vmem_limit_bytes=VMEM_LIMIT_BYTES),
        name="memkv",
    )(mem2d, g_mem, w_xk, w_xv)


def _gate_store(o, gain, gate, out_ref, rows, cols):
    out_ref[rows, cols] = (_head_norm(o) * gain * _silu(gate)).astype(BF16)


def _mix_kernel(x_ref, gmix_ref, win_ref, rgain_ref, hgain_ref, hglb_ref, wout_ref,
                cos_ref, sin_ref, dec_ref, qdec_ref, kdec_ref, sdec_ref,
                x1_ref, sret_ref, shg_ref, shgt_scr, omix_scr):
    t = pl.program_id(1)

    @pl.when(t == 0)
    def _():
        sret_ref[...] = jnp.zeros_like(sret_ref)
        shgt_scr[...] = jnp.zeros_like(shgt_scr)

    x = x_ref[...]
    h = _rms(x, gmix_ref[...]).astype(BF16)
    proj = _dot(h, win_ref[...])

    ret_units = []
    for c in range(MIX_TOKENS // RET_CHUNK_LEN):
        rows = slice(c * RET_CHUNK_LEN, (c + 1) * RET_CHUNK_LEN)
        cos2 = cos_ref[rows, :]
        sin2 = sin_ref[rows, :]
        for hd in range(RET_HEADS):
            q = _rope(proj[rows, _RQ + hd * LANE:_RQ + (hd + 1) * LANE], cos2, sin2)
            k = _rope(proj[rows, _RK + hd * LANE:_RK + (hd + 1) * LANE], cos2, sin2) * (RET_DK ** -0.5)
            vb = proj[rows, _RV + hd * LANE:_RV + (hd + 1) * LANE].astype(BF16)
            qb = q.astype(BF16)
            att = (_dot_nt(qb, k.astype(BF16)) * dec_ref[hd]).astype(BF16)
            kv = _dot_tn((k * kdec_ref[hd]).astype(BF16), vb)
            ret_units.append((rows, hd, qb, att, vb, kv))

    lower = _lower_bound(hglb_ref[...])
    n_sub = HG_CHUNK_LEN // HG_SUB
    crow = lax.broadcasted_iota(jnp.int32, (HG_CHUNK_LEN, HG_CHUNK_LEN), 0)
    ccol = lax.broadcasted_iota(jnp.int32, (HG_CHUNK_LEN, HG_CHUNK_LEN), 1)
    causal = crow >= ccol
    hg_units = []
    for c in range(MIX_TOKENS // HG_CHUNK_LEN):
        rows = slice(c * HG_CHUNK_LEN, (c + 1) * HG_CHUNK_LEN)
        f = lower + (1.0 - lower) * _sigmoid(proj[rows, _HF:_HF + HG_W])
        kk = 1.0 - f
        qq = _silu(proj[rows, _HQ:_HQ + HG_W])
        b = _cumsum_rows(jnp.log(f), HG_CHUNK_LEN)
        b_last = b[HG_CHUNK_LEN - 1:HG_CHUNK_LEN, :]
        q_inter = (qq * jnp.exp(b)).astype(BF16)
        k_upd = (kk * jnp.exp(b_last - b)).astype(BF16)
        d_last = jnp.exp(b_last)
        for hd in range(HG_HEADS):
            cols = slice(hd * LANE, (hd + 1) * LANE)
            bh = b[:, cols]
            qh = qq[:, cols]
            kh = kk[:, cols]
            vb = proj[rows, _HI + hd * LANE:_HI + (hd + 1) * LANE].astype(BF16)
            q_parts, k_parts = [], []
            for j in range(n_sub):
                lo, hi = j * HG_SUB, (j + 1) * HG_SUB
                ref = bh[lo + HG_SUB // 2 - 1:lo + HG_SUB // 2, :]
                qt = qh[lo:, :] * jnp.exp(bh[lo:, :] - ref)
                kt = kh[lo:hi, :] * jnp.exp(ref - bh[lo:hi, :])
                if lo:
                    qt = jnp.concatenate([jnp.zeros((lo, LANE), F32), qt], axis=0)
                    kt = jnp.concatenate([jnp.zeros((lo, LANE), F32), kt], axis=0)
                if hi < HG_CHUNK_LEN:
                    kt = jnp.concatenate([kt, jnp.zeros((HG_CHUNK_LEN - hi, LANE), F32)], axis=0)
                q_parts.append(qt.astype(BF16))
                k_parts.append(kt.astype(BF16))
            qcat = jnp.concatenate(q_parts, axis=1)
            kcat = jnp.concatenate(k_parts, axis=1)
            att = jnp.where(causal, _dot_nt(qcat, kcat), 0.0).astype(BF16)
            kv = _dot_tn(vb, k_upd[:, cols])
            hg_units.append((rows, hd, q_inter[:, cols], att, vb, kv, d_last[:, cols]))

    for hd in range(RET_HEADS):
        cols = slice(hd * LANE, (hd + 1) * LANE)
        s = sret_ref[0, hd]
        for rows, uh, qb, att, vb, kv in ret_units:
            if uh != hd:
                continue
            o = _dot(att, vb) + qdec_ref[hd] * _dot(qb, s.astype(BF16))
            s = sdec_ref[hd] * s + kv
            gate = proj[rows, _RG + hd * LANE:_RG + (hd + 1) * LANE]
            _gate_store(o, rgain_ref[:, cols], gate, omix_scr, rows, cols)
        sret_ref[0, hd] = s

    for hd in range(HG_HEADS):
        cols = slice(hd * LANE, (hd + 1) * LANE)
        st = shgt_scr[hd]
        for rows, uh, qi, att, vb, kv, dl in hg_units:
            if uh != hd:
                continue
            o = _dot(att, vb) + _dot_nt(qi, st.astype(BF16))
            st = st * dl + kv
            gate = proj[rows, _HGATE + hd * LANE:_HGATE + (hd + 1) * LANE]
            _gate_store(o, hgain_ref[:, cols], gate, omix_scr, rows,
                        slice(RET_W + hd * LANE, RET_W + (hd + 1) * LANE))
        shgt_scr[hd] = st

    x1_ref[...] = x + _dot(omix_scr[...], wout_ref[...])

    @pl.when(t == pl.num_programs(1) - 1)
    def _():
        for hd in range(HG_HEADS):
            shg_ref[0, hd] = shgt_scr[hd].T


def _ret_tables(length, period):
    log_g = np.log(1.0 - 2.0 ** (-5.0 - np.arange(RET_HEADS, dtype=np.float64)))
    idx = np.arange(length)
    pos = idx % period
    rel = (idx[:, None] - idx[None, :]).astype(np.float64)
    same = (idx[:, None] // period) == (idx[None, :] // period)
    valid = (rel >= 0) & same
    dec = np.where(valid[None], np.exp(log_g[:, None, None] * np.where(valid, rel, 0.0)[None]), 0.0)
    qdec = np.exp(log_g[:, None] * (pos + 1.0))[:, :, None] * np.ones((1, 1, LANE))
    kdec = np.exp(log_g[:, None] * (period - 1.0 - pos))[:, :, None] * np.ones((1, 1, LANE))
    sdec = np.exp(log_g * period)[:, None, None] * np.ones((1, 1, LANE))
    as32 = lambda a: jnp.asarray(a, dtype=F32)
    return as32(dec), as32(qdec), as32(kdec), as32(sdec)


def _rope_tables(pos):
    half = RET_DK // 2
    inv_freq = ROPE_BASE ** (-jnp.arange(half, dtype=F32) / half)
    ang = pos[:, None] * inv_freq[None, :]
    cos, sin = jnp.cos(ang), jnp.sin(ang)
    return jnp.concatenate([cos, cos], axis=-1), jnp.concatenate([-sin, sin], axis=-1)


def _mix_prompt(x2d, batch, seq, g_mix, w_in, ret_gain, hg_gain, hg_lb, w_out):
    nt = seq // MIX_TOKENS
    cos2, sin2 = _rope_tables(jnp.arange(seq, dtype=F32))
    dec, qdec, kdec, sdec = _ret_tables(RET_CHUNK_LEN, RET_CHUNK_LEN)
    c2 = lambda b, t: (0, 0)
    c3 = lambda b, t: (0, 0, 0)
    tok = pl.BlockSpec((MIX_TOKENS, D_MODEL), lambda b, t: (b * nt + t, 0))
    state = pl.BlockSpec((1, RET_HEADS, RET_DK, RET_DV), lambda b, t: (b, 0, 0, 0))
    return pl.pallas_call(
        _mix_kernel,
        grid=(batch, nt),
        in_specs=[tok, pl.BlockSpec((1, D_MODEL), c2), pl.BlockSpec((D_MODEL, IN_W), c2),
                  pl.BlockSpec((1, RET_W), c2), pl.BlockSpec((1, HG_W), c2),
                  pl.BlockSpec(hg_lb.shape, c2), pl.BlockSpec((MIX_W, D_MODEL), c2),
                  pl.BlockSpec((MIX_TOKENS, LANE), lambda b, t: (t, 0)),
                  pl.BlockSpec((MIX_TOKENS, LANE), lambda b, t: (t, 0)),
                  pl.BlockSpec(dec.shape, c3), pl.BlockSpec(qdec.shape, c3),
                  pl.BlockSpec(kdec.shape, c3), pl.BlockSpec(sdec.shape, c3)],
        out_specs=[tok, state, state],
        out_shape=[jax.ShapeDtypeStruct(x2d.shape, F32),
                   jax.ShapeDtypeStruct((batch, RET_HEADS, RET_DK, RET_DV), F32),
                   jax.ShapeDtypeStruct((batch, HG_HEADS, HG_DK, HG_DV), F32)],
        scratch_shapes=[pltpu.VMEM((HG_HEADS, HG_DV, HG_DK), F32),
                        pltpu.VMEM((MIX_TOKENS, MIX_W), BF16)],
        compiler_params=pltpu.CompilerParams(dimension_semantics=("arbitrary", "arbitrary"),
                                             vmem_limit_bytes=VMEM_LIMIT_BYTES),
        name="mix_prompt",
    )(x2d, g_mix, w_in, ret_gain, hg_gain, hg_lb, w_out, cos2, sin2, dec, qdec, kdec, sdec)


def _query(x1, gxa, wxq):
    return _dot(_rms(x1, gxa).astype(BF16), wxq) * (XA_HD ** -0.5)


def _softmax_rows(s):
    p = jnp.exp(s - jnp.max(s, axis=-1, keepdims=True))
    return p / jnp.sum(p, axis=-1, keepdims=True)


def _ffn_final(x1, ox, wxo, gffn, wgate, wup, wdown, gfinal):
    x2 = x1 + _dot(ox, wxo)
    hn = _rms(x2, gffn).astype(BF16)
    a = (_silu(_dot(hn, wgate)) * _dot(hn, wup)).astype(BF16)
    x3 = x2 + _dot(a, wdown)
    return _rms(x3, gfinal)


def _tail_kernel(x1_ref, mk_ref, mv_ref, gxa_ref, wxq_ref, wxo_ref, gffn_ref, wgate_ref, wup_ref,
                 wdown_ref, gfinal_ref, y_ref, ox_scr):
    x1 = x1_ref[...]
    q = _query(x1, gxa_ref[...], wxq_ref[...]).astype(BF16)
    for hd in range(XA_HEADS):
        cols = slice(hd * XA_HD, (hd + 1) * XA_HD)
        p = _softmax_rows(_dot_nt(q[:, cols], mk_ref[0, :, cols]))
        ox_scr[:, cols] = _dot(p.astype(BF16), mv_ref[0, :, cols]).astype(BF16)
    y_ref[...] = _ffn_final(x1, ox_scr[...], wxo_ref[...], gffn_ref[...], wgate_ref[...],
                            wup_ref[...], wdown_ref[...], gfinal_ref[...])


def _resident(shape):
    zeros = (0,) * len(shape)
    return pl.BlockSpec(shape, lambda *_: zeros, pipeline_mode=pl.Buffered(1))


def _tail_prompt(x1, batch, seq, mkb, mvb, g_xa, w_xq, w_xo, g_ffn, w_gate, w_up, w_down, g_final):
    nt = seq // TAIL_TOKENS
    d_ff = w_gate.shape[1]
    tok = pl.BlockSpec((TAIL_TOKENS, D_MODEL), lambda b, t: (b * nt + t, 0))
    mem = pl.BlockSpec((1, N_MEM, D_MODEL), lambda b, t: (b, 0, 0))
    return pl.pallas_call(
        _tail_kernel,
        grid=(batch, nt),
        in_specs=[tok, mem, mem, _resident((1, D_MODEL)), _resident((D_MODEL, D_MODEL)),
                  _resident((D_MODEL, D_MODEL)), _resident((1, D_MODEL)),
                  _resident((D_MODEL, d_ff)), _resident((D_MODEL, d_ff)),
                  _resident((d_ff, D_MODEL)), _resident((1, D_MODEL))],
        out_specs=tok,
        out_shape=jax.ShapeDtypeStruct(x1.shape, F32),
        scratch_shapes=[pltpu.VMEM((TAIL_TOKENS, D_MODEL), BF16)],
        compiler_params=pltpu.CompilerParams(dimension_semantics=("arbitrary", "arbitrary"),
                                             vmem_limit_bytes=VMEM_LIMIT_BYTES),
        name="tail_prompt",
    )(x1, mkb, mvb, g_xa, w_xq, w_xo, g_ffn, w_gate, w_up, w_down, g_final)


def _proj_kernel(x_ref, g_ref, w_ref, o_ref):
    o_ref[...] = _dot(_rms(x_ref[...], g_ref[...]).astype(BF16), w_ref[...])


def _proj_sample(x2d, g_mix, w_in):
    n = x2d.shape[0]
    nb = IN_W // D_MODEL
    return pl.pallas_call(
        _proj_kernel,
        grid=(nb,),
        in_specs=[pl.BlockSpec((n, D_MODEL), lambda j: (0, 0)),
                  pl.BlockSpec((1, D_MODEL), lambda j: (0, 0)),
                  pl.BlockSpec((D_MODEL, D_MODEL), lambda j: (0, j))],
        out_specs=pl.BlockSpec((n, D_MODEL), lambda j: (0, j)),
        out_shape=jax.ShapeDtypeStruct((n, IN_W), F32),
        compiler_params=pltpu.CompilerParams(dimension_semantics=("arbitrary",),
                                             vmem_limit_bytes=VMEM_LIMIT_BYTES),
        name="proj_sample",
    )(x2d, g_mix, w_in)


def _state_kernel(dec_len, proj_ref, sret_ref, shg_ref, rgain_ref, hgain_ref, hglb_ref,
                  cos_ref, sin_ref, dec_ref, qdec_ref, kdec_ref, sdec_ref,
                  omix_ref, nret_ref, nhg_ref):
    n_rows = STATE_SEQS * dec_len
    pair_rows = 8
    seqs_per_pair = pair_rows // dec_len
    row8 = lax.broadcasted_iota(jnp.int32, (pair_rows, LANE), 0)
    rown = lax.broadcasted_iota(jnp.int32, (n_rows, n_rows), 0)
    coln = lax.broadcasted_iota(jnp.int32, (n_rows, n_rows), 1)
    same_seq_causal = (rown >= coln) & ((rown // dec_len) == (coln // dec_len))
    proj = proj_ref[...]
    cos2 = cos_ref[...]
    sin2 = sin_ref[...]

    def per_sequence(q_all, kd_all, vb_all, state_ref, new_ref, hd, scale_of):
        outs = []
        for p in range(n_rows // pair_rows):
            prow = slice(p * pair_rows, (p + 1) * pair_rows)
            q8, kd8, v8 = q_all[prow], kd_all[prow], vb_all[prow]
            acc = jnp.zeros((pair_rows, LANE), F32)
            for j in range(seqs_per_pair):
                seq = p * seqs_per_pair + j
                mine = (row8 >= j * dec_len) & (row8 < (j + 1) * dec_len)
                s0 = state_ref[seq, hd]
                acc = jnp.where(mine, _dot(q8, s0.astype(BF16)), acc)
                kz = jnp.where(mine, kd8, jnp.zeros_like(kd8))
                new_ref[seq, hd] = scale_of(seq) * s0 + _dot_tn(kz, v8)
            outs.append(acc)
        return jnp.concatenate(outs, axis=0)

    for hd in range(RET_HEADS):
        cols = slice(hd * LANE, (hd + 1) * LANE)
        q = _rope(proj[:, _RQ + hd * LANE:_RQ + (hd + 1) * LANE], cos2, sin2)
        k = _rope(proj[:, _RK + hd * LANE:_RK + (hd + 1) * LANE], cos2, sin2) * (RET_DK ** -0.5)
        vb = proj[:, _RV + hd * LANE:_RV + (hd + 1) * LANE].astype(BF16)
        gate = proj[:, _RG + hd * LANE:_RG + (hd + 1) * LANE]
        qb = q.astype(BF16)
        att = _dot_nt(qb, k.astype(BF16)) * dec_ref[hd]
        kd = (k * kdec_ref[hd]).astype(BF16)
        sdec = sdec_ref[hd]
        qs = per_sequence(qb, kd, vb, sret_ref, nret_ref, hd, lambda seq: sdec)
        o = _dot(att.astype(BF16), vb) + qdec_ref[hd] * qs
        _gate_store(o, rgain_ref[:, cols], gate, omix_ref, slice(None), cols)

    lower = _lower_bound(hglb_ref[...])
    f = lower + (1.0 - lower) * _sigmoid(proj[:, _HF:_HF + HG_W])
    kk = 1.0 - f
    qq = _silu(proj[:, _HQ:_HQ + HG_W])
    b = _cumsum_rows(jnp.log(f), dec_len)
    pos = lax.broadcasted_iota(jnp.int32, b.shape, 0) & (dec_len - 1)

    def spread(row_in_seq):
        picked = jnp.where(pos == row_in_seq, b, 0.0)
        out = picked
        for s in range(1, dec_len):
            out = out + jnp.where(pos == (row_in_seq + s), pltpu.roll(picked, s, 0), 0.0)
            out = out + jnp.where(pos == (row_in_seq - s), pltpu.roll(picked, n_rows - s, 0), 0.0)
        return out

    ref = spread(dec_len // 2 - 1)
    b_last = spread(dec_len - 1)
    q_intra = (qq * jnp.exp(b - ref)).astype(BF16)
    k_intra = (kk * jnp.exp(ref - b)).astype(BF16)
    q_inter = (qq * jnp.exp(b)).astype(BF16)
    k_upd = (kk * jnp.exp(b_last - b)).astype(BF16)
    d_last = jnp.exp(b_last)
    for hd in range(HG_HEADS):
        cols = slice(hd * LANE, (hd + 1) * LANE)
        vb = proj[:, _HI + hd * LANE:_HI + (hd + 1) * LANE].astype(BF16)
        gate = proj[:, _HGATE + hd * LANE:_HGATE + (hd + 1) * LANE]
        att = jnp.where(same_seq_causal, _dot_nt(q_intra[:, cols], k_intra[:, cols]), 0.0)
        dpad = jnp.concatenate([d_last[:, cols], jnp.zeros((LANE - n_rows, LANE), F32)], axis=0)
        dcol = dpad.T
        scale_of = lambda seq: dcol[:, seq * dec_len:seq * dec_len + 1]
        qs = per_sequence(q_inter[:, cols], k_upd[:, cols], vb, shg_ref, nhg_ref, hd, scale_of)
        o = _dot(att.astype(BF16), vb) + qs
        _gate_store(o, hgain_ref[:, cols], gate, omix_ref, slice(None),
                    slice(RET_W + hd * LANE, RET_W + (hd + 1) * LANE))


def _state_sample(proj, state_ret, state_hgrn, dec_len, ret_gain, hg_gain, hg_lb):
    n_seq = state_ret.shape[0]
    n_rows = STATE_SEQS * dec_len
    pos = jnp.tile(jnp.arange(dec_len, dtype=F32) + PAST_LEN, STATE_SEQS)
    cos2, sin2 = _rope_tables(pos)
    dec, qdec, kdec, sdec = _ret_tables(n_rows, dec_len)
    c2 = lambda i: (0, 0)
    c3 = lambda i: (0, 0, 0)
    state = pl.BlockSpec((STATE_SEQS, RET_HEADS, RET_DK, RET_DV), lambda i: (i, 0, 0, 0))
    return pl.pallas_call(
        functools.partial(_state_kernel, dec_len),
        grid=(n_seq // STATE_SEQS,),
        in_specs=[pl.BlockSpec((n_rows, IN_W), lambda i: (i, 0)), state, state,
                  pl.BlockSpec((1, RET_W), c2), pl.BlockSpec((1, HG_W), c2),
                  pl.BlockSpec(hg_lb.shape, c2),
                  pl.BlockSpec((n_rows, LANE), c2), pl.BlockSpec((n_rows, LANE), c2),
                  pl.BlockSpec(dec.shape, c3), pl.BlockSpec(qdec.shape, c3),
                  pl.BlockSpec(kdec.shape, c3), pl.BlockSpec(sdec.shape, c3)],
        out_specs=[pl.BlockSpec((n_rows, MIX_W), lambda i: (i, 0)), state, state],
        out_shape=[jax.ShapeDtypeStruct((proj.shape[0], MIX_W), BF16),
                   jax.ShapeDtypeStruct(state_ret.shape, F32),
                   jax.ShapeDtypeStruct(state_hgrn.shape, F32)],
        compiler_params=pltpu.CompilerParams(dimension_semantics=("arbitrary",),
                                             vmem_limit_bytes=VMEM_LIMIT_BYTES),
        name="state_sample",
    )(proj, state_ret, state_hgrn, ret_gain, hg_gain, hg_lb, cos2, sin2, dec, qdec, kdec, sdec)


def _outq_kernel(x_ref, omix_ref, wout_ref, gxa_ref, wxq_ref, x1_ref, q_ref):
    x1 = x_ref[...] + _dot(omix_ref[...], wout_ref[...])
    x1_ref[...] = x1
    q_ref[...] = _query(x1, gxa_ref[...], wxq_ref[...])


def _outq_sample(x2d, omix, w_out, g_xa, w_xq):
    n = x2d.shape[0]
    z = lambda i: (0, 0)
    return pl.pallas_call(
        _outq_kernel,
        grid=(1,),
        in_specs=[pl.BlockSpec((n, D_MODEL), z), pl.BlockSpec((n, MIX_W), z),
                  pl.BlockSpec((MIX_W, D_MODEL), z), pl.BlockSpec((1, D_MODEL), z),
                  pl.BlockSpec((D_MODEL, D_MODEL), z)],
        out_specs=[pl.BlockSpec((n, D_MODEL), z), pl.BlockSpec((n, D_MODEL), z)],
        out_shape=[jax.ShapeDtypeStruct((n, D_MODEL), F32), jax.ShapeDtypeStruct((n, D_MODEL), F32)],
        compiler_params=pltpu.CompilerParams(dimension_semantics=("arbitrary",),
                                             vmem_limit_bytes=VMEM_LIMIT_BYTES),
        name="outq_sample",
    )(x2d, omix, w_out, g_xa, w_xq)


def _xattn_kernel(dec_len, q_ref, k_ref, v_ref, o_ref):
    pair_rows = 8
    seqs_per_pair = pair_rows // dec_len
    n_kv = N_MEM * XA_HEADS
    q_rows = XA_HEADS * pair_rows
    own_head = (lax.broadcasted_iota(jnp.int32, (q_rows, n_kv), 1) % XA_HEADS
                == lax.broadcasted_iota(jnp.int32, (q_rows, n_kv), 0) // pair_rows)
    row8 = lax.broadcasted_iota(jnp.int32, (q_rows, XA_HD), 0) % pair_rows
    for p in range(XATTN_SEQS // seqs_per_pair):
        prow = slice(p * pair_rows, (p + 1) * pair_rows)
        q8 = q_ref[prow, :]
        qs = jnp.concatenate([q8[:, hd * XA_HD:(hd + 1) * XA_HD] for hd in range(XA_HEADS)],
                             axis=0).astype(BF16)
        o = jnp.zeros((q_rows, XA_HD), F32)
        for j in range(seqs_per_pair):
            seq = p * seqs_per_pair + j
            kb = k_ref[seq].reshape(n_kv, XA_HD).astype(BF16)
            vb = v_ref[seq].reshape(n_kv, XA_HD).astype(BF16)
            s = jnp.where(own_head, _dot_nt(qs, kb), -jnp.inf)
            oj = _dot(_softmax_rows(s).astype(BF16), vb)
            mine = (row8 >= j * dec_len) & (row8 < (j + 1) * dec_len)
            o = jnp.where(mine, oj, o)
        for hd in range(XA_HEADS):
            o_ref[prow, hd * XA_HD:(hd + 1) * XA_HD] = o[hd * pair_rows:(hd + 1) * pair_rows, :]


def _xattn_sample(q, cache_k, cache_v, dec_len):
    n_seq = cache_k.shape[0]
    rows = XATTN_SEQS * dec_len
    qblk = pl.BlockSpec((rows, D_MODEL), lambda i: (i, 0))
    cblk = pl.BlockSpec((XATTN_SEQS, N_MEM, XA_HEADS, XA_HD), lambda i: (i, 0, 0, 0))
    return pl.pallas_call(
        functools.partial(_xattn_kernel, dec_len),
        grid=(n_seq // XATTN_SEQS,),
        in_specs=[qblk, cblk, cblk],
        out_specs=qblk,
        out_shape=jax.ShapeDtypeStruct(q.shape, F32),
        compiler_params=pltpu.CompilerParams(dimension_semantics=("arbitrary",),
                                             vmem_limit_bytes=VMEM_LIMIT_BYTES),
        name="xattn_sample",
    )(q, cache_k, cache_v)


def _post_kernel(x1_ref, ox_ref, wxo_ref, gffn_ref, wgate_ref, wup_ref, wdown_ref, gfinal_ref, y_ref):
    y_ref[...] = _ffn_final(x1_ref[...], ox_ref[...].astype(BF16), wxo_ref[...], gffn_ref[...],
                            wgate_ref[...], wup_ref[...], wdown_ref[...], gfinal_ref[...])


def _post_sample(x1, ox, w_xo, g_ffn, w_gate, w_up, w_down, g_final):
    n = x1.shape[0]
    d_ff = w_gate.shape[1]
    return pl.pallas_call(
        _post_kernel,
        grid=(1,),
        in_specs=[_resident((n, D_MODEL)), _resident((n, D_MODEL)), _resident((D_MODEL, D_MODEL)),
                  _resident((1, D_MODEL)), _resident((D_MODEL, d_ff)), _resident((D_MODEL, d_ff)),
                  _resident((d_ff, D_MODEL)), _resident((1, D_MODEL))],
        out_specs=pl.BlockSpec((n, D_MODEL), lambda i: (0, 0)),
        out_shape=jax.ShapeDtypeStruct((n, D_MODEL), F32),
        compiler_params=pltpu.CompilerParams(dimension_semantics=("arbitrary",),
                                             vmem_limit_bytes=VMEM_LIMIT_BYTES),
        name="post_sample",
    )(x1, ox, w_xo, g_ffn, w_gate, w_up, w_down, g_final)


def kernel(x_prompt, x_sample, mem_prompt, state_ret, state_hgrn, cache_mem_k, cache_mem_v, g_mix, w_in,
           ret_gain, hg_gain, hg_lb, w_out, g_xa, g_mem, w_xq, w_xk, w_xv, w_xo, g_ffn, w_gate, w_up,
           w_down, g_final):
    depth = w_in.shape[0]
    assert depth == 1, "single-layer step"
    batch, seq, d = x_prompt.shape
    dec_batch, dec_len, _ = x_sample.shape
    assert d == D_MODEL and seq % MIX_TOKENS == 0 and seq % TAIL_TOKENS == 0
    assert 8 % dec_len == 0 and dec_batch % STATE_SEQS == 0 and dec_batch % XATTN_SEQS == 0

    bf = lambda w: w[0].astype(BF16)
    w_in_b, w_out_b, w_xq_b, w_xk_b, w_xv_b, w_xo_b = map(bf, (w_in, w_out, w_xq, w_xk, w_xv, w_xo))
    w_gate_b, w_up_b, w_down_b = map(bf, (w_gate, w_up, w_down))
    g_final2 = g_final.reshape(1, D_MODEL)

    mk, mv, mkb, mvb = _memkv(mem_prompt.reshape(batch * N_MEM, D_MODEL), g_mem, w_xk_b, w_xv_b)
    x1_p, ret_p, hg_p = _mix_prompt(x_prompt.reshape(batch * seq, D_MODEL), batch, seq, g_mix, w_in_b,
                                    ret_gain, hg_gain, hg_lb, w_out_b)
    y_p = _tail_prompt(x1_p, batch, seq, mkb.reshape(batch, N_MEM, D_MODEL),
                       mvb.reshape(batch, N_MEM, D_MODEL), g_xa, w_xq_b, w_xo_b, g_ffn,
                       w_gate_b, w_up_b, w_down_b, g_final2)

    xs = x_sample.reshape(dec_batch * dec_len, D_MODEL)
    proj_s = _proj_sample(xs, g_mix, w_in_b)
    omix_s, ret_s, hg_s = _state_sample(proj_s, state_ret[0], state_hgrn[0], dec_len,
                                        ret_gain, hg_gain, hg_lb)
    x1_s, q_s = _outq_sample(xs, omix_s, w_out_b, g_xa, w_xq_b)
    ox_s = _xattn_sample(q_s, cache_mem_k[0], cache_mem_v[0], dec_len)
    y_s = _post_sample(x1_s, ox_s, w_xo_b, g_ffn, w_gate_b, w_up_b, w_down_b, g_final2)

    kv_shape = (depth, batch, N_MEM, XA_HEADS, XA_HD)
    return (y_p.reshape(batch, seq, D_MODEL), y_s.reshape(dec_batch, dec_len, D_MODEL),
            ret_p[None], hg_p[None], mk.reshape(kv_shape), mv.reshape(kv_shape),
            ret_s[None], hg_s[None])
```

```python
import functools
import math

import jax
import jax.numpy as jnp
import numpy as np
from jax import lax
from jax.experimental import pallas as pl
from jax.experimental.pallas import tpu as pltpu

D_MODEL = 1024
RET_HEADS = 4
RET_DK = 128
RET_DV = 128
RET_W = RET_HEADS * RET_DV
HG_HEADS = 4
HG_DK = 128
HG_DV = 128
HG_W = HG_HEADS * HG_DV
MIX_W = RET_W + HG_W
IN_W = 2 * RET_HEADS * RET_DK + 2 * RET_W + 2 * HG_HEADS * HG_DK + 2 * HG_W
N_MEM = 256
XA_HEADS = 4
XA_HD = D_MODEL // XA_HEADS
PAST_LEN = 16384
ROPE_BASE = 10000.0
EPS = 1e-6

_RQ, _RK, _RV, _RG = 0, 512, 1024, 1536
_HQ, _HF, _HI, _HGATE = 2048, 2560, 3072, 3584

LANE = 128
MIX_TOKENS = 512
RET_CHUNK_LEN = 128
HG_CHUNK_LEN = 64
HG_SUB = 16
TAIL_TOKENS = 512
MEMKV_ROWS = 512
STATE_SEQS = 8
XATTN_SEQS = 4
VMEM_LIMIT_BYTES = 56 * 1024 * 1024

F32 = jnp.float32
BF16 = jnp.bfloat16


def _dot(a, b):
    return jnp.dot(a, b, preferred_element_type=F32)


def _dot_nt(a, b):
    return lax.dot_general(a, b, (((1,), (1,)), ((), ())), preferred_element_type=F32)


def _dot_tn(a, b):
    return lax.dot_general(a, b, (((0,), (0,)), ((), ())), preferred_element_type=F32)


def _rms(x, g):
    ms = jnp.mean(x * x, axis=-1, keepdims=True)
    return x * lax.rsqrt(ms + EPS) * g


def _sigmoid(x):
    return 1.0 / (1.0 + jnp.exp(-x))


def _silu(x):
    return x * _sigmoid(x)


def _head_norm(o):
    return o * lax.rsqrt(jnp.mean(o * o, axis=-1, keepdims=True) + EPS)


def _rope(x, cos2, sin2):
    return x * cos2 + pltpu.roll(x, x.shape[-1] // 2, 1) * sin2


def _lower_bound(hglb):
    m = jnp.max(hglb, axis=0, keepdims=True)
    e = jnp.exp(hglb - m)
    return e[0:1, :] / jnp.sum(e, axis=0, keepdims=True)


def _cumsum_rows(x, period):
    row = lax.broadcasted_iota(jnp.int32, x.shape, 0) & (period - 1)
    s = 1
    while s < period:
        x = x + jnp.where(row >= s, pltpu.roll(x, s, 0), 0.0)
        s *= 2
    return x


def _memkv_kernel(mem_ref, g_ref, wk_ref, wv_ref, k_ref, v_ref, kb_ref, vb_ref):
    m = _rms(mem_ref[...], g_ref[...]).astype(BF16)
    k = _dot(m, wk_ref[...])
    v = _dot(m, wv_ref[...])
    k_ref[...] = k
    v_ref[...] = v
    kb_ref[...] = k.astype(BF16)
    vb_ref[...] = v.astype(BF16)


def _memkv(mem2d, g_mem, w_xk, w_xv):
    n = mem2d.shape[0]
    full = lambda i: (0, 0)
    row = lambda i: (i, 0)
    blk = pl.BlockSpec((MEMKV_ROWS, D_MODEL), row)
    return pl.pallas_call(
        _memkv_kernel,
        grid=(n // MEMKV_ROWS,),
        in_specs=[blk, pl.BlockSpec((1, D_MODEL), full),
                  pl.BlockSpec((D_MODEL, D_MODEL), full), pl.BlockSpec((D_MODEL, D_MODEL), full)],
        out_specs=[blk, blk, blk, blk],
        out_shape=[jax.ShapeDtypeStruct((n, D_MODEL), F32), jax.ShapeDtypeStruct((n, D_MODEL), F32),
                   jax.ShapeDtypeStruct((n, D_MODEL), BF16), jax.ShapeDtypeStruct((n, D_MODEL), BF16)],
        compiler_params=pltpu.CompilerParams(dimension_semantics=("arbitrary",),
                                             vmem_limit_bytes=VMEM_LIMIT_BYTES),
        name="memkv",
    )(mem2d, g_mem, w_xk, w_xv)


def _gate_store(o, gain, gate, out_ref, rows, cols):
    out_ref[rows, cols] = (_head_norm(o) * gain * _silu(gate)).astype(BF16)


def _mix_kernel(x_ref, gmix_ref, win_ref, rgain_ref, hgain_ref, hglb_ref, wout_ref,
                cos_ref, sin_ref, dec_ref, qdec_ref, kdec_ref, sdec_ref,
                x1_ref, sret_ref, shg_ref, shgt_scr, omix_scr):
    t = pl.program_id(1)

    @pl.when(t == 0)
    def _():
        sret_ref[...] = jnp.zeros_like(sret_ref)
        shgt_scr[...] = jnp.zeros_like(shgt_scr)

    x = x_ref[...]
    h = _rms(x, gmix_ref[...]).astype(BF16)
    groups = {}
    for g0 in (_HF, _HQ, _RQ, _RK, _RV, _HI, _RG, _HGATE):
        groups[g0] = _dot(h, win_ref[:, g0:g0 + RET_W])

    class _Proj:
        def __getitem__(self, idx):
            rows, cols = idx
            g0 = (cols.start // RET_W) * RET_W
            return groups[g0][rows, cols.start - g0:cols.stop - g0]

    proj = _Proj()

    ret_units = []
    for c in range(MIX_TOKENS // RET_CHUNK_LEN):
        rows = slice(c * RET_CHUNK_LEN, (c + 1) * RET_CHUNK_LEN)
        cos2 = cos_ref[rows, :]
        sin2 = sin_ref[rows, :]
        for hd in range(RET_HEADS):
            q = _rope(proj[rows, _RQ + hd * LANE:_RQ + (hd + 1) * LANE], cos2, sin2)
            k = _rope(proj[rows, _RK + hd * LANE:_RK + (hd + 1) * LANE], cos2, sin2) * (RET_DK ** -0.5)
            vb = proj[rows, _RV + hd * LANE:_RV + (hd + 1) * LANE].astype(BF16)
            qb = q.astype(BF16)
            att = (_dot_nt(qb, k.astype(BF16)) * dec_ref[hd]).astype(BF16)
            kv = _dot_tn((k * kdec_ref[hd]).astype(BF16), vb)
            ret_units.append((rows, hd, qb, att, vb, kv))

    lower = _lower_bound(hglb_ref[...])
    n_sub = HG_CHUNK_LEN // HG_SUB
    crow = lax.broadcasted_iota(jnp.int32, (HG_CHUNK_LEN, HG_CHUNK_LEN), 0)
    ccol = lax.broadcasted_iota(jnp.int32, (HG_CHUNK_LEN, HG_CHUNK_LEN), 1)
    causal = crow >= ccol
    hg_units = []
    for c in range(MIX_TOKENS // HG_CHUNK_LEN):
        rows = slice(c * HG_CHUNK_LEN, (c + 1) * HG_CHUNK_LEN)
        f = lower + (1.0 - lower) * _sigmoid(proj[rows, _HF:_HF + HG_W])
        kk = 1.0 - f
        qq = _silu(proj[rows, _HQ:_HQ + HG_W])
        b = _cumsum_rows(jnp.log(f), HG_CHUNK_LEN)
        b_last = b[HG_CHUNK_LEN - 1:HG_CHUNK_LEN, :]
        q_inter = (qq * jnp.exp(b)).astype(BF16)
        k_upd = (kk * jnp.exp(b_last - b)).astype(BF16)
        d_last = jnp.exp(b_last)
        for hd in range(HG_HEADS):
            cols = slice(hd * LANE, (hd + 1) * LANE)
            bh = b[:, cols]
            qh = qq[:, cols]
            kh = kk[:, cols]
            vb = proj[rows, _HI + hd * LANE:_HI + (hd + 1) * LANE].astype(BF16)
            q_parts, k_parts = [], []
            for j in range(n_sub):
                lo, hi = j * HG_SUB, (j + 1) * HG_SUB
                ref = bh[lo + HG_SUB // 2 - 1:lo + HG_SUB // 2, :]
                qt = qh[lo:, :] * jnp.exp(bh[lo:, :] - ref)
                kt = kh[lo:hi, :] * jnp.exp(ref - bh[lo:hi, :])
                if lo:
                    qt = jnp.concatenate([jnp.zeros((lo, LANE), F32), qt], axis=0)
                    kt = jnp.concatenate([jnp.zeros((lo, LANE), F32), kt], axis=0)
                if hi < HG_CHUNK_LEN:
                    kt = jnp.concatenate([kt, jnp.zeros((HG_CHUNK_LEN - hi, LANE), F32)], axis=0)
                q_parts.append(qt.astype(BF16))
                k_parts.append(kt.astype(BF16))
            qcat = jnp.concatenate(q_parts, axis=1)
            kcat = jnp.concatenate(k_parts, axis=1)
            att = jnp.where(causal, _dot_nt(qcat, kcat), 0.0).astype(BF16)
            kv = _dot_tn(vb, k_upd[:, cols])
            hg_units.append((rows, hd, q_inter[:, cols], att, vb, kv, d_last[:, cols]))

    for hd in range(RET_HEADS):
        cols = slice(hd * LANE, (hd + 1) * LANE)
        s = sret_ref[0, hd]
        for rows, uh, qb, att, vb, kv in ret_units:
            if uh != hd:
                continue
            o = _dot(att, vb) + qdec_ref[hd] * _dot(qb, s.astype(BF16))
            s = sdec_ref[hd] * s + kv
            gate = proj[rows, _RG + hd * LANE:_RG + (hd + 1) * LANE]
            _gate_store(o, rgain_ref[:, cols], gate, omix_scr, rows, cols)
        sret_ref[0, hd] = s

    for hd in range(HG_HEADS):
        cols = slice(hd * LANE, (hd + 1) * LANE)
        st = shgt_scr[hd]
        for rows, uh, qi, att, vb, kv, dl in hg_units:
            if uh != hd:
                continue
            o = _dot(att, vb) + _dot_nt(qi, st.astype(BF16))
            st = st * dl + kv
            gate = proj[rows, _HGATE + hd * LANE:_HGATE + (hd + 1) * LANE]
            _gate_store(o, hgain_ref[:, cols], gate, omix_scr, rows,
                        slice(RET_W + hd * LANE, RET_W + (hd + 1) * LANE))
        shgt_scr[hd] = st

    x1_ref[...] = x + _dot(omix_scr[...], wout_ref[...])

    @pl.when(t == pl.num_programs(1) - 1)
    def _():
        for hd in range(HG_HEADS):
            shg_ref[0, hd] = shgt_scr[hd].T


def _ret_tables(length, period):
    log_g = np.log(1.0 - 2.0 ** (-5.0 - np.arange(RET_HEADS, dtype=np.float64)))
    idx = np.arange(length)
    pos = idx % period
    rel = (idx[:, None] - idx[None, :]).astype(np.float64)
    same = (idx[:, None] // period) == (idx[None, :] // period)
    valid = (rel >= 0) & same
    dec = np.where(valid[None], np.exp(log_g[:, None, None] * np.where(valid, rel, 0.0)[None]), 0.0)
    qdec = np.exp(log_g[:, None] * (pos + 1.0))[:, :, None] * np.ones((1, 1, LANE))
    kdec = np.exp(log_g[:, None] * (period - 1.0 - pos))[:, :, None] * np.ones((1, 1, LANE))
    sdec = np.exp(log_g * period)[:, None, None] * np.ones((1, 1, LANE))
    as32 = lambda a: jnp.asarray(a, dtype=F32)
    return as32(dec), as32(qdec), as32(kdec), as32(sdec)


def _rope_tables(pos):
    half = RET_DK // 2
    inv_freq = ROPE_BASE ** (-jnp.arange(half, dtype=F32) / half)
    ang = pos[:, None] * inv_freq[None, :]
    cos, sin = jnp.cos(ang), jnp.sin(ang)
    return jnp.concatenate([cos, cos], axis=-1), jnp.concatenate([-sin, sin], axis=-1)


def _mix_prompt(x2d, batch, seq, g_mix, w_in, ret_gain, hg_gain, hg_lb, w_out):
    nt = seq // MIX_TOKENS
    cos2, sin2 = _rope_tables(jnp.arange(seq, dtype=F32))
    dec, qdec, kdec, sdec = _ret_tables(RET_CHUNK_LEN, RET_CHUNK_LEN)
    c2 = lambda b, t: (0, 0)
    c3 = lambda b, t: (0, 0, 0)
    tok = pl.BlockSpec((MIX_TOKENS, D_MODEL), lambda b, t: (b * nt + t, 0))
    state = pl.BlockSpec((1, RET_HEADS, RET_DK, RET_DV), lambda b, t: (b, 0, 0, 0))
    return pl.pallas_call(
        _mix_kernel,
        grid=(batch, nt),
        in_specs=[tok, pl.BlockSpec((1, D_MODEL), c2), pl.BlockSpec((D_MODEL, IN_W), c2),
                  pl.BlockSpec((1, RET_W), c2), pl.BlockSpec((1, HG_W), c2),
                  pl.BlockSpec(hg_lb.shape, c2), pl.BlockSpec((MIX_W, D_MODEL), c2),
                  pl.BlockSpec((MIX_TOKENS, LANE), lambda b, t: (t, 0)),
                  pl.BlockSpec((MIX_TOKENS, LANE), lambda b, t: (t, 0)),
                  pl.BlockSpec(dec.shape, c3), pl.BlockSpec(qdec.shape, c3),
                  pl.BlockSpec(kdec.shape, c3), pl.BlockSpec(sdec.shape, c3)],
        out_specs=[tok, state, state],
        out_shape=[jax.ShapeDtypeStruct(x2d.shape, F32),
                   jax.ShapeDtypeStruct((batch, RET_HEADS, RET_DK, RET_DV), F32),
                   jax.ShapeDtypeStruct((batch, HG_HEADS, HG_DK, HG_DV), F32)],
        scratch_shapes=[pltpu.VMEM((HG_HEADS, HG_DV, HG_DK), F32),
                        pltpu.VMEM((MIX_TOKENS, MIX_W), BF16)],
        compiler_params=pltpu.CompilerParams(dimension_semantics=("arbitrary", "arbitrary"),
                                             vmem_limit_bytes=VMEM_LIMIT_BYTES),
        name="mix_prompt",
    )(x2d, g_mix, w_in, ret_gain, hg_gain, hg_lb, w_out, cos2, sin2, dec, qdec, kdec, sdec)


def _query(x1, gxa, wxq):
    return _dot(_rms(x1, gxa).astype(BF16), wxq) * (XA_HD ** -0.5)


def _softmax_rows(s):
    p = jnp.exp(s - jnp.max(s, axis=-1, keepdims=True))
    return p / jnp.sum(p, axis=-1, keepdims=True)


def _ffn_final(x1, ox, wxo, gffn, wgate, wup, wdown, gfinal):
    x2 = x1 + _dot(ox, wxo)
    hn = _rms(x2, gffn).astype(BF16)
    a = (_silu(_dot(hn, wgate)) * _dot(hn, wup)).astype(BF16)
    x3 = x2 + _dot(a, wdown)
    return _rms(x3, gfinal)


def _tail_kernel(x1_ref, mk_ref, mv_ref, gxa_ref, wxq_ref, wxo_ref, gffn_ref, wgate_ref, wup_ref,
                 wdown_ref, gfinal_ref, y_ref, ox_scr):
    x1 = x1_ref[...]
    q = _query(x1, gxa_ref[...], wxq_ref[...]).astype(BF16)
    for hd in range(XA_HEADS):
        cols = slice(hd * XA_HD, (hd + 1) * XA_HD)
        p = _softmax_rows(_dot_nt(q[:, cols], mk_ref[0, :, cols]))
        ox_scr[:, cols] = _dot(p.astype(BF16), mv_ref[0, :, cols]).astype(BF16)
    y_ref[...] = _ffn_final(x1, ox_scr[...], wxo_ref[...], gffn_ref[...], wgate_ref[...],
                            wup_ref[...], wdown_ref[...], gfinal_ref[...])


def _resident(shape):
    zeros = (0,) * len(shape)
    return pl.BlockSpec(shape, lambda *_: zeros, pipeline_mode=pl.Buffered(1))


def _tail_prompt(x1, batch, seq, mkb, mvb, g_xa, w_xq, w_xo, g_ffn, w_gate, w_up, w_down, g_final):
    nt = seq // TAIL_TOKENS
    d_ff = w_gate.shape[1]
    tok = pl.BlockSpec((TAIL_TOKENS, D_MODEL), lambda b, t: (b * nt + t, 0))
    mem = pl.BlockSpec((1, N_MEM, D_MODEL), lambda b, t: (b, 0, 0))
    return pl.pallas_call(
        _tail_kernel,
        grid=(batch, nt),
        in_specs=[tok, mem, mem, _resident((1, D_MODEL)), _resident((D_MODEL, D_MODEL)),
                  _resident((D_MODEL, D_MODEL)), _resident((1, D_MODEL)),
                  _resident((D_MODEL, d_ff)), _resident((D_MODEL, d_ff)),
                  _resident((d_ff, D_MODEL)), _resident((1, D_MODEL))],
        out_specs=tok,
        out_shape=jax.ShapeDtypeStruct(x1.shape, F32),
        scratch_shapes=[pltpu.VMEM((TAIL_TOKENS, D_MODEL), BF16)],
        compiler_params=pltpu.CompilerParams(dimension_semantics=("arbitrary", "arbitrary"),
                                             vmem_limit_bytes=VMEM_LIMIT_BYTES),
        name="tail_prompt",
    )(x1, mkb, mvb, g_xa, w_xq, w_xo, g_ffn, w_gate, w_up, w_down, g_final)


def _proj_kernel(x_ref, g_ref, w_ref, o_ref):
    o_ref[...] = _dot(_rms(x_ref[...], g_ref[...]).astype(BF16), w_ref[...])


def _proj_sample(x2d, g_mix, w_in):
    n = x2d.shape[0]
    nb = IN_W // D_MODEL
    return pl.pallas_call(
        _proj_kernel,
        grid=(nb,),
        in_specs=[pl.BlockSpec((n, D_MODEL), lambda j: (0, 0)),
                  pl.BlockSpec((1, D_MODEL), lambda j: (0, 0)),
                  pl.BlockSpec((D_MODEL, D_MODEL), lambda j: (0, j))],
        out_specs=pl.BlockSpec((n, D_MODEL), lambda j: (0, j)),
        out_shape=jax.ShapeDtypeStruct((n, IN_W), F32),
        compiler_params=pltpu.CompilerParams(dimension_semantics=("arbitrary",),
                                             vmem_limit_bytes=VMEM_LIMIT_BYTES),
        name="proj_sample",
    )(x2d, g_mix, w_in)


def _state_kernel(dec_len, proj_ref, sret_ref, shg_ref, rgain_ref, hgain_ref, hglb_ref,
                  cos_ref, sin_ref, dec_ref, qdec_ref, kdec_ref, sdec_ref,
                  omix_ref, nret_ref, nhg_ref):
    n_rows = STATE_SEQS * dec_len
    pair_rows = 8
    seqs_per_pair = pair_rows // dec_len
    row8 = lax.broadcasted_iota(jnp.int32, (pair_rows, LANE), 0)
    rown = lax.broadcasted_iota(jnp.int32, (n_rows, n_rows), 0)
    coln = lax.broadcasted_iota(jnp.int32, (n_rows, n_rows), 1)
    same_seq_causal = (rown >= coln) & ((rown // dec_len) == (coln // dec_len))
    proj = proj_ref[...]
    cos2 = cos_ref[...]
    sin2 = sin_ref[...]

    def per_sequence(q_all, kd_all, vb_all, state_ref, new_ref, hd, scale_of):
        outs = []
        for p in range(n_rows // pair_rows):
            prow = slice(p * pair_rows, (p + 1) * pair_rows)
            q8, kd8, v8 = q_all[prow], kd_all[prow], vb_all[prow]
            acc = jnp.zeros((pair_rows, LANE), F32)
            for j in range(seqs_per_pair):
                seq = p * seqs_per_pair + j
                mine = (row8 >= j * dec_len) & (row8 < (j + 1) * dec_len)
                s0 = state_ref[seq, hd]
                acc = jnp.where(mine, _dot(q8, s0.astype(BF16)), acc)
                kz = jnp.where(mine, kd8, jnp.zeros_like(kd8))
                new_ref[seq, hd] = scale_of(seq) * s0 + _dot_tn(kz, v8)
            outs.append(acc)
        return jnp.concatenate(outs, axis=0)

    for hd in range(RET_HEADS):
        cols = slice(hd * LANE, (hd + 1) * LANE)
        q = _rope(proj[:, _RQ + hd * LANE:_RQ + (hd + 1) * LANE], cos2, sin2)
        k = _rope(proj[:, _RK + hd * LANE:_RK + (hd + 1) * LANE], cos2, sin2) * (RET_DK ** -0.5)
        vb = proj[:, _RV + hd * LANE:_RV + (hd + 1) * LANE].astype(BF16)
        gate = proj[:, _RG + hd * LANE:_RG + (hd + 1) * LANE]
        qb = q.astype(BF16)
        att = _dot_nt(qb, k.astype(BF16)) * dec_ref[hd]
        kd = (k * kdec_ref[hd]).astype(BF16)
        sdec = sdec_ref[hd]
        qs = per_sequence(qb, kd, vb, sret_ref, nret_ref, hd, lambda seq: sdec)
        o = _dot(att.astype(BF16), vb) + qdec_ref[hd] * qs
        _gate_store(o, rgain_ref[:, cols], gate, omix_ref, slice(None), cols)

    lower = _lower_bound(hglb_ref[...])
    f = lower + (1.0 - lower) * _sigmoid(proj[:, _HF:_HF + HG_W])
    kk = 1.0 - f
    qq = _silu(proj[:, _HQ:_HQ + HG_W])
    b = _cumsum_rows(jnp.log(f), dec_len)
    pos = lax.broadcasted_iota(jnp.int32, b.shape, 0) & (dec_len - 1)

    def spread(row_in_seq):
        picked = jnp.where(pos == row_in_seq, b, 0.0)
        out = picked
        for s in range(1, dec_len):
            out = out + jnp.where(pos == (row_in_seq + s), pltpu.roll(picked, s, 0), 0.0)
            out = out + jnp.where(pos == (row_in_seq - s), pltpu.roll(picked, n_rows - s, 0), 0.0)
        return out

    ref = spread(dec_len // 2 - 1)
    b_last = spread(dec_len - 1)
    q_intra = (qq * jnp.exp(b - ref)).astype(BF16)
    k_intra = (kk * jnp.exp(ref - b)).astype(BF16)
    q_inter = (qq * jnp.exp(b)).astype(BF16)
    k_upd = (kk * jnp.exp(b_last - b)).astype(BF16)
    d_last = jnp.exp(b_last)
    for hd in range(HG_HEADS):
        cols = slice(hd * LANE, (hd + 1) * LANE)
        vb = proj[:, _HI + hd * LANE:_HI + (hd + 1) * LANE].astype(BF16)
        gate = proj[:, _HGATE + hd * LANE:_HGATE + (hd + 1) * LANE]
        att = jnp.where(same_seq_causal, _dot_nt(q_intra[:, cols], k_intra[:, cols]), 0.0)
        dpad = jnp.concatenate([d_last[:, cols], jnp.zeros((LANE - n_rows, LANE), F32)], axis=0)
        dcol = dpad.T
        scale_of = lambda seq: dcol[:, seq * dec_len:seq * dec_len + 1]
        qs = per_sequence(q_inter[:, cols], k_upd[:, cols], vb, shg_ref, nhg_ref, hd, scale_of)
        o = _dot(att.astype(BF16), vb) + qs
        _gate_store(o, hgain_ref[:, cols], gate, omix_ref, slice(None),
                    slice(RET_W + hd * LANE, RET_W + (hd + 1) * LANE))


def _state_sample(proj, state_ret, state_hgrn, dec_len, ret_gain, hg_gain, hg_lb):
    n_seq = state_ret.shape[0]
    n_rows = STATE_SEQS * dec_len
    pos = jnp.tile(jnp.arange(dec_len, dtype=F32) + PAST_LEN, STATE_SEQS)
    cos2, sin2 = _rope_tables(pos)
    dec, qdec, kdec, sdec = _ret_tables(n_rows, dec_len)
    c2 = lambda i: (0, 0)
    c3 = lambda i: (0, 0, 0)
    state = pl.BlockSpec((STATE_SEQS, RET_HEADS, RET_DK, RET_DV), lambda i: (i, 0, 0, 0))
    return pl.pallas_call(
        functools.partial(_state_kernel, dec_len),
        grid=(n_seq // STATE_SEQS,),
        in_specs=[pl.BlockSpec((n_rows, IN_W), lambda i: (i, 0)), state, state,
                  pl.BlockSpec((1, RET_W), c2), pl.BlockSpec((1, HG_W), c2),
                  pl.BlockSpec(hg_lb.shape, c2),
                  pl.BlockSpec((n_rows, LANE), c2), pl.BlockSpec((n_rows, LANE), c2),
                  pl.BlockSpec(dec.shape, c3), pl.BlockSpec(qdec.shape, c3),
                  pl.BlockSpec(kdec.shape, c3), pl.BlockSpec(sdec.shape, c3)],
        out_specs=[pl.BlockSpec((n_rows, MIX_W), lambda i: (i, 0)), state, state],
        out_shape=[jax.ShapeDtypeStruct((proj.shape[0], MIX_W), BF16),
                   jax.ShapeDtypeStruct(state_ret.shape, F32),
                   jax.ShapeDtypeStruct(state_hgrn.shape, F32)],
        compiler_params=pltpu.CompilerParams(dimension_semantics=("arbitrary",),
                                             vmem_limit_bytes=VMEM_LIMIT_BYTES),
        name="state_sample",
    )(proj, state_ret, state_hgrn, ret_gain, hg_gain, hg_lb, cos2, sin2, dec, qdec, kdec, sdec)


def _outq_kernel(x_ref, omix_ref, wout_ref, gxa_ref, wxq_ref, x1_ref, q_ref):
    x1 = x_ref[...] + _dot(omix_ref[...], wout_ref[...])
    x1_ref[...] = x1
    q_ref[...] = _query(x1, gxa_ref[...], wxq_ref[...])


def _outq_sample(x2d, omix, w_out, g_xa, w_xq):
    n = x2d.shape[0]
    z = lambda i: (0, 0)
    return pl.pallas_call(
        _outq_kernel,
        grid=(1,),
        in_specs=[pl.BlockSpec((n, D_MODEL), z), pl.BlockSpec((n, MIX_W), z),
                  pl.BlockSpec((MIX_W, D_MODEL), z), pl.BlockSpec((1, D_MODEL), z),
                  pl.BlockSpec((D_MODEL, D_MODEL), z)],
        out_specs=[pl.BlockSpec((n, D_MODEL), z), pl.BlockSpec((n, D_MODEL), z)],
        out_shape=[jax.ShapeDtypeStruct((n, D_MODEL), F32), jax.ShapeDtypeStruct((n, D_MODEL), F32)],
        compiler_params=pltpu.CompilerParams(dimension_semantics=("arbitrary",),
                                             vmem_limit_bytes=VMEM_LIMIT_BYTES),
        name="outq_sample",
    )(x2d, omix, w_out, g_xa, w_xq)


def _xattn_kernel(dec_len, q_ref, k_ref, v_ref, o_ref):
    pair_rows = 8
    seqs_per_pair = pair_rows // dec_len
    n_kv = N_MEM * XA_HEADS
    q_rows = XA_HEADS * pair_rows
    own_head = (lax.broadcasted_iota(jnp.int32, (q_rows, n_kv), 1) % XA_HEADS
                == lax.broadcasted_iota(jnp.int32, (q_rows, n_kv), 0) // pair_rows)
    row8 = lax.broadcasted_iota(jnp.int32, (q_rows, XA_HD), 0) % pair_rows
    for p in range(XATTN_SEQS // seqs_per_pair):
        prow = slice(p * pair_rows, (p + 1) * pair_rows)
        q8 = q_ref[prow, :]
        qs = jnp.concatenate([q8[:, hd * XA_HD:(hd + 1) * XA_HD] for hd in range(XA_HEADS)],
                             axis=0).astype(BF16)
        o = jnp.zeros((q_rows, XA_HD), F32)
        for j in range(seqs_per_pair):
            seq = p * seqs_per_pair + j
            kb = k_ref[seq].reshape(n_kv, XA_HD).astype(BF16)
            vb = v_ref[seq].reshape(n_kv, XA_HD).astype(BF16)
            s = jnp.where(own_head, _dot_nt(qs, kb), -jnp.inf)
            oj = _dot(_softmax_rows(s).astype(BF16), vb)
            mine = (row8 >= j * dec_len) & (row8 < (j + 1) * dec_len)
            o = jnp.where(mine, oj, o)
        for hd in range(XA_HEADS):
            o_ref[prow, hd * XA_HD:(hd + 1) * XA_HD] = o[hd * pair_rows:(hd + 1) * pair_rows, :]


def _xattn_sample(q, cache_k, cache_v, dec_len):
    n_seq = cache_k.shape[0]
    rows = XATTN_SEQS * dec_len
    qblk = pl.BlockSpec((rows, D_MODEL), lambda i: (i, 0))
    cblk = pl.BlockSpec((XATTN_SEQS, N_MEM, XA_HEADS, XA_HD), lambda i: (i, 0, 0, 0))
    return pl.pallas_call(
        functools.partial(_xattn_kernel, dec_len),
        grid=(n_seq // XATTN_SEQS,),
        in_specs=[qblk, cblk, cblk],
        out_specs=qblk,
        out_shape=jax.ShapeDtypeStruct(q.shape, F32),
        compiler_params=pltpu.CompilerParams(dimension_semantics=("arbitrary",),
                                             vmem_limit_bytes=VMEM_LIMIT_BYTES),
        name="xattn_sample",
    )(q, cache_k, cache_v)


def _post_kernel(x1_ref, ox_ref, wxo_ref, gffn_ref, wgate_ref, wup_ref, wdown_ref, gfinal_ref, y_ref):
    y_ref[...] = _ffn_final(x1_ref[...], ox_ref[...].astype(BF16), wxo_ref[...], gffn_ref[...],
                            wgate_ref[...], wup_ref[...], wdown_ref[...], gfinal_ref[...])


def _post_sample(x1, ox, w_xo, g_ffn, w_gate, w_up, w_down, g_final):
    n = x1.shape[0]
    d_ff = w_gate.shape[1]
    return pl.pallas_call(
        _post_kernel,
        grid=(1,),
        in_specs=[_resident((n, D_MODEL)), _resident((n, D_MODEL)), _resident((D_MODEL, D_MODEL)),
                  _resident((1, D_MODEL)), _resident((D_MODEL, d_ff)), _resident((D_MODEL, d_ff)),
                  _resident((d_ff, D_MODEL)), _resident((1, D_MODEL))],
        out_specs=pl.BlockSpec((n, D_MODEL), lambda i: (0, 0)),
        out_shape=jax.ShapeDtypeStruct((n, D_MODEL), F32),
        compiler_params=pltpu.CompilerParams(dimension_semantics=("arbitrary",),
                                             vmem_limit_bytes=VMEM_LIMIT_BYTES),
        name="post_sample",
    )(x1, ox, w_xo, g_ffn, w_gate, w_up, w_down, g_final)


def kernel(x_prompt, x_sample, mem_prompt, state_ret, state_hgrn, cache_mem_k, cache_mem_v, g_mix, w_in,
           ret_gain, hg_gain, hg_lb, w_out, g_xa, g_mem, w_xq, w_xk, w_xv, w_xo, g_ffn, w_gate, w_up,
           w_down, g_final):
    depth = w_in.shape[0]
    assert depth == 1, "single-layer step"
    batch, seq, d = x_prompt.shape
    dec_batch, dec_len, _ = x_sample.shape
    assert d == D_MODEL and seq % MIX_TOKENS == 0 and seq % TAIL_TOKENS == 0
    assert 8 % dec_len == 0 and dec_batch % STATE_SEQS == 0 and dec_batch % XATTN_SEQS == 0

    bf = lambda w: w[0].astype(BF16)
    w_in_b, w_out_b, w_xq_b, w_xk_b, w_xv_b, w_xo_b = map(bf, (w_in, w_out, w_xq, w_xk, w_xv, w_xo))
    w_gate_b, w_up_b, w_down_b = map(bf, (w_gate, w_up, w_down))
    g_final2 = g_final.reshape(1, D_MODEL)

    mk, mv, mkb, mvb = _memkv(mem_prompt.reshape(batch * N_MEM, D_MODEL), g_mem, w_xk_b, w_xv_b)
    x1_p, ret_p, hg_p = _mix_prompt(x_prompt.reshape(batch * seq, D_MODEL), batch, seq, g_mix, w_in_b,
                                    ret_gain, hg_gain, hg_lb, w_out_b)
    y_p = _tail_prompt(x1_p, batch, seq, mkb.reshape(batch, N_MEM, D_MODEL),
                       mvb.reshape(batch, N_MEM, D_MODEL), g_xa, w_xq_b, w_xo_b, g_ffn,
                       w_gate_b, w_up_b, w_down_b, g_final2)

    xs = x_sample.reshape(dec_batch * dec_len, D_MODEL)
    proj_s = _proj_sample(xs, g_mix, w_in_b)
    omix_s, ret_s, hg_s = _state_sample(proj_s, state_ret[0], state_hgrn[0], dec_len,
                                        ret_gain, hg_gain, hg_lb)
    x1_s, q_s = _outq_sample(xs, omix_s, w_out_b, g_xa, w_xq_b)
    ox_s = _xattn_sample(q_s, cache_mem_k[0], cache_mem_v[0], dec_len)
    y_s = _post_sample(x1_s, ox_s, w_xo_b, g_ffn, w_gate_b, w_up_b, w_down_b, g_final2)

    kv_shape = (depth, batch, N_MEM, XA_HEADS, XA_HD)
    return (y_p.reshape(batch, seq, D_MODEL), y_s.reshape(dec_batch, dec_len, D_MODEL),
            ret_p[None], hg_p[None], mk.reshape(kv_shape), mv.reshape(kv_shape),
            ret_s[None], hg_s[None])
```

```python
import functools
import math

import jax
import jax.numpy as jnp
import numpy as np
from jax import lax
from jax.experimental import pallas as pl
from jax.experimental.pallas import tpu as pltpu

D_MODEL = 1024
RET_HEADS = 4
RET_DK = 128
RET_DV = 128
RET_W = RET_HEADS * RET_DV
HG_HEADS = 4
HG_DK = 128
HG_DV = 128
HG_W = HG_HEADS * HG_DV
MIX_W = RET_W + HG_W
IN_W = 2 * RET_HEADS * RET_DK + 2 * RET_W + 2 * HG_HEADS * HG_DK + 2 * HG_W
N_MEM = 256
XA_HEADS = 4
XA_HD = D_MODEL // XA_HEADS
PAST_LEN = 16384
ROPE_BASE = 10000.0
EPS = 1e-6

_RQ, _RK, _RV, _RG = 0, 512, 1024, 1536
_HQ, _HF, _HI, _HGATE = 2048, 2560, 3072, 3584

LANE = 128
MIX_TOKENS = 512
RET_CHUNK_LEN = 128
HG_CHUNK_LEN = 64
HG_SUB = 16
TAIL_TOKENS = 512
MEMKV_ROWS = 512
STATE_SEQS = 8
XATTN_SEQS = 8
VMEM_LIMIT_BYTES = 56 * 1024 * 1024

F32 = jnp.float32
BF16 = jnp.bfloat16


def _dot(a, b):
    return jnp.dot(a, b, preferred_element_type=F32)


def _dot_nt(a, b):
    return lax.dot_general(a, b, (((1,), (1,)), ((), ())), preferred_element_type=F32)


def _dot_tn(a, b):
    return lax.dot_general(a, b, (((0,), (0,)), ((), ())), preferred_element_type=F32)


def _rms(x, g):
    ms = jnp.mean(x * x, axis=-1, keepdims=True)
    return x * lax.rsqrt(ms + EPS) * g


def _sigmoid(x):
    return 1.0 / (1.0 + jnp.exp(-x))


def _silu(x):
    return x * _sigmoid(x)


def _head_norm(o):
    return o * lax.rsqrt(jnp.mean(o * o, axis=-1, keepdims=True) + EPS)


def _rope(x, cos2, sin2):
    return x * cos2 + pltpu.roll(x, x.shape[-1] // 2, 1) * sin2


def _lower_bound(hglb):
    m = jnp.max(hglb, axis=0, keepdims=True)
    e = jnp.exp(hglb - m)
    return e[0:1, :] / jnp.sum(e, axis=0, keepdims=True)


def _cumsum_rows(x, period):
    row = lax.broadcasted_iota(jnp.int32, x.shape, 0) & (period - 1)
    s = 1
    while s < period:
        x = x + jnp.where(row >= s, pltpu.roll(x, s, 0), 0.0)
        s *= 2
    return x


def _memkv_kernel(mem_ref, g_ref, wk_ref, wv_ref, k_ref, v_ref, kb_ref, vb_ref):
    m = _rms(mem_ref[...], g_ref[...]).astype(BF16)
    k = _dot(m, wk_ref[...])
    v = _dot(m, wv_ref[...])
    k_ref[...] = k
    v_ref[...] = v
    kb_ref[...] = k.astype(BF16)
    vb_ref[...] = v.astype(BF16)


def _memkv(mem2d, g_mem, w_xk, w_xv):
    n = mem2d.shape[0]
    full = lambda i: (0, 0)
    row = lambda i: (i, 0)
    blk = pl.BlockSpec((MEMKV_ROWS, D_MODEL), row)
    return pl.pallas_call(
        _memkv_kernel,
        grid=(n // MEMKV_ROWS,),
        in_specs=[blk, pl.BlockSpec((1, D_MODEL), full),
                  pl.BlockSpec((D_MODEL, D_MODEL), full), pl.BlockSpec((D_MODEL, D_MODEL), full)],
        out_specs=[blk, blk, blk, blk],
        out_shape=[jax.ShapeDtypeStruct((n, D_MODEL), F32), jax.ShapeDtypeStruct((n, D_MODEL), F32),
                   jax.ShapeDtypeStruct((n, D_MODEL), BF16), jax.ShapeDtypeStruct((n, D_MODEL), BF16)],
        compiler_params=pltpu.CompilerParams(dimension_semantics=("arbitrary",),
                                             vmem_limit_bytes=VMEM_LIMIT_BYTES),
        name="memkv",
    )(mem2d, g_mem, w_xk, w_xv)


def _gate_store(o, gain, gate, out_ref, rows, cols):
    out_ref[rows, cols] = (_head_norm(o) * gain * _silu(gate)).astype(BF16)


def _mix_kernel(x_ref, gmix_ref, win_ref, rgain_ref, hgain_ref, hglb_ref, wout_ref,
                cos_ref, sin_ref, dec_ref, qdec_ref, kdec_ref, sdec_ref,
                x1_ref, sret_ref, shg_ref, omix_scr):
    t = pl.program_id(1)

    @pl.when(t == 0)
    def _():
        sret_ref[...] = jnp.zeros_like(sret_ref)
        shg_ref[...] = jnp.zeros_like(shg_ref)

    x = x_ref[...]
    h = _rms(x, gmix_ref[...]).astype(BF16)
    groups = {}
    for g0 in (_HF, _HQ, _RQ, _RK, _RV, _HI, _RG, _HGATE):
        groups[g0] = _dot(h, win_ref[:, g0:g0 + RET_W])

    class _Proj:
        def __getitem__(self, idx):
            rows, cols = idx
            g0 = (cols.start // RET_W) * RET_W
            return groups[g0][rows, cols.start - g0:cols.stop - g0]

    proj = _Proj()

    ret_units = []
    for c in range(MIX_TOKENS // RET_CHUNK_LEN):
        rows = slice(c * RET_CHUNK_LEN, (c + 1) * RET_CHUNK_LEN)
        cos2 = cos_ref[rows, :]
        sin2 = sin_ref[rows, :]
        for hd in range(RET_HEADS):
            q = _rope(proj[rows, _RQ + hd * LANE:_RQ + (hd + 1) * LANE], cos2, sin2)
            k = _rope(proj[rows, _RK + hd * LANE:_RK + (hd + 1) * LANE], cos2, sin2) * (RET_DK ** -0.5)
            vb = proj[rows, _RV + hd * LANE:_RV + (hd + 1) * LANE].astype(BF16)
            att = (_dot_nt(q.astype(BF16), k.astype(BF16)) * dec_ref[hd]).astype(BF16)
            kv = _dot_tn((k * kdec_ref[hd]).astype(BF16), vb)
            lhs = jnp.concatenate([att, (q * qdec_ref[hd]).astype(BF16)], axis=1)
            ret_units.append((rows, hd, lhs, vb, kv))

    lower = _lower_bound(hglb_ref[...])
    n_sub = HG_CHUNK_LEN // HG_SUB
    crow = lax.broadcasted_iota(jnp.int32, (HG_CHUNK_LEN, HG_CHUNK_LEN), 0)
    ccol = lax.broadcasted_iota(jnp.int32, (HG_CHUNK_LEN, HG_CHUNK_LEN), 1)
    causal = crow >= ccol
    hg_units = []
    for c in range(MIX_TOKENS // HG_CHUNK_LEN):
        rows = slice(c * HG_CHUNK_LEN, (c + 1) * HG_CHUNK_LEN)
        f = lower + (1.0 - lower) * _sigmoid(proj[rows, _HF:_HF + HG_W])
        kk = 1.0 - f
        qq = _silu(proj[rows, _HQ:_HQ + HG_W])
        b = _cumsum_rows(jnp.log(f), HG_CHUNK_LEN)
        b_last = b[HG_CHUNK_LEN - 1:HG_CHUNK_LEN, :]
        q_inter = (qq * jnp.exp(b)).astype(BF16)
        k_upd = (kk * jnp.exp(b_last - b)).astype(BF16)
        d_last = jnp.exp(b_last)
        for hd in range(HG_HEADS):
            cols = slice(hd * LANE, (hd + 1) * LANE)
            bh = b[:, cols]
            qh = qq[:, cols]
            kh = kk[:, cols]
            vb = proj[rows, _HI + hd * LANE:_HI + (hd + 1) * LANE].astype(BF16)
            q_parts, k_parts = [], []
            for j in range(n_sub):
                lo, hi = j * HG_SUB, (j + 1) * HG_SUB
                ref = bh[lo + HG_SUB // 2 - 1:lo + HG_SUB // 2, :]
                qt = qh[lo:, :] * jnp.exp(bh[lo:, :] - ref)
                kt = kh[lo:hi, :] * jnp.exp(ref - bh[lo:hi, :])
                if lo:
                    qt = jnp.concatenate([jnp.zeros((lo, LANE), F32), qt], axis=0)
                    kt = jnp.concatenate([jnp.zeros((lo, LANE), F32), kt], axis=0)
                if hi < HG_CHUNK_LEN:
                    kt = jnp.concatenate([kt, jnp.zeros((HG_CHUNK_LEN - hi, LANE), F32)], axis=0)
                q_parts.append(qt.astype(BF16))
                k_parts.append(kt.astype(BF16))
            qcat = jnp.concatenate(q_parts, axis=1)
            kcat = jnp.concatenate(k_parts, axis=1)
            att = jnp.where(causal, _dot_nt(qcat, kcat), 0.0).astype(BF16)
            kv = _dot_tn(k_upd[:, cols], vb)
            lhs = jnp.concatenate([q_inter[:, cols], att], axis=1)
            hg_units.append((rows, hd, lhs, vb, kv, d_last[:, cols]))

    for hd in range(RET_HEADS):
        cols = slice(hd * LANE, (hd + 1) * LANE)
        s = sret_ref[0, hd]
        for rows, uh, lhs, vb, kv in ret_units:
            if uh != hd:
                continue
            o = _dot(lhs, jnp.concatenate([vb, s.astype(BF16)], axis=0))
            s = sdec_ref[hd] * s + kv
            gate = proj[rows, _RG + hd * LANE:_RG + (hd + 1) * LANE]
            _gate_store(o, rgain_ref[:, cols], gate, omix_scr, rows, cols)
        sret_ref[0, hd] = s

    for hd in range(HG_HEADS):
        cols = slice(hd * LANE, (hd + 1) * LANE)
        mine = [u for u in hg_units if u[1] == hd]
        dl = jnp.concatenate([u[5] for u in mine] + [jnp.zeros((LANE - len(mine), LANE), F32)], axis=0).T
        s = shg_ref[0, hd]
        for i, (rows, _, lhs, vb, kv, _) in enumerate(mine):
            o = _dot(lhs, jnp.concatenate([s.astype(BF16), vb], axis=0))
            s = s * dl[:, i:i + 1] + kv
            gate = proj[rows, _HGATE + hd * LANE:_HGATE + (hd + 1) * LANE]
            _gate_store(o, hgain_ref[:, cols], gate, omix_scr, rows,
                        slice(RET_W + hd * LANE, RET_W + (hd + 1) * LANE))
        shg_ref[0, hd] = s

    x1_ref[...] = x + _dot(omix_scr[...], wout_ref[...])


def _ret_tables(length, period):
    log_g = np.log(1.0 - 2.0 ** (-5.0 - np.arange(RET_HEADS, dtype=np.float64)))
    idx = np.arange(length)
    pos = idx % period
    rel = (idx[:, None] - idx[None, :]).astype(np.float64)
    same = (idx[:, None] // period) == (idx[None, :] // period)
    valid = (rel >= 0) & same
    dec = np.where(valid[None], np.exp(log_g[:, None, None] * np.where(valid, rel, 0.0)[None]), 0.0)
    qdec = np.exp(log_g[:, None] * (pos + 1.0))[:, :, None] * np.ones((1, 1, LANE))
    kdec = np.exp(log_g[:, None] * (period - 1.0 - pos))[:, :, None] * np.ones((1, 1, LANE))
    sdec = np.exp(log_g * period)[:, None, None] * np.ones((1, 1, LANE))
    as32 = lambda a: jnp.asarray(a, dtype=F32)
    return as32(dec), as32(qdec), as32(kdec), as32(sdec)


def _rope_tables(pos):
    half = RET_DK // 2
    inv_freq = ROPE_BASE ** (-jnp.arange(half, dtype=F32) / half)
    ang = pos[:, None] * inv_freq[None, :]
    cos, sin = jnp.cos(ang), jnp.sin(ang)
    return jnp.concatenate([cos, cos], axis=-1), jnp.concatenate([-sin, sin], axis=-1)


def _mix_prompt(x2d, batch, seq, g_mix, w_in, ret_gain, hg_gain, hg_lb, w_out):
    nt = seq // MIX_TOKENS
    cos2, sin2 = _rope_tables(jnp.arange(seq, dtype=F32))
    dec, qdec, kdec, sdec = _ret_tables(RET_CHUNK_LEN, RET_CHUNK_LEN)
    c2 = lambda b, t: (0, 0)
    c3 = lambda b, t: (0, 0, 0)
    tok = pl.BlockSpec((MIX_TOKENS, D_MODEL), lambda b, t: (b * nt + t, 0))
    state = pl.BlockSpec((1, RET_HEADS, RET_DK, RET_DV), lambda b, t: (b, 0, 0, 0))
    return pl.pallas_call(
        _mix_kernel,
        grid=(batch, nt),
        in_specs=[tok, pl.BlockSpec((1, D_MODEL), c2), pl.BlockSpec((D_MODEL, IN_W), c2),
                  pl.BlockSpec((1, RET_W), c2), pl.BlockSpec((1, HG_W), c2),
                  pl.BlockSpec(hg_lb.shape, c2), pl.BlockSpec((MIX_W, D_MODEL), c2),
                  pl.BlockSpec((MIX_TOKENS, LANE), lambda b, t: (t, 0)),
                  pl.BlockSpec((MIX_TOKENS, LANE), lambda b, t: (t, 0)),
                  pl.BlockSpec(dec.shape, c3), pl.BlockSpec(qdec.shape, c3),
                  pl.BlockSpec(kdec.shape, c3), pl.BlockSpec(sdec.shape, c3)],
        out_specs=[tok, state, state],
        out_shape=[jax.ShapeDtypeStruct(x2d.shape, F32),
                   jax.ShapeDtypeStruct((batch, RET_HEADS, RET_DK, RET_DV), F32),
                   jax.ShapeDtypeStruct((batch, HG_HEADS, HG_DK, HG_DV), F32)],
        scratch_shapes=[pltpu.VMEM((MIX_TOKENS, MIX_W), BF16)],
        compiler_params=pltpu.CompilerParams(dimension_semantics=("arbitrary", "arbitrary"),
                                             vmem_limit_bytes=VMEM_LIMIT_BYTES),
        name="mix_prompt",
    )(x2d, g_mix, w_in, ret_gain, hg_gain, hg_lb, w_out, cos2, sin2, dec, qdec, kdec, sdec)


def _query(x1, gxa, wxq):
    return _dot(_rms(x1, gxa).astype(BF16), wxq) * (XA_HD ** -0.5)


def _softmax_rows(s):
    p = jnp.exp(s - jnp.max(s, axis=-1, keepdims=True))
    return p / jnp.sum(p, axis=-1, keepdims=True)


def _ffn_final(x1, ox, wxo, gffn, wgate, wup, wdown, gfinal):
    x2 = x1 + _dot(ox, wxo)
    hn = _rms(x2, gffn).astype(BF16)
    a = (_silu(_dot(hn, wgate)) * _dot(hn, wup)).astype(BF16)
    x3 = x2 + _dot(a, wdown)
    return _rms(x3, gfinal)


def _tail_kernel(x1_ref, mk_ref, mv_ref, gxa_ref, wxq_ref, wxo_ref, gffn_ref, wgate_ref, wup_ref,
                 wdown_ref, gfinal_ref, y_ref, ox_scr):
    x1 = x1_ref[...]
    q = _query(x1, gxa_ref[...], wxq_ref[...]).astype(BF16)
    for hd in range(XA_HEADS):
        cols = slice(hd * XA_HD, (hd + 1) * XA_HD)
        p = _softmax_rows(_dot_nt(q[:, cols], mk_ref[0, :, cols]))
        ox_scr[:, cols] = _dot(p.astype(BF16), mv_ref[0, :, cols]).astype(BF16)
    y_ref[...] = _ffn_final(x1, ox_scr[...], wxo_ref[...], gffn_ref[...], wgate_ref[...],
                            wup_ref[...], wdown_ref[...], gfinal_ref[...])


def _resident(shape):
    zeros = (0,) * len(shape)
    return pl.BlockSpec(shape, lambda *_: zeros, pipeline_mode=pl.Buffered(1))


def _tail_prompt(x1, batch, seq, mkb, mvb, g_xa, w_xq, w_xo, g_ffn, w_gate, w_up, w_down, g_final):
    nt = seq // TAIL_TOKENS
    d_ff = w_gate.shape[1]
    tok = pl.BlockSpec((TAIL_TOKENS, D_MODEL), lambda b, t: (b * nt + t, 0))
    mem = pl.BlockSpec((1, N_MEM, D_MODEL), lambda b, t: (b, 0, 0))
    return pl.pallas_call(
        _tail_kernel,
        grid=(batch, nt),
        in_specs=[tok, mem, mem, _resident((1, D_MODEL)), _resident((D_MODEL, D_MODEL)),
                  _resident((D_MODEL, D_MODEL)), _resident((1, D_MODEL)),
                  _resident((D_MODEL, d_ff)), _resident((D_MODEL, d_ff)),
                  _resident((d_ff, D_MODEL)), _resident((1, D_MODEL))],
        out_specs=tok,
        out_shape=jax.ShapeDtypeStruct(x1.shape, F32),
        scratch_shapes=[pltpu.VMEM((TAIL_TOKENS, D_MODEL), BF16)],
        compiler_params=pltpu.CompilerParams(dimension_semantics=("arbitrary", "arbitrary"),
                                             vmem_limit_bytes=VMEM_LIMIT_BYTES),
        name="tail_prompt",
    )(x1, mkb, mvb, g_xa, w_xq, w_xo, g_ffn, w_gate, w_up, w_down, g_final)


def _proj_kernel(x_ref, g_ref, w_ref, o_ref):
    o_ref[...] = _dot(_rms(x_ref[...], g_ref[...]).astype(BF16), w_ref[...])


def _proj_sample(x2d, g_mix, w_in):
    n = x2d.shape[0]
    nb = IN_W // D_MODEL
    return pl.pallas_call(
        _proj_kernel,
        grid=(nb,),
        in_specs=[pl.BlockSpec((n, D_MODEL), lambda j: (0, 0)),
                  pl.BlockSpec((1, D_MODEL), lambda j: (0, 0)),
                  pl.BlockSpec((D_MODEL, D_MODEL), lambda j: (0, j))],
        out_specs=pl.BlockSpec((n, D_MODEL), lambda j: (0, j)),
        out_shape=jax.ShapeDtypeStruct((n, IN_W), F32),
        compiler_params=pltpu.CompilerParams(dimension_semantics=("arbitrary",),
                                             vmem_limit_bytes=VMEM_LIMIT_BYTES),
        name="proj_sample",
    )(x2d, g_mix, w_in)


def _state_kernel(dec_len, proj_ref, sret_ref, shg_ref, rgain_ref, hgain_ref, hglb_ref,
                  cos_ref, sin_ref, dec_ref, qdec_ref, kdec_ref, sdec_ref,
                  omix_ref, nret_ref, nhg_ref):
    n_rows = STATE_SEQS * dec_len
    pair_rows = 8
    seqs_per_pair = pair_rows // dec_len
    row8 = lax.broadcasted_iota(jnp.int32, (pair_rows, LANE), 0)
    rown = lax.broadcasted_iota(jnp.int32, (n_rows, n_rows), 0)
    coln = lax.broadcasted_iota(jnp.int32, (n_rows, n_rows), 1)
    same_seq_causal = (rown >= coln) & ((rown // dec_len) == (coln // dec_len))
    proj = proj_ref[...]
    cos2 = cos_ref[...]
    sin2 = sin_ref[...]

    def per_sequence(q_all, kd_all, vb_all, state_ref, new_ref, hd, scale_of):
        outs = []
        for p in range(n_rows // pair_rows):
            prow = slice(p * pair_rows, (p + 1) * pair_rows)
            q8, kd8, v8 = q_all[prow], kd_all[prow], vb_all[prow]
            acc = jnp.zeros((pair_rows, LANE), F32)
            for j in range(seqs_per_pair):
                seq = p * seqs_per_pair + j
                mine = (row8 >= j * dec_len) & (row8 < (j + 1) * dec_len)
                s0 = state_ref[seq, hd]
                acc = jnp.where(mine, _dot(q8, s0.astype(BF16)), acc)
                kz = jnp.where(mine, kd8, jnp.zeros_like(kd8))
                new_ref[seq, hd] = scale_of(seq) * s0 + _dot_tn(kz, v8)
            outs.append(acc)
        return jnp.concatenate(outs, axis=0)

    for hd in range(RET_HEADS):
        cols = slice(hd * LANE, (hd + 1) * LANE)
        q = _rope(proj[:, _RQ + hd * LANE:_RQ + (hd + 1) * LANE], cos2, sin2)
        k = _rope(proj[:, _RK + hd * LANE:_RK + (hd + 1) * LANE], cos2, sin2) * (RET_DK ** -0.5)
        vb = proj[:, _RV + hd * LANE:_RV + (hd + 1) * LANE].astype(BF16)
        gate = proj[:, _RG + hd * LANE:_RG + (hd + 1) * LANE]
        qb = q.astype(BF16)
        att = _dot_nt(qb, k.astype(BF16)) * dec_ref[hd]
        kd = (k * kdec_ref[hd]).astype(BF16)
        sdec = sdec_ref[hd]
        qs = per_sequence(qb, kd, vb, sret_ref, nret_ref, hd, lambda seq: sdec)
        o = _dot(att.astype(BF16), vb) + qdec_ref[hd] * qs
        _gate_store(o, rgain_ref[:, cols], gate, omix_ref, slice(None), cols)

    lower = _lower_bound(hglb_ref[...])
    f = lower + (1.0 - lower) * _sigmoid(proj[:, _HF:_HF + HG_W])
    kk = 1.0 - f
    qq = _silu(proj[:, _HQ:_HQ + HG_W])
    b = _cumsum_rows(jnp.log(f), dec_len)
    pos = lax.broadcasted_iota(jnp.int32, b.shape, 0) & (dec_len - 1)

    def spread(row_in_seq):
        picked = jnp.where(pos == row_in_seq, b, 0.0)
        out = picked
        for s in range(1, dec_len):
            out = out + jnp.where(pos == (row_in_seq + s), pltpu.roll(picked, s, 0), 0.0)
            out = out + jnp.where(pos == (row_in_seq - s), pltpu.roll(picked, n_rows - s, 0), 0.0)
        return out

    ref = spread(dec_len // 2 - 1)
    b_last = spread(dec_len - 1)
    q_intra = (qq * jnp.exp(b - ref)).astype(BF16)
    k_intra = (kk * jnp.exp(ref - b)).astype(BF16)
    q_inter = (qq * jnp.exp(b)).astype(BF16)
    k_upd = (kk * jnp.exp(b_last - b)).astype(BF16)
    d_last = jnp.exp(b_last)
    for hd in range(HG_HEADS):
        cols = slice(hd * LANE, (hd + 1) * LANE)
        vb = proj[:, _HI + hd * LANE:_HI + (hd + 1) * LANE].astype(BF16)
        gate = proj[:, _HGATE + hd * LANE:_HGATE + (hd + 1) * LANE]
        att = jnp.where(same_seq_causal, _dot_nt(q_intra[:, cols], k_intra[:, cols]), 0.0)
        dpad = jnp.concatenate([d_last[:, cols], jnp.zeros((LANE - n_rows, LANE), F32)], axis=0)
        dcol = dpad.T
        scale_of = lambda seq: dcol[:, seq * dec_len:seq * dec_len + 1]
        qs = per_sequence(q_inter[:, cols], k_upd[:, cols], vb, shg_ref, nhg_ref, hd, scale_of)
        o = _dot(att.astype(BF16), vb) + qs
        _gate_store(o, hgain_ref[:, cols], gate, omix_ref, slice(None),
                    slice(RET_W + hd * LANE, RET_W + (hd + 1) * LANE))


def _state_sample(proj, state_ret, state_hgrn, dec_len, ret_gain, hg_gain, hg_lb):
    n_seq = state_ret.shape[0]
    n_rows = STATE_SEQS * dec_len
    pos = jnp.tile(jnp.arange(dec_len, dtype=F32) + PAST_LEN, STATE_SEQS)
    cos2, sin2 = _rope_tables(pos)
    dec, qdec, kdec, sdec = _ret_tables(n_rows, dec_len)
    c2 = lambda i: (0, 0)
    c3 = lambda i: (0, 0, 0)
    state = pl.BlockSpec((STATE_SEQS, RET_HEADS, RET_DK, RET_DV), lambda i: (i, 0, 0, 0))
    return pl.pallas_call(
        functools.partial(_state_kernel, dec_len),
        grid=(n_seq // STATE_SEQS,),
        in_specs=[pl.BlockSpec((n_rows, IN_W), lambda i: (i, 0)), state, state,
                  pl.BlockSpec((1, RET_W), c2), pl.BlockSpec((1, HG_W), c2),
                  pl.BlockSpec(hg_lb.shape, c2),
                  pl.BlockSpec((n_rows, LANE), c2), pl.BlockSpec((n_rows, LANE), c2),
                  pl.BlockSpec(dec.shape, c3), pl.BlockSpec(qdec.shape, c3),
                  pl.BlockSpec(kdec.shape, c3), pl.BlockSpec(sdec.shape, c3)],
        out_specs=[pl.BlockSpec((n_rows, MIX_W), lambda i: (i, 0)), state, state],
        out_shape=[jax.ShapeDtypeStruct((proj.shape[0], MIX_W), BF16),
                   jax.ShapeDtypeStruct(state_ret.shape, F32),
                   jax.ShapeDtypeStruct(state_hgrn.shape, F32)],
        compiler_params=pltpu.CompilerParams(dimension_semantics=("arbitrary",),
                                             vmem_limit_bytes=VMEM_LIMIT_BYTES),
        name="state_sample",
    )(proj, state_ret, state_hgrn, ret_gain, hg_gain, hg_lb, cos2, sin2, dec, qdec, kdec, sdec)


def _outq_kernel(x_ref, omix_ref, wout_ref, gxa_ref, wxq_ref, x1_ref, q_ref):
    x1 = x_ref[...] + _dot(omix_ref[...], wout_ref[...])
    x1_ref[...] = x1
    q_ref[...] = _query(x1, gxa_ref[...], wxq_ref[...])


def _outq_sample(x2d, omix, w_out, g_xa, w_xq):
    n = x2d.shape[0]
    z = lambda i: (0, 0)
    return pl.pallas_call(
        _outq_kernel,
        grid=(1,),
        in_specs=[pl.BlockSpec((n, D_MODEL), z), pl.BlockSpec((n, MIX_W), z),
                  pl.BlockSpec((MIX_W, D_MODEL), z), pl.BlockSpec((1, D_MODEL), z),
                  pl.BlockSpec((D_MODEL, D_MODEL), z)],
        out_specs=[pl.BlockSpec((n, D_MODEL), z), pl.BlockSpec((n, D_MODEL), z)],
        out_shape=[jax.ShapeDtypeStruct((n, D_MODEL), F32), jax.ShapeDtypeStruct((n, D_MODEL), F32)],
        compiler_params=pltpu.CompilerParams(dimension_semantics=("arbitrary",),
                                             vmem_limit_bytes=VMEM_LIMIT_BYTES),
        name="outq_sample",
    )(x2d, omix, w_out, g_xa, w_xq)


def _xattn_kernel(dec_len, q_ref, k_ref, v_ref, o_ref):
    pair_rows = 8
    seqs_per_pair = pair_rows // dec_len
    n_kv = N_MEM * XA_HEADS
    q_rows = XA_HEADS * pair_rows
    own_head = (lax.broadcasted_iota(jnp.int32, (q_rows, n_kv), 1) % XA_HEADS
                == lax.broadcasted_iota(jnp.int32, (q_rows, n_kv), 0) // pair_rows)
    row8 = lax.broadcasted_iota(jnp.int32, (q_rows, XA_HD), 0) % pair_rows
    for p in range(XATTN_SEQS // seqs_per_pair):
        prow = slice(p * pair_rows, (p + 1) * pair_rows)
        q8 = q_ref[prow, :]
        qs = jnp.concatenate([q8[:, hd * XA_HD:(hd + 1) * XA_HD] for hd in range(XA_HEADS)],
                             axis=0).astype(BF16)
        o = jnp.zeros((q_rows, XA_HD), F32)
        for j in range(seqs_per_pair):
            seq = p * seqs_per_pair + j
            kb = k_ref[seq].reshape(n_kv, XA_HD).astype(BF16)
            vb = v_ref[seq].reshape(n_kv, XA_HD).astype(BF16)
            s = jnp.where(own_head, _dot_nt(qs, kb), -jnp.inf)
            oj = _dot(_softmax_rows(s).astype(BF16), vb)
            mine = (row8 >= j * dec_len) & (row8 < (j + 1) * dec_len)
            o = jnp.where(mine, oj, o)
        for hd in range(XA_HEADS):
            o_ref[prow, hd * XA_HD:(hd + 1) * XA_HD] = o[hd * pair_rows:(hd + 1) * pair_rows, :]


def _xattn_sample(q, cache_k, cache_v, dec_len):
    n_seq = cache_k.shape[0]
    rows = XATTN_SEQS * dec_len
    qblk = pl.BlockSpec((rows, D_MODEL), lambda i: (i, 0))
    cblk = pl.BlockSpec((XATTN_SEQS, N_MEM, XA_HEADS, XA_HD), lambda i: (i, 0, 0, 0))
    return pl.pallas_call(
        functools.partial(_xattn_kernel, dec_len),
        grid=(n_seq // XATTN_SEQS,),
        in_specs=[qblk, cblk, cblk],
        out_specs=qblk,
        out_shape=jax.ShapeDtypeStruct(q.shape, F32),
        compiler_params=pltpu.CompilerParams(dimension_semantics=("arbitrary",),
                                             vmem_limit_bytes=VMEM_LIMIT_BYTES),
        name="xattn_sample",
    )(q, cache_k, cache_v)


def _post_kernel(x1_ref, ox_ref, wxo_ref, gffn_ref, wgate_ref, wup_ref, wdown_ref, gfinal_ref, y_ref):
    y_ref[...] = _ffn_final(x1_ref[...], ox_ref[...].astype(BF16), wxo_ref[...], gffn_ref[...],
                            wgate_ref[...], wup_ref[...], wdown_ref[...], gfinal_ref[...])


def _post_sample(x1, ox, w_xo, g_ffn, w_gate, w_up, w_down, g_final):
    n = x1.shape[0]
    d_ff = w_gate.shape[1]
    return pl.pallas_call(
        _post_kernel,
        grid=(1,),
        in_specs=[_resident((n, D_MODEL)), _resident((n, D_MODEL)), _resident((D_MODEL, D_MODEL)),
                  _resident((1, D_MODEL)), _resident((D_MODEL, d_ff)), _resident((D_MODEL, d_ff)),
                  _resident((d_ff, D_MODEL)), _resident((1, D_MODEL))],
        out_specs=pl.BlockSpec((n, D_MODEL), lambda i: (0, 0)),
        out_shape=jax.ShapeDtypeStruct((n, D_MODEL), F32),
        compiler_params=pltpu.CompilerParams(dimension_semantics=("arbitrary",),
                                             vmem_limit_bytes=VMEM_LIMIT_BYTES),
        name="post_sample",
    )(x1, ox, w_xo, g_ffn, w_gate, w_up, w_down, g_final)


def kernel(x_prompt, x_sample, mem_prompt, state_ret, state_hgrn, cache_mem_k, cache_mem_v, g_mix, w_in,
           ret_gain, hg_gain, hg_lb, w_out, g_xa, g_mem, w_xq, w_xk, w_xv, w_xo, g_ffn, w_gate, w_up,
           w_down, g_final):
    depth = w_in.shape[0]
    assert depth == 1, "single-layer step"
    batch, seq, d = x_prompt.shape
    dec_batch, dec_len, _ = x_sample.shape
    assert d == D_MODEL and seq % MIX_TOKENS == 0 and seq % TAIL_TOKENS == 0
    assert 8 % dec_len == 0 and dec_batch % STATE_SEQS == 0 and dec_batch % XATTN_SEQS == 0

    bf = lambda w: w[0].astype(BF16)
    w_in_b, w_out_b, w_xq_b, w_xk_b, w_xv_b, w_xo_b = map(bf, (w_in, w_out, w_xq, w_xk, w_xv, w_xo))
    w_gate_b, w_up_b, w_down_b = map(bf, (w_gate, w_up, w_down))
    g_final2 = g_final.reshape(1, D_MODEL)

    mk, mv, mkb, mvb = _memkv(mem_prompt.reshape(batch * N_MEM, D_MODEL), g_mem, w_xk_b, w_xv_b)
    x1_p, ret_p, hg_p = _mix_prompt(x_prompt.reshape(batch * seq, D_MODEL), batch, seq, g_mix, w_in_b,
                                    ret_gain, hg_gain, hg_lb, w_out_b)
    y_p = _tail_prompt(x1_p, batch, seq, mkb.reshape(batch, N_MEM, D_MODEL),
                       mvb.reshape(batch, N_MEM, D_MODEL), g_xa, w_xq_b, w_xo_b, g_ffn,
                       w_gate_b, w_up_b, w_down_b, g_final2)

    xs = x_sample.reshape(dec_batch * dec_len, D_MODEL)
    proj_s = _proj_sample(xs, g_mix, w_in_b)
    omix_s, ret_s, hg_s = _state_sample(proj_s, state_ret[0], state_hgrn[0], dec_len,
                                        ret_gain, hg_gain, hg_lb)
    x1_s, q_s = _outq_sample(xs, omix_s, w_out_b, g_xa, w_xq_b)
    ox_s = _xattn_sample(q_s, cache_mem_k[0], cache_mem_v[0], dec_len)
    y_s = _post_sample(x1_s, ox_s, w_xo_b, g_ffn, w_gate_b, w_up_b, w_down_b, g_final2)

    kv_shape = (depth, batch, N_MEM, XA_HEADS, XA_HD)
    return (y_p.reshape(batch, seq, D_MODEL), y_s.reshape(dec_batch, dec_len, D_MODEL),
            ret_p[None], hg_p[None], mk.reshape(kv_shape), mv.reshape(kv_shape),
            ret_s[None], hg_s[None])
```

```python
import functools

import jax
import jax.numpy as jnp
import numpy as np
from jax import lax
from jax.experimental import pallas as pl
from jax.experimental.pallas import tpu as pltpu

D_MODEL = 1024
RET_HEADS = 4
RET_DK = 128
RET_DV = 128
RET_W = RET_HEADS * RET_DV
HG_HEADS = 4
HG_DK = 128
HG_DV = 128
HG_W = HG_HEADS * HG_DV
MIX_W = RET_W + HG_W
IN_W = 2 * RET_HEADS * RET_DK + 2 * RET_W + 2 * HG_HEADS * HG_DK + 2 * HG_W
N_MEM = 256
XA_HEADS = 4
XA_HD = D_MODEL // XA_HEADS
PAST_LEN = 16384
ROPE_BASE = 10000.0
EPS = 1e-6

_RQ, _RK, _RV, _RG = 0, 512, 1024, 1536
_HQ, _HF, _HI, _HGATE = 2048, 2560, 3072, 3584

LANE = 128
MIX_TOKENS = 512
RET_CHUNK_LEN = 128
HG_CHUNK_LEN = 64
HG_SUB = 16
TAIL_TOKENS = 512
MEMKV_ROWS = 512
STATE_SEQS = 8
VMEM_LIMIT_BYTES = 56 * 1024 * 1024

F32 = jnp.float32
BF16 = jnp.bfloat16


def _dot(a, b):
    return jnp.dot(a, b, preferred_element_type=F32)


def _dot_nt(a, b):
    return lax.dot_general(a, b, (((1,), (1,)), ((), ())), preferred_element_type=F32)


def _dot_tn(a, b):
    return lax.dot_general(a, b, (((0,), (0,)), ((), ())), preferred_element_type=F32)


def _rms(x, g):
    ms = jnp.mean(x * x, axis=-1, keepdims=True)
    return x * lax.rsqrt(ms + EPS) * g


def _sigmoid(x):
    return 1.0 / (1.0 + jnp.exp(-x))


def _silu(x):
    return x * _sigmoid(x)


def _head_norm(o):
    return o * lax.rsqrt(jnp.mean(o * o, axis=-1, keepdims=True) + EPS)


def _softmax_rows(s):
    p = jnp.exp(s - jnp.max(s, axis=-1, keepdims=True))
    return p / jnp.sum(p, axis=-1, keepdims=True)


def _rope(x, cos2, sin2):
    return x * cos2 + pltpu.roll(x, x.shape[-1] // 2, 1) * sin2


def _lower_bound(hglb):
    m = jnp.max(hglb, axis=0, keepdims=True)
    e = jnp.exp(hglb - m)
    return e[0:1, :] / jnp.sum(e, axis=0, keepdims=True)


def _cumsum_rows(x, period):
    row = lax.broadcasted_iota(jnp.int32, x.shape, 0) & (period - 1)
    s = 1
    while s < period:
        x = x + jnp.where(row >= s, pltpu.roll(x, s, 0), 0.0)
        s *= 2
    return x


def _resident(shape):
    zeros = (0,) * len(shape)
    return pl.BlockSpec(shape, lambda *_: zeros, pipeline_mode=pl.Buffered(1))


def _memkv_kernel(mem_ref, g_ref, wk_ref, wv_ref, k_ref, v_ref, kb_ref, vb_ref):
    m = _rms(mem_ref[...], g_ref[...]).astype(BF16)
    k = _dot(m, wk_ref[...])
    v = _dot(m, wv_ref[...])
    k_ref[...] = k
    v_ref[...] = v
    kb_ref[...] = k.astype(BF16)
    vb_ref[...] = v.astype(BF16)


def _memkv(mem2d, g_mem, w_xk, w_xv):
    n = mem2d.shape[0]
    full = lambda i: (0, 0)
    row = lambda i: (i, 0)
    blk = pl.BlockSpec((MEMKV_ROWS, D_MODEL), row)
    return pl.pallas_call(
        _memkv_kernel,
        grid=(n // MEMKV_ROWS,),
        in_specs=[blk, pl.BlockSpec((1, D_MODEL), full),
                  pl.BlockSpec((D_MODEL, D_MODEL), full), pl.BlockSpec((D_MODEL, D_MODEL), full)],
        out_specs=[blk, blk, blk, blk],
        out_shape=[jax.ShapeDtypeStruct((n, D_MODEL), F32), jax.ShapeDtypeStruct((n, D_MODEL), F32),
                   jax.ShapeDtypeStruct((n, D_MODEL), BF16), jax.ShapeDtypeStruct((n, D_MODEL), BF16)],
        compiler_params=pltpu.CompilerParams(dimension_semantics=("arbitrary",),
                                             vmem_limit_bytes=VMEM_LIMIT_BYTES),
        name="memkv",
    )(mem2d, g_mem, w_xk, w_xv)


def _sample_xattn_stages(dec_len, n_seqs, q_ref, k_ref, v_ref, o_ref):
    pair_rows = 8
    seqs_per_pair = pair_rows // dec_len
    n_kv = N_MEM * XA_HEADS
    q_rows = XA_HEADS * pair_rows
    own_head = (lax.broadcasted_iota(jnp.int32, (q_rows, n_kv), 1) % XA_HEADS
                == lax.broadcasted_iota(jnp.int32, (q_rows, n_kv), 0) // pair_rows)
    row8 = lax.broadcasted_iota(jnp.int32, (q_rows, XA_HD), 0) % pair_rows
    probs, outs = {}, {}

    def score(seq):
        prow = slice((seq // seqs_per_pair) * pair_rows, (seq // seqs_per_pair + 1) * pair_rows)
        q8 = q_ref[prow, :]
        qs = jnp.concatenate([q8[:, hd * XA_HD:(hd + 1) * XA_HD] for hd in range(XA_HEADS)],
                             axis=0).astype(BF16)
        kb = k_ref[seq].reshape(n_kv, XA_HD).astype(BF16)
        probs[seq] = _softmax_rows(jnp.where(own_head, _dot_nt(qs, kb), -jnp.inf)).astype(BF16)

    def attend(seq):
        p, j = divmod(seq, seqs_per_pair)
        oj = _dot(probs.pop(seq), v_ref[seq].reshape(n_kv, XA_HD).astype(BF16))
        mine = (row8 >= j * dec_len) & (row8 < (j + 1) * dec_len)
        outs[p] = jnp.where(mine, oj, outs[p]) if p in outs else oj
        if j == seqs_per_pair - 1:
            o = outs.pop(p)
            for hd in range(XA_HEADS):
                o_ref[p * pair_rows:(p + 1) * pair_rows, hd * XA_HD:(hd + 1) * XA_HD] = \
                    o[hd * pair_rows:(hd + 1) * pair_rows, :]

    return ([functools.partial(score, s) for s in range(n_seqs)]
            + [functools.partial(attend, s) for s in range(n_seqs)])


def _gate_store(o, gain, gate, out_ref, rows, cols):
    out_ref[rows, cols] = (_head_norm(o) * gain * _silu(gate)).astype(BF16)


def _mix_kernel(dec_len, x_ref, gmix_ref, win_ref, rgain_ref, hgain_ref, hglb_ref, wout_ref,
                cos_ref, sin_ref, dec_ref, qdec_ref, kdec_ref, sdec_ref, qs_ref, ck_ref, cv_ref,
                x1_ref, sret_ref, shg_ref, oxs_ref, omix_scr):
    t = pl.program_id(1)

    @pl.when(t == 0)
    def _():
        sret_ref[...] = jnp.zeros_like(sret_ref)
        shg_ref[...] = jnp.zeros_like(shg_ref)

    x = x_ref[...]
    h = _rms(x, gmix_ref[...]).astype(BF16)
    side_work = _sample_xattn_stages(dec_len, ck_ref.shape[0], qs_ref, ck_ref, cv_ref, oxs_ref)
    groups = {}
    for g0 in (_HF, _HQ, _RQ, _RK, _RV, _HI, _RG, _HGATE):
        groups[g0] = _dot(h, win_ref[:, g0:g0 + RET_W])
        if side_work:
            side_work.pop(0)()
    while side_work:
        side_work.pop(0)()

    class _Proj:
        def __getitem__(self, idx):
            rows, cols = idx
            g0 = (cols.start // RET_W) * RET_W
            return groups[g0][rows, cols.start - g0:cols.stop - g0]

    proj = _Proj()

    ret_units = []
    for c in range(MIX_TOKENS // RET_CHUNK_LEN):
        rows = slice(c * RET_CHUNK_LEN, (c + 1) * RET_CHUNK_LEN)
        cos2 = cos_ref[rows, :]
        sin2 = sin_ref[rows, :]
        for hd in range(RET_HEADS):
            q = _rope(proj[rows, _RQ + hd * LANE:_RQ + (hd + 1) * LANE], cos2, sin2)
            k = _rope(proj[rows, _RK + hd * LANE:_RK + (hd + 1) * LANE], cos2, sin2) * (RET_DK ** -0.5)
            vb = proj[rows, _RV + hd * LANE:_RV + (hd + 1) * LANE].astype(BF16)
            att = (_dot_nt(q.astype(BF16), k.astype(BF16)) * dec_ref[hd]).astype(BF16)
            kv = _dot_tn((k * kdec_ref[hd]).astype(BF16), vb)
            lhs = jnp.concatenate([att, (q * qdec_ref[hd]).astype(BF16)], axis=1)
            ret_units.append((rows, hd, lhs, vb, kv))

    lower = _lower_bound(hglb_ref[...])
    n_sub = HG_CHUNK_LEN // HG_SUB
    crow = lax.broadcasted_iota(jnp.int32, (HG_CHUNK_LEN, HG_CHUNK_LEN), 0)
    ccol = lax.broadcasted_iota(jnp.int32, (HG_CHUNK_LEN, HG_CHUNK_LEN), 1)
    causal = crow >= ccol
    hg_units = []
    for c in range(MIX_TOKENS // HG_CHUNK_LEN):
        rows = slice(c * HG_CHUNK_LEN, (c + 1) * HG_CHUNK_LEN)
        f = lower + (1.0 - lower) * _sigmoid(proj[rows, _HF:_HF + HG_W])
        kk = 1.0 - f
        qq = _silu(proj[rows, _HQ:_HQ + HG_W])
        b = _cumsum_rows(jnp.log(f), HG_CHUNK_LEN)
        b_last = b[HG_CHUNK_LEN - 1:HG_CHUNK_LEN, :]
        q_inter = (qq * jnp.exp(b)).astype(BF16)
        k_upd = (kk * jnp.exp(b_last - b)).astype(BF16)
        d_last = jnp.exp(b_last)
        for hd in range(HG_HEADS):
            cols = slice(hd * LANE, (hd + 1) * LANE)
            bh = b[:, cols]
            qh = qq[:, cols]
            kh = kk[:, cols]
            vb = proj[rows, _HI + hd * LANE:_HI + (hd + 1) * LANE].astype(BF16)
            q_parts, k_parts = [], []
            for j in range(n_sub):
                lo, hi = j * HG_SUB, (j + 1) * HG_SUB
                ref = bh[lo + HG_SUB // 2 - 1:lo + HG_SUB // 2, :]
                qt = qh[lo:, :] * jnp.exp(bh[lo:, :] - ref)
                kt = kh[lo:hi, :] * jnp.exp(ref - bh[lo:hi, :])
                if lo:
                    qt = jnp.concatenate([jnp.zeros((lo, LANE), F32), qt], axis=0)
                    kt = jnp.concatenate([jnp.zeros((lo, LANE), F32), kt], axis=0)
                if hi < HG_CHUNK_LEN:
                    kt = jnp.concatenate([kt, jnp.zeros((HG_CHUNK_LEN - hi, LANE), F32)], axis=0)
                q_parts.append(qt.astype(BF16))
                k_parts.append(kt.astype(BF16))
            qcat = jnp.concatenate(q_parts, axis=1)
            kcat = jnp.concatenate(k_parts, axis=1)
            att = jnp.where(causal, _dot_nt(qcat, kcat), 0.0).astype(BF16)
            kv = _dot_tn(k_upd[:, cols], vb)
            lhs = jnp.concatenate([q_inter[:, cols], att], axis=1)
            hg_units.append((rows, hd, lhs, vb, kv, d_last[:, cols]))

    for hd in range(RET_HEADS):
        cols = slice(hd * LANE, (hd + 1) * LANE)
        s = sret_ref[0, hd]
        for rows, uh, lhs, vb, kv in ret_units:
            if uh != hd:
                continue
            o = _dot(lhs, jnp.concatenate([vb, s.astype(BF16)], axis=0))
            s = sdec_ref[hd] * s + kv
            gate = proj[rows, _RG + hd * LANE:_RG + (hd + 1) * LANE]
            _gate_store(o, rgain_ref[:, cols], gate, omix_scr, rows, cols)
        sret_ref[0, hd] = s

    for hd in range(HG_HEADS):
        cols = slice(hd * LANE, (hd + 1) * LANE)
        mine = [u for u in hg_units if u[1] == hd]
        dl = jnp.concatenate([u[5] for u in mine] + [jnp.zeros((LANE - len(mine), LANE), F32)], axis=0).T
        s = shg_ref[0, hd]
        for i, (rows, _, lhs, vb, kv, _) in enumerate(mine):
            o = _dot(lhs, jnp.concatenate([s.astype(BF16), vb], axis=0))
            s = s * dl[:, i:i + 1] + kv
            gate = proj[rows, _HGATE + hd * LANE:_HGATE + (hd + 1) * LANE]
            _gate_store(o, hgain_ref[:, cols], gate, omix_scr, rows,
                        slice(RET_W + hd * LANE, RET_W + (hd + 1) * LANE))
        shg_ref[0, hd] = s

    x1_ref[...] = x + _dot(omix_scr[...], wout_ref[...])


def _ret_tables(length, period):
    log_g = np.log(1.0 - 2.0 ** (-5.0 - np.arange(RET_HEADS, dtype=np.float64)))
    idx = np.arange(length)
    pos = idx % period
    rel = (idx[:, None] - idx[None, :]).astype(np.float64)
    same = (idx[:, None] // period) == (idx[None, :] // period)
    valid = (rel >= 0) & same
    dec = np.where(valid[None], np.exp(log_g[:, None, None] * np.where(valid, rel, 0.0)[None]), 0.0)
    qdec = np.exp(log_g[:, None] * (pos + 1.0))[:, :, None] * np.ones((1, 1, LANE))
    kdec = np.exp(log_g[:, None] * (period - 1.0 - pos))[:, :, None] * np.ones((1, 1, LANE))
    sdec = np.exp(log_g * period)[:, None, None] * np.ones((1, 1, LANE))
    as32 = lambda a: jnp.asarray(a, dtype=F32)
    return as32(dec), as32(qdec), as32(kdec), as32(sdec)


def _rope_tables(pos):
    half = RET_DK // 2
    inv_freq = ROPE_BASE ** (-jnp.arange(half, dtype=F32) / half)
    ang = pos[:, None] * inv_freq[None, :]
    cos, sin = jnp.cos(ang), jnp.sin(ang)
    return jnp.concatenate([cos, cos], axis=-1), jnp.concatenate([-sin, sin], axis=-1)


def _mix_prompt(x2d, batch, seq, g_mix, w_in, ret_gain, hg_gain, hg_lb, w_out,
                q_s, cache_k, cache_v, dec_len):
    nt = seq // MIX_TOKENS
    n_seq = cache_k.shape[0]
    seqs_per_step = n_seq // (batch * nt)
    assert seqs_per_step * batch * nt == n_seq and (seqs_per_step * dec_len) % 8 == 0
    srows = pl.BlockSpec((seqs_per_step * dec_len, D_MODEL), lambda b, t: (b * nt + t, 0))
    cblk = pl.BlockSpec((seqs_per_step, N_MEM, XA_HEADS, XA_HD), lambda b, t: (b * nt + t, 0, 0, 0))
    cos2, sin2 = _rope_tables(jnp.arange(seq, dtype=F32))
    dec, qdec, kdec, sdec = _ret_tables(RET_CHUNK_LEN, RET_CHUNK_LEN)
    c2 = lambda b, t: (0, 0)
    c3 = lambda b, t: (0, 0, 0)
    tok = pl.BlockSpec((MIX_TOKENS, D_MODEL), lambda b, t: (b * nt + t, 0))
    state = pl.BlockSpec((1, RET_HEADS, RET_DK, RET_DV), lambda b, t: (b, 0, 0, 0))
    return pl.pallas_call(
        functools.partial(_mix_kernel, dec_len),
        grid=(batch, nt),
        in_specs=[tok, pl.BlockSpec((1, D_MODEL), c2), _resident((D_MODEL, IN_W)),
                  pl.BlockSpec((1, RET_W), c2), pl.BlockSpec((1, HG_W), c2),
                  pl.BlockSpec(hg_lb.shape, c2), _resident((MIX_W, D_MODEL)),
                  pl.BlockSpec((MIX_TOKENS, LANE), lambda b, t: (t, 0)),
                  pl.BlockSpec((MIX_TOKENS, LANE), lambda b, t: (t, 0)),
                  pl.BlockSpec(dec.shape, c3), pl.BlockSpec(qdec.shape, c3),
                  pl.BlockSpec(kdec.shape, c3), pl.BlockSpec(sdec.shape, c3), srows, cblk, cblk],
        out_specs=[tok, state, state, srows],
        out_shape=[jax.ShapeDtypeStruct(x2d.shape, F32),
                   jax.ShapeDtypeStruct((batch, RET_HEADS, RET_DK, RET_DV), F32),
                   jax.ShapeDtypeStruct((batch, HG_HEADS, HG_DK, HG_DV), F32),
                   jax.ShapeDtypeStruct(q_s.shape, F32)],
        scratch_shapes=[pltpu.VMEM((MIX_TOKENS, MIX_W), BF16)],
        compiler_params=pltpu.CompilerParams(dimension_semantics=("arbitrary", "arbitrary"),
                                             vmem_limit_bytes=VMEM_LIMIT_BYTES),
        name="mix_prompt",
    )(x2d, g_mix, w_in, ret_gain, hg_gain, hg_lb, w_out, cos2, sin2, dec, qdec, kdec, sdec,
      q_s, cache_k, cache_v)


def _query(x1, gxa, wxq):
    return _dot(_rms(x1, gxa).astype(BF16), wxq) * (XA_HD ** -0.5)


def _ffn_final(x1, ox, wxo, gffn, wgate, wup, wdown, gfinal):
    x2 = x1 + _dot(ox, wxo)
    hn = _rms(x2, gffn).astype(BF16)
    a = (_silu(_dot(hn, wgate)) * _dot(hn, wup)).astype(BF16)
    x3 = x2 + _dot(a, wdown)
    return _rms(x3, gfinal)


def _tail_kernel(x1_ref, mk_ref, mv_ref, gxa_ref, wxq_ref, wxo_ref, gffn_ref, wgate_ref, wup_ref,
                 wdown_ref, gfinal_ref, y_ref, ox_scr):
    x1 = x1_ref[...]
    q = _query(x1, gxa_ref[...], wxq_ref[...]).astype(BF16)
    for hd in range(XA_HEADS):
        cols = slice(hd * XA_HD, (hd + 1) * XA_HD)
        p = _softmax_rows(_dot_nt(q[:, cols], mk_ref[0, :, cols]))
        ox_scr[:, cols] = _dot(p.astype(BF16), mv_ref[0, :, cols]).astype(BF16)
    y_ref[...] = _ffn_final(x1, ox_scr[...], wxo_ref[...], gffn_ref[...], wgate_ref[...],
                            wup_ref[...], wdown_ref[...], gfinal_ref[...])


def _tail_prompt(x1, batch, seq, mkb, mvb, g_xa, w_xq, w_xo, g_ffn, w_gate, w_up, w_down, g_final):
    nt = seq // TAIL_TOKENS
    d_ff = w_gate.shape[1]
    tok = pl.BlockSpec((TAIL_TOKENS, D_MODEL), lambda b, t: (b * nt + t, 0))
    mem = pl.BlockSpec((1, N_MEM, D_MODEL), lambda b, t: (b, 0, 0))
    return pl.pallas_call(
        _tail_kernel,
        grid=(batch, nt),
        in_specs=[tok, mem, mem, _resident((1, D_MODEL)), _resident((D_MODEL, D_MODEL)),
                  _resident((D_MODEL, D_MODEL)), _resident((1, D_MODEL)),
                  _resident((D_MODEL, d_ff)), _resident((D_MODEL, d_ff)),
                  _resident((d_ff, D_MODEL)), _resident((1, D_MODEL))],
        out_specs=tok,
        out_shape=jax.ShapeDtypeStruct(x1.shape, F32),
        scratch_shapes=[pltpu.VMEM((TAIL_TOKENS, D_MODEL), BF16)],
        compiler_params=pltpu.CompilerParams(dimension_semantics=("arbitrary", "arbitrary"),
                                             vmem_limit_bytes=VMEM_LIMIT_BYTES),
        name="tail_prompt",
    )(x1, mkb, mvb, g_xa, w_xq, w_xo, g_ffn, w_gate, w_up, w_down, g_final)


def _proj_kernel(x_ref, g_ref, w_ref, o_ref):
    o_ref[...] = _dot(_rms(x_ref[...], g_ref[...]).astype(BF16), w_ref[...])


def _proj_sample(x2d, g_mix, w_in):
    n = x2d.shape[0]
    nb = IN_W // D_MODEL
    return pl.pallas_call(
        _proj_kernel,
        grid=(nb,),
        in_specs=[pl.BlockSpec((n, D_MODEL), lambda j: (0, 0)),
                  pl.BlockSpec((1, D_MODEL), lambda j: (0, 0)),
                  pl.BlockSpec((D_MODEL, D_MODEL), lambda j: (0, j))],
        out_specs=pl.BlockSpec((n, D_MODEL), lambda j: (0, j)),
        out_shape=jax.ShapeDtypeStruct((n, IN_W), F32),
        compiler_params=pltpu.CompilerParams(dimension_semantics=("arbitrary",),
                                             vmem_limit_bytes=VMEM_LIMIT_BYTES),
        name="proj_sample",
    )(x2d, g_mix, w_in)


def _state_kernel(dec_len, proj_ref, sret_ref, shg_ref, rgain_ref, hgain_ref, hglb_ref,
                  cos_ref, sin_ref, dec_ref, qdec_ref, kdec_ref, sdec_ref,
                  omix_ref, nret_ref, nhg_ref):
    n_rows = STATE_SEQS * dec_len
    pair_rows = 8
    seqs_per_pair = pair_rows // dec_len
    row8 = lax.broadcasted_iota(jnp.int32, (pair_rows, LANE), 0)
    rown = lax.broadcasted_iota(jnp.int32, (n_rows, n_rows), 0)
    coln = lax.broadcasted_iota(jnp.int32, (n_rows, n_rows), 1)
    same_seq_causal = (rown >= coln) & ((rown // dec_len) == (coln // dec_len))
    proj = proj_ref[...]
    cos2 = cos_ref[...]
    sin2 = sin_ref[...]

    def per_sequence(q_all, kd_all, vb_all, state_ref, new_ref, hd, scale_of):
        outs = []
        for p in range(n_rows // pair_rows):
            prow = slice(p * pair_rows, (p + 1) * pair_rows)
            q8, kd8, v8 = q_all[prow], kd_all[prow], vb_all[prow]
            acc = jnp.zeros((pair_rows, LANE), F32)
            for j in range(seqs_per_pair):
                seq = p * seqs_per_pair + j
                mine = (row8 >= j * dec_len) & (row8 < (j + 1) * dec_len)
                s0 = state_ref[seq, hd]
                acc = jnp.where(mine, _dot(q8, s0.astype(BF16)), acc)
                kz = jnp.where(mine, kd8, jnp.zeros_like(kd8))
                new_ref[seq, hd] = scale_of(seq) * s0 + _dot_tn(kz, v8)
            outs.append(acc)
        return jnp.concatenate(outs, axis=0)

    for hd in range(RET_HEADS):
        cols = slice(hd * LANE, (hd + 1) * LANE)
        q = _rope(proj[:, _RQ + hd * LANE:_RQ + (hd + 1) * LANE], cos2, sin2)
        k = _rope(proj[:, _RK + hd * LANE:_RK + (hd + 1) * LANE], cos2, sin2) * (RET_DK ** -0.5)
        vb = proj[:, _RV + hd * LANE:_RV + (hd + 1) * LANE].astype(BF16)
        gate = proj[:, _RG + hd * LANE:_RG + (hd + 1) * LANE]
        qb = q.astype(BF16)
        att = _dot_nt(qb, k.astype(BF16)) * dec_ref[hd]
        kd = (k * kdec_ref[hd]).astype(BF16)
        sdec = sdec_ref[hd]
        qs = per_sequence(qb, kd, vb, sret_ref, nret_ref, hd, lambda seq: sdec)
        o = _dot(att.astype(BF16), vb) + qdec_ref[hd] * qs
        _gate_store(o, rgain_ref[:, cols], gate, omix_ref, slice(None), cols)

    lower = _lower_bound(hglb_ref[...])
    f = lower + (1.0 - lower) * _sigmoid(proj[:, _HF:_HF + HG_W])
    kk = 1.0 - f
    qq = _silu(proj[:, _HQ:_HQ + HG_W])
    b = _cumsum_rows(jnp.log(f), dec_len)
    pos = lax.broadcasted_iota(jnp.int32, b.shape, 0) & (dec_len - 1)

    def spread(row_in_seq):
        picked = jnp.where(pos == row_in_seq, b, 0.0)
        out = picked
        for s in range(1, dec_len):
            out = out + jnp.where(pos == (row_in_seq + s), pltpu.roll(picked, s, 0), 0.0)
            out = out + jnp.where(pos == (row_in_seq - s), pltpu.roll(picked, n_rows - s, 0), 0.0)
        return out

    ref = spread(dec_len // 2 - 1)
    b_last = spread(dec_len - 1)
    q_intra = (qq * jnp.exp(b - ref)).astype(BF16)
    k_intra = (kk * jnp.exp(ref - b)).astype(BF16)
    q_inter = (qq * jnp.exp(b)).astype(BF16)
    k_upd = (kk * jnp.exp(b_last - b)).astype(BF16)
    d_last = jnp.exp(b_last)
    for hd in range(HG_HEADS):
        cols = slice(hd * LANE, (hd + 1) * LANE)
        vb = proj[:, _HI + hd * LANE:_HI + (hd + 1) * LANE].astype(BF16)
        gate = proj[:, _HGATE + hd * LANE:_HGATE + (hd + 1) * LANE]
        att = jnp.where(same_seq_causal, _dot_nt(q_intra[:, cols], k_intra[:, cols]), 0.0)
        dpad = jnp.concatenate([d_last[:, cols], jnp.zeros((LANE - n_rows, LANE), F32)], axis=0)
        dcol = dpad.T
        scale_of = lambda seq: dcol[:, seq * dec_len:seq * dec_len + 1]
        qs = per_sequence(q_inter[:, cols], k_upd[:, cols], vb, shg_ref, nhg_ref, hd, scale_of)
        o = _dot(att.astype(BF16), vb) + qs
        _gate_store(o, hgain_ref[:, cols], gate, omix_ref, slice(None),
                    slice(RET_W + hd * LANE, RET_W + (hd + 1) * LANE))


def _state_sample(proj, state_ret, state_hgrn, dec_len, ret_gain, hg_gain, hg_lb):
    n_seq = state_ret.shape[0]
    n_rows = STATE_SEQS * dec_len
    pos = jnp.tile(jnp.arange(dec_len, dtype=F32) + PAST_LEN, STATE_SEQS)
    cos2, sin2 = _rope_tables(pos)
    dec, qdec, kdec, sdec = _ret_tables(n_rows, dec_len)
    c2 = lambda i: (0, 0)
    c3 = lambda i: (0, 0, 0)
    state = pl.BlockSpec((STATE_SEQS, RET_HEADS, RET_DK, RET_DV), lambda i: (i, 0, 0, 0))
    return pl.pallas_call(
        functools.partial(_state_kernel, dec_len),
        grid=(n_seq // STATE_SEQS,),
        in_specs=[pl.BlockSpec((n_rows, IN_W), lambda i: (i, 0)), state, state,
                  pl.BlockSpec((1, RET_W), c2), pl.BlockSpec((1, HG_W), c2),
                  pl.BlockSpec(hg_lb.shape, c2),
                  pl.BlockSpec((n_rows, LANE), c2), pl.BlockSpec((n_rows, LANE), c2),
                  pl.BlockSpec(dec.shape, c3), pl.BlockSpec(qdec.shape, c3),
                  pl.BlockSpec(kdec.shape, c3), pl.BlockSpec(sdec.shape, c3)],
        out_specs=[pl.BlockSpec((n_rows, MIX_W), lambda i: (i, 0)), state, state],
        out_shape=[jax.ShapeDtypeStruct((proj.shape[0], MIX_W), BF16),
                   jax.ShapeDtypeStruct(state_ret.shape, F32),
                   jax.ShapeDtypeStruct(state_hgrn.shape, F32)],
        compiler_params=pltpu.CompilerParams(dimension_semantics=("arbitrary",),
                                             vmem_limit_bytes=VMEM_LIMIT_BYTES),
        name="state_sample",
    )(proj, state_ret, state_hgrn, ret_gain, hg_gain, hg_lb, cos2, sin2, dec, qdec, kdec, sdec)


def _outq_kernel(x_ref, omix_ref, wout_ref, gxa_ref, wxq_ref, x1_ref, q_ref):
    x1 = x_ref[...] + _dot(omix_ref[...], wout_ref[...])
    x1_ref[...] = x1
    q_ref[...] = _query(x1, gxa_ref[...], wxq_ref[...])


def _outq_sample(x2d, omix, w_out, g_xa, w_xq):
    n = x2d.shape[0]
    z = lambda i: (0, 0)
    return pl.pallas_call(
        _outq_kernel,
        grid=(1,),
        in_specs=[pl.BlockSpec((n, D_MODEL), z), pl.BlockSpec((n, MIX_W), z),
                  pl.BlockSpec((MIX_W, D_MODEL), z), pl.BlockSpec((1, D_MODEL), z),
                  pl.BlockSpec((D_MODEL, D_MODEL), z)],
        out_specs=[pl.BlockSpec((n, D_MODEL), z), pl.BlockSpec((n, D_MODEL), z)],
        out_shape=[jax.ShapeDtypeStruct((n, D_MODEL), F32), jax.ShapeDtypeStruct((n, D_MODEL), F32)],
        compiler_params=pltpu.CompilerParams(dimension_semantics=("arbitrary",),
                                             vmem_limit_bytes=VMEM_LIMIT_BYTES),
        name="outq_sample",
    )(x2d, omix, w_out, g_xa, w_xq)


def _post_kernel(x1_ref, ox_ref, wxo_ref, gffn_ref, wgate_ref, wup_ref, wdown_ref, gfinal_ref, y_ref):
    y_ref[...] = _ffn_final(x1_ref[...], ox_ref[...].astype(BF16), wxo_ref[...], gffn_ref[...],
                            wgate_ref[...], wup_ref[...], wdown_ref[...], gfinal_ref[...])


def _post_sample(x1, ox, w_xo, g_ffn, w_gate, w_up, w_down, g_final):
    n = x1.shape[0]
    d_ff = w_gate.shape[1]
    return pl.pallas_call(
        _post_kernel,
        grid=(1,),
        in_specs=[_resident((n, D_MODEL)), _resident((n, D_MODEL)), _resident((D_MODEL, D_MODEL)),
                  _resident((1, D_MODEL)), _resident((D_MODEL, d_ff)), _resident((D_MODEL, d_ff)),
                  _resident((d_ff, D_MODEL)), _resident((1, D_MODEL))],
        out_specs=pl.BlockSpec((n, D_MODEL), lambda i: (0, 0)),
        out_shape=jax.ShapeDtypeStruct((n, D_MODEL), F32),
        compiler_params=pltpu.CompilerParams(dimension_semantics=("arbitrary",),
                                             vmem_limit_bytes=VMEM_LIMIT_BYTES),
        name="post_sample",
    )(x1, ox, w_xo, g_ffn, w_gate, w_up, w_down, g_final)


def kernel(x_prompt, x_sample, mem_prompt, state_ret, state_hgrn, cache_mem_k, cache_mem_v, g_mix, w_in,
           ret_gain, hg_gain, hg_lb, w_out, g_xa, g_mem, w_xq, w_xk, w_xv, w_xo, g_ffn, w_gate, w_up,
           w_down, g_final):
    depth = w_in.shape[0]
    assert depth == 1, "single-layer step"
    batch, seq, d = x_prompt.shape
    dec_batch, dec_len, _ = x_sample.shape
    assert d == D_MODEL and seq % MIX_TOKENS == 0 and seq % TAIL_TOKENS == 0
    assert 8 % dec_len == 0 and dec_batch % STATE_SEQS == 0

    bf = lambda w: w[0].astype(BF16)
    w_in_b, w_out_b, w_xq_b, w_xk_b, w_xv_b, w_xo_b = map(bf, (w_in, w_out, w_xq, w_xk, w_xv, w_xo))
    w_gate_b, w_up_b, w_down_b = map(bf, (w_gate, w_up, w_down))
    g_final2 = g_final.reshape(1, D_MODEL)

    xs = x_sample.reshape(dec_batch * dec_len, D_MODEL)
    proj_s = _proj_sample(xs, g_mix, w_in_b)
    omix_s, ret_s, hg_s = _state_sample(proj_s, state_ret[0], state_hgrn[0], dec_len,
                                        ret_gain, hg_gain, hg_lb)
    x1_s, q_s = _outq_sample(xs, omix_s, w_out_b, g_xa, w_xq_b)

    mk, mv, mkb, mvb = _memkv(mem_prompt.reshape(batch * N_MEM, D_MODEL), g_mem, w_xk_b, w_xv_b)
    x1_p, ret_p, hg_p, ox_s = _mix_prompt(x_prompt.reshape(batch * seq, D_MODEL), batch, seq, g_mix,
                                          w_in_b, ret_gain, hg_gain, hg_lb, w_out_b,
                                          q_s, cache_mem_k[0], cache_mem_v[0], dec_len)
    y_p = _tail_prompt(x1_p, batch, seq, mkb.reshape(batch, N_MEM, D_MODEL),
                       mvb.reshape(batch, N_MEM, D_MODEL), g_xa, w_xq_b, w_xo_b, g_ffn,
                       w_gate_b, w_up_b, w_down_b, g_final2)

    y_s = _post_sample(x1_s, ox_s, w_xo_b, g_ffn, w_gate_b, w_up_b, w_down_b, g_final2)

    kv_shape = (depth, batch, N_MEM, XA_HEADS, XA_HD)
    return (y_p.reshape(batch, seq, D_MODEL), y_s.reshape(dec_batch, dec_len, D_MODEL),
            ret_p[None], hg_p[None], mk.reshape(kv_shape), mv.reshape(kv_shape),
            ret_s[None], hg_s[None])
```

```python
import functools

import jax
import jax.numpy as jnp
import numpy as np
from jax import lax
from jax.experimental import pallas as pl
from jax.experimental.pallas import tpu as pltpu

D_MODEL = 1024
RET_HEADS = 4
RET_DK = 128
RET_DV = 128
RET_W = RET_HEADS * RET_DV
HG_HEADS = 4
HG_DK = 128
HG_DV = 128
HG_W = HG_HEADS * HG_DV
MIX_W = RET_W + HG_W
IN_W = 2 * RET_HEADS * RET_DK + 2 * RET_W + 2 * HG_HEADS * HG_DK + 2 * HG_W
N_MEM = 256
XA_HEADS = 4
XA_HD = D_MODEL // XA_HEADS
PAST_LEN = 16384
ROPE_BASE = 10000.0
EPS = 1e-6

_RQ, _RK, _RV, _RG = 0, 512, 1024, 1536
_HQ, _HF, _HI, _HGATE = 2048, 2560, 3072, 3584

LANE = 128
MIX_TOKENS = 512
RET_CHUNK_LEN = 128
HG_CHUNK_LEN = 64
HG_SUB = 16
TAIL_TOKENS = 512
MEMKV_ROWS = 512
STATE_SEQS = 8
VMEM_LIMIT_BYTES = 56 * 1024 * 1024

F32 = jnp.float32
BF16 = jnp.bfloat16


def _dot(a, b):
    return jnp.dot(a, b, preferred_element_type=F32)


def _dot_nt(a, b):
    return lax.dot_general(a, b, (((1,), (1,)), ((), ())), preferred_element_type=F32)


def _dot_tn(a, b):
    return lax.dot_general(a, b, (((0,), (0,)), ((), ())), preferred_element_type=F32)


def _rms(x, g):
    ms = jnp.mean(x * x, axis=-1, keepdims=True)
    return x * lax.rsqrt(ms + EPS) * g


def _sigmoid(x):
    return 1.0 / (1.0 + jnp.exp(-x))


def _silu(x):
    return x * _sigmoid(x)


def _head_norm(o):
    return o * lax.rsqrt(jnp.mean(o * o, axis=-1, keepdims=True) + EPS)


def _softmax_rows(s):
    p = jnp.exp(s - jnp.max(s, axis=-1, keepdims=True))
    return p / jnp.sum(p, axis=-1, keepdims=True)


def _rope(x, cos2, sin2):
    return x * cos2 + pltpu.roll(x, x.shape[-1] // 2, 1) * sin2


def _lower_bound(hglb):
    m = jnp.max(hglb, axis=0, keepdims=True)
    e = jnp.exp(hglb - m)
    return e[0:1, :] / jnp.sum(e, axis=0, keepdims=True)


def _cumsum_rows(x, period):
    row = lax.broadcasted_iota(jnp.int32, x.shape, 0) & (period - 1)
    s = 1
    while s < period:
        x = x + jnp.where(row >= s, pltpu.roll(x, s, 0), 0.0)
        s *= 2
    return x


def _resident(shape):
    zeros = (0,) * len(shape)
    return pl.BlockSpec(shape, lambda *_: zeros, pipeline_mode=pl.Buffered(1))


def _memkv_kernel(mem_ref, g_ref, wk_ref, wv_ref, k_ref, v_ref, kb_ref, vb_ref):
    m = _rms(mem_ref[...], g_ref[...]).astype(BF16)
    k = _dot(m, wk_ref[...])
    v = _dot(m, wv_ref[...])
    k_ref[...] = k
    v_ref[...] = v
    kb_ref[...] = k.astype(BF16)
    vb_ref[...] = v.astype(BF16)


def _memkv(mem2d, g_mem, w_xk, w_xv):
    n = mem2d.shape[0]
    full = lambda i: (0, 0)
    row = lambda i: (i, 0)
    blk = pl.BlockSpec((MEMKV_ROWS, D_MODEL), row)
    return pl.pallas_call(
        _memkv_kernel,
        grid=(n // MEMKV_ROWS,),
        in_specs=[blk, pl.BlockSpec((1, D_MODEL), full),
                  pl.BlockSpec((D_MODEL, D_MODEL), full), pl.BlockSpec((D_MODEL, D_MODEL), full)],
        out_specs=[blk, blk, blk, blk],
        out_shape=[jax.ShapeDtypeStruct((n, D_MODEL), F32), jax.ShapeDtypeStruct((n, D_MODEL), F32),
                   jax.ShapeDtypeStruct((n, D_MODEL), BF16), jax.ShapeDtypeStruct((n, D_MODEL), BF16)],
        compiler_params=pltpu.CompilerParams(dimension_semantics=("arbitrary",),
                                             vmem_limit_bytes=VMEM_LIMIT_BYTES),
        name="memkv",
    )(mem2d, g_mem, w_xk, w_xv)


def _sample_xattn_stages(dec_len, n_seqs, q_ref, k_ref, v_ref, o_ref):
    pair_rows = 8
    seqs_per_pair = pair_rows // dec_len
    n_kv = N_MEM * XA_HEADS
    q_rows = XA_HEADS * pair_rows
    own_head = (lax.broadcasted_iota(jnp.int32, (q_rows, n_kv), 1) % XA_HEADS
                == lax.broadcasted_iota(jnp.int32, (q_rows, n_kv), 0) // pair_rows)
    row8 = lax.broadcasted_iota(jnp.int32, (q_rows, XA_HD), 0) % pair_rows
    probs, outs = {}, {}

    def score(seq):
        prow = slice((seq // seqs_per_pair) * pair_rows, (seq // seqs_per_pair + 1) * pair_rows)
        q8 = q_ref[prow, :]
        qs = jnp.concatenate([q8[:, hd * XA_HD:(hd + 1) * XA_HD] for hd in range(XA_HEADS)],
                             axis=0).astype(BF16)
        kb = k_ref[seq].reshape(n_kv, XA_HD).astype(BF16)
        probs[seq] = _softmax_rows(jnp.where(own_head, _dot_nt(qs, kb), -jnp.inf)).astype(BF16)

    def attend(seq):
        p, j = divmod(seq, seqs_per_pair)
        oj = _dot(probs.pop(seq), v_ref[seq].reshape(n_kv, XA_HD).astype(BF16))
        mine = (row8 >= j * dec_len) & (row8 < (j + 1) * dec_len)
        outs[p] = jnp.where(mine, oj, outs[p]) if p in outs else oj
        if j == seqs_per_pair - 1:
            o = outs.pop(p)
            for hd in range(XA_HEADS):
                o_ref[p * pair_rows:(p + 1) * pair_rows, hd * XA_HD:(hd + 1) * XA_HD] = \
                    o[hd * pair_rows:(hd + 1) * pair_rows, :]

    return ([functools.partial(score, s) for s in range(n_seqs)]
            + [functools.partial(attend, s) for s in range(n_seqs)])


def _gate_store(o, gain, gate, out_ref, rows, cols):
    out_ref[rows, cols] = (_head_norm(o) * gain * _silu(gate)).astype(BF16)


def _mix_kernel(dec_len, x_ref, gmix_ref, win_ref, rgain_ref, hgain_ref, hglb_ref, wout_ref,
                cos_ref, sin_ref, dec_ref, qdec_ref, kdec_ref, sdec_ref, qs_ref, ck_ref, cv_ref,
                x1_ref, sret_ref, shg_ref, oxs_ref, omix_scr):
    t = pl.program_id(1)

    @pl.when(t == 0)
    def _():
        sret_ref[...] = jnp.zeros_like(sret_ref)
        shg_ref[...] = jnp.zeros_like(shg_ref)

    x = x_ref[...]
    h = _rms(x, gmix_ref[...]).astype(BF16)
    side_work = _sample_xattn_stages(dec_len, ck_ref.shape[0], qs_ref, ck_ref, cv_ref, oxs_ref)
    groups = {}
    for g0 in (_HF, _HQ, _RQ, _RK, _RV, _HI, _RG, _HGATE):
        groups[g0] = _dot(h, win_ref[:, g0:g0 + RET_W])
        if side_work:
            side_work.pop(0)()
    while side_work:
        side_work.pop(0)()

    class _Proj:
        def __getitem__(self, idx):
            rows, cols = idx
            g0 = (cols.start // RET_W) * RET_W
            return groups[g0][rows, cols.start - g0:cols.stop - g0]

    proj = _Proj()

    ret_units = []
    for c in range(MIX_TOKENS // RET_CHUNK_LEN):
        rows = slice(c * RET_CHUNK_LEN, (c + 1) * RET_CHUNK_LEN)
        cos2 = cos_ref[rows, :]
        sin2 = sin_ref[rows, :]
        for hd in range(RET_HEADS):
            q = _rope(proj[rows, _RQ + hd * LANE:_RQ + (hd + 1) * LANE], cos2, sin2)
            k = _rope(proj[rows, _RK + hd * LANE:_RK + (hd + 1) * LANE], cos2, sin2) * (RET_DK ** -0.5)
            vb = proj[rows, _RV + hd * LANE:_RV + (hd + 1) * LANE].astype(BF16)
            att = (_dot_nt(q.astype(BF16), k.astype(BF16)) * dec_ref[hd]).astype(BF16)
            kv = _dot_tn((k * kdec_ref[hd]).astype(BF16), vb)
            lhs = jnp.concatenate([att, (q * qdec_ref[hd]).astype(BF16)], axis=1)
            ret_units.append((rows, hd, lhs, vb, kv))

    lower = _lower_bound(hglb_ref[...])
    n_sub = HG_CHUNK_LEN // HG_SUB
    crow = lax.broadcasted_iota(jnp.int32, (HG_CHUNK_LEN, HG_CHUNK_LEN), 0)
    ccol = lax.broadcasted_iota(jnp.int32, (HG_CHUNK_LEN, HG_CHUNK_LEN), 1)
    causal = crow >= ccol
    hg_units = []
    for c in range(MIX_TOKENS // HG_CHUNK_LEN):
        rows = slice(c * HG_CHUNK_LEN, (c + 1) * HG_CHUNK_LEN)
        f = lower + (1.0 - lower) * _sigmoid(proj[rows, _HF:_HF + HG_W])
        kk = 1.0 - f
        qq = _silu(proj[rows, _HQ:_HQ + HG_W])
        b = _cumsum_rows(jnp.log(f), HG_CHUNK_LEN)
        b_last = b[HG_CHUNK_LEN - 1:HG_CHUNK_LEN, :]
        q_inter = (qq * jnp.exp(b)).astype(BF16)
        k_upd = (kk * jnp.exp(b_last - b)).astype(BF16)
        d_last = jnp.exp(b_last)
        for hd in range(HG_HEADS):
            cols = slice(hd * LANE, (hd + 1) * LANE)
            bh = b[:, cols]
            qh = qq[:, cols]
            kh = kk[:, cols]
            vb = proj[rows, _HI + hd * LANE:_HI + (hd + 1) * LANE].astype(BF16)
            q_parts, k_parts = [], []
            for j in range(n_sub):
                lo, hi = j * HG_SUB, (j + 1) * HG_SUB
                ref = bh[lo + HG_SUB // 2 - 1:lo + HG_SUB // 2, :]
                qt = qh[lo:, :] * jnp.exp(bh[lo:, :] - ref)
                kt = kh[lo:hi, :] * jnp.exp(ref - bh[lo:hi, :])
                if lo:
                    qt = jnp.concatenate([jnp.zeros((lo, LANE), F32), qt], axis=0)
                    kt = jnp.concatenate([jnp.zeros((lo, LANE), F32), kt], axis=0)
                if hi < HG_CHUNK_LEN:
                    kt = jnp.concatenate([kt, jnp.zeros((HG_CHUNK_LEN - hi, LANE), F32)], axis=0)
                q_parts.append(qt.astype(BF16))
                k_parts.append(kt.astype(BF16))
            qcat = jnp.concatenate(q_parts, axis=1)
            kcat = jnp.concatenate(k_parts, axis=1)
            att = jnp.where(causal, _dot_nt(qcat, kcat), 0.0).astype(BF16)
            kv = _dot_tn(k_upd[:, cols], vb)
            lhs = jnp.concatenate([q_inter[:, cols], att], axis=1)
            hg_units.append((rows, hd, lhs, vb, kv, d_last[:, cols]))

    for hd in range(RET_HEADS):
        cols = slice(hd * LANE, (hd + 1) * LANE)
        s = sret_ref[0, hd]
        for rows, uh, lhs, vb, kv in ret_units:
            if uh != hd:
                continue
            o = _dot(lhs, jnp.concatenate([vb, s.astype(BF16)], axis=0))
            s = sdec_ref[hd] * s + kv
            gate = proj[rows, _RG + hd * LANE:_RG + (hd + 1) * LANE]
            _gate_store(o, rgain_ref[:, cols], gate, omix_scr, rows, cols)
        sret_ref[0, hd] = s

    for hd in range(HG_HEADS):
        cols = slice(hd * LANE, (hd + 1) * LANE)
        mine = [u for u in hg_units if u[1] == hd]
        dl = jnp.concatenate([u[5] for u in mine] + [jnp.zeros((LANE - len(mine), LANE), F32)], axis=0).T
        s = shg_ref[0, hd]
        for i, (rows, _, lhs, vb, kv, _) in enumerate(mine):
            o = _dot(lhs, jnp.concatenate([s.astype(BF16), vb], axis=0))
            s = s * dl[:, i:i + 1] + kv
            gate = proj[rows, _HGATE + hd * LANE:_HGATE + (hd + 1) * LANE]
            _gate_store(o, hgain_ref[:, cols], gate, omix_scr, rows,
                        slice(RET_W + hd * LANE, RET_W + (hd + 1) * LANE))
        shg_ref[0, hd] = s

    x1_ref[...] = x + _dot(omix_scr[...], wout_ref[...])


def _ret_tables(length, period):
    log_g = np.log(1.0 - 2.0 ** (-5.0 - np.arange(RET_HEADS, dtype=np.float64)))
    idx = np.arange(length)
    pos = idx % period
    rel = (idx[:, None] - idx[None, :]).astype(np.float64)
    same = (idx[:, None] // period) == (idx[None, :] // period)
    valid = (rel >= 0) & same
    dec = np.where(valid[None], np.exp(log_g[:, None, None] * np.where(valid, rel, 0.0)[None]), 0.0)
    qdec = np.exp(log_g[:, None] * (pos + 1.0))[:, :, None] * np.ones((1, 1, LANE))
    kdec = np.exp(log_g[:, None] * (period - 1.0 - pos))[:, :, None] * np.ones((1, 1, LANE))
    sdec = np.exp(log_g * period)[:, None, None] * np.ones((1, 1, LANE))
    as32 = lambda a: jnp.asarray(a, dtype=F32)
    return as32(dec), as32(qdec), as32(kdec), as32(sdec)


def _rope_tables(pos):
    half = RET_DK // 2
    inv_freq = ROPE_BASE ** (-jnp.arange(half, dtype=F32) / half)
    ang = pos[:, None] * inv_freq[None, :]
    cos, sin = jnp.cos(ang), jnp.sin(ang)
    return jnp.concatenate([cos, cos], axis=-1), jnp.concatenate([-sin, sin], axis=-1)


def _mix_prompt(x2d, batch, seq, g_mix, w_in, ret_gain, hg_gain, hg_lb, w_out,
                q_s, cache_k, cache_v, dec_len):
    nt = seq // MIX_TOKENS
    n_seq = cache_k.shape[0]
    seqs_per_step = n_seq // (batch * nt)
    assert seqs_per_step * batch * nt == n_seq and (seqs_per_step * dec_len) % 8 == 0
    srows = pl.BlockSpec((seqs_per_step * dec_len, D_MODEL), lambda b, t: (b * nt + t, 0))
    cblk = pl.BlockSpec((seqs_per_step, N_MEM, XA_HEADS, XA_HD), lambda b, t: (b * nt + t, 0, 0, 0))
    cos2, sin2 = _rope_tables(jnp.arange(seq, dtype=F32))
    dec, qdec, kdec, sdec = _ret_tables(RET_CHUNK_LEN, RET_CHUNK_LEN)
    c2 = lambda b, t: (0, 0)
    c3 = lambda b, t: (0, 0, 0)
    tok = pl.BlockSpec((MIX_TOKENS, D_MODEL), lambda b, t: (b * nt + t, 0))
    state = pl.BlockSpec((1, RET_HEADS, RET_DK, RET_DV), lambda b, t: (b, 0, 0, 0))
    return pl.pallas_call(
        functools.partial(_mix_kernel, dec_len),
        grid=(batch, nt),
        in_specs=[tok, pl.BlockSpec((1, D_MODEL), c2), _resident((D_MODEL, IN_W)),
                  pl.BlockSpec((1, RET_W), c2), pl.BlockSpec((1, HG_W), c2),
                  pl.BlockSpec(hg_lb.shape, c2), _resident((MIX_W, D_MODEL)),
                  pl.BlockSpec((MIX_TOKENS, LANE), lambda b, t: (t, 0)),
                  pl.BlockSpec((MIX_TOKENS, LANE), lambda b, t: (t, 0)),
                  pl.BlockSpec(dec.shape, c3), pl.BlockSpec(qdec.shape, c3),
                  pl.BlockSpec(kdec.shape, c3), pl.BlockSpec(sdec.shape, c3), srows, cblk, cblk],
        out_specs=[tok, state, state, srows],
        out_shape=[jax.ShapeDtypeStruct(x2d.shape, F32),
                   jax.ShapeDtypeStruct((batch, RET_HEADS, RET_DK, RET_DV), F32),
                   jax.ShapeDtypeStruct((batch, HG_HEADS, HG_DK, HG_DV), F32),
                   jax.ShapeDtypeStruct(q_s.shape, F32)],
        scratch_shapes=[pltpu.VMEM((MIX_TOKENS, MIX_W), BF16)],
        compiler_params=pltpu.CompilerParams(dimension_semantics=("arbitrary", "arbitrary"),
                                             vmem_limit_bytes=VMEM_LIMIT_BYTES),
        name="mix_prompt",
    )(x2d, g_mix, w_in, ret_gain, hg_gain, hg_lb, w_out, cos2, sin2, dec, qdec, kdec, sdec,
      q_s, cache_k, cache_v)


def _inv_rms(x):
    return lax.rsqrt(jnp.mean(x * x, axis=-1, keepdims=True) + EPS)


def _query(x1, gxa, wxq):
    return _dot((x1 * gxa).astype(BF16), wxq) * (_inv_rms(x1) * (XA_HD ** -0.5))


def _ffn_final(x1, ox, wxo, gffn, wgate, wup, wdown, gfinal):
    x2 = x1 + _dot(ox, wxo)
    hb = (x2 * gffn).astype(BF16)
    r = _inv_rms(x2)
    a = (_silu(_dot(hb, wgate) * r) * (_dot(hb, wup) * r)).astype(BF16)
    x3 = x2 + _dot(a, wdown)
    return _rms(x3, gfinal)


def _tail_kernel(x1_ref, mk_ref, mv_ref, gxa_ref, wxq_ref, wxo_ref, gffn_ref, wgate_ref, wup_ref,
                 wdown_ref, gfinal_ref, y_ref, ox_scr):
    x1 = x1_ref[...]
    q = _query(x1, gxa_ref[...], wxq_ref[...]).astype(BF16)
    heads = [slice(hd * XA_HD, (hd + 1) * XA_HD) for hd in range(XA_HEADS)]
    scores = [_dot_nt(q[:, cols], mk_ref[0, :, cols]) for cols in heads]
    probs = [_softmax_rows(s).astype(BF16) for s in scores]
    for cols, p in zip(heads, probs):
        ox_scr[:, cols] = _dot(p, mv_ref[0, :, cols]).astype(BF16)
    y_ref[...] = _ffn_final(x1, ox_scr[...], wxo_ref[...], gffn_ref[...], wgate_ref[...],
                            wup_ref[...], wdown_ref[...], gfinal_ref[...])


def _tail_prompt(x1, batch, seq, mkb, mvb, g_xa, w_xq, w_xo, g_ffn, w_gate, w_up, w_down, g_final):
    nt = seq // TAIL_TOKENS
    d_ff = w_gate.shape[1]
    tok = pl.BlockSpec((TAIL_TOKENS, D_MODEL), lambda b, t: (b * nt + t, 0))
    mem = pl.BlockSpec((1, N_MEM, D_MODEL), lambda b, t: (b, 0, 0))
    return pl.pallas_call(
        _tail_kernel,
        grid=(batch, nt),
        in_specs=[tok, mem, mem, _resident((1, D_MODEL)), _resident((D_MODEL, D_MODEL)),
                  _resident((D_MODEL, D_MODEL)), _resident((1, D_MODEL)),
                  _resident((D_MODEL, d_ff)), _resident((D_MODEL, d_ff)),
                  _resident((d_ff, D_MODEL)), _resident((1, D_MODEL))],
        out_specs=tok,
        out_shape=jax.ShapeDtypeStruct(x1.shape, F32),
        scratch_shapes=[pltpu.VMEM((TAIL_TOKENS, D_MODEL), BF16)],
        compiler_params=pltpu.CompilerParams(dimension_semantics=("arbitrary", "arbitrary"),
                                             vmem_limit_bytes=VMEM_LIMIT_BYTES),
        name="tail_prompt",
    )(x1, mkb, mvb, g_xa, w_xq, w_xo, g_ffn, w_gate, w_up, w_down, g_final)


def _proj_kernel(x_ref, g_ref, w_ref, o_ref):
    o_ref[...] = _dot(_rms(x_ref[...], g_ref[...]).astype(BF16), w_ref[...])


def _proj_sample(x2d, g_mix, w_in):
    n = x2d.shape[0]
    nb = IN_W // D_MODEL
    return pl.pallas_call(
        _proj_kernel,
        grid=(nb,),
        in_specs=[pl.BlockSpec((n, D_MODEL), lambda j: (0, 0)),
                  pl.BlockSpec((1, D_MODEL), lambda j: (0, 0)),
                  pl.BlockSpec((D_MODEL, D_MODEL), lambda j: (0, j))],
        out_specs=pl.BlockSpec((n, D_MODEL), lambda j: (0, j)),
        out_shape=jax.ShapeDtypeStruct((n, IN_W), F32),
        compiler_params=pltpu.CompilerParams(dimension_semantics=("arbitrary",),
                                             vmem_limit_bytes=VMEM_LIMIT_BYTES),
        name="proj_sample",
    )(x2d, g_mix, w_in)


def _state_kernel(dec_len, proj_ref, sret_ref, shg_ref, rgain_ref, hgain_ref, hglb_ref,
                  cos_ref, sin_ref, dec_ref, qdec_ref, kdec_ref, sdec_ref,
                  omix_ref, nret_ref, nhg_ref):
    n_rows = STATE_SEQS * dec_len
    pair_rows = 8
    seqs_per_pair = pair_rows // dec_len
    row8 = lax.broadcasted_iota(jnp.int32, (pair_rows, LANE), 0)
    rown = lax.broadcasted_iota(jnp.int32, (n_rows, n_rows), 0)
    coln = lax.broadcasted_iota(jnp.int32, (n_rows, n_rows), 1)
    same_seq_causal = (rown >= coln) & ((rown // dec_len) == (coln // dec_len))
    proj = proj_ref[...]
    cos2 = cos_ref[...]
    sin2 = sin_ref[...]

    def per_sequence(q_all, kd_all, vb_all, state_ref, new_ref, hd, scale_of):
        outs = []
        for p in range(n_rows // pair_rows):
            prow = slice(p * pair_rows, (p + 1) * pair_rows)
            q8, kd8, v8 = q_all[prow], kd_all[prow], vb_all[prow]
            acc = jnp.zeros((pair_rows, LANE), F32)
            for j in range(seqs_per_pair):
                seq = p * seqs_per_pair + j
                mine = (row8 >= j * dec_len) & (row8 < (j + 1) * dec_len)
                s0 = state_ref[seq, hd]
                acc = jnp.where(mine, _dot(q8, s0.astype(BF16)), acc)
                kz = jnp.where(mine, kd8, jnp.zeros_like(kd8))
                new_ref[seq, hd] = scale_of(seq) * s0 + _dot_tn(kz, v8)
            outs.append(acc)
        return jnp.concatenate(outs, axis=0)

    for hd in range(RET_HEADS):
        cols = slice(hd * LANE, (hd + 1) * LANE)
        q = _rope(proj[:, _RQ + hd * LANE:_RQ + (hd + 1) * LANE], cos2, sin2)
        k = _rope(proj[:, _RK + hd * LANE:_RK + (hd + 1) * LANE], cos2, sin2) * (RET_DK ** -0.5)
        vb = proj[:, _RV + hd * LANE:_RV + (hd + 1) * LANE].astype(BF16)
        gate = proj[:, _RG + hd * LANE:_RG + (hd + 1) * LANE]
        qb = q.astype(BF16)
        att = _dot_nt(qb, k.astype(BF16)) * dec_ref[hd]
        kd = (k * kdec_ref[hd]).astype(BF16)
        sdec = sdec_ref[hd]
        qs = per_sequence(qb, kd, vb, sret_ref, nret_ref, hd, lambda seq: sdec)
        o = _dot(att.astype(BF16), vb) + qdec_ref[hd] * qs
        _gate_store(o, rgain_ref[:, cols], gate, omix_ref, slice(None), cols)

    lower = _lower_bound(hglb_ref[...])
    f = lower + (1.0 - lower) * _sigmoid(proj[:, _HF:_HF + HG_W])
    kk = 1.0 - f
    qq = _silu(proj[:, _HQ:_HQ + HG_W])
    b = _cumsum_rows(jnp.log(f), dec_len)
    pos = lax.broadcasted_iota(jnp.int32, b.shape, 0) & (dec_len - 1)

    def spread(row_in_seq):
        picked = jnp.where(pos == row_in_seq, b, 0.0)
        out = picked
        for s in range(1, dec_len):
            out = out + jnp.where(pos == (row_in_seq + s), pltpu.roll(picked, s, 0), 0.0)
            out = out + jnp.where(pos == (row_in_seq - s), pltpu.roll(picked, n_rows - s, 0), 0.0)
        return out

    ref = spread(dec_len // 2 - 1)
    b_last = spread(dec_len - 1)
    q_intra = (qq * jnp.exp(b - ref)).astype(BF16)
    k_intra = (kk * jnp.exp(ref - b)).astype(BF16)
    q_inter = (qq * jnp.exp(b)).astype(BF16)
    k_upd = (kk * jnp.exp(b_last - b)).astype(BF16)
    d_last = jnp.exp(b_last)
    for hd in range(HG_HEADS):
        cols = slice(hd * LANE, (hd + 1) * LANE)
        vb = proj[:, _HI + hd * LANE:_HI + (hd + 1) * LANE].astype(BF16)
        gate = proj[:, _HGATE + hd * LANE:_HGATE + (hd + 1) * LANE]
        att = jnp.where(same_seq_causal, _dot_nt(q_intra[:, cols], k_intra[:, cols]), 0.0)
        dpad = jnp.concatenate([d_last[:, cols], jnp.zeros((LANE - n_rows, LANE), F32)], axis=0)
        dcol = dpad.T
        scale_of = lambda seq: dcol[:, seq * dec_len:seq * dec_len + 1]
        qs = per_sequence(q_inter[:, cols], k_upd[:, cols], vb, shg_ref, nhg_ref, hd, scale_of)
        o = _dot(att.astype(BF16), vb) + qs
        _gate_store(o, hgain_ref[:, cols], gate, omix_ref, slice(None),
                    slice(RET_W + hd * LANE, RET_W + (hd + 1) * LANE))


def _state_sample(proj, state_ret, state_hgrn, dec_len, ret_gain, hg_gain, hg_lb):
    n_seq = state_ret.shape[0]
    n_rows = STATE_SEQS * dec_len
    pos = jnp.tile(jnp.arange(dec_len, dtype=F32) + PAST_LEN, STATE_SEQS)
    cos2, sin2 = _rope_tables(pos)
    dec, qdec, kdec, sdec = _ret_tables(n_rows, dec_len)
    c2 = lambda i: (0, 0)
    c3 = lambda i: (0, 0, 0)
    state = pl.BlockSpec((STATE_SEQS, RET_HEADS, RET_DK, RET_DV), lambda i: (i, 0, 0, 0))
    return pl.pallas_call(
        functools.partial(_state_kernel, dec_len),
        grid=(n_seq // STATE_SEQS,),
        in_specs=[pl.BlockSpec((n_rows, IN_W), lambda i: (i, 0)), state, state,
                  pl.BlockSpec((1, RET_W), c2), pl.BlockSpec((1, HG_W), c2),
                  pl.BlockSpec(hg_lb.shape, c2),
                  pl.BlockSpec((n_rows, LANE), c2), pl.BlockSpec((n_rows, LANE), c2),
                  pl.BlockSpec(dec.shape, c3), pl.BlockSpec(qdec.shape, c3),
                  pl.BlockSpec(kdec.shape, c3), pl.BlockSpec(sdec.shape, c3)],
        out_specs=[pl.BlockSpec((n_rows, MIX_W), lambda i: (i, 0)), state, state],
        out_shape=[jax.ShapeDtypeStruct((proj.shape[0], MIX_W), BF16),
                   jax.ShapeDtypeStruct(state_ret.shape, F32),
                   jax.ShapeDtypeStruct(state_hgrn.shape, F32)],
        compiler_params=pltpu.CompilerParams(dimension_semantics=("arbitrary",),
                                             vmem_limit_bytes=VMEM_LIMIT_BYTES),
        name="state_sample",
    )(proj, state_ret, state_hgrn, ret_gain, hg_gain, hg_lb, cos2, sin2, dec, qdec, kdec, sdec)


def _outq_kernel(x_ref, omix_ref, wout_ref, gxa_ref, wxq_ref, x1_ref, q_ref):
    x1 = x_ref[...] + _dot(omix_ref[...], wout_ref[...])
    x1_ref[...] = x1
    q_ref[...] = _query(x1, gxa_ref[...], wxq_ref[...])


def _outq_sample(x2d, omix, w_out, g_xa, w_xq):
    n = x2d.shape[0]
    z = lambda i: (0, 0)
    return pl.pallas_call(
        _outq_kernel,
        grid=(1,),
        in_specs=[pl.BlockSpec((n, D_MODEL), z), pl.BlockSpec((n, MIX_W), z),
                  pl.BlockSpec((MIX_W, D_MODEL), z), pl.BlockSpec((1, D_MODEL), z),
                  pl.BlockSpec((D_MODEL, D_MODEL), z)],
        out_specs=[pl.BlockSpec((n, D_MODEL), z), pl.BlockSpec((n, D_MODEL), z)],
        out_shape=[jax.ShapeDtypeStruct((n, D_MODEL), F32), jax.ShapeDtypeStruct((n, D_MODEL), F32)],
        compiler_params=pltpu.CompilerParams(dimension_semantics=("arbitrary",),
                                             vmem_limit_bytes=VMEM_LIMIT_BYTES),
        name="outq_sample",
    )(x2d, omix, w_out, g_xa, w_xq)


def _post_kernel(x1_ref, ox_ref, wxo_ref, gffn_ref, wgate_ref, wup_ref, wdown_ref, gfinal_ref, y_ref):
    y_ref[...] = _ffn_final(x1_ref[...], ox_ref[...].astype(BF16), wxo_ref[...], gffn_ref[...],
                            wgate_ref[...], wup_ref[...], wdown_ref[...], gfinal_ref[...])


def _post_sample(x1, ox, w_xo, g_ffn, w_gate, w_up, w_down, g_final):
    n = x1.shape[0]
    d_ff = w_gate.shape[1]
    return pl.pallas_call(
        _post_kernel,
        grid=(1,),
        in_specs=[_resident((n, D_MODEL)), _resident((n, D_MODEL)), _resident((D_MODEL, D_MODEL)),
                  _resident((1, D_MODEL)), _resident((D_MODEL, d_ff)), _resident((D_MODEL, d_ff)),
                  _resident((d_ff, D_MODEL)), _resident((1, D_MODEL))],
        out_specs=pl.BlockSpec((n, D_MODEL), lambda i: (0, 0)),
        out_shape=jax.ShapeDtypeStruct((n, D_MODEL), F32),
        compiler_params=pltpu.CompilerParams(dimension_semantics=("arbitrary",),
                                             vmem_limit_bytes=VMEM_LIMIT_BYTES),
        name="post_sample",
    )(x1, ox, w_xo, g_ffn, w_gate, w_up, w_down, g_final)


def kernel(x_prompt, x_sample, mem_prompt, state_ret, state_hgrn, cache_mem_k, cache_mem_v, g_mix, w_in,
           ret_gain, hg_gain, hg_lb, w_out, g_xa, g_mem, w_xq, w_xk, w_xv, w_xo, g_ffn, w_gate, w_up,
           w_down, g_final):
    depth = w_in.shape[0]
    assert depth == 1, "single-layer step"
    batch, seq, d = x_prompt.shape
    dec_batch, dec_len, _ = x_sample.shape
    assert d == D_MODEL and seq % MIX_TOKENS == 0 and seq % TAIL_TOKENS == 0
    assert 8 % dec_len == 0 and dec_batch % STATE_SEQS == 0

    bf = lambda w: w[0].astype(BF16)
    w_in_b, w_out_b, w_xq_b, w_xk_b, w_xv_b, w_xo_b = map(bf, (w_in, w_out, w_xq, w_xk, w_xv, w_xo))
    w_gate_b, w_up_b, w_down_b = map(bf, (w_gate, w_up, w_down))
    g_final2 = g_final.reshape(1, D_MODEL)

    xs = x_sample.reshape(dec_batch * dec_len, D_MODEL)
    proj_s = _proj_sample(xs, g_mix, w_in_b)
    omix_s, ret_s, hg_s = _state_sample(proj_s, state_ret[0], state_hgrn[0], dec_len,
                                        ret_gain, hg_gain, hg_lb)
    x1_s, q_s = _outq_sample(xs, omix_s, w_out_b, g_xa, w_xq_b)

    mk, mv, mkb, mvb = _memkv(mem_prompt.reshape(batch * N_MEM, D_MODEL), g_mem, w_xk_b, w_xv_b)
    x1_p, ret_p, hg_p, ox_s = _mix_prompt(x_prompt.reshape(batch * seq, D_MODEL), batch, seq, g_mix,
                                          w_in_b, ret_gain, hg_gain, hg_lb, w_out_b,
                                          q_s, cache_mem_k[0], cache_mem_v[0], dec_len)
    y_p = _tail_prompt(x1_p, batch, seq, mkb.reshape(batch, N_MEM, D_MODEL),
                       mvb.reshape(batch, N_MEM, D_MODEL), g_xa, w_xq_b, w_xo_b, g_ffn,
                       w_gate_b, w_up_b, w_down_b, g_final2)

    y_s = _post_sample(x1_s, ox_s, w_xo_b, g_ffn, w_gate_b, w_up_b, w_down_b, g_final2)

    kv_shape = (depth, batch, N_MEM, XA_HEADS, XA_HD)
    return (y_p.reshape(batch, seq, D_MODEL), y_s.reshape(dec_batch, dec_len, D_MODEL),
            ret_p[None], hg_p[None], mk.reshape(kv_shape), mv.reshape(kv_shape),
            ret_s[None], hg_s[None])
```

```python
import functools

import jax
import jax.numpy as jnp
import numpy as np
from jax import lax
from jax.experimental import pallas as pl
from jax.experimental.pallas import tpu as pltpu

D_MODEL = 1024
RET_HEADS = 4
RET_DK = 128
RET_DV = 128
RET_W = RET_HEADS * RET_DV
HG_HEADS = 4
HG_DK = 128
HG_DV = 128
HG_W = HG_HEADS * HG_DV
MIX_W = RET_W + HG_W
IN_W = 2 * RET_HEADS * RET_DK + 2 * RET_W + 2 * HG_HEADS * HG_DK + 2 * HG_W
N_MEM = 256
XA_HEADS = 4
XA_HD = D_MODEL // XA_HEADS
PAST_LEN = 16384
ROPE_BASE = 10000.0
EPS = 1e-6

_RQ, _RK, _RV, _RG = 0, 512, 1024, 1536
_HQ, _HF, _HI, _HGATE = 2048, 2560, 3072, 3584

LANE = 128
MIX_TOKENS = 512
RET_CHUNK_LEN = 128
HG_CHUNK_LEN = 64
HG_SUB = 16
TAIL_TOKENS = 512
MEMKV_ROWS = 512
STATE_SEQS = 16
VMEM_LIMIT_BYTES = 56 * 1024 * 1024

F32 = jnp.float32
BF16 = jnp.bfloat16


def _dot(a, b):
    return jnp.dot(a, b, preferred_element_type=F32)


def _dot_nt(a, b):
    return lax.dot_general(a, b, (((1,), (1,)), ((), ())), preferred_element_type=F32)


def _dot_tn(a, b):
    return lax.dot_general(a, b, (((0,), (0,)), ((), ())), preferred_element_type=F32)


def _rms(x, g):
    ms = jnp.mean(x * x, axis=-1, keepdims=True)
    return x * lax.rsqrt(ms + EPS) * g


def _sigmoid(x):
    return 1.0 / (1.0 + jnp.exp(-x))


def _silu(x):
    return x * _sigmoid(x)


def _head_norm(o):
    return o * lax.rsqrt(jnp.mean(o * o, axis=-1, keepdims=True) + EPS)


def _softmax_rows(s):
    p = jnp.exp(s - jnp.max(s, axis=-1, keepdims=True))
    return p / jnp.sum(p, axis=-1, keepdims=True)


def _rope(x, cos2, sin2):
    return x * cos2 + pltpu.roll(x, x.shape[-1] // 2, 1) * sin2


def _lower_bound(hglb):
    m = jnp.max(hglb, axis=0, keepdims=True)
    e = jnp.exp(hglb - m)
    return e[0:1, :] / jnp.sum(e, axis=0, keepdims=True)


def _cumsum_rows(x, period):
    row = lax.broadcasted_iota(jnp.int32, x.shape, 0) & (period - 1)
    s = 1
    while s < period:
        x = x + jnp.where(row >= s, pltpu.roll(x, s, 0), 0.0)
        s *= 2
    return x


def _resident(shape):
    zeros = (0,) * len(shape)
    return pl.BlockSpec(shape, lambda *_: zeros, pipeline_mode=pl.Buffered(1))


def _memkv_kernel(mem_ref, g_ref, wk_ref, wv_ref, k_ref, v_ref, kb_ref, vb_ref):
    m = _rms(mem_ref[...], g_ref[...]).astype(BF16)
    k = _dot(m, wk_ref[...])
    v = _dot(m, wv_ref[...])
    for hd in range(XA_HEADS):
        cols = slice(hd * XA_HD, (hd + 1) * XA_HD)
        k_ref[:, hd, :] = k[:, cols]
        v_ref[:, hd, :] = v[:, cols]
    kb_ref[...] = k.astype(BF16)
    vb_ref[...] = v.astype(BF16)


def _memkv(mem2d, g_mem, w_xk, w_xv):
    n = mem2d.shape[0]
    full = lambda i: (0, 0)
    row = lambda i: (i, 0)
    blk = pl.BlockSpec((MEMKV_ROWS, D_MODEL), row)
    hblk = pl.BlockSpec((MEMKV_ROWS, XA_HEADS, XA_HD), lambda i: (i, 0, 0))
    return pl.pallas_call(
        _memkv_kernel,
        grid=(n // MEMKV_ROWS,),
        in_specs=[blk, pl.BlockSpec((1, D_MODEL), full),
                  pl.BlockSpec((D_MODEL, D_MODEL), full), pl.BlockSpec((D_MODEL, D_MODEL), full)],
        out_specs=[hblk, hblk, blk, blk],
        out_shape=[jax.ShapeDtypeStruct((n, XA_HEADS, XA_HD), F32),
                   jax.ShapeDtypeStruct((n, XA_HEADS, XA_HD), F32),
                   jax.ShapeDtypeStruct((n, D_MODEL), BF16), jax.ShapeDtypeStruct((n, D_MODEL), BF16)],
        compiler_params=pltpu.CompilerParams(dimension_semantics=("arbitrary",),
                                             vmem_limit_bytes=VMEM_LIMIT_BYTES),
        name="memkv",
    )(mem2d, g_mem, w_xk, w_xv)


def _sample_xattn_stages(dec_len, n_seqs, q_ref, k_ref, v_ref, o_ref):
    pair_rows = 8
    seqs_per_pair = pair_rows // dec_len
    n_kv = N_MEM * XA_HEADS
    q_rows = XA_HEADS * pair_rows
    own_head = (lax.broadcasted_iota(jnp.int32, (q_rows, n_kv), 1) % XA_HEADS
                == lax.broadcasted_iota(jnp.int32, (q_rows, n_kv), 0) // pair_rows)
    row8 = lax.broadcasted_iota(jnp.int32, (q_rows, XA_HD), 0) % pair_rows
    probs, outs = {}, {}

    def score(seq):
        prow = slice((seq // seqs_per_pair) * pair_rows, (seq // seqs_per_pair + 1) * pair_rows)
        q8 = q_ref[prow, :]
        qs = jnp.concatenate([q8[:, hd * XA_HD:(hd + 1) * XA_HD] for hd in range(XA_HEADS)],
                             axis=0).astype(BF16)
        kb = k_ref[seq].reshape(n_kv, XA_HD).astype(BF16)
        probs[seq] = _softmax_rows(jnp.where(own_head, _dot_nt(qs, kb), -jnp.inf)).astype(BF16)

    def attend(seq):
        p, j = divmod(seq, seqs_per_pair)
        oj = _dot(probs.pop(seq), v_ref[seq].reshape(n_kv, XA_HD).astype(BF16))
        mine = (row8 >= j * dec_len) & (row8 < (j + 1) * dec_len)
        outs[p] = jnp.where(mine, oj, outs[p]) if p in outs else oj
        if j == seqs_per_pair - 1:
            o = outs.pop(p)
            for hd in range(XA_HEADS):
                o_ref[p * pair_rows:(p + 1) * pair_rows, hd * XA_HD:(hd + 1) * XA_HD] = \
                    o[hd * pair_rows:(hd + 1) * pair_rows, :]

    return ([functools.partial(score, s) for s in range(n_seqs)]
            + [functools.partial(attend, s) for s in range(n_seqs)])


def _gate_store(o, gain, gate, out_ref, rows, cols):
    out_ref[rows, cols] = (_head_norm(o) * gain * _silu(gate)).astype(BF16)


def _mix_kernel(dec_len, x_ref, gmix_ref, win_ref, rgain_ref, hgain_ref, hglb_ref, wout_ref,
                cos_ref, sin_ref, dec_ref, qdec_ref, kdec_ref, sdec_ref, qs_ref, ck_ref, cv_ref,
                x1_ref, sret_ref, shg_ref, oxs_ref, omix_scr):
    t = pl.program_id(1)

    @pl.when(t == 0)
    def _():
        sret_ref[...] = jnp.zeros_like(sret_ref)
        shg_ref[...] = jnp.zeros_like(shg_ref)

    x = x_ref[...]
    h = _rms(x, gmix_ref[...]).astype(BF16)
    side_work = _sample_xattn_stages(dec_len, ck_ref.shape[0], qs_ref, ck_ref, cv_ref, oxs_ref)
    groups = {}
    for g0 in (_HF, _HQ, _RQ, _RK, _RV, _HI, _RG, _HGATE):
        groups[g0] = _dot(h, win_ref[:, g0:g0 + RET_W])
        if side_work:
            side_work.pop(0)()
    while side_work:
        side_work.pop(0)()

    class _Proj:
        def __getitem__(self, idx):
            rows, cols = idx
            g0 = (cols.start // RET_W) * RET_W
            return groups[g0][rows, cols.start - g0:cols.stop - g0]

    proj = _Proj()

    ret_units = []
    for c in range(MIX_TOKENS // RET_CHUNK_LEN):
        rows = slice(c * RET_CHUNK_LEN, (c + 1) * RET_CHUNK_LEN)
        cos2 = cos_ref[rows, :]
        sin2 = sin_ref[rows, :]
        for hd in range(RET_HEADS):
            q = _rope(proj[rows, _RQ + hd * LANE:_RQ + (hd + 1) * LANE], cos2, sin2)
            k = _rope(proj[rows, _RK + hd * LANE:_RK + (hd + 1) * LANE], cos2, sin2) * (RET_DK ** -0.5)
            vb = proj[rows, _RV + hd * LANE:_RV + (hd + 1) * LANE].astype(BF16)
            att = (_dot_nt(q.astype(BF16), k.astype(BF16)) * dec_ref[hd]).astype(BF16)
            kv = _dot_tn((k * kdec_ref[hd]).astype(BF16), vb)
            lhs = jnp.concatenate([att, (q * qdec_ref[hd]).astype(BF16)], axis=1)
            ret_units.append((rows, hd, lhs, vb, kv))

    lower = _lower_bound(hglb_ref[...])
    n_sub = HG_CHUNK_LEN // HG_SUB
    crow = lax.broadcasted_iota(jnp.int32, (HG_CHUNK_LEN, HG_CHUNK_LEN), 0)
    ccol = lax.broadcasted_iota(jnp.int32, (HG_CHUNK_LEN, HG_CHUNK_LEN), 1)
    causal = crow >= ccol
    hg_units = []
    for c in range(MIX_TOKENS // HG_CHUNK_LEN):
        rows = slice(c * HG_CHUNK_LEN, (c + 1) * HG_CHUNK_LEN)
        f = lower + (1.0 - lower) * _sigmoid(proj[rows, _HF:_HF + HG_W])
        kk = 1.0 - f
        qq = _silu(proj[rows, _HQ:_HQ + HG_W])
        b = _cumsum_rows(jnp.log(f), HG_CHUNK_LEN)
        b_last = b[HG_CHUNK_LEN - 1:HG_CHUNK_LEN, :]
        q_inter = (qq * jnp.exp(b)).astype(BF16)
        k_upd = (kk * jnp.exp(b_last - b)).astype(BF16)
        d_last = jnp.exp(b_last)
        for hd in range(HG_HEADS):
            cols = slice(hd * LANE, (hd + 1) * LANE)
            bh = b[:, cols]
            qh = qq[:, cols]
            kh = kk[:, cols]
            vb = proj[rows, _HI + hd * LANE:_HI + (hd + 1) * LANE].astype(BF16)
            q_parts, k_parts = [], []
            for j in range(n_sub):
                lo, hi = j * HG_SUB, (j + 1) * HG_SUB
                ref = bh[lo + HG_SUB // 2 - 1:lo + HG_SUB // 2, :]
                qt = qh[lo:, :] * jnp.exp(bh[lo:, :] - ref)
                kt = kh[lo:hi, :] * jnp.exp(ref - bh[lo:hi, :])
                if lo:
                    qt = jnp.concatenate([jnp.zeros((lo, LANE), F32), qt], axis=0)
                    kt = jnp.concatenate([jnp.zeros((lo, LANE), F32), kt], axis=0)
                if hi < HG_CHUNK_LEN:
                    kt = jnp.concatenate([kt, jnp.zeros((HG_CHUNK_LEN - hi, LANE), F32)], axis=0)
                q_parts.append(qt.astype(BF16))
                k_parts.append(kt.astype(BF16))
            qcat = jnp.concatenate(q_parts, axis=1)
            kcat = jnp.concatenate(k_parts, axis=1)
            att = jnp.where(causal, _dot_nt(qcat, kcat), 0.0).astype(BF16)
            kv = _dot_tn(k_upd[:, cols], vb)
            lhs = jnp.concatenate([q_inter[:, cols], att], axis=1)
            hg_units.append((rows, hd, lhs, vb, kv, d_last[:, cols]))

    for hd in range(RET_HEADS):
        cols = slice(hd * LANE, (hd + 1) * LANE)
        s = sret_ref[0, hd]
        for rows, uh, lhs, vb, kv in ret_units:
            if uh != hd:
                continue
            o = _dot(lhs, jnp.concatenate([vb, s.astype(BF16)], axis=0))
            s = sdec_ref[hd] * s + kv
            gate = proj[rows, _RG + hd * LANE:_RG + (hd + 1) * LANE]
            _gate_store(o, rgain_ref[:, cols], gate, omix_scr, rows, cols)
        sret_ref[0, hd] = s

    for hd in range(HG_HEADS):
        cols = slice(hd * LANE, (hd + 1) * LANE)
        mine = [u for u in hg_units if u[1] == hd]
        dl = jnp.concatenate([u[5] for u in mine] + [jnp.zeros((LANE - len(mine), LANE), F32)], axis=0).T
        s = shg_ref[0, hd]
        for i, (rows, _, lhs, vb, kv, _) in enumerate(mine):
            o = _dot(lhs, jnp.concatenate([s.astype(BF16), vb], axis=0))
            s = s * dl[:, i:i + 1] + kv
            gate = proj[rows, _HGATE + hd * LANE:_HGATE + (hd + 1) * LANE]
            _gate_store(o, hgain_ref[:, cols], gate, omix_scr, rows,
                        slice(RET_W + hd * LANE, RET_W + (hd + 1) * LANE))
        shg_ref[0, hd] = s

    x1_ref[...] = x + _dot(omix_scr[...], wout_ref[...])


def _ret_tables(length, period):
    log_g = np.log(1.0 - 2.0 ** (-5.0 - np.arange(RET_HEADS, dtype=np.float64)))
    idx = np.arange(length)
    pos = idx % period
    rel = (idx[:, None] - idx[None, :]).astype(np.float64)
    same = (idx[:, None] // period) == (idx[None, :] // period)
    valid = (rel >= 0) & same
    dec = np.where(valid[None], np.exp(log_g[:, None, None] * np.where(valid, rel, 0.0)[None]), 0.0)
    qdec = np.exp(log_g[:, None] * (pos + 1.0))[:, :, None] * np.ones((1, 1, LANE))
    kdec = np.exp(log_g[:, None] * (period - 1.0 - pos))[:, :, None] * np.ones((1, 1, LANE))
    sdec = np.exp(log_g * period)[:, None, None] * np.ones((1, 1, LANE))
    as32 = lambda a: jnp.asarray(a, dtype=F32)
    return as32(dec), as32(qdec), as32(kdec), as32(sdec)


def _rope_tables(pos):
    half = RET_DK // 2
    inv_freq = ROPE_BASE ** (-jnp.arange(half, dtype=F32) / half)
    ang = pos[:, None] * inv_freq[None, :]
    cos, sin = jnp.cos(ang), jnp.sin(ang)
    return jnp.concatenate([cos, cos], axis=-1), jnp.concatenate([-sin, sin], axis=-1)


def _mix_prompt(x2d, batch, seq, g_mix, w_in, ret_gain, hg_gain, hg_lb, w_out,
                q_s, cache_k, cache_v, dec_len):
    nt = seq // MIX_TOKENS
    n_seq = cache_k.shape[0]
    seqs_per_step = n_seq // (batch * nt)
    assert seqs_per_step * batch * nt == n_seq and (seqs_per_step * dec_len) % 8 == 0
    srows = pl.BlockSpec((seqs_per_step * dec_len, D_MODEL), lambda b, t: (b * nt + t, 0))
    cblk = pl.BlockSpec((seqs_per_step, N_MEM, XA_HEADS, XA_HD), lambda b, t: (b * nt + t, 0, 0, 0))
    cos2, sin2 = _rope_tables(jnp.arange(seq, dtype=F32))
    dec, qdec, kdec, sdec = _ret_tables(RET_CHUNK_LEN, RET_CHUNK_LEN)
    c2 = lambda b, t: (0, 0)
    c3 = lambda b, t: (0, 0, 0)
    tok = pl.BlockSpec((MIX_TOKENS, D_MODEL), lambda b, t: (b * nt + t, 0))
    state = pl.BlockSpec((1, RET_HEADS, RET_DK, RET_DV), lambda b, t: (b, 0, 0, 0))
    return pl.pallas_call(
        functools.partial(_mix_kernel, dec_len),
        grid=(batch, nt),
        in_specs=[tok, pl.BlockSpec((1, D_MODEL), c2), _resident((D_MODEL, IN_W)),
                  pl.BlockSpec((1, RET_W), c2), pl.BlockSpec((1, HG_W), c2),
                  pl.BlockSpec(hg_lb.shape, c2), _resident((MIX_W, D_MODEL)),
                  pl.BlockSpec((MIX_TOKENS, LANE), lambda b, t: (t, 0)),
                  pl.BlockSpec((MIX_TOKENS, LANE), lambda b, t: (t, 0)),
                  pl.BlockSpec(dec.shape, c3), pl.BlockSpec(qdec.shape, c3),
                  pl.BlockSpec(kdec.shape, c3), pl.BlockSpec(sdec.shape, c3), srows, cblk, cblk],
        out_specs=[tok, state, state, srows],
        out_shape=[jax.ShapeDtypeStruct(x2d.shape, F32),
                   jax.ShapeDtypeStruct((batch, RET_HEADS, RET_DK, RET_DV), F32),
                   jax.ShapeDtypeStruct((batch, HG_HEADS, HG_DK, HG_DV), F32),
                   jax.ShapeDtypeStruct(q_s.shape, F32)],
        scratch_shapes=[pltpu.VMEM((MIX_TOKENS, MIX_W), BF16)],
        compiler_params=pltpu.CompilerParams(dimension_semantics=("arbitrary", "arbitrary"),
                                             vmem_limit_bytes=VMEM_LIMIT_BYTES),
        name="mix_prompt",
    )(x2d, g_mix, w_in, ret_gain, hg_gain, hg_lb, w_out, cos2, sin2, dec, qdec, kdec, sdec,
      q_s, cache_k, cache_v)


def _inv_rms(x):
    return lax.rsqrt(jnp.mean(x * x, axis=-1, keepdims=True) + EPS)


def _query(x1, gxa, wxq):
    return _dot((x1 * gxa).astype(BF16), wxq) * (_inv_rms(x1) * (XA_HD ** -0.5))


def _ffn_final(x1, ox, wxo, gffn, wgate, wup, wdown, gfinal):
    x2 = x1 + _dot(ox, wxo)
    hb = (x2 * gffn).astype(BF16)
    r = _inv_rms(x2)
    a = (_silu(_dot(hb, wgate) * r) * (_dot(hb, wup) * r)).astype(BF16)
    x3 = x2 + _dot(a, wdown)
    return _rms(x3, gfinal)


def _tail_kernel(x1_ref, mk_ref, mv_ref, gxa_ref, wxq_ref, wxo_ref, gffn_ref, wgate_ref, wup_ref,
                 wdown_ref, gfinal_ref, y_ref, ox_scr):
    x1 = x1_ref[...]
    q = _query(x1, gxa_ref[...], wxq_ref[...]).astype(BF16)
    heads = [slice(hd * XA_HD, (hd + 1) * XA_HD) for hd in range(XA_HEADS)]
    scores = [_dot_nt(q[:, cols], mk_ref[0, :, cols]) for cols in heads]
    probs = [_softmax_rows(s).astype(BF16) for s in scores]
    for cols, p in zip(heads, probs):
        ox_scr[:, cols] = _dot(p, mv_ref[0, :, cols]).astype(BF16)
    y_ref[...] = _ffn_final(x1, ox_scr[...], wxo_ref[...], gffn_ref[...], wgate_ref[...],
                            wup_ref[...], wdown_ref[...], gfinal_ref[...])


def _tail_prompt(x1, batch, seq, mkb, mvb, g_xa, w_xq, w_xo, g_ffn, w_gate, w_up, w_down, g_final):
    nt = seq // TAIL_TOKENS
    d_ff = w_gate.shape[1]
    tok = pl.BlockSpec((TAIL_TOKENS, D_MODEL), lambda b, t: (b * nt + t, 0))
    mem = pl.BlockSpec((1, N_MEM, D_MODEL), lambda b, t: (b, 0, 0))
    return pl.pallas_call(
        _tail_kernel,
        grid=(batch, nt),
        in_specs=[tok, mem, mem, _resident((1, D_MODEL)), _resident((D_MODEL, D_MODEL)),
                  _resident((D_MODEL, D_MODEL)), _resident((1, D_MODEL)),
                  _resident((D_MODEL, d_ff)), _resident((D_MODEL, d_ff)),
                  _resident((d_ff, D_MODEL)), _resident((1, D_MODEL))],
        out_specs=tok,
        out_shape=jax.ShapeDtypeStruct(x1.shape, F32),
        scratch_shapes=[pltpu.VMEM((TAIL_TOKENS, D_MODEL), BF16)],
        compiler_params=pltpu.CompilerParams(dimension_semantics=("arbitrary", "arbitrary"),
                                             vmem_limit_bytes=VMEM_LIMIT_BYTES),
        name="tail_prompt",
    )(x1, mkb, mvb, g_xa, w_xq, w_xo, g_ffn, w_gate, w_up, w_down, g_final)


def _proj_kernel(x_ref, g_ref, w_ref, o_ref):
    o_ref[...] = _dot(_rms(x_ref[...], g_ref[...]).astype(BF16), w_ref[...])


def _proj_sample(x2d, g_mix, w_in):
    n = x2d.shape[0]
    nb = IN_W // D_MODEL
    return pl.pallas_call(
        _proj_kernel,
        grid=(nb,),
        in_specs=[pl.BlockSpec((n, D_MODEL), lambda j: (0, 0)),
                  pl.BlockSpec((1, D_MODEL), lambda j: (0, 0)),
                  pl.BlockSpec((D_MODEL, D_MODEL), lambda j: (0, j))],
        out_specs=pl.BlockSpec((n, D_MODEL), lambda j: (0, j)),
        out_shape=jax.ShapeDtypeStruct((n, IN_W), F32),
        compiler_params=pltpu.CompilerParams(dimension_semantics=("arbitrary",),
                                             vmem_limit_bytes=VMEM_LIMIT_BYTES),
        name="proj_sample",
    )(x2d, g_mix, w_in)


def _state_kernel(dec_len, proj_ref, sret_ref, shg_ref, rgain_ref, hgain_ref, hglb_ref,
                  cos_ref, sin_ref, dec_ref, qdec_ref, kdec_ref, sdec_ref,
                  omix_ref, nret_ref, nhg_ref):
    n_rows = STATE_SEQS * dec_len
    pair_rows = 8
    seqs_per_pair = pair_rows // dec_len
    row8 = lax.broadcasted_iota(jnp.int32, (pair_rows, LANE), 0)
    rown = lax.broadcasted_iota(jnp.int32, (n_rows, n_rows), 0)
    coln = lax.broadcasted_iota(jnp.int32, (n_rows, n_rows), 1)
    same_seq_causal = (rown >= coln) & ((rown // dec_len) == (coln // dec_len))
    proj = proj_ref[...]
    cos2 = cos_ref[...]
    sin2 = sin_ref[...]

    def per_sequence(q_all, kd_all, vb_all, state_ref, new_ref, hd, scale_of):
        outs = []
        for p in range(n_rows // pair_rows):
            prow = slice(p * pair_rows, (p + 1) * pair_rows)
            q8, kd8, v8 = q_all[prow], kd_all[prow], vb_all[prow]
            acc = jnp.zeros((pair_rows, LANE), F32)
            for j in range(seqs_per_pair):
                seq = p * seqs_per_pair + j
                mine = (row8 >= j * dec_len) & (row8 < (j + 1) * dec_len)
                s0 = state_ref[seq, hd]
                acc = jnp.where(mine, _dot(q8, s0.astype(BF16)), acc)
                kz = jnp.where(mine, kd8, jnp.zeros_like(kd8))
                new_ref[seq, hd] = scale_of(seq) * s0 + _dot_tn(kz, v8)
            outs.append(acc)
        return jnp.concatenate(outs, axis=0)

    for hd in range(RET_HEADS):
        cols = slice(hd * LANE, (hd + 1) * LANE)
        q = _rope(proj[:, _RQ + hd * LANE:_RQ + (hd + 1) * LANE], cos2, sin2)
        k = _rope(proj[:, _RK + hd * LANE:_RK + (hd + 1) * LANE], cos2, sin2) * (RET_DK ** -0.5)
        vb = proj[:, _RV + hd * LANE:_RV + (hd + 1) * LANE].astype(BF16)
        gate = proj[:, _RG + hd * LANE:_RG + (hd + 1) * LANE]
        qb = q.astype(BF16)
        att = _dot_nt(qb, k.astype(BF16)) * dec_ref[hd]
        kd = (k * kdec_ref[hd]).astype(BF16)
        sdec = sdec_ref[hd]
        qs = per_sequence(qb, kd, vb, sret_ref, nret_ref, hd, lambda seq: sdec)
        o = _dot(att.astype(BF16), vb) + qdec_ref[hd] * qs
        _gate_store(o, rgain_ref[:, cols], gate, omix_ref, slice(None), cols)

    lower = _lower_bound(hglb_ref[...])
    f = lower + (1.0 - lower) * _sigmoid(proj[:, _HF:_HF + HG_W])
    kk = 1.0 - f
    qq = _silu(proj[:, _HQ:_HQ + HG_W])
    b = _cumsum_rows(jnp.log(f), dec_len)
    pos = lax.broadcasted_iota(jnp.int32, b.shape, 0) & (dec_len - 1)

    def spread(row_in_seq):
        picked = jnp.where(pos == row_in_seq, b, 0.0)
        out = picked
        for s in range(1, dec_len):
            out = out + jnp.where(pos == (row_in_seq + s), pltpu.roll(picked, s, 0), 0.0)
            out = out + jnp.where(pos == (row_in_seq - s), pltpu.roll(picked, n_rows - s, 0), 0.0)
        return out

    ref = spread(dec_len // 2 - 1)
    b_last = spread(dec_len - 1)
    q_intra = (qq * jnp.exp(b - ref)).astype(BF16)
    k_intra = (kk * jnp.exp(ref - b)).astype(BF16)
    q_inter = (qq * jnp.exp(b)).astype(BF16)
    k_upd = (kk * jnp.exp(b_last - b)).astype(BF16)
    d_last = jnp.exp(b_last)
    for hd in range(HG_HEADS):
        cols = slice(hd * LANE, (hd + 1) * LANE)
        vb = proj[:, _HI + hd * LANE:_HI + (hd + 1) * LANE].astype(BF16)
        gate = proj[:, _HGATE + hd * LANE:_HGATE + (hd + 1) * LANE]
        att = jnp.where(same_seq_causal, _dot_nt(q_intra[:, cols], k_intra[:, cols]), 0.0)
        dpad = jnp.concatenate([d_last[:, cols], jnp.zeros((LANE - n_rows, LANE), F32)], axis=0)
        dcol = dpad.T
        scale_of = lambda seq: dcol[:, seq * dec_len:seq * dec_len + 1]
        qs = per_sequence(q_inter[:, cols], k_upd[:, cols], vb, shg_ref, nhg_ref, hd, scale_of)
        o = _dot(att.astype(BF16), vb) + qs
        _gate_store(o, hgain_ref[:, cols], gate, omix_ref, slice(None),
                    slice(RET_W + hd * LANE, RET_W + (hd + 1) * LANE))


def _state_sample(proj, state_ret, state_hgrn, dec_len, ret_gain, hg_gain, hg_lb):
    n_seq = state_ret.shape[0]
    n_rows = STATE_SEQS * dec_len
    pos = jnp.tile(jnp.arange(dec_len, dtype=F32) + PAST_LEN, STATE_SEQS)
    cos2, sin2 = _rope_tables(pos)
    dec, qdec, kdec, sdec = _ret_tables(n_rows, dec_len)
    c2 = lambda i: (0, 0)
    c3 = lambda i: (0, 0, 0)
    state = pl.BlockSpec((STATE_SEQS, RET_HEADS, RET_DK, RET_DV), lambda i: (i, 0, 0, 0))
    return pl.pallas_call(
        functools.partial(_state_kernel, dec_len),
        grid=(n_seq // STATE_SEQS,),
        in_specs=[pl.BlockSpec((n_rows, IN_W), lambda i: (i, 0)), state, state,
                  pl.BlockSpec((1, RET_W), c2), pl.BlockSpec((1, HG_W), c2),
                  pl.BlockSpec(hg_lb.shape, c2),
                  pl.BlockSpec((n_rows, LANE), c2), pl.BlockSpec((n_rows, LANE), c2),
                  pl.BlockSpec(dec.shape, c3), pl.BlockSpec(qdec.shape, c3),
                  pl.BlockSpec(kdec.shape, c3), pl.BlockSpec(sdec.shape, c3)],
        out_specs=[pl.BlockSpec((n_rows, MIX_W), lambda i: (i, 0)), state, state],
        out_shape=[jax.ShapeDtypeStruct((proj.shape[0], MIX_W), BF16),
                   jax.ShapeDtypeStruct(state_ret.shape, F32),
                   jax.ShapeDtypeStruct(state_hgrn.shape, F32)],
        compiler_params=pltpu.CompilerParams(dimension_semantics=("arbitrary",),
                                             vmem_limit_bytes=VMEM_LIMIT_BYTES),
        name="state_sample",
    )(proj, state_ret, state_hgrn, ret_gain, hg_gain, hg_lb, cos2, sin2, dec, qdec, kdec, sdec)


def _outq_kernel(x_ref, omix_ref, wout_ref, gxa_ref, wxq_ref, x1_ref, q_ref):
    x1 = x_ref[...] + _dot(omix_ref[...], wout_ref[...])
    x1_ref[...] = x1
    q_ref[...] = _query(x1, gxa_ref[...], wxq_ref[...])


def _outq_sample(x2d, omix, w_out, g_xa, w_xq):
    n = x2d.shape[0]
    z = lambda i: (0, 0)
    return pl.pallas_call(
        _outq_kernel,
        grid=(1,),
        in_specs=[pl.BlockSpec((n, D_MODEL), z), pl.BlockSpec((n, MIX_W), z),
                  pl.BlockSpec((MIX_W, D_MODEL), z), pl.BlockSpec((1, D_MODEL), z),
                  pl.BlockSpec((D_MODEL, D_MODEL), z)],
        out_specs=[pl.BlockSpec((n, D_MODEL), z), pl.BlockSpec((n, D_MODEL), z)],
        out_shape=[jax.ShapeDtypeStruct((n, D_MODEL), F32), jax.ShapeDtypeStruct((n, D_MODEL), F32)],
        compiler_params=pltpu.CompilerParams(dimension_semantics=("arbitrary",),
                                             vmem_limit_bytes=VMEM_LIMIT_BYTES),
        name="outq_sample",
    )(x2d, omix, w_out, g_xa, w_xq)


def _post_kernel(x1_ref, ox_ref, wxo_ref, gffn_ref, wgate_ref, wup_ref, wdown_ref, gfinal_ref, y_ref):
    y_ref[...] = _ffn_final(x1_ref[...], ox_ref[...].astype(BF16), wxo_ref[...], gffn_ref[...],
                            wgate_ref[...], wup_ref[...], wdown_ref[...], gfinal_ref[...])


def _post_sample(x1, ox, w_xo, g_ffn, w_gate, w_up, w_down, g_final):
    n = x1.shape[0]
    d_ff = w_gate.shape[1]
    return pl.pallas_call(
        _post_kernel,
        grid=(1,),
        in_specs=[_resident((n, D_MODEL)), _resident((n, D_MODEL)), _resident((D_MODEL, D_MODEL)),
                  _resident((1, D_MODEL)), _resident((D_MODEL, d_ff)), _resident((D_MODEL, d_ff)),
                  _resident((d_ff, D_MODEL)), _resident((1, D_MODEL))],
        out_specs=pl.BlockSpec((n, D_MODEL), lambda i: (0, 0)),
        out_shape=jax.ShapeDtypeStruct((n, D_MODEL), F32),
        compiler_params=pltpu.CompilerParams(dimension_semantics=("arbitrary",),
                                             vmem_limit_bytes=VMEM_LIMIT_BYTES),
        name="post_sample",
    )(x1, ox, w_xo, g_ffn, w_gate, w_up, w_down, g_final)


def kernel(x_prompt, x_sample, mem_prompt, state_ret, state_hgrn, cache_mem_k, cache_mem_v, g_mix, w_in,
           ret_gain, hg_gain, hg_lb, w_out, g_xa, g_mem, w_xq, w_xk, w_xv, w_xo, g_ffn, w_gate, w_up,
           w_down, g_final):
    depth = w_in.shape[0]
    assert depth == 1, "single-layer step"
    batch, seq, d = x_prompt.shape
    dec_batch, dec_len, _ = x_sample.shape
    assert d == D_MODEL and seq % MIX_TOKENS == 0 and seq % TAIL_TOKENS == 0
    assert 8 % dec_len == 0 and dec_batch % STATE_SEQS == 0

    bf = lambda w: w[0].astype(BF16)
    w_in_b, w_out_b, w_xq_b, w_xk_b, w_xv_b, w_xo_b = map(bf, (w_in, w_out, w_xq, w_xk, w_xv, w_xo))
    w_gate_b, w_up_b, w_down_b = map(bf, (w_gate, w_up, w_down))
    g_final2 = g_final.reshape(1, D_MODEL)

    xs = x_sample.reshape(dec_batch * dec_len, D_MODEL)
    proj_s = _proj_sample(xs, g_mix, w_in_b)
    omix_s, ret_s, hg_s = _state_sample(proj_s, state_ret[0], state_hgrn[0], dec_len,
                                        ret_gain, hg_gain, hg_lb)
    x1_s, q_s = _outq_sample(xs, omix_s, w_out_b, g_xa, w_xq_b)

    mk, mv, mkb, mvb = _memkv(mem_prompt.reshape(batch * N_MEM, D_MODEL), g_mem, w_xk_b, w_xv_b)
    x1_p, ret_p, hg_p, ox_s = _mix_prompt(x_prompt.reshape(batch * seq, D_MODEL), batch, seq, g_mix,
                                          w_in_b, ret_gain, hg_gain, hg_lb, w_out_b,
                                          q_s, cache_mem_k[0], cache_mem_v[0], dec_len)
    y_p = _tail_prompt(x1_p, batch, seq, mkb.reshape(batch, N_MEM, D_MODEL),
                       mvb.reshape(batch, N_MEM, D_MODEL), g_xa, w_xq_b, w_xo_b, g_ffn,
                       w_gate_b, w_up_b, w_down_b, g_final2)

    y_s = _post_sample(x1_s, ox_s, w_xo_b, g_ffn, w_gate_b, w_up_b, w_down_b, g_final2)

    kv_shape = (depth, batch, N_MEM, XA_HEADS, XA_HD)
    return (y_p.reshape(batch, seq, D_MODEL), y_s.reshape(dec_batch, dec_len, D_MODEL),
            ret_p[None], hg_p[None], mk.reshape(kv_shape), mv.reshape(kv_shape),
            ret_s[None], hg_s[None])
```

```python
import functools

import jax
import jax.numpy as jnp
import numpy as np
from jax import lax
from jax.experimental import pallas as pl
from jax.experimental.pallas import tpu as pltpu

D_MODEL = 1024
RET_HEADS = 4
RET_DK = 128
RET_DV = 128
RET_W = RET_HEADS * RET_DV
HG_HEADS = 4
HG_DK = 128
HG_DV = 128
HG_W = HG_HEADS * HG_DV
MIX_W = RET_W + HG_W
IN_W = 2 * RET_HEADS * RET_DK + 2 * RET_W + 2 * HG_HEADS * HG_DK + 2 * HG_W
N_MEM = 256
XA_HEADS = 4
XA_HD = D_MODEL // XA_HEADS
PAST_LEN = 16384
ROPE_BASE = 10000.0
EPS = 1e-6

_RQ, _RK, _RV, _RG = 0, 512, 1024, 1536
_HQ, _HF, _HI, _HGATE = 2048, 2560, 3072, 3584

LANE = 128
MIX_TOKENS = 512
RET_CHUNK_LEN = 128
HG_CHUNK_LEN = 64
HG_SUB = 16
TAIL_TOKENS = 512
MEMKV_ROWS = 512
STATE_SEQS = 16
POST_FF_CHUNK = 256
VMEM_LIMIT_BYTES = 56 * 1024 * 1024

F32 = jnp.float32
BF16 = jnp.bfloat16


def _dot(a, b):
    return jnp.dot(a, b, preferred_element_type=F32)


def _dot_nt(a, b):
    return lax.dot_general(a, b, (((1,), (1,)), ((), ())), preferred_element_type=F32)


def _dot_tn(a, b):
    return lax.dot_general(a, b, (((0,), (0,)), ((), ())), preferred_element_type=F32)


def _rms(x, g):
    ms = jnp.mean(x * x, axis=-1, keepdims=True)
    return x * lax.rsqrt(ms + EPS) * g


def _sigmoid(x):
    return 1.0 / (1.0 + jnp.exp(-x))


def _silu(x):
    return x * _sigmoid(x)


def _head_norm(o):
    return o * lax.rsqrt(jnp.mean(o * o, axis=-1, keepdims=True) + EPS)


def _softmax_rows(s):
    p = jnp.exp(s - jnp.max(s, axis=-1, keepdims=True))
    return p / jnp.sum(p, axis=-1, keepdims=True)


def _rope(x, cos2, sin2):
    return x * cos2 + pltpu.roll(x, x.shape[-1] // 2, 1) * sin2


def _lower_bound(hglb):
    m = jnp.max(hglb, axis=0, keepdims=True)
    e = jnp.exp(hglb - m)
    return e[0:1, :] / jnp.sum(e, axis=0, keepdims=True)


def _cumsum_rows(x, period):
    row = lax.broadcasted_iota(jnp.int32, x.shape, 0) & (period - 1)
    s = 1
    while s < period:
        x = x + jnp.where(row >= s, pltpu.roll(x, s, 0), 0.0)
        s *= 2
    return x


def _resident(shape):
    zeros = (0,) * len(shape)
    return pl.BlockSpec(shape, lambda *_: zeros, pipeline_mode=pl.Buffered(1))


def _memkv_kernel(mem_ref, g_ref, wk_ref, wv_ref, k_ref, v_ref, kb_ref, vb_ref):
    m = _rms(mem_ref[...], g_ref[...]).astype(BF16)
    k = _dot(m, wk_ref[...].astype(BF16))
    v = _dot(m, wv_ref[...].astype(BF16))
    for hd in range(XA_HEADS):
        cols = slice(hd * XA_HD, (hd + 1) * XA_HD)
        k_ref[:, hd, :] = k[:, cols]
        v_ref[:, hd, :] = v[:, cols]
    kb_ref[...] = k.astype(BF16)
    vb_ref[...] = v.astype(BF16)


def _memkv(mem2d, g_mem, w_xk, w_xv):
    n = mem2d.shape[0]
    full = lambda i: (0, 0)
    row = lambda i: (i, 0)
    blk = pl.BlockSpec((MEMKV_ROWS, D_MODEL), row)
    hblk = pl.BlockSpec((MEMKV_ROWS, XA_HEADS, XA_HD), lambda i: (i, 0, 0))
    return pl.pallas_call(
        _memkv_kernel,
        grid=(n // MEMKV_ROWS,),
        in_specs=[blk, pl.BlockSpec((1, D_MODEL), full),
                  _resident((D_MODEL, D_MODEL)), _resident((D_MODEL, D_MODEL))],
        out_specs=[hblk, hblk, blk, blk],
        out_shape=[jax.ShapeDtypeStruct((n, XA_HEADS, XA_HD), F32),
                   jax.ShapeDtypeStruct((n, XA_HEADS, XA_HD), F32),
                   jax.ShapeDtypeStruct((n, D_MODEL), BF16), jax.ShapeDtypeStruct((n, D_MODEL), BF16)],
        compiler_params=pltpu.CompilerParams(dimension_semantics=("arbitrary",),
                                             vmem_limit_bytes=VMEM_LIMIT_BYTES),
        name="memkv",
    )(mem2d, g_mem, w_xk, w_xv)


def _sample_xattn_stages(dec_len, n_seqs, q_ref, k_ref, v_ref, o_ref):
    pair_rows = 8
    seqs_per_pair = pair_rows // dec_len
    n_kv = N_MEM * XA_HEADS
    q_rows = XA_HEADS * pair_rows
    own_head = (lax.broadcasted_iota(jnp.int32, (q_rows, n_kv), 1) % XA_HEADS
                == lax.broadcasted_iota(jnp.int32, (q_rows, n_kv), 0) // pair_rows)
    row8 = lax.broadcasted_iota(jnp.int32, (q_rows, XA_HD), 0) % pair_rows
    probs, outs = {}, {}

    def score(seq):
        prow = slice((seq // seqs_per_pair) * pair_rows, (seq // seqs_per_pair + 1) * pair_rows)
        q8 = q_ref[prow, :]
        qs = jnp.concatenate([q8[:, hd * XA_HD:(hd + 1) * XA_HD] for hd in range(XA_HEADS)],
                             axis=0).astype(BF16)
        kb = k_ref[seq].reshape(n_kv, XA_HD).astype(BF16)
        probs[seq] = _softmax_rows(jnp.where(own_head, _dot_nt(qs, kb), -jnp.inf)).astype(BF16)

    def attend(seq):
        p, j = divmod(seq, seqs_per_pair)
        oj = _dot(probs.pop(seq), v_ref[seq].reshape(n_kv, XA_HD).astype(BF16))
        mine = (row8 >= j * dec_len) & (row8 < (j + 1) * dec_len)
        outs[p] = jnp.where(mine, oj, outs[p]) if p in outs else oj
        if j == seqs_per_pair - 1:
            o = outs.pop(p)
            for hd in range(XA_HEADS):
                o_ref[p * pair_rows:(p + 1) * pair_rows, hd * XA_HD:(hd + 1) * XA_HD] = \
                    o[hd * pair_rows:(hd + 1) * pair_rows, :]

    return ([functools.partial(score, s) for s in range(n_seqs)]
            + [functools.partial(attend, s) for s in range(n_seqs)])


def _gate_store(o, gain, gate, out_ref, rows, cols):
    out_ref[rows, cols] = (_head_norm(o) * gain * _silu(gate)).astype(BF16)


def _mix_kernel(dec_len, x_ref, gmix_ref, win_ref, rgain_ref, hgain_ref, hglb_ref, wout_ref,
                cos_ref, sin_ref, dec_ref, qdec_ref, kdec_ref, sdec_ref, qs_ref, ck_ref, cv_ref,
                x1_ref, sret_ref, shg_ref, oxs_ref, omix_scr):
    t = pl.program_id(1)

    @pl.when(t == 0)
    def _():
        sret_ref[...] = jnp.zeros_like(sret_ref)
        shg_ref[...] = jnp.zeros_like(shg_ref)

    x = x_ref[...]
    h = _rms(x, gmix_ref[...]).astype(BF16)
    side_work = _sample_xattn_stages(dec_len, ck_ref.shape[0], qs_ref, ck_ref, cv_ref, oxs_ref)
    groups = {}
    for g0 in (_HF, _HQ, _RQ, _RK, _RV, _HI, _RG, _HGATE):
        groups[g0] = _dot(h, win_ref[:, g0:g0 + RET_W])
        if side_work:
            side_work.pop(0)()
    while side_work:
        side_work.pop(0)()

    class _Proj:
        def __getitem__(self, idx):
            rows, cols = idx
            g0 = (cols.start // RET_W) * RET_W
            return groups[g0][rows, cols.start - g0:cols.stop - g0]

    proj = _Proj()

    ret_units = []
    for c in range(MIX_TOKENS // RET_CHUNK_LEN):
        rows = slice(c * RET_CHUNK_LEN, (c + 1) * RET_CHUNK_LEN)
        cos2 = cos_ref[rows, :]
        sin2 = sin_ref[rows, :]
        for hd in range(RET_HEADS):
            q = _rope(proj[rows, _RQ + hd * LANE:_RQ + (hd + 1) * LANE], cos2, sin2)
            k = _rope(proj[rows, _RK + hd * LANE:_RK + (hd + 1) * LANE], cos2, sin2) * (RET_DK ** -0.5)
            vb = proj[rows, _RV + hd * LANE:_RV + (hd + 1) * LANE].astype(BF16)
            att = (_dot_nt(q.astype(BF16), k.astype(BF16)) * dec_ref[hd]).astype(BF16)
            kv = _dot_tn((k * kdec_ref[hd]).astype(BF16), vb)
            lhs = jnp.concatenate([att, (q * qdec_ref[hd]).astype(BF16)], axis=1)
            ret_units.append((rows, hd, lhs, vb, kv))

    lower = _lower_bound(hglb_ref[...])
    n_sub = HG_CHUNK_LEN // HG_SUB
    crow = lax.broadcasted_iota(jnp.int32, (HG_CHUNK_LEN, HG_CHUNK_LEN), 0)
    ccol = lax.broadcasted_iota(jnp.int32, (HG_CHUNK_LEN, HG_CHUNK_LEN), 1)
    causal = crow >= ccol
    hg_units = []
    for c in range(MIX_TOKENS // HG_CHUNK_LEN):
        rows = slice(c * HG_CHUNK_LEN, (c + 1) * HG_CHUNK_LEN)
        f = lower + (1.0 - lower) * _sigmoid(proj[rows, _HF:_HF + HG_W])
        kk = 1.0 - f
        qq = _silu(proj[rows, _HQ:_HQ + HG_W])
        b = _cumsum_rows(jnp.log(f), HG_CHUNK_LEN)
        b_last = b[HG_CHUNK_LEN - 1:HG_CHUNK_LEN, :]
        q_inter = (qq * jnp.exp(b)).astype(BF16)
        k_upd = (kk * jnp.exp(b_last - b)).astype(BF16)
        d_last = jnp.exp(b_last)
        for hd in range(HG_HEADS):
            cols = slice(hd * LANE, (hd + 1) * LANE)
            bh = b[:, cols]
            qh = qq[:, cols]
            kh = kk[:, cols]
            vb = proj[rows, _HI + hd * LANE:_HI + (hd + 1) * LANE].astype(BF16)
            q_parts, k_parts = [], []
            for j in range(n_sub):
                lo, hi = j * HG_SUB, (j + 1) * HG_SUB
                ref = bh[lo + HG_SUB // 2 - 1:lo + HG_SUB // 2, :]
                qt = qh[lo:, :] * jnp.exp(bh[lo:, :] - ref)
                kt = kh[lo:hi, :] * jnp.exp(ref - bh[lo:hi, :])
                if lo:
                    qt = jnp.concatenate([jnp.zeros((lo, LANE), F32), qt], axis=0)
                    kt = jnp.concatenate([jnp.zeros((lo, LANE), F32), kt], axis=0)
                if hi < HG_CHUNK_LEN:
                    kt = jnp.concatenate([kt, jnp.zeros((HG_CHUNK_LEN - hi, LANE), F32)], axis=0)
                q_parts.append(qt.astype(BF16))
                k_parts.append(kt.astype(BF16))
            qcat = jnp.concatenate(q_parts, axis=1)
            kcat = jnp.concatenate(k_parts, axis=1)
            att = jnp.where(causal, _dot_nt(qcat, kcat), 0.0).astype(BF16)
            kv = _dot_tn(k_upd[:, cols], vb)
            lhs = jnp.concatenate([q_inter[:, cols], att], axis=1)
            hg_units.append((rows, hd, lhs, vb, kv, d_last[:, cols]))

    for hd in range(RET_HEADS):
        cols = slice(hd * LANE, (hd + 1) * LANE)
        s = sret_ref[0, hd]
        for rows, uh, lhs, vb, kv in ret_units:
            if uh != hd:
                continue
            o = _dot(lhs, jnp.concatenate([vb, s.astype(BF16)], axis=0))
            s = sdec_ref[hd] * s + kv
            gate = proj[rows, _RG + hd * LANE:_RG + (hd + 1) * LANE]
            _gate_store(o, rgain_ref[:, cols], gate, omix_scr, rows, cols)
        sret_ref[0, hd] = s

    for hd in range(HG_HEADS):
        cols = slice(hd * LANE, (hd + 1) * LANE)
        mine = [u for u in hg_units if u[1] == hd]
        dl = jnp.concatenate([u[5] for u in mine] + [jnp.zeros((LANE - len(mine), LANE), F32)], axis=0).T
        s = shg_ref[0, hd]
        for i, (rows, _, lhs, vb, kv, _) in enumerate(mine):
            o = _dot(lhs, jnp.concatenate([s.astype(BF16), vb], axis=0))
            s = s * dl[:, i:i + 1] + kv
            gate = proj[rows, _HGATE + hd * LANE:_HGATE + (hd + 1) * LANE]
            _gate_store(o, hgain_ref[:, cols], gate, omix_scr, rows,
                        slice(RET_W + hd * LANE, RET_W + (hd + 1) * LANE))
        shg_ref[0, hd] = s

    x1_ref[...] = x + _dot(omix_scr[...], wout_ref[...])


def _ret_tables(length, period):
    log_g = np.log(1.0 - 2.0 ** (-5.0 - np.arange(RET_HEADS, dtype=np.float64)))
    idx = np.arange(length)
    pos = idx % period
    rel = (idx[:, None] - idx[None, :]).astype(np.float64)
    same = (idx[:, None] // period) == (idx[None, :] // period)
    valid = (rel >= 0) & same
    dec = np.where(valid[None], np.exp(log_g[:, None, None] * np.where(valid, rel, 0.0)[None]), 0.0)
    qdec = np.exp(log_g[:, None] * (pos + 1.0))[:, :, None] * np.ones((1, 1, LANE))
    kdec = np.exp(log_g[:, None] * (period - 1.0 - pos))[:, :, None] * np.ones((1, 1, LANE))
    sdec = np.exp(log_g * period)[:, None, None] * np.ones((1, 1, LANE))
    as32 = lambda a: jnp.asarray(a, dtype=F32)
    return as32(dec), as32(qdec), as32(kdec), as32(sdec)


def _rope_tables(pos):
    half = RET_DK // 2
    inv_freq = ROPE_BASE ** (-jnp.arange(half, dtype=F32) / half)
    ang = pos[:, None] * inv_freq[None, :]
    cos, sin = jnp.cos(ang), jnp.sin(ang)
    return jnp.concatenate([cos, cos], axis=-1), jnp.concatenate([-sin, sin], axis=-1)


def _mix_prompt(x2d, batch, seq, g_mix, w_in, ret_gain, hg_gain, hg_lb, w_out,
                q_s, cache_k, cache_v, dec_len):
    nt = seq // MIX_TOKENS
    n_seq = cache_k.shape[0]
    seqs_per_step = n_seq // (batch * nt)
    assert seqs_per_step * batch * nt == n_seq and (seqs_per_step * dec_len) % 8 == 0
    srows = pl.BlockSpec((seqs_per_step * dec_len, D_MODEL), lambda b, t: (b * nt + t, 0))
    cblk = pl.BlockSpec((seqs_per_step, N_MEM, XA_HEADS, XA_HD), lambda b, t: (b * nt + t, 0, 0, 0))
    cos2, sin2 = _rope_tables(jnp.arange(seq, dtype=F32))
    dec, qdec, kdec, sdec = _ret_tables(RET_CHUNK_LEN, RET_CHUNK_LEN)
    c2 = lambda b, t: (0, 0)
    c3 = lambda b, t: (0, 0, 0)
    tok = pl.BlockSpec((MIX_TOKENS, D_MODEL), lambda b, t: (b * nt + t, 0))
    state = pl.BlockSpec((1, RET_HEADS, RET_DK, RET_DV), lambda b, t: (b, 0, 0, 0))
    return pl.pallas_call(
        functools.partial(_mix_kernel, dec_len),
        grid=(batch, nt),
        in_specs=[tok, pl.BlockSpec((1, D_MODEL), c2), _resident((D_MODEL, IN_W)),
                  pl.BlockSpec((1, RET_W), c2), pl.BlockSpec((1, HG_W), c2),
                  pl.BlockSpec(hg_lb.shape, c2), _resident((MIX_W, D_MODEL)),
                  pl.BlockSpec((MIX_TOKENS, LANE), lambda b, t: (t, 0)),
                  pl.BlockSpec((MIX_TOKENS, LANE), lambda b, t: (t, 0)),
                  pl.BlockSpec(dec.shape, c3), pl.BlockSpec(qdec.shape, c3),
                  pl.BlockSpec(kdec.shape, c3), pl.BlockSpec(sdec.shape, c3), srows, cblk, cblk],
        out_specs=[tok, state, state, srows],
        out_shape=[jax.ShapeDtypeStruct(x2d.shape, F32),
                   jax.ShapeDtypeStruct((batch, RET_HEADS, RET_DK, RET_DV), F32),
                   jax.ShapeDtypeStruct((batch, HG_HEADS, HG_DK, HG_DV), F32),
                   jax.ShapeDtypeStruct(q_s.shape, F32)],
        scratch_shapes=[pltpu.VMEM((MIX_TOKENS, MIX_W), BF16)],
        compiler_params=pltpu.CompilerParams(dimension_semantics=("arbitrary", "arbitrary"),
                                             vmem_limit_bytes=VMEM_LIMIT_BYTES),
        name="mix_prompt",
    )(x2d, g_mix, w_in, ret_gain, hg_gain, hg_lb, w_out, cos2, sin2, dec, qdec, kdec, sdec,
      q_s, cache_k, cache_v)


def _inv_rms(x):
    return lax.rsqrt(jnp.mean(x * x, axis=-1, keepdims=True) + EPS)


def _query(x1, gxa, wxq):
    return _dot((x1 * gxa).astype(BF16), wxq) * (_inv_rms(x1) * (XA_HD ** -0.5))


def _ffn_final(x1, ox, wxo, gffn, wgate, wup, wdown, gfinal):
    x2 = x1 + _dot(ox, wxo)
    hb = (x2 * gffn).astype(BF16)
    r = _inv_rms(x2)
    a = (_silu(_dot(hb, wgate) * r) * (_dot(hb, wup) * r)).astype(BF16)
    x3 = x2 + _dot(a, wdown)
    return _rms(x3, gfinal)


def _tail_kernel(x1_ref, mk_ref, mv_ref, gxa_ref, wxq_ref, wxo_ref, gffn_ref, wgate_ref, wup_ref,
                 wdown_ref, gfinal_ref, y_ref, ox_scr):
    x1 = x1_ref[...]
    q = _query(x1, gxa_ref[...], wxq_ref[...]).astype(BF16)
    heads = [slice(hd * XA_HD, (hd + 1) * XA_HD) for hd in range(XA_HEADS)]
    scores = [_dot_nt(q[:, cols], mk_ref[0, :, cols]) for cols in heads]
    probs = [_softmax_rows(s).astype(BF16) for s in scores]
    for cols, p in zip(heads, probs):
        ox_scr[:, cols] = _dot(p, mv_ref[0, :, cols]).astype(BF16)
    y_ref[...] = _ffn_final(x1, ox_scr[...], wxo_ref[...], gffn_ref[...], wgate_ref[...],
                            wup_ref[...], wdown_ref[...], gfinal_ref[...])


def _tail_prompt(x1, batch, seq, mkb, mvb, g_xa, w_xq, w_xo, g_ffn, w_gate, w_up, w_down, g_final):
    nt = seq // TAIL_TOKENS
    d_ff = w_gate.shape[1]
    tok = pl.BlockSpec((TAIL_TOKENS, D_MODEL), lambda b, t: (b * nt + t, 0))
    mem = pl.BlockSpec((1, N_MEM, D_MODEL), lambda b, t: (b, 0, 0))
    return pl.pallas_call(
        _tail_kernel,
        grid=(batch, nt),
        in_specs=[tok, mem, mem, _resident((1, D_MODEL)), _resident((D_MODEL, D_MODEL)),
                  _resident((D_MODEL, D_MODEL)), _resident((1, D_MODEL)),
                  _resident((D_MODEL, d_ff)), _resident((D_MODEL, d_ff)),
                  _resident((d_ff, D_MODEL)), _resident((1, D_MODEL))],
        out_specs=tok,
        out_shape=jax.ShapeDtypeStruct(x1.shape, F32),
        scratch_shapes=[pltpu.VMEM((TAIL_TOKENS, D_MODEL), BF16)],
        compiler_params=pltpu.CompilerParams(dimension_semantics=("arbitrary", "arbitrary"),
                                             vmem_limit_bytes=VMEM_LIMIT_BYTES),
        name="tail_prompt",
    )(x1, mkb, mvb, g_xa, w_xq, w_xo, g_ffn, w_gate, w_up, w_down, g_final)


def _proj_kernel(x_ref, g_ref, w_ref, o_ref, wb_ref):
    wb = w_ref[...].astype(BF16)
    wb_ref[...] = wb
    o_ref[...] = _dot(_rms(x_ref[...], g_ref[...]).astype(BF16), wb)


def _proj_sample(x2d, g_mix, w_in):
    n = x2d.shape[0]
    nb = IN_W // D_MODEL
    wblk = pl.BlockSpec((D_MODEL, D_MODEL), lambda j: (0, j))
    return pl.pallas_call(
        _proj_kernel,
        grid=(nb,),
        in_specs=[pl.BlockSpec((n, D_MODEL), lambda j: (0, 0)),
                  pl.BlockSpec((1, D_MODEL), lambda j: (0, 0)), wblk],
        out_specs=[pl.BlockSpec((n, D_MODEL), lambda j: (0, j)), wblk],
        out_shape=[jax.ShapeDtypeStruct((n, IN_W), F32), jax.ShapeDtypeStruct(w_in.shape, BF16)],
        compiler_params=pltpu.CompilerParams(dimension_semantics=("arbitrary",),
                                             vmem_limit_bytes=VMEM_LIMIT_BYTES),
        name="proj_sample",
    )(x2d, g_mix, w_in)


def _state_kernel(dec_len, proj_ref, sret_ref, shg_ref, rgain_ref, hgain_ref, hglb_ref,
                  cos_ref, sin_ref, dec_ref, qdec_ref, kdec_ref, sdec_ref,
                  omix_ref, nret_ref, nhg_ref):
    n_rows = STATE_SEQS * dec_len
    pair_rows = 8
    seqs_per_pair = pair_rows // dec_len
    row8 = lax.broadcasted_iota(jnp.int32, (pair_rows, LANE), 0)
    rown = lax.broadcasted_iota(jnp.int32, (n_rows, n_rows), 0)
    coln = lax.broadcasted_iota(jnp.int32, (n_rows, n_rows), 1)
    same_seq_causal = (rown >= coln) & ((rown // dec_len) == (coln // dec_len))
    proj = proj_ref[...]
    cos2 = cos_ref[...]
    sin2 = sin_ref[...]

    def per_sequence(q_all, kd_all, vb_all, state_ref, new_ref, hd, scale_of):
        outs = []
        for p in range(n_rows // pair_rows):
            prow = slice(p * pair_rows, (p + 1) * pair_rows)
            q8, kd8, v8 = q_all[prow], kd_all[prow], vb_all[prow]
            acc = jnp.zeros((pair_rows, LANE), F32)
            for j in range(seqs_per_pair):
                seq = p * seqs_per_pair + j
                mine = (row8 >= j * dec_len) & (row8 < (j + 1) * dec_len)
                s0 = state_ref[seq, hd]
                acc = jnp.where(mine, _dot(q8, s0.astype(BF16)), acc)
                kz = jnp.where(mine, kd8, jnp.zeros_like(kd8))
                new_ref[seq, hd] = scale_of(seq) * s0 + _dot_tn(kz, v8)
            outs.append(acc)
        return jnp.concatenate(outs, axis=0)

    for hd in range(RET_HEADS):
        cols = slice(hd * LANE, (hd + 1) * LANE)
        q = _rope(proj[:, _RQ + hd * LANE:_RQ + (hd + 1) * LANE], cos2, sin2)
        k = _rope(proj[:, _RK + hd * LANE:_RK + (hd + 1) * LANE], cos2, sin2) * (RET_DK ** -0.5)
        vb = proj[:, _RV + hd * LANE:_RV + (hd + 1) * LANE].astype(BF16)
        gate = proj[:, _RG + hd * LANE:_RG + (hd + 1) * LANE]
        qb = q.astype(BF16)
        att = _dot_nt(qb, k.astype(BF16)) * dec_ref[hd]
        kd = (k * kdec_ref[hd]).astype(BF16)
        sdec = sdec_ref[hd]
        qs = per_sequence(qb, kd, vb, sret_ref, nret_ref, hd, lambda seq: sdec)
        o = _dot(att.astype(BF16), vb) + qdec_ref[hd] * qs
        _gate_store(o, rgain_ref[:, cols], gate, omix_ref, slice(None), cols)

    lower = _lower_bound(hglb_ref[...])
    f = lower + (1.0 - lower) * _sigmoid(proj[:, _HF:_HF + HG_W])
    kk = 1.0 - f
    qq = _silu(proj[:, _HQ:_HQ + HG_W])
    b = _cumsum_rows(jnp.log(f), dec_len)
    pos = lax.broadcasted_iota(jnp.int32, b.shape, 0) & (dec_len - 1)

    def spread(row_in_seq):
        picked = jnp.where(pos == row_in_seq, b, 0.0)
        out = picked
        for s in range(1, dec_len):
            out = out + jnp.where(pos == (row_in_seq + s), pltpu.roll(picked, s, 0), 0.0)
            out = out + jnp.where(pos == (row_in_seq - s), pltpu.roll(picked, n_rows - s, 0), 0.0)
        return out

    ref = spread(dec_len // 2 - 1)
    b_last = spread(dec_len - 1)
    q_intra = (qq * jnp.exp(b - ref)).astype(BF16)
    k_intra = (kk * jnp.exp(ref - b)).astype(BF16)
    q_inter = (qq * jnp.exp(b)).astype(BF16)
    k_upd = (kk * jnp.exp(b_last - b)).astype(BF16)
    d_last = jnp.exp(b_last)
    for hd in range(HG_HEADS):
        cols = slice(hd * LANE, (hd + 1) * LANE)
        vb = proj[:, _HI + hd * LANE:_HI + (hd + 1) * LANE].astype(BF16)
        gate = proj[:, _HGATE + hd * LANE:_HGATE + (hd + 1) * LANE]
        att = jnp.where(same_seq_causal, _dot_nt(q_intra[:, cols], k_intra[:, cols]), 0.0)
        dpad = jnp.concatenate([d_last[:, cols], jnp.zeros((LANE - n_rows, LANE), F32)], axis=0)
        dcol = dpad.T
        scale_of = lambda seq: dcol[:, seq * dec_len:seq * dec_len + 1]
        qs = per_sequence(q_inter[:, cols], k_upd[:, cols], vb, shg_ref, nhg_ref, hd, scale_of)
        o = _dot(att.astype(BF16), vb) + qs
        _gate_store(o, hgain_ref[:, cols], gate, omix_ref, slice(None),
                    slice(RET_W + hd * LANE, RET_W + (hd + 1) * LANE))


def _state_sample(proj, state_ret, state_hgrn, dec_len, ret_gain, hg_gain, hg_lb):
    n_seq = state_ret.shape[0]
    n_rows = STATE_SEQS * dec_len
    pos = jnp.tile(jnp.arange(dec_len, dtype=F32) + PAST_LEN, STATE_SEQS)
    cos2, sin2 = _rope_tables(pos)
    dec, qdec, kdec, sdec = _ret_tables(n_rows, dec_len)
    c2 = lambda i: (0, 0)
    c3 = lambda i: (0, 0, 0)
    state = pl.BlockSpec((STATE_SEQS, RET_HEADS, RET_DK, RET_DV), lambda i: (i, 0, 0, 0))
    return pl.pallas_call(
        functools.partial(_state_kernel, dec_len),
        grid=(n_seq // STATE_SEQS,),
        in_specs=[pl.BlockSpec((n_rows, IN_W), lambda i: (i, 0)), state, state,
                  pl.BlockSpec((1, RET_W), c2), pl.BlockSpec((1, HG_W), c2),
                  pl.BlockSpec(hg_lb.shape, c2),
                  pl.BlockSpec((n_rows, LANE), c2), pl.BlockSpec((n_rows, LANE), c2),
                  pl.BlockSpec(dec.shape, c3), pl.BlockSpec(qdec.shape, c3),
                  pl.BlockSpec(kdec.shape, c3), pl.BlockSpec(sdec.shape, c3)],
        out_specs=[pl.BlockSpec((n_rows, MIX_W), lambda i: (i, 0)), state, state],
        out_shape=[jax.ShapeDtypeStruct((proj.shape[0], MIX_W), BF16),
                   jax.ShapeDtypeStruct(state_ret.shape, F32),
                   jax.ShapeDtypeStruct(state_hgrn.shape, F32)],
        compiler_params=pltpu.CompilerParams(dimension_semantics=("arbitrary",),
                                             vmem_limit_bytes=VMEM_LIMIT_BYTES),
        name="state_sample",
    )(proj, state_ret, state_hgrn, ret_gain, hg_gain, hg_lb, cos2, sin2, dec, qdec, kdec, sdec)


def _outq_kernel(x_ref, omix_ref, wout_ref, gxa_ref, wxq_ref, x1_ref, q_ref, woutb_ref, wxqb_ref):
    wout = wout_ref[...].astype(BF16)
    wxq = wxq_ref[...].astype(BF16)
    woutb_ref[...] = wout
    wxqb_ref[...] = wxq
    x1 = x_ref[...] + _dot(omix_ref[...], wout)
    x1_ref[...] = x1
    q_ref[...] = _query(x1, gxa_ref[...], wxq)


def _outq_sample(x2d, omix, w_out, g_xa, w_xq):
    n = x2d.shape[0]
    z = lambda i: (0, 0)
    return pl.pallas_call(
        _outq_kernel,
        grid=(1,),
        in_specs=[pl.BlockSpec((n, D_MODEL), z), pl.BlockSpec((n, MIX_W), z),
                  pl.BlockSpec((MIX_W, D_MODEL), z), pl.BlockSpec((1, D_MODEL), z),
                  pl.BlockSpec((D_MODEL, D_MODEL), z)],
        out_specs=[pl.BlockSpec((n, D_MODEL), z), pl.BlockSpec((n, D_MODEL), z),
                   pl.BlockSpec((MIX_W, D_MODEL), z), pl.BlockSpec((D_MODEL, D_MODEL), z)],
        out_shape=[jax.ShapeDtypeStruct((n, D_MODEL), F32), jax.ShapeDtypeStruct((n, D_MODEL), F32),
                   jax.ShapeDtypeStruct((MIX_W, D_MODEL), BF16),
                   jax.ShapeDtypeStruct((D_MODEL, D_MODEL), BF16)],
        compiler_params=pltpu.CompilerParams(dimension_semantics=("arbitrary",),
                                             vmem_limit_bytes=VMEM_LIMIT_BYTES),
        name="outq_sample",
    )(x2d, omix, w_out, g_xa, w_xq)


def _post_kernel(x1_ref, ox_ref, wxo_ref, gffn_ref, wgate_ref, wup_ref, wdown_ref, gfinal_ref,
                 y_ref, wxob_ref, wgateb_ref, wupb_ref, wdownb_ref, x2_scr, hb_scr, r_scr, acc_scr):
    c = pl.program_id(0)

    @pl.when(c == 0)
    def _():
        wxo = wxo_ref[...].astype(BF16)
        wxob_ref[...] = wxo
        x2 = x1_ref[...] + _dot(ox_ref[...].astype(BF16), wxo)
        x2_scr[...] = x2
        hb_scr[...] = (x2 * gffn_ref[...]).astype(BF16)
        r_scr[...] = jnp.broadcast_to(_inv_rms(x2), r_scr.shape)
        acc_scr[...] = jnp.zeros_like(acc_scr)

    wgate = wgate_ref[...].astype(BF16)
    wup = wup_ref[...].astype(BF16)
    wdown = wdown_ref[...].astype(BF16)
    wgateb_ref[...] = wgate
    wupb_ref[...] = wup
    wdownb_ref[...] = wdown
    hb = hb_scr[...]
    r = r_scr[:, 0:1]
    a = (_silu(_dot(hb, wgate) * r) * (_dot(hb, wup) * r)).astype(BF16)
    acc_scr[...] += _dot(a, wdown)

    @pl.when(c == pl.num_programs(0) - 1)
    def _():
        y_ref[...] = _rms(x2_scr[...] + acc_scr[...], gfinal_ref[...])


def _post_sample(x1, ox, w_xo, g_ffn, w_gate, w_up, w_down, g_final):
    n = x1.shape[0]
    d_ff = w_gate.shape[1]
    chunk = POST_FF_CHUNK
    assert d_ff % chunk == 0
    z = lambda c: (0, 0)
    colblk = pl.BlockSpec((D_MODEL, chunk), lambda c: (0, c))
    rowblk = pl.BlockSpec((chunk, D_MODEL), lambda c: (c, 0))
    full = pl.BlockSpec((n, D_MODEL), z)
    sq = pl.BlockSpec((D_MODEL, D_MODEL), z)
    vec = pl.BlockSpec((1, D_MODEL), z)
    return pl.pallas_call(
        _post_kernel,
        grid=(d_ff // chunk,),
        in_specs=[full, full, sq, vec, colblk, colblk, rowblk, vec],
        out_specs=[full, sq, colblk, colblk, rowblk],
        out_shape=[jax.ShapeDtypeStruct((n, D_MODEL), F32),
                   jax.ShapeDtypeStruct(w_xo.shape, BF16), jax.ShapeDtypeStruct(w_gate.shape, BF16),
                   jax.ShapeDtypeStruct(w_up.shape, BF16), jax.ShapeDtypeStruct(w_down.shape, BF16)],
        scratch_shapes=[pltpu.VMEM((n, D_MODEL), F32), pltpu.VMEM((n, D_MODEL), BF16),
                        pltpu.VMEM((n, LANE), F32), pltpu.VMEM((n, D_MODEL), F32)],
        compiler_params=pltpu.CompilerParams(dimension_semantics=("arbitrary",),
                                             vmem_limit_bytes=VMEM_LIMIT_BYTES),
        name="post_sample",
    )(x1, ox, w_xo, g_ffn, w_gate, w_up, w_down, g_final)


def kernel(x_prompt, x_sample, mem_prompt, state_ret, state_hgrn, cache_mem_k, cache_mem_v, g_mix, w_in,
           ret_gain, hg_gain, hg_lb, w_out, g_xa, g_mem, w_xq, w_xk, w_xv, w_xo, g_ffn, w_gate, w_up,
           w_down, g_final):
    depth = w_in.shape[0]
    assert depth == 1, "single-layer step"
    batch, seq, d = x_prompt.shape
    dec_batch, dec_len, _ = x_sample.shape
    assert d == D_MODEL and seq % MIX_TOKENS == 0 and seq % TAIL_TOKENS == 0
    assert 8 % dec_len == 0 and dec_batch % STATE_SEQS == 0

    g_final2 = g_final.reshape(1, D_MODEL)

    xs = x_sample.reshape(dec_batch * dec_len, D_MODEL)
    proj_s, w_in_b = _proj_sample(xs, g_mix, w_in[0])
    omix_s, ret_s, hg_s = _state_sample(proj_s, state_ret[0], state_hgrn[0], dec_len,
                                        ret_gain, hg_gain, hg_lb)
    x1_s, q_s, w_out_b, w_xq_b = _outq_sample(xs, omix_s, w_out[0], g_xa, w_xq[0])

    mk, mv, mkb, mvb = _memkv(mem_prompt.reshape(batch * N_MEM, D_MODEL), g_mem, w_xk[0], w_xv[0])
    x1_p, ret_p, hg_p, ox_s = _mix_prompt(x_prompt.reshape(batch * seq, D_MODEL), batch, seq, g_mix,
                                          w_in_b, ret_gain, hg_gain, hg_lb, w_out_b,
                                          q_s, cache_mem_k[0], cache_mem_v[0], dec_len)
    y_s, w_xo_b, w_gate_b, w_up_b, w_down_b = _post_sample(x1_s, ox_s, w_xo[0], g_ffn, w_gate[0],
                                                           w_up[0], w_down[0], g_final2)

    y_p = _tail_prompt(x1_p, batch, seq, mkb.reshape(batch, N_MEM, D_MODEL),
                       mvb.reshape(batch, N_MEM, D_MODEL), g_xa, w_xq_b, w_xo_b, g_ffn,
                       w_gate_b, w_up_b, w_down_b, g_final2)

    kv_shape = (depth, batch, N_MEM, XA_HEADS, XA_HD)
    return (y_p.reshape(batch, seq, D_MODEL), y_s.reshape(dec_batch, dec_len, D_MODEL),
            ret_p[None], hg_p[None], mk.reshape(kv_shape), mv.reshape(kv_shape),
            ret_s[None], hg_s[None])
```

```python
import functools

import jax
import jax.numpy as jnp
import numpy as np
from jax import lax
from jax.experimental import pallas as pl
from jax.experimental.pallas import tpu as pltpu

D_MODEL = 1024
RET_HEADS = 4
RET_DK = 128
RET_DV = 128
RET_W = RET_HEADS * RET_DV
HG_HEADS = 4
HG_DK = 128
HG_DV = 128
HG_W = HG_HEADS * HG_DV
MIX_W = RET_W + HG_W
IN_W = 2 * RET_HEADS * RET_DK + 2 * RET_W + 2 * HG_HEADS * HG_DK + 2 * HG_W
N_MEM = 256
XA_HEADS = 4
XA_HD = D_MODEL // XA_HEADS
PAST_LEN = 16384
ROPE_BASE = 10000.0
EPS = 1e-6

_RQ, _RK, _RV, _RG = 0, 512, 1024, 1536
_HQ, _HF, _HI, _HGATE = 2048, 2560, 3072, 3584

LANE = 128
MIX_TOKENS = 512
RET_CHUNK_LEN = 128
HG_CHUNK_LEN = 64
HG_SUB = 16
TAIL_TOKENS = 512
MEMKV_ROWS = 512
STATE_SEQS = 16
POST_FF_CHUNK = 256
VMEM_LIMIT_BYTES = 56 * 1024 * 1024

F32 = jnp.float32
BF16 = jnp.bfloat16


def _dot(a, b):
    return jnp.dot(a, b, preferred_element_type=F32)


def _dot_nt(a, b):
    return lax.dot_general(a, b, (((1,), (1,)), ((), ())), preferred_element_type=F32)


def _dot_tn(a, b):
    return lax.dot_general(a, b, (((0,), (0,)), ((), ())), preferred_element_type=F32)


def _rms(x, g):
    ms = jnp.mean(x * x, axis=-1, keepdims=True)
    return x * lax.rsqrt(ms + EPS) * g


def _sigmoid(x):
    return 1.0 / (1.0 + jnp.exp(-x))


def _silu(x):
    return x * _sigmoid(x)


def _head_norm(o):
    return o * lax.rsqrt(jnp.mean(o * o, axis=-1, keepdims=True) + EPS)


def _softmax_rows(s):
    p = jnp.exp(s - jnp.max(s, axis=-1, keepdims=True))
    return p / jnp.sum(p, axis=-1, keepdims=True)


def _rope(x, cos2, sin2):
    return x * cos2 + pltpu.roll(x, x.shape[-1] // 2, 1) * sin2


def _lower_bound(hglb):
    m = jnp.max(hglb, axis=0, keepdims=True)
    e = jnp.exp(hglb - m)
    return e[0:1, :] / jnp.sum(e, axis=0, keepdims=True)


def _cumsum_rows(x, period):
    row = lax.broadcasted_iota(jnp.int32, x.shape, 0) & (period - 1)
    s = 1
    while s < period:
        x = x + jnp.where(row >= s, pltpu.roll(x, s, 0), 0.0)
        s *= 2
    return x


def _resident(shape):
    zeros = (0,) * len(shape)
    return pl.BlockSpec(shape, lambda *_: zeros, pipeline_mode=pl.Buffered(1))


def _memkv_kernel(mem_ref, g_ref, wk_ref, wv_ref, k_ref, v_ref, kb_ref, vb_ref):
    m = _rms(mem_ref[...], g_ref[...]).astype(BF16)
    k = _dot(m, wk_ref[...].astype(BF16))
    v = _dot(m, wv_ref[...].astype(BF16))
    for hd in range(XA_HEADS):
        cols = slice(hd * XA_HD, (hd + 1) * XA_HD)
        k_ref[:, hd, :] = k[:, cols]
        v_ref[:, hd, :] = v[:, cols]
    kb_ref[...] = k.astype(BF16)
    vb_ref[...] = v.astype(BF16)


def _memkv(mem2d, g_mem, w_xk, w_xv):
    n = mem2d.shape[0]
    full = lambda i: (0, 0)
    row = lambda i: (i, 0)
    blk = pl.BlockSpec((MEMKV_ROWS, D_MODEL), row)
    hblk = pl.BlockSpec((MEMKV_ROWS, XA_HEADS, XA_HD), lambda i: (i, 0, 0))
    return pl.pallas_call(
        _memkv_kernel,
        grid=(n // MEMKV_ROWS,),
        in_specs=[blk, pl.BlockSpec((1, D_MODEL), full),
                  _resident((D_MODEL, D_MODEL)), _resident((D_MODEL, D_MODEL))],
        out_specs=[hblk, hblk, blk, blk],
        out_shape=[jax.ShapeDtypeStruct((n, XA_HEADS, XA_HD), F32),
                   jax.ShapeDtypeStruct((n, XA_HEADS, XA_HD), F32),
                   jax.ShapeDtypeStruct((n, D_MODEL), BF16), jax.ShapeDtypeStruct((n, D_MODEL), BF16)],
        compiler_params=pltpu.CompilerParams(dimension_semantics=("arbitrary",),
                                             vmem_limit_bytes=VMEM_LIMIT_BYTES),
        name="memkv",
    )(mem2d, g_mem, w_xk, w_xv)


def _sample_xattn_stages(dec_len, n_seqs, q_ref, k_ref, v_ref, o_ref):
    pair_rows = 8
    seqs_per_pair = pair_rows // dec_len
    n_kv = N_MEM * XA_HEADS
    q_rows = XA_HEADS * pair_rows
    own_head = (lax.broadcasted_iota(jnp.int32, (q_rows, n_kv), 1) % XA_HEADS
                == lax.broadcasted_iota(jnp.int32, (q_rows, n_kv), 0) // pair_rows)
    row8 = lax.broadcasted_iota(jnp.int32, (q_rows, XA_HD), 0) % pair_rows
    probs, outs = {}, {}

    def score(seq):
        prow = slice((seq // seqs_per_pair) * pair_rows, (seq // seqs_per_pair + 1) * pair_rows)
        q8 = q_ref[prow, :]
        qs = jnp.concatenate([q8[:, hd * XA_HD:(hd + 1) * XA_HD] for hd in range(XA_HEADS)],
                             axis=0).astype(BF16)
        kb = k_ref[seq].reshape(n_kv, XA_HD).astype(BF16)
        probs[seq] = _softmax_rows(jnp.where(own_head, _dot_nt(qs, kb), -jnp.inf)).astype(BF16)

    def attend(seq):
        p, j = divmod(seq, seqs_per_pair)
        oj = _dot(probs.pop(seq), v_ref[seq].reshape(n_kv, XA_HD).astype(BF16))
        mine = (row8 >= j * dec_len) & (row8 < (j + 1) * dec_len)
        outs[p] = jnp.where(mine, oj, outs[p]) if p in outs else oj
        if j == seqs_per_pair - 1:
            o = outs.pop(p)
            for hd in range(XA_HEADS):
                o_ref[p * pair_rows:(p + 1) * pair_rows, hd * XA_HD:(hd + 1) * XA_HD] = \
                    o[hd * pair_rows:(hd + 1) * pair_rows, :]

    return ([functools.partial(score, s) for s in range(n_seqs)]
            + [functools.partial(attend, s) for s in range(n_seqs)])


def _gate_store(o, gain, gate, out_ref, rows, cols):
    out_ref[rows, cols] = (_head_norm(o) * gain * _silu(gate)).astype(BF16)


def _mix_kernel(dec_len, x_ref, gmix_ref, win_ref, rgain_ref, hgain_ref, hglb_ref, wout_ref,
                cos_ref, sin_ref, dec_ref, qdec_ref, kdec_ref, sdec_ref, qs_ref, ck_ref, cv_ref,
                x1_ref, sret_ref, shg_ref, oxs_ref, omix_scr):
    t = pl.program_id(1)

    @pl.when(t == 0)
    def _():
        sret_ref[...] = jnp.zeros_like(sret_ref)
        shg_ref[...] = jnp.zeros_like(shg_ref)

    x = x_ref[...]
    h = _rms(x, gmix_ref[...]).astype(BF16)
    side_work = _sample_xattn_stages(dec_len, ck_ref.shape[0], qs_ref, ck_ref, cv_ref, oxs_ref)
    groups = {}
    for g0 in (_HF, _HQ, _RQ, _RK, _RV, _HI, _RG, _HGATE):
        groups[g0] = _dot(h, win_ref[:, g0:g0 + RET_W])
        if side_work:
            side_work.pop(0)()
    while side_work:
        side_work.pop(0)()

    class _Proj:
        def __getitem__(self, idx):
            rows, cols = idx
            g0 = (cols.start // RET_W) * RET_W
            return groups[g0][rows, cols.start - g0:cols.stop - g0]

    proj = _Proj()

    ret_units = []
    for c in range(MIX_TOKENS // RET_CHUNK_LEN):
        rows = slice(c * RET_CHUNK_LEN, (c + 1) * RET_CHUNK_LEN)
        cos2 = cos_ref[rows, :]
        sin2 = sin_ref[rows, :]
        for hd in range(RET_HEADS):
            q = _rope(proj[rows, _RQ + hd * LANE:_RQ + (hd + 1) * LANE], cos2, sin2)
            k = _rope(proj[rows, _RK + hd * LANE:_RK + (hd + 1) * LANE], cos2, sin2) * (RET_DK ** -0.5)
            vb = proj[rows, _RV + hd * LANE:_RV + (hd + 1) * LANE].astype(BF16)
            att = (_dot_nt(q.astype(BF16), k.astype(BF16)) * dec_ref[hd]).astype(BF16)
            kv = _dot_tn((k * kdec_ref[hd]).astype(BF16), vb)
            lhs = jnp.concatenate([att, (q * qdec_ref[hd]).astype(BF16)], axis=1)
            ret_units.append((rows, hd, lhs, vb, kv))

    lower = _lower_bound(hglb_ref[...])
    n_sub = HG_CHUNK_LEN // HG_SUB
    crow = lax.broadcasted_iota(jnp.int32, (HG_CHUNK_LEN, HG_CHUNK_LEN), 0)
    ccol = lax.broadcasted_iota(jnp.int32, (HG_CHUNK_LEN, HG_CHUNK_LEN), 1)
    causal = crow >= ccol
    hg_units = []
    for c in range(MIX_TOKENS // HG_CHUNK_LEN):
        rows = slice(c * HG_CHUNK_LEN, (c + 1) * HG_CHUNK_LEN)
        f = lower + (1.0 - lower) * _sigmoid(proj[rows, _HF:_HF + HG_W])
        kk = 1.0 - f
        qq = _silu(proj[rows, _HQ:_HQ + HG_W])
        b = _cumsum_rows(jnp.log(f), HG_CHUNK_LEN)
        b_last = b[HG_CHUNK_LEN - 1:HG_CHUNK_LEN, :]
        q_inter = (qq * jnp.exp(b)).astype(BF16)
        k_upd = (kk * jnp.exp(b_last - b)).astype(BF16)
        d_last = jnp.exp(b_last)
        for hd in range(HG_HEADS):
            cols = slice(hd * LANE, (hd + 1) * LANE)
            bh = b[:, cols]
            qh = qq[:, cols]
            kh = kk[:, cols]
            vb = proj[rows, _HI + hd * LANE:_HI + (hd + 1) * LANE].astype(BF16)
            q_parts, k_parts = [], []
            for j in range(n_sub):
                lo, hi = j * HG_SUB, (j + 1) * HG_SUB
                ref = bh[lo + HG_SUB // 2 - 1:lo + HG_SUB // 2, :]
                qt = qh[lo:, :] * jnp.exp(bh[lo:, :] - ref)
                kt = kh[lo:hi, :] * jnp.exp(ref - bh[lo:hi, :])
                if lo:
                    qt = jnp.concatenate([jnp.zeros((lo, LANE), F32), qt], axis=0)
                    kt = jnp.concatenate([jnp.zeros((lo, LANE), F32), kt], axis=0)
                if hi < HG_CHUNK_LEN:
                    kt = jnp.concatenate([kt, jnp.zeros((HG_CHUNK_LEN - hi, LANE), F32)], axis=0)
                q_parts.append(qt.astype(BF16))
                k_parts.append(kt.astype(BF16))
            qcat = jnp.concatenate(q_parts, axis=1)
            kcat = jnp.concatenate(k_parts, axis=1)
            att = jnp.where(causal, _dot_nt(qcat, kcat), 0.0).astype(BF16)
            kv = _dot_tn(k_upd[:, cols], vb)
            lhs = jnp.concatenate([q_inter[:, cols], att], axis=1)
            hg_units.append((rows, hd, lhs, vb, kv, d_last[:, cols]))

    for hd in range(RET_HEADS):
        cols = slice(hd * LANE, (hd + 1) * LANE)
        s = sret_ref[0, hd]
        for rows, uh, lhs, vb, kv in ret_units:
            if uh != hd:
                continue
            o = _dot(lhs, jnp.concatenate([vb, s.astype(BF16)], axis=0))
            s = sdec_ref[hd] * s + kv
            gate = proj[rows, _RG + hd * LANE:_RG + (hd + 1) * LANE]
            _gate_store(o, rgain_ref[:, cols], gate, omix_scr, rows, cols)
        sret_ref[0, hd] = s

    for hd in range(HG_HEADS):
        cols = slice(hd * LANE, (hd + 1) * LANE)
        mine = [u for u in hg_units if u[1] == hd]
        dl = jnp.concatenate([u[5] for u in mine] + [jnp.zeros((LANE - len(mine), LANE), F32)], axis=0).T
        s = shg_ref[0, hd]
        for i, (rows, _, lhs, vb, kv, _) in enumerate(mine):
            o = _dot(lhs, jnp.concatenate([s.astype(BF16), vb], axis=0))
            s = s * dl[:, i:i + 1] + kv
            gate = proj[rows, _HGATE + hd * LANE:_HGATE + (hd + 1) * LANE]
            _gate_store(o, hgain_ref[:, cols], gate, omix_scr, rows,
                        slice(RET_W + hd * LANE, RET_W + (hd + 1) * LANE))
        shg_ref[0, hd] = s

    x1_ref[...] = x + _dot(omix_scr[...], wout_ref[...])


def _ret_tables(length, period):
    log_g = np.log(1.0 - 2.0 ** (-5.0 - np.arange(RET_HEADS, dtype=np.float64)))
    idx = np.arange(length)
    pos = idx % period
    rel = (idx[:, None] - idx[None, :]).astype(np.float64)
    same = (idx[:, None] // period) == (idx[None, :] // period)
    valid = (rel >= 0) & same
    dec = np.where(valid[None], np.exp(log_g[:, None, None] * np.where(valid, rel, 0.0)[None]), 0.0)
    qdec = np.exp(log_g[:, None] * (pos + 1.0))[:, :, None] * np.ones((1, 1, LANE))
    kdec = np.exp(log_g[:, None] * (period - 1.0 - pos))[:, :, None] * np.ones((1, 1, LANE))
    sdec = np.exp(log_g * period)[:, None, None] * np.ones((1, 1, LANE))
    as32 = lambda a: jnp.asarray(a, dtype=F32)
    return as32(dec), as32(qdec), as32(kdec), as32(sdec)


def _rope_tables(pos):
    half = RET_DK // 2
    inv_freq = ROPE_BASE ** (-np.arange(half, dtype=np.float64) / half)
    ang = np.asarray(pos, dtype=np.float64)[:, None] * inv_freq[None, :]
    cos, sin = np.cos(ang), np.sin(ang)
    return (jnp.asarray(np.concatenate([cos, cos], axis=-1), dtype=F32),
            jnp.asarray(np.concatenate([-sin, sin], axis=-1), dtype=F32))


def _mix_prompt(x2d, batch, seq, g_mix, w_in, ret_gain, hg_gain, hg_lb, w_out,
                q_s, cache_k, cache_v, dec_len):
    nt = seq // MIX_TOKENS
    n_seq = cache_k.shape[0]
    seqs_per_step = n_seq // (batch * nt)
    assert seqs_per_step * batch * nt == n_seq and (seqs_per_step * dec_len) % 8 == 0
    srows = pl.BlockSpec((seqs_per_step * dec_len, D_MODEL), lambda b, t: (b * nt + t, 0))
    cblk = pl.BlockSpec((seqs_per_step, N_MEM, XA_HEADS, XA_HD), lambda b, t: (b * nt + t, 0, 0, 0))
    cos2, sin2 = _rope_tables(np.arange(seq))
    dec, qdec, kdec, sdec = _ret_tables(RET_CHUNK_LEN, RET_CHUNK_LEN)
    c2 = lambda b, t: (0, 0)
    c3 = lambda b, t: (0, 0, 0)
    tok = pl.BlockSpec((MIX_TOKENS, D_MODEL), lambda b, t: (b * nt + t, 0))
    state = pl.BlockSpec((1, RET_HEADS, RET_DK, RET_DV), lambda b, t: (b, 0, 0, 0))
    return pl.pallas_call(
        functools.partial(_mix_kernel, dec_len),
        grid=(batch, nt),
        in_specs=[tok, pl.BlockSpec((1, D_MODEL), c2), _resident((D_MODEL, IN_W)),
                  pl.BlockSpec((1, RET_W), c2), pl.BlockSpec((1, HG_W), c2),
                  pl.BlockSpec(hg_lb.shape, c2), _resident((MIX_W, D_MODEL)),
                  pl.BlockSpec((MIX_TOKENS, LANE), lambda b, t: (t, 0)),
                  pl.BlockSpec((MIX_TOKENS, LANE), lambda b, t: (t, 0)),
                  pl.BlockSpec(dec.shape, c3), pl.BlockSpec(qdec.shape, c3),
                  pl.BlockSpec(kdec.shape, c3), pl.BlockSpec(sdec.shape, c3), srows, cblk, cblk],
        out_specs=[tok, state, state, srows],
        out_shape=[jax.ShapeDtypeStruct(x2d.shape, F32),
                   jax.ShapeDtypeStruct((batch, RET_HEADS, RET_DK, RET_DV), F32),
                   jax.ShapeDtypeStruct((batch, HG_HEADS, HG_DK, HG_DV), F32),
                   jax.ShapeDtypeStruct(q_s.shape, F32)],
        scratch_shapes=[pltpu.VMEM((MIX_TOKENS, MIX_W), BF16)],
        compiler_params=pltpu.CompilerParams(dimension_semantics=("arbitrary", "arbitrary"),
                                             vmem_limit_bytes=VMEM_LIMIT_BYTES),
        name="mix_prompt",
    )(x2d, g_mix, w_in, ret_gain, hg_gain, hg_lb, w_out, cos2, sin2, dec, qdec, kdec, sdec,
      q_s, cache_k, cache_v)


def _inv_rms(x):
    return lax.rsqrt(jnp.mean(x * x, axis=-1, keepdims=True) + EPS)


def _query(x1, gxa, wxq):
    return _dot((x1 * gxa).astype(BF16), wxq) * (_inv_rms(x1) * (XA_HD ** -0.5))


def _ffn_final(x1, ox, wxo, gffn, wgate, wup, wdown, gfinal):
    x2 = x1 + _dot(ox, wxo)
    hb = (x2 * gffn).astype(BF16)
    r = _inv_rms(x2)
    a = (_silu(_dot(hb, wgate) * r) * (_dot(hb, wup) * r)).astype(BF16)
    x3 = x2 + _dot(a, wdown)
    return _rms(x3, gfinal)


def _tail_kernel(x1_ref, mk_ref, mv_ref, gxa_ref, wxq_ref, wxo_ref, gffn_ref, wgate_ref, wup_ref,
                 wdown_ref, gfinal_ref, y_ref, ox_scr):
    x1 = x1_ref[...]
    q = _query(x1, gxa_ref[...], wxq_ref[...]).astype(BF16)
    heads = [slice(hd * XA_HD, (hd + 1) * XA_HD) for hd in range(XA_HEADS)]
    scores = [_dot_nt(q[:, cols], mk_ref[0, :, cols]) for cols in heads]
    probs = [_softmax_rows(s).astype(BF16) for s in scores]
    for cols, p in zip(heads, probs):
        ox_scr[:, cols] = _dot(p, mv_ref[0, :, cols]).astype(BF16)
    y_ref[...] = _ffn_final(x1, ox_scr[...], wxo_ref[...], gffn_ref[...], wgate_ref[...],
                            wup_ref[...], wdown_ref[...], gfinal_ref[...])


def _tail_prompt(x1, batch, seq, mkb, mvb, g_xa, w_xq, w_xo, g_ffn, w_gate, w_up, w_down, g_final):
    nt = seq // TAIL_TOKENS
    d_ff = w_gate.shape[1]
    tok = pl.BlockSpec((TAIL_TOKENS, D_MODEL), lambda b, t: (b * nt + t, 0))
    mem = pl.BlockSpec((1, N_MEM, D_MODEL), lambda b, t: (b, 0, 0))
    return pl.pallas_call(
        _tail_kernel,
        grid=(batch, nt),
        in_specs=[tok, mem, mem, _resident((1, D_MODEL)), _resident((D_MODEL, D_MODEL)),
                  _resident((D_MODEL, D_MODEL)), _resident((1, D_MODEL)),
                  _resident((D_MODEL, d_ff)), _resident((D_MODEL, d_ff)),
                  _resident((d_ff, D_MODEL)), _resident((1, D_MODEL))],
        out_specs=tok,
        out_shape=jax.ShapeDtypeStruct(x1.shape, F32),
        scratch_shapes=[pltpu.VMEM((TAIL_TOKENS, D_MODEL), BF16)],
        compiler_params=pltpu.CompilerParams(dimension_semantics=("arbitrary", "arbitrary"),
                                             vmem_limit_bytes=VMEM_LIMIT_BYTES),
        name="tail_prompt",
    )(x1, mkb, mvb, g_xa, w_xq, w_xo, g_ffn, w_gate, w_up, w_down, g_final)


def _proj_kernel(x_ref, g_ref, w_ref, o_ref, wb_ref):
    wb = w_ref[...].astype(BF16)
    wb_ref[...] = wb
    x = x_ref[...].reshape(-1, D_MODEL)
    o_ref[...] = _dot(_rms(x, g_ref[...]).astype(BF16), wb)


def _proj_sample(x3d, g_mix, w_in):
    n = x3d.shape[0] * x3d.shape[1]
    nb = IN_W // D_MODEL
    wblk = pl.BlockSpec((D_MODEL, D_MODEL), lambda j: (0, j))
    return pl.pallas_call(
        _proj_kernel,
        grid=(nb,),
        in_specs=[pl.BlockSpec(x3d.shape, lambda j: (0, 0, 0)),
                  pl.BlockSpec((1, D_MODEL), lambda j: (0, 0)), wblk],
        out_specs=[pl.BlockSpec((n, D_MODEL), lambda j: (0, j)), wblk],
        out_shape=[jax.ShapeDtypeStruct((n, IN_W), F32), jax.ShapeDtypeStruct(w_in.shape, BF16)],
        compiler_params=pltpu.CompilerParams(dimension_semantics=("arbitrary",),
                                             vmem_limit_bytes=VMEM_LIMIT_BYTES),
        name="proj_sample",
    )(x3d, g_mix, w_in)


def _state_kernel(dec_len, proj_ref, sret_ref, shg_ref, rgain_ref, hgain_ref, hglb_ref,
                  cos_ref, sin_ref, dec_ref, qdec_ref, kdec_ref, sdec_ref,
                  omix_ref, nret_ref, nhg_ref):
    n_rows = STATE_SEQS * dec_len
    pair_rows = 8
    seqs_per_pair = pair_rows // dec_len
    row8 = lax.broadcasted_iota(jnp.int32, (pair_rows, LANE), 0)
    rown = lax.broadcasted_iota(jnp.int32, (n_rows, n_rows), 0)
    coln = lax.broadcasted_iota(jnp.int32, (n_rows, n_rows), 1)
    same_seq_causal = (rown >= coln) & ((rown // dec_len) == (coln // dec_len))
    proj = proj_ref[...]
    cos2 = cos_ref[...]
    sin2 = sin_ref[...]

    def per_sequence(q_all, kd_all, vb_all, state_ref, new_ref, hd, scale_of):
        outs = []
        for p in range(n_rows // pair_rows):
            prow = slice(p * pair_rows, (p + 1) * pair_rows)
            q8, kd8, v8 = q_all[prow], kd_all[prow], vb_all[prow]
            acc = jnp.zeros((pair_rows, LANE), F32)
            for j in range(seqs_per_pair):
                seq = p * seqs_per_pair + j
                mine = (row8 >= j * dec_len) & (row8 < (j + 1) * dec_len)
                s0 = state_ref[seq, hd]
                acc = jnp.where(mine, _dot(q8, s0.astype(BF16)), acc)
                kz = jnp.where(mine, kd8, jnp.zeros_like(kd8))
                new_ref[seq, hd] = scale_of(seq) * s0 + _dot_tn(kz, v8)
            outs.append(acc)
        return jnp.concatenate(outs, axis=0)

    for hd in range(RET_HEADS):
        cols = slice(hd * LANE, (hd + 1) * LANE)
        q = _rope(proj[:, _RQ + hd * LANE:_RQ + (hd + 1) * LANE], cos2, sin2)
        k = _rope(proj[:, _RK + hd * LANE:_RK + (hd + 1) * LANE], cos2, sin2) * (RET_DK ** -0.5)
        vb = proj[:, _RV + hd * LANE:_RV + (hd + 1) * LANE].astype(BF16)
        gate = proj[:, _RG + hd * LANE:_RG + (hd + 1) * LANE]
        qb = q.astype(BF16)
        att = _dot_nt(qb, k.astype(BF16)) * dec_ref[hd]
        kd = (k * kdec_ref[hd]).astype(BF16)
        sdec = sdec_ref[hd]
        qs = per_sequence(qb, kd, vb, sret_ref, nret_ref, hd, lambda seq: sdec)
        o = _dot(att.astype(BF16), vb) + qdec_ref[hd] * qs
        _gate_store(o, rgain_ref[:, cols], gate, omix_ref, slice(None), cols)

    lower = _lower_bound(hglb_ref[...])
    f = lower + (1.0 - lower) * _sigmoid(proj[:, _HF:_HF + HG_W])
    kk = 1.0 - f
    qq = _silu(proj[:, _HQ:_HQ + HG_W])
    b = _cumsum_rows(jnp.log(f), dec_len)
    pos = lax.broadcasted_iota(jnp.int32, b.shape, 0) & (dec_len - 1)

    def spread(row_in_seq):
        picked = jnp.where(pos == row_in_seq, b, 0.0)
        out = picked
        for s in range(1, dec_len):
            out = out + jnp.where(pos == (row_in_seq + s), pltpu.roll(picked, s, 0), 0.0)
            out = out + jnp.where(pos == (row_in_seq - s), pltpu.roll(picked, n_rows - s, 0), 0.0)
        return out

    ref = spread(dec_len // 2 - 1)
    b_last = spread(dec_len - 1)
    q_intra = (qq * jnp.exp(b - ref)).astype(BF16)
    k_intra = (kk * jnp.exp(ref - b)).astype(BF16)
    q_inter = (qq * jnp.exp(b)).astype(BF16)
    k_upd = (kk * jnp.exp(b_last - b)).astype(BF16)
    d_last = jnp.exp(b_last)
    for hd in range(HG_HEADS):
        cols = slice(hd * LANE, (hd + 1) * LANE)
        vb = proj[:, _HI + hd * LANE:_HI + (hd + 1) * LANE].astype(BF16)
        gate = proj[:, _HGATE + hd * LANE:_HGATE + (hd + 1) * LANE]
        att = jnp.where(same_seq_causal, _dot_nt(q_intra[:, cols], k_intra[:, cols]), 0.0)
        dpad = jnp.concatenate([d_last[:, cols], jnp.zeros((LANE - n_rows, LANE), F32)], axis=0)
        dcol = dpad.T
        scale_of = lambda seq: dcol[:, seq * dec_len:seq * dec_len + 1]
        qs = per_sequence(q_inter[:, cols], k_upd[:, cols], vb, shg_ref, nhg_ref, hd, scale_of)
        o = _dot(att.astype(BF16), vb) + qs
        _gate_store(o, hgain_ref[:, cols], gate, omix_ref, slice(None),
                    slice(RET_W + hd * LANE, RET_W + (hd + 1) * LANE))


def _state_sample(proj, state_ret, state_hgrn, dec_len, ret_gain, hg_gain, hg_lb):
    n_seq = state_ret.shape[0]
    n_rows = STATE_SEQS * dec_len
    cos2, sin2 = _rope_tables(np.tile(np.arange(dec_len) + PAST_LEN, STATE_SEQS))
    dec, qdec, kdec, sdec = _ret_tables(n_rows, dec_len)
    c2 = lambda i: (0, 0)
    c3 = lambda i: (0, 0, 0)
    state = pl.BlockSpec((STATE_SEQS, RET_HEADS, RET_DK, RET_DV), lambda i: (i, 0, 0, 0))
    return pl.pallas_call(
        functools.partial(_state_kernel, dec_len),
        grid=(n_seq // STATE_SEQS,),
        in_specs=[pl.BlockSpec((n_rows, IN_W), lambda i: (i, 0)), state, state,
                  pl.BlockSpec((1, RET_W), c2), pl.BlockSpec((1, HG_W), c2),
                  pl.BlockSpec(hg_lb.shape, c2),
                  pl.BlockSpec((n_rows, LANE), c2), pl.BlockSpec((n_rows, LANE), c2),
                  pl.BlockSpec(dec.shape, c3), pl.BlockSpec(qdec.shape, c3),
                  pl.BlockSpec(kdec.shape, c3), pl.BlockSpec(sdec.shape, c3)],
        out_specs=[pl.BlockSpec((n_rows, MIX_W), lambda i: (i, 0)), state, state],
        out_shape=[jax.ShapeDtypeStruct((proj.shape[0], MIX_W), BF16),
                   jax.ShapeDtypeStruct(state_ret.shape, F32),
                   jax.ShapeDtypeStruct(state_hgrn.shape, F32)],
        compiler_params=pltpu.CompilerParams(dimension_semantics=("arbitrary",),
                                             vmem_limit_bytes=VMEM_LIMIT_BYTES),
        name="state_sample",
    )(proj, state_ret, state_hgrn, ret_gain, hg_gain, hg_lb, cos2, sin2, dec, qdec, kdec, sdec)


def _outq_kernel(x_ref, omix_ref, wout_ref, gxa_ref, wxq_ref, x1_ref, q_ref, woutb_ref, wxqb_ref):
    wout = wout_ref[...].astype(BF16)
    wxq = wxq_ref[...].astype(BF16)
    woutb_ref[...] = wout
    wxqb_ref[...] = wxq
    x1 = x_ref[...].reshape(-1, D_MODEL) + _dot(omix_ref[...], wout)
    x1_ref[...] = x1
    q_ref[...] = _query(x1, gxa_ref[...], wxq)


def _outq_sample(x3d, omix, w_out, g_xa, w_xq):
    n = x3d.shape[0] * x3d.shape[1]
    z = lambda i: (0, 0)
    return pl.pallas_call(
        _outq_kernel,
        grid=(1,),
        in_specs=[pl.BlockSpec(x3d.shape, lambda i: (0, 0, 0)), pl.BlockSpec((n, MIX_W), z),
                  pl.BlockSpec((MIX_W, D_MODEL), z), pl.BlockSpec((1, D_MODEL), z),
                  pl.BlockSpec((D_MODEL, D_MODEL), z)],
        out_specs=[pl.BlockSpec((n, D_MODEL), z), pl.BlockSpec((n, D_MODEL), z),
                   pl.BlockSpec((MIX_W, D_MODEL), z), pl.BlockSpec((D_MODEL, D_MODEL), z)],
        out_shape=[jax.ShapeDtypeStruct((n, D_MODEL), F32), jax.ShapeDtypeStruct((n, D_MODEL), F32),
                   jax.ShapeDtypeStruct((MIX_W, D_MODEL), BF16),
                   jax.ShapeDtypeStruct((D_MODEL, D_MODEL), BF16)],
        compiler_params=pltpu.CompilerParams(dimension_semantics=("arbitrary",),
                                             vmem_limit_bytes=VMEM_LIMIT_BYTES),
        name="outq_sample",
    )(x3d, omix, w_out, g_xa, w_xq)


def _post_kernel(x1_ref, ox_ref, wxo_ref, gffn_ref, wgate_ref, wup_ref, wdown_ref, gfinal_ref,
                 y_ref, wxob_ref, wgateb_ref, wupb_ref, wdownb_ref, x2_scr, hb_scr, r_scr, acc_scr):
    c = pl.program_id(0)

    @pl.when(c == 0)
    def _():
        wxo = wxo_ref[...].astype(BF16)
        wxob_ref[...] = wxo
        x2 = x1_ref[...] + _dot(ox_ref[...].astype(BF16), wxo)
        x2_scr[...] = x2
        hb_scr[...] = (x2 * gffn_ref[...]).astype(BF16)
        r_scr[...] = jnp.broadcast_to(_inv_rms(x2), r_scr.shape)
        acc_scr[...] = jnp.zeros_like(acc_scr)

    wgate = wgate_ref[...].astype(BF16)
    wup = wup_ref[...].astype(BF16)
    wdown = wdown_ref[...].astype(BF16)
    wgateb_ref[...] = wgate
    wupb_ref[...] = wup
    wdownb_ref[...] = wdown
    hb = hb_scr[...]
    r = r_scr[:, 0:1]
    a = (_silu(_dot(hb, wgate) * r) * (_dot(hb, wup) * r)).astype(BF16)
    acc_scr[...] += _dot(a, wdown)

    @pl.when(c == pl.num_programs(0) - 1)
    def _():
        y_ref[...] = _rms(x2_scr[...] + acc_scr[...], gfinal_ref[...]).reshape(y_ref.shape)


def _post_sample(x1, ox, dec_len, w_xo, g_ffn, w_gate, w_up, w_down, g_final):
    n = x1.shape[0]
    d_ff = w_gate.shape[1]
    chunk = POST_FF_CHUNK
    assert d_ff % chunk == 0
    z = lambda c: (0, 0)
    colblk = pl.BlockSpec((D_MODEL, chunk), lambda c: (0, c))
    rowblk = pl.BlockSpec((chunk, D_MODEL), lambda c: (c, 0))
    full = pl.BlockSpec((n, D_MODEL), z)
    sq = pl.BlockSpec((D_MODEL, D_MODEL), z)
    vec = pl.BlockSpec((1, D_MODEL), z)
    return pl.pallas_call(
        _post_kernel,
        grid=(d_ff // chunk,),
        in_specs=[full, full, sq, vec, colblk, colblk, rowblk, vec],
        out_specs=[pl.BlockSpec((n // dec_len, dec_len, D_MODEL), lambda c: (0, 0, 0)),
                   sq, colblk, colblk, rowblk],
        out_shape=[jax.ShapeDtypeStruct((n // dec_len, dec_len, D_MODEL), F32),
                   jax.ShapeDtypeStruct(w_xo.shape, BF16), jax.ShapeDtypeStruct(w_gate.shape, BF16),
                   jax.ShapeDtypeStruct(w_up.shape, BF16), jax.ShapeDtypeStruct(w_down.shape, BF16)],
        scratch_shapes=[pltpu.VMEM((n, D_MODEL), F32), pltpu.VMEM((n, D_MODEL), BF16),
                        pltpu.VMEM((n, LANE), F32), pltpu.VMEM((n, D_MODEL), F32)],
        compiler_params=pltpu.CompilerParams(dimension_semantics=("arbitrary",),
                                             vmem_limit_bytes=VMEM_LIMIT_BYTES),
        name="post_sample",
    )(x1, ox, w_xo, g_ffn, w_gate, w_up, w_down, g_final)


def kernel(x_prompt, x_sample, mem_prompt, state_ret, state_hgrn, cache_mem_k, cache_mem_v, g_mix, w_in,
           ret_gain, hg_gain, hg_lb, w_out, g_xa, g_mem, w_xq, w_xk, w_xv, w_xo, g_ffn, w_gate, w_up,
           w_down, g_final):
    depth = w_in.shape[0]
    assert depth == 1, "single-layer step"
    batch, seq, d = x_prompt.shape
    dec_batch, dec_len, _ = x_sample.shape
    assert d == D_MODEL and seq % MIX_TOKENS == 0 and seq % TAIL_TOKENS == 0
    assert 8 % dec_len == 0 and dec_batch % STATE_SEQS == 0

    g_final2 = g_final.reshape(1, D_MODEL)

    proj_s, w_in_b = _proj_sample(x_sample, g_mix, w_in[0])
    omix_s, ret_s, hg_s = _state_sample(proj_s, state_ret[0], state_hgrn[0], dec_len,
                                        ret_gain, hg_gain, hg_lb)
    x1_s, q_s, w_out_b, w_xq_b = _outq_sample(x_sample, omix_s, w_out[0], g_xa, w_xq[0])

    mk, mv, mkb, mvb = _memkv(mem_prompt.reshape(batch * N_MEM, D_MODEL), g_mem, w_xk[0], w_xv[0])
    x1_p, ret_p, hg_p, ox_s = _mix_prompt(x_prompt.reshape(batch * seq, D_MODEL), batch, seq, g_mix,
                                          w_in_b, ret_gain, hg_gain, hg_lb, w_out_b,
                                          q_s, cache_mem_k[0], cache_mem_v[0], dec_len)
    y_s, w_xo_b, w_gate_b, w_up_b, w_down_b = _post_sample(x1_s, ox_s, dec_len, w_xo[0], g_ffn, w_gate[0],
                                                           w_up[0], w_down[0], g_final2)

    y_p = _tail_prompt(x1_p, batch, seq, mkb.reshape(batch, N_MEM, D_MODEL),
                       mvb.reshape(batch, N_MEM, D_MODEL), g_xa, w_xq_b, w_xo_b, g_ffn,
                       w_gate_b, w_up_b, w_down_b, g_final2)

    kv_shape = (depth, batch, N_MEM, XA_HEADS, XA_HD)
    return (y_p.reshape(batch, seq, D_MODEL), y_s,
            ret_p[None], hg_p[None], mk.reshape(kv_shape), mv.reshape(kv_shape),
            ret_s[None], hg_s[None])
```

```python
import functools

import jax
import jax.numpy as jnp
import numpy as np
from jax import lax
from jax.experimental import pallas as pl
from jax.experimental.pallas import tpu as pltpu

D_MODEL = 1024
RET_HEADS = 4
RET_DK = 128
RET_DV = 128
RET_W = RET_HEADS * RET_DV
HG_HEADS = 4
HG_DK = 128
HG_DV = 128
HG_W = HG_HEADS * HG_DV
MIX_W = RET_W + HG_W
IN_W = 2 * RET_HEADS * RET_DK + 2 * RET_W + 2 * HG_HEADS * HG_DK + 2 * HG_W
N_MEM = 256
XA_HEADS = 4
XA_HD = D_MODEL // XA_HEADS
PAST_LEN = 16384
ROPE_BASE = 10000.0
EPS = 1e-6

_RQ, _RK, _RV, _RG = 0, 512, 1024, 1536
_HQ, _HF, _HI, _HGATE = 2048, 2560, 3072, 3584

LANE = 128
SUBLANES = 8
MIX_TOKENS = 512
RET_CHUNK_LEN = 128
HG_CHUNK_LEN = 64
HG_SUB = 16
TAIL_TOKENS = 512
MEMKV_ROWS = 512
STATE_SEQS = 16
POST_WEIGHT_CHUNKS = 8
VMEM_LIMIT_BYTES = 56 * 1024 * 1024

F32 = jnp.float32
BF16 = jnp.bfloat16


def _dot(a, b):
    return jnp.dot(a, b, preferred_element_type=F32)


def _dot_nt(a, b):
    return lax.dot_general(a, b, (((1,), (1,)), ((), ())), preferred_element_type=F32)


def _dot_tn(a, b):
    return lax.dot_general(a, b, (((0,), (0,)), ((), ())), preferred_element_type=F32)


def _rms(x, g):
    ms = jnp.mean(x * x, axis=-1, keepdims=True)
    return x * lax.rsqrt(ms + EPS) * g


def _sigmoid(x):
    return 1.0 / (1.0 + jnp.exp(-x))


def _silu(x):
    return x * _sigmoid(x)


def _head_norm(o):
    return o * lax.rsqrt(jnp.mean(o * o, axis=-1, keepdims=True) + EPS)


def _softmax_rows(s):
    p = jnp.exp(s - jnp.max(s, axis=-1, keepdims=True))
    return p / jnp.sum(p, axis=-1, keepdims=True)


def _rope(x, cos2, sin2):
    return x * cos2 + pltpu.roll(x, x.shape[-1] // 2, 1) * sin2


def _lower_bound(hglb):
    m = jnp.max(hglb, axis=0, keepdims=True)
    e = jnp.exp(hglb - m)
    return e[0:1, :] / jnp.sum(e, axis=0, keepdims=True)


def _cumsum_rows(x, period):
    row = lax.broadcasted_iota(jnp.int32, x.shape, 0) & (period - 1)
    s = 1
    while s < period:
        x = x + jnp.where(row >= s, pltpu.roll(x, s, 0), 0.0)
        s *= 2
    return x


def _resident(shape):
    zeros = (0,) * len(shape)
    return pl.BlockSpec(shape, lambda *_: zeros, pipeline_mode=pl.Buffered(1))


def _memkv_kernel(mem_ref, g_ref, wk_ref, wv_ref, k_ref, v_ref, kb_ref, vb_ref):
    m = _rms(mem_ref[...], g_ref[...]).astype(BF16)
    k = _dot(m, wk_ref[...].astype(BF16))
    v = _dot(m, wv_ref[...].astype(BF16))
    for hd in range(XA_HEADS):
        cols = slice(hd * XA_HD, (hd + 1) * XA_HD)
        k_ref[:, hd, :] = k[:, cols]
        v_ref[:, hd, :] = v[:, cols]
    kb_ref[...] = k.astype(BF16)
    vb_ref[...] = v.astype(BF16)


def _memkv(mem2d, g_mem, w_xk, w_xv):
    n = mem2d.shape[0]
    full = lambda i: (0, 0)
    row = lambda i: (i, 0)
    blk = pl.BlockSpec((MEMKV_ROWS, D_MODEL), row)
    hblk = pl.BlockSpec((MEMKV_ROWS, XA_HEADS, XA_HD), lambda i: (i, 0, 0))
    return pl.pallas_call(
        _memkv_kernel,
        grid=(n // MEMKV_ROWS,),
        in_specs=[blk, pl.BlockSpec((1, D_MODEL), full),
                  _resident((D_MODEL, D_MODEL)), _resident((D_MODEL, D_MODEL))],
        out_specs=[hblk, hblk, blk, blk],
        out_shape=[jax.ShapeDtypeStruct((n, XA_HEADS, XA_HD), F32),
                   jax.ShapeDtypeStruct((n, XA_HEADS, XA_HD), F32),
                   jax.ShapeDtypeStruct((n, D_MODEL), BF16), jax.ShapeDtypeStruct((n, D_MODEL), BF16)],
        compiler_params=pltpu.CompilerParams(dimension_semantics=("arbitrary",),
                                             vmem_limit_bytes=VMEM_LIMIT_BYTES),
        name="memkv",
    )(mem2d, g_mem, w_xk, w_xv)


def _sample_xattn_stages(dec_len, n_seqs, q_ref, k_ref, v_ref, o_ref):
    pair_rows = SUBLANES
    seqs_per_pair = pair_rows // dec_len
    n_kv = N_MEM * XA_HEADS
    q_rows = XA_HEADS * pair_rows
    own_head = (lax.broadcasted_iota(jnp.int32, (q_rows, n_kv), 1) % XA_HEADS
                == lax.broadcasted_iota(jnp.int32, (q_rows, n_kv), 0) // pair_rows)
    row8 = lax.broadcasted_iota(jnp.int32, (q_rows, XA_HD), 0) % pair_rows
    probs, outs = {}, {}

    def score(seq):
        prow = slice((seq // seqs_per_pair) * pair_rows, (seq // seqs_per_pair + 1) * pair_rows)
        q8 = q_ref[prow, :]
        qs = jnp.concatenate([q8[:, hd * XA_HD:(hd + 1) * XA_HD] for hd in range(XA_HEADS)],
                             axis=0).astype(BF16)
        kb = k_ref[seq].reshape(n_kv, XA_HD).astype(BF16)
        probs[seq] = _softmax_rows(jnp.where(own_head, _dot_nt(qs, kb), -jnp.inf)).astype(BF16)

    def attend(seq):
        p, j = divmod(seq, seqs_per_pair)
        oj = _dot(probs.pop(seq), v_ref[seq].reshape(n_kv, XA_HD).astype(BF16))
        mine = (row8 >= j * dec_len) & (row8 < (j + 1) * dec_len)
        outs[p] = jnp.where(mine, oj, outs[p]) if p in outs else oj
        if j == seqs_per_pair - 1:
            o = outs.pop(p)
            for hd in range(XA_HEADS):
                o_ref[p * pair_rows:(p + 1) * pair_rows, hd * XA_HD:(hd + 1) * XA_HD] = \
                    o[hd * pair_rows:(hd + 1) * pair_rows, :]

    return ([functools.partial(score, s) for s in range(n_seqs)]
            + [functools.partial(attend, s) for s in range(n_seqs)])


def _gate_store(o, gain, gate, out_ref, rows, cols):
    out_ref[rows, cols] = (_head_norm(o) * gain * _silu(gate)).astype(BF16)


def _mix_kernel(dec_len, x_ref, gmix_ref, win_ref, rgain_ref, hgain_ref, hglb_ref, wout_ref,
                cos_ref, sin_ref, dec_ref, qdec_ref, kdec_ref, sdec_ref, qs_ref, ck_ref, cv_ref,
                x1_ref, sret_ref, shg_ref, oxs_ref, omix_scr):
    t = pl.program_id(1)

    @pl.when(t == 0)
    def _():
        sret_ref[...] = jnp.zeros_like(sret_ref)
        shg_ref[...] = jnp.zeros_like(shg_ref)

    x = x_ref[...]
    h = _rms(x, gmix_ref[...]).astype(BF16)
    side_work = _sample_xattn_stages(dec_len, ck_ref.shape[0], qs_ref, ck_ref, cv_ref, oxs_ref)
    groups = {}
    for g0 in (_HF, _HQ, _RQ, _RK, _RV, _HI, _RG, _HGATE):
        groups[g0] = _dot(h, win_ref[:, g0:g0 + RET_W])
        if side_work:
            side_work.pop(0)()
    while side_work:
        side_work.pop(0)()

    class _Proj:
        def __getitem__(self, idx):
            rows, cols = idx
            g0 = (cols.start // RET_W) * RET_W
            return groups[g0][rows, cols.start - g0:cols.stop - g0]

    proj = _Proj()

    ret_units = []
    for c in range(MIX_TOKENS // RET_CHUNK_LEN):
        rows = slice(c * RET_CHUNK_LEN, (c + 1) * RET_CHUNK_LEN)
        cos2 = cos_ref[rows, :]
        sin2 = sin_ref[rows, :]
        for hd in range(RET_HEADS):
            q = _rope(proj[rows, _RQ + hd * LANE:_RQ + (hd + 1) * LANE], cos2, sin2)
            k = _rope(proj[rows, _RK + hd * LANE:_RK + (hd + 1) * LANE], cos2, sin2) * (RET_DK ** -0.5)
            vb = proj[rows, _RV + hd * LANE:_RV + (hd + 1) * LANE].astype(BF16)
            att = (_dot_nt(q.astype(BF16), k.astype(BF16)) * dec_ref[hd]).astype(BF16)
            kv = _dot_tn((k * kdec_ref[hd]).astype(BF16), vb)
            lhs = jnp.concatenate([att, (q * qdec_ref[hd]).astype(BF16)], axis=1)
            ret_units.append((rows, hd, lhs, vb, kv))

    lower = _lower_bound(hglb_ref[...])
    n_sub = HG_CHUNK_LEN // HG_SUB
    crow = lax.broadcasted_iota(jnp.int32, (HG_CHUNK_LEN, HG_CHUNK_LEN), 0)
    ccol = lax.broadcasted_iota(jnp.int32, (HG_CHUNK_LEN, HG_CHUNK_LEN), 1)
    causal = crow >= ccol
    hg_units = []
    for c in range(MIX_TOKENS // HG_CHUNK_LEN):
        rows = slice(c * HG_CHUNK_LEN, (c + 1) * HG_CHUNK_LEN)
        f = lower + (1.0 - lower) * _sigmoid(proj[rows, _HF:_HF + HG_W])
        kk = 1.0 - f
        qq = _silu(proj[rows, _HQ:_HQ + HG_W])
        b = _cumsum_rows(jnp.log(f), HG_CHUNK_LEN)
        b_last = b[HG_CHUNK_LEN - 1:HG_CHUNK_LEN, :]
        q_inter = (qq * jnp.exp(b)).astype(BF16)
        k_upd = (kk * jnp.exp(b_last - b)).astype(BF16)
        d_last = jnp.exp(b_last)
        for hd in range(HG_HEADS):
            cols = slice(hd * LANE, (hd + 1) * LANE)
            bh = b[:, cols]
            qh = qq[:, cols]
            kh = kk[:, cols]
            vb = proj[rows, _HI + hd * LANE:_HI + (hd + 1) * LANE].astype(BF16)
            q_parts, k_parts = [], []
            for j in range(n_sub):
                lo, hi = j * HG_SUB, (j + 1) * HG_SUB
                ref = bh[lo + HG_SUB // 2 - 1:lo + HG_SUB // 2, :]
                qt = qh[lo:, :] * jnp.exp(bh[lo:, :] - ref)
                kt = kh[lo:hi, :] * jnp.exp(ref - bh[lo:hi, :])
                if lo:
                    qt = jnp.concatenate([jnp.zeros((lo, LANE), F32), qt], axis=0)
                    kt = jnp.concatenate([jnp.zeros((lo, LANE), F32), kt], axis=0)
                if hi < HG_CHUNK_LEN:
                    kt = jnp.concatenate([kt, jnp.zeros((HG_CHUNK_LEN - hi, LANE), F32)], axis=0)
                q_parts.append(qt.astype(BF16))
                k_parts.append(kt.astype(BF16))
            qcat = jnp.concatenate(q_parts, axis=1)
            kcat = jnp.concatenate(k_parts, axis=1)
            att = jnp.where(causal, _dot_nt(qcat, kcat), 0.0).astype(BF16)
            kv = _dot_tn(k_upd[:, cols], vb)
            lhs = jnp.concatenate([q_inter[:, cols], att], axis=1)
            hg_units.append((rows, hd, lhs, vb, kv, d_last[:, cols]))

    for hd in range(RET_HEADS):
        cols = slice(hd * LANE, (hd + 1) * LANE)
        s = sret_ref[0, hd]
        for rows, uh, lhs, vb, kv in ret_units:
            if uh != hd:
                continue
            o = _dot(lhs, jnp.concatenate([vb, s.astype(BF16)], axis=0))
            s = sdec_ref[hd] * s + kv
            gate = proj[rows, _RG + hd * LANE:_RG + (hd + 1) * LANE]
            _gate_store(o, rgain_ref[:, cols], gate, omix_scr, rows, cols)
        sret_ref[0, hd] = s

    for hd in range(HG_HEADS):
        cols = slice(hd * LANE, (hd + 1) * LANE)
        mine = [u for u in hg_units if u[1] == hd]
        dl = jnp.concatenate([u[5] for u in mine] + [jnp.zeros((LANE - len(mine), LANE), F32)], axis=0).T
        s = shg_ref[0, hd]
        for i, (rows, _, lhs, vb, kv, _) in enumerate(mine):
            o = _dot(lhs, jnp.concatenate([s.astype(BF16), vb], axis=0))
            s = s * dl[:, i:i + 1] + kv
            gate = proj[rows, _HGATE + hd * LANE:_HGATE + (hd + 1) * LANE]
            _gate_store(o, hgain_ref[:, cols], gate, omix_scr, rows,
                        slice(RET_W + hd * LANE, RET_W + (hd + 1) * LANE))
        shg_ref[0, hd] = s

    x1_ref[...] = x + _dot(omix_scr[...], wout_ref[...])


def _ret_tables(length, period):
    log_g = np.log(1.0 - 2.0 ** (-5.0 - np.arange(RET_HEADS, dtype=np.float64)))
    idx = np.arange(length)
    pos = idx % period
    rel = (idx[:, None] - idx[None, :]).astype(np.float64)
    same = (idx[:, None] // period) == (idx[None, :] // period)
    valid = (rel >= 0) & same
    dec = np.where(valid[None], np.exp(log_g[:, None, None] * np.where(valid, rel, 0.0)[None]), 0.0)
    qdec = np.exp(log_g[:, None] * (pos + 1.0))[:, :, None] * np.ones((1, 1, LANE))
    kdec = np.exp(log_g[:, None] * (period - 1.0 - pos))[:, :, None] * np.ones((1, 1, LANE))
    sdec = np.exp(log_g * period)[:, None, None] * np.ones((1, 1, LANE))
    as32 = lambda a: jnp.asarray(a, dtype=F32)
    return as32(dec), as32(qdec), as32(kdec), as32(sdec)


def _rope_tables(pos):
    half = RET_DK // 2
    inv_freq = ROPE_BASE ** (-np.arange(half, dtype=np.float64) / half)
    ang = np.asarray(pos, dtype=np.float64)[:, None] * inv_freq[None, :]
    cos, sin = np.cos(ang), np.sin(ang)
    return (jnp.asarray(np.concatenate([cos, cos], axis=-1), dtype=F32),
            jnp.asarray(np.concatenate([-sin, sin], axis=-1), dtype=F32))


def _mix_prompt(x2d, batch, seq, g_mix, w_in, ret_gain, hg_gain, hg_lb, w_out,
                q_s, cache_k, cache_v, dec_len):
    nt = seq // MIX_TOKENS
    n_seq = cache_k.shape[0]
    seqs_per_step = n_seq // (batch * nt)
    assert seqs_per_step * batch * nt == n_seq and (seqs_per_step * dec_len) % SUBLANES == 0
    srows = pl.BlockSpec((seqs_per_step * dec_len, D_MODEL), lambda b, t: (b * nt + t, 0))
    cblk = pl.BlockSpec((seqs_per_step, N_MEM, XA_HEADS, XA_HD), lambda b, t: (b * nt + t, 0, 0, 0))
    cos2, sin2 = _rope_tables(np.arange(seq))
    dec, qdec, kdec, sdec = _ret_tables(RET_CHUNK_LEN, RET_CHUNK_LEN)
    c2 = lambda b, t: (0, 0)
    c3 = lambda b, t: (0, 0, 0)
    tok = pl.BlockSpec((MIX_TOKENS, D_MODEL), lambda b, t: (b * nt + t, 0))
    state = pl.BlockSpec((1, RET_HEADS, RET_DK, RET_DV), lambda b, t: (b, 0, 0, 0))
    return pl.pallas_call(
        functools.partial(_mix_kernel, dec_len),
        grid=(batch, nt),
        in_specs=[tok, pl.BlockSpec((1, D_MODEL), c2), _resident((D_MODEL, IN_W)),
                  pl.BlockSpec((1, RET_W), c2), pl.BlockSpec((1, HG_W), c2),
                  pl.BlockSpec(hg_lb.shape, c2), _resident((MIX_W, D_MODEL)),
                  pl.BlockSpec((MIX_TOKENS, LANE), lambda b, t: (t, 0)),
                  pl.BlockSpec((MIX_TOKENS, LANE), lambda b, t: (t, 0)),
                  pl.BlockSpec(dec.shape, c3), pl.BlockSpec(qdec.shape, c3),
                  pl.BlockSpec(kdec.shape, c3), pl.BlockSpec(sdec.shape, c3), srows, cblk, cblk],
        out_specs=[tok, state, state, srows],
        out_shape=[jax.ShapeDtypeStruct(x2d.shape, F32),
                   jax.ShapeDtypeStruct((batch, RET_HEADS, RET_DK, RET_DV), F32),
                   jax.ShapeDtypeStruct((batch, HG_HEADS, HG_DK, HG_DV), F32),
                   jax.ShapeDtypeStruct(q_s.shape, F32)],
        scratch_shapes=[pltpu.VMEM((MIX_TOKENS, MIX_W), BF16)],
        compiler_params=pltpu.CompilerParams(dimension_semantics=("arbitrary", "arbitrary"),
                                             vmem_limit_bytes=VMEM_LIMIT_BYTES),
        name="mix_prompt",
    )(x2d, g_mix, w_in, ret_gain, hg_gain, hg_lb, w_out, cos2, sin2, dec, qdec, kdec, sdec,
      q_s, cache_k, cache_v)


def _inv_rms(x):
    return lax.rsqrt(jnp.mean(x * x, axis=-1, keepdims=True) + EPS)


def _query(x1, gxa, wxq):
    return _dot((x1 * gxa).astype(BF16), wxq) * (_inv_rms(x1) * (XA_HD ** -0.5))


def _ffn_final(x1, ox, wxo, gffn, wgate, wup, wdown, gfinal):
    x2 = x1 + _dot(ox, wxo)
    hb = (x2 * gffn).astype(BF16)
    r = _inv_rms(x2)
    a = (_silu(_dot(hb, wgate) * r) * (_dot(hb, wup) * r)).astype(BF16)
    half = x2.shape[0] // 2
    return jnp.concatenate([_rms(x2[rows] + _dot(a[rows], wdown), gfinal)
                            for rows in (slice(0, half), slice(half, None))], axis=0)


def _tail_kernel(x1_ref, mk_ref, mv_ref, gxa_ref, wxq_ref, wxo_ref, gffn_ref, wgate_ref, wup_ref,
                 wdown_ref, gfinal_ref, y_ref, ox_scr):
    x1 = x1_ref[...]
    q = _query(x1, gxa_ref[...], wxq_ref[...]).astype(BF16)
    heads = [slice(hd * XA_HD, (hd + 1) * XA_HD) for hd in range(XA_HEADS)]
    scores = [_dot_nt(q[:, cols], mk_ref[0, :, cols]) for cols in heads]
    probs = [_softmax_rows(s).astype(BF16) for s in scores]
    for cols, p in zip(heads, probs):
        ox_scr[:, cols] = _dot(p, mv_ref[0, :, cols]).astype(BF16)
    y_ref[...] = _ffn_final(x1, ox_scr[...], wxo_ref[...], gffn_ref[...], wgate_ref[...],
                            wup_ref[...], wdown_ref[...], gfinal_ref[...])


def _tail_prompt(x1, batch, seq, mkb, mvb, g_xa, w_xq, w_xo, g_ffn, w_gate, w_up, w_down, g_final):
    nt = seq // TAIL_TOKENS
    d_ff = w_gate.shape[1]
    tok = pl.BlockSpec((TAIL_TOKENS, D_MODEL), lambda b, t: (b * nt + t, 0))
    mem = pl.BlockSpec((1, N_MEM, D_MODEL), lambda b, t: (b, 0, 0))
    return pl.pallas_call(
        _tail_kernel,
        grid=(batch, nt),
        in_specs=[tok, mem, mem, _resident((1, D_MODEL)), _resident((D_MODEL, D_MODEL)),
                  _resident((D_MODEL, D_MODEL)), _resident((1, D_MODEL)),
                  _resident((D_MODEL, d_ff)), _resident((D_MODEL, d_ff)),
                  _resident((d_ff, D_MODEL)), _resident((1, D_MODEL))],
        out_specs=tok,
        out_shape=jax.ShapeDtypeStruct(x1.shape, F32),
        scratch_shapes=[pltpu.VMEM((TAIL_TOKENS, D_MODEL), BF16)],
        compiler_params=pltpu.CompilerParams(dimension_semantics=("arbitrary", "arbitrary"),
                                             vmem_limit_bytes=VMEM_LIMIT_BYTES),
        name="tail_prompt",
    )(x1, mkb, mvb, g_xa, w_xq, w_xo, g_ffn, w_gate, w_up, w_down, g_final)


def _proj_kernel(x_ref, g_ref, w_ref, o_ref, wb_ref):
    wb = w_ref[...].astype(BF16)
    wb_ref[...] = wb
    x = x_ref[...].reshape(-1, D_MODEL)
    o_ref[...] = _dot(_rms(x, g_ref[...]).astype(BF16), wb)


def _proj_sample(x3d, g_mix, w_in):
    n = x3d.shape[0] * x3d.shape[1]
    nb = IN_W // D_MODEL
    wblk = pl.BlockSpec((D_MODEL, D_MODEL), lambda j: (0, j))
    return pl.pallas_call(
        _proj_kernel,
        grid=(nb,),
        in_specs=[pl.BlockSpec(x3d.shape, lambda j: (0, 0, 0)),
                  pl.BlockSpec((1, D_MODEL), lambda j: (0, 0)), wblk],
        out_specs=[pl.BlockSpec((n, D_MODEL), lambda j: (0, j)), wblk],
        out_shape=[jax.ShapeDtypeStruct((n, IN_W), F32), jax.ShapeDtypeStruct(w_in.shape, BF16)],
        compiler_params=pltpu.CompilerParams(dimension_semantics=("arbitrary",),
                                             vmem_limit_bytes=VMEM_LIMIT_BYTES),
        name="proj_sample",
    )(x3d, g_mix, w_in)


def _state_kernel(dec_len, proj_ref, sret_ref, shg_ref, rgain_ref, hgain_ref, hglb_ref,
                  cos_ref, sin_ref, dec_ref, qdec_ref, kdec_ref, sdec_ref,
                  omix_ref, nret_ref, nhg_ref):
    n_rows = STATE_SEQS * dec_len
    pair_rows = SUBLANES
    seqs_per_pair = pair_rows // dec_len
    row8 = lax.broadcasted_iota(jnp.int32, (pair_rows, LANE), 0)
    rown = lax.broadcasted_iota(jnp.int32, (n_rows, n_rows), 0)
    coln = lax.broadcasted_iota(jnp.int32, (n_rows, n_rows), 1)
    same_seq_causal = (rown >= coln) & ((rown // dec_len) == (coln // dec_len))
    proj = proj_ref[...]
    cos2 = cos_ref[...]
    sin2 = sin_ref[...]

    def per_sequence(q_all, kd_all, vb_all, state_ref, new_ref, hd, scale_of):
        outs = []
        for p in range(n_rows // pair_rows):
            prow = slice(p * pair_rows, (p + 1) * pair_rows)
            q8, kd8, v8 = q_all[prow], kd_all[prow], vb_all[prow]
            acc = jnp.zeros((pair_rows, LANE), F32)
            for j in range(seqs_per_pair):
                seq = p * seqs_per_pair + j
                mine = (row8 >= j * dec_len) & (row8 < (j + 1) * dec_len)
                s0 = state_ref[seq, hd]
                acc = jnp.where(mine, _dot(q8, s0.astype(BF16)), acc)
                kz = jnp.where(mine, kd8, jnp.zeros_like(kd8))
                new_ref[seq, hd] = scale_of(seq) * s0 + _dot_tn(kz, v8)
            outs.append(acc)
        return jnp.concatenate(outs, axis=0)

    for hd in range(RET_HEADS):
        cols = slice(hd * LANE, (hd + 1) * LANE)
        q = _rope(proj[:, _RQ + hd * LANE:_RQ + (hd + 1) * LANE], cos2, sin2)
        k = _rope(proj[:, _RK + hd * LANE:_RK + (hd + 1) * LANE], cos2, sin2) * (RET_DK ** -0.5)
        vb = proj[:, _RV + hd * LANE:_RV + (hd + 1) * LANE].astype(BF16)
        gate = proj[:, _RG + hd * LANE:_RG + (hd + 1) * LANE]
        qb = q.astype(BF16)
        att = _dot_nt(qb, k.astype(BF16)) * dec_ref[hd]
        kd = (k * kdec_ref[hd]).astype(BF16)
        sdec = sdec_ref[hd]
        qs = per_sequence(qb, kd, vb, sret_ref, nret_ref, hd, lambda seq: sdec)
        o = _dot(att.astype(BF16), vb) + qdec_ref[hd] * qs
        _gate_store(o, rgain_ref[:, cols], gate, omix_ref, slice(None), cols)

    lower = _lower_bound(hglb_ref[...])
    f = lower + (1.0 - lower) * _sigmoid(proj[:, _HF:_HF + HG_W])
    kk = 1.0 - f
    qq = _silu(proj[:, _HQ:_HQ + HG_W])
    b = _cumsum_rows(jnp.log(f), dec_len)
    pos = lax.broadcasted_iota(jnp.int32, b.shape, 0) & (dec_len - 1)

    def spread(row_in_seq):
        picked = jnp.where(pos == row_in_seq, b, 0.0)
        out = picked
        for s in range(1, dec_len):
            out = out + jnp.where(pos == (row_in_seq + s), pltpu.roll(picked, s, 0), 0.0)
            out = out + jnp.where(pos == (row_in_seq - s), pltpu.roll(picked, n_rows - s, 0), 0.0)
        return out

    ref = spread(dec_len // 2 - 1)
    b_last = spread(dec_len - 1)
    q_intra = (qq * jnp.exp(b - ref)).astype(BF16)
    k_intra = (kk * jnp.exp(ref - b)).astype(BF16)
    q_inter = (qq * jnp.exp(b)).astype(BF16)
    k_upd = (kk * jnp.exp(b_last - b)).astype(BF16)
    d_last = jnp.exp(b_last)
    for hd in range(HG_HEADS):
        cols = slice(hd * LANE, (hd + 1) * LANE)
        vb = proj[:, _HI + hd * LANE:_HI + (hd + 1) * LANE].astype(BF16)
        gate = proj[:, _HGATE + hd * LANE:_HGATE + (hd + 1) * LANE]
        att = jnp.where(same_seq_causal, _dot_nt(q_intra[:, cols], k_intra[:, cols]), 0.0)
        dpad = jnp.concatenate([d_last[:, cols], jnp.zeros((LANE - n_rows, LANE), F32)], axis=0)
        dcol = dpad.T
        scale_of = lambda seq: dcol[:, seq * dec_len:seq * dec_len + 1]
        qs = per_sequence(q_inter[:, cols], k_upd[:, cols], vb, shg_ref, nhg_ref, hd, scale_of)
        o = _dot(att.astype(BF16), vb) + qs
        _gate_store(o, hgain_ref[:, cols], gate, omix_ref, slice(None),
                    slice(RET_W + hd * LANE, RET_W + (hd + 1) * LANE))


def _state_sample(proj, state_ret, state_hgrn, dec_len, ret_gain, hg_gain, hg_lb):
    n_seq = state_ret.shape[0]
    n_rows = STATE_SEQS * dec_len
    cos2, sin2 = _rope_tables(np.tile(np.arange(dec_len) + PAST_LEN, STATE_SEQS))
    dec, qdec, kdec, sdec = _ret_tables(n_rows, dec_len)
    c2 = lambda i: (0, 0)
    c3 = lambda i: (0, 0, 0)
    state = pl.BlockSpec((STATE_SEQS, RET_HEADS, RET_DK, RET_DV), lambda i: (i, 0, 0, 0))
    return pl.pallas_call(
        functools.partial(_state_kernel, dec_len),
        grid=(n_seq // STATE_SEQS,),
        in_specs=[pl.BlockSpec((n_rows, IN_W), lambda i: (i, 0)), state, state,
                  pl.BlockSpec((1, RET_W), c2), pl.BlockSpec((1, HG_W), c2),
                  pl.BlockSpec(hg_lb.shape, c2),
                  pl.BlockSpec((n_rows, LANE), c2), pl.BlockSpec((n_rows, LANE), c2),
                  pl.BlockSpec(dec.shape, c3), pl.BlockSpec(qdec.shape, c3),
                  pl.BlockSpec(kdec.shape, c3), pl.BlockSpec(sdec.shape, c3)],
        out_specs=[pl.BlockSpec((n_rows, MIX_W), lambda i: (i, 0)), state, state],
        out_shape=[jax.ShapeDtypeStruct((proj.shape[0], MIX_W), BF16),
                   jax.ShapeDtypeStruct(state_ret.shape, F32),
                   jax.ShapeDtypeStruct(state_hgrn.shape, F32)],
        compiler_params=pltpu.CompilerParams(dimension_semantics=("arbitrary",),
                                             vmem_limit_bytes=VMEM_LIMIT_BYTES),
        name="state_sample",
    )(proj, state_ret, state_hgrn, ret_gain, hg_gain, hg_lb, cos2, sin2, dec, qdec, kdec, sdec)


def _outq_kernel(x_ref, omix_ref, wout_ref, gxa_ref, wxq_ref, x1_ref, q_ref, woutb_ref, wxqb_ref):
    wout = wout_ref[...].astype(BF16)
    wxq = wxq_ref[...].astype(BF16)
    woutb_ref[...] = wout
    wxqb_ref[...] = wxq
    x1 = x_ref[...].reshape(-1, D_MODEL) + _dot(omix_ref[...], wout)
    x1_ref[...] = x1
    q_ref[...] = _query(x1, gxa_ref[...], wxq)


def _outq_sample(x3d, omix, w_out, g_xa, w_xq):
    n = x3d.shape[0] * x3d.shape[1]
    z = lambda i: (0, 0)
    return pl.pallas_call(
        _outq_kernel,
        grid=(1,),
        in_specs=[pl.BlockSpec(x3d.shape, lambda i: (0, 0, 0)), pl.BlockSpec((n, MIX_W), z),
                  pl.BlockSpec((MIX_W, D_MODEL), z), pl.BlockSpec((1, D_MODEL), z),
                  pl.BlockSpec((D_MODEL, D_MODEL), z)],
        out_specs=[pl.BlockSpec((n, D_MODEL), z), pl.BlockSpec((n, D_MODEL), z),
                   pl.BlockSpec((MIX_W, D_MODEL), z), pl.BlockSpec((D_MODEL, D_MODEL), z)],
        out_shape=[jax.ShapeDtypeStruct((n, D_MODEL), F32), jax.ShapeDtypeStruct((n, D_MODEL), F32),
                   jax.ShapeDtypeStruct((MIX_W, D_MODEL), BF16),
                   jax.ShapeDtypeStruct((D_MODEL, D_MODEL), BF16)],
        compiler_params=pltpu.CompilerParams(dimension_semantics=("arbitrary",),
                                             vmem_limit_bytes=VMEM_LIMIT_BYTES),
        name="outq_sample",
    )(x3d, omix, w_out, g_xa, w_xq)


def _post_kernel(x1_ref, ox_ref, wxo_ref, gffn_ref, wgate_ref, wup_ref, wdown_ref, gfinal_ref,
                 y_ref, wxob_ref, wgateb_ref, wupb_ref, wdownb_ref,
                 x2_scr, hb_scr, g_scr, u_scr, wd_scr):
    c = pl.program_id(0)
    n_chunks = hb_scr.shape[0]
    k_rows = hb_scr.shape[2]
    d_rows = wdown_ref.shape[0]

    @pl.when(c == 0)
    def _():
        wxo = wxo_ref[...].astype(BF16)
        wxob_ref[...] = wxo
        x2 = x1_ref[...] + _dot(ox_ref[...].astype(BF16), wxo)
        x2_scr[...] = x2
        hb = (x2 * gffn_ref[...]).astype(BF16)
        for j in range(n_chunks):
            hb_scr[j] = hb[:, j * k_rows:(j + 1) * k_rows]
        g_scr[...] = jnp.zeros_like(g_scr)
        u_scr[...] = jnp.zeros_like(u_scr)

    wgate = wgate_ref[...].astype(BF16)
    wup = wup_ref[...].astype(BF16)
    wdown = wdown_ref[...].astype(BF16)
    wgateb_ref[...] = wgate
    wupb_ref[...] = wup
    wdownb_ref[...] = wdown
    wd_scr[pl.ds(pl.multiple_of(c * d_rows, d_rows), d_rows), :] = wdown
    hb_c = hb_scr[c]
    g_scr[...] += _dot(hb_c, wgate)
    u_scr[...] += _dot(hb_c, wup)

    @pl.when(c == n_chunks - 1)
    def _():
        x2 = x2_scr[...]
        r = _inv_rms(x2)
        a = (_silu(g_scr[...] * r) * (u_scr[...] * r)).astype(BF16)
        y_ref[...] = _rms(x2 + _dot(a, wd_scr[...]), gfinal_ref[...]).reshape(y_ref.shape)


def _post_sample(x1, ox, dec_len, w_xo, g_ffn, w_gate, w_up, w_down, g_final):
    n = x1.shape[0]
    d_ff = w_gate.shape[1]
    n_chunks = POST_WEIGHT_CHUNKS
    k_rows = D_MODEL // n_chunks
    d_rows = d_ff // n_chunks
    assert k_rows * n_chunks == D_MODEL and d_rows * n_chunks == d_ff and d_rows % 16 == 0
    z = lambda c: (0, 0)
    kblk = pl.BlockSpec((k_rows, d_ff), lambda c: (c, 0))
    dblk = pl.BlockSpec((d_rows, D_MODEL), lambda c: (c, 0))
    full = pl.BlockSpec((n, D_MODEL), z)
    sq = pl.BlockSpec((D_MODEL, D_MODEL), z)
    vec = pl.BlockSpec((1, D_MODEL), z)
    return pl.pallas_call(
        _post_kernel,
        grid=(n_chunks,),
        in_specs=[full, full, sq, vec, kblk, kblk, dblk, vec],
        out_specs=[pl.BlockSpec((n // dec_len, dec_len, D_MODEL), lambda c: (0, 0, 0)),
                   sq, kblk, kblk, dblk],
        out_shape=[jax.ShapeDtypeStruct((n // dec_len, dec_len, D_MODEL), F32),
                   jax.ShapeDtypeStruct(w_xo.shape, BF16), jax.ShapeDtypeStruct(w_gate.shape, BF16),
                   jax.ShapeDtypeStruct(w_up.shape, BF16), jax.ShapeDtypeStruct(w_down.shape, BF16)],
        scratch_shapes=[pltpu.VMEM((n, D_MODEL), F32), pltpu.VMEM((n_chunks, n, k_rows), BF16),
                        pltpu.VMEM((n, d_ff), F32), pltpu.VMEM((n, d_ff), F32),
                        pltpu.VMEM((d_ff, D_MODEL), BF16)],
        compiler_params=pltpu.CompilerParams(dimension_semantics=("arbitrary",),
                                             vmem_limit_bytes=VMEM_LIMIT_BYTES),
        name="post_sample",
    )(x1, ox, w_xo, g_ffn, w_gate, w_up, w_down, g_final)


def kernel(x_prompt, x_sample, mem_prompt, state_ret, state_hgrn, cache_mem_k, cache_mem_v, g_mix, w_in,
           ret_gain, hg_gain, hg_lb, w_out, g_xa, g_mem, w_xq, w_xk, w_xv, w_xo, g_ffn, w_gate, w_up,
           w_down, g_final):
    depth = w_in.shape[0]
    assert depth == 1, "single-layer step"
    batch, seq, d = x_prompt.shape
    dec_batch, dec_len, _ = x_sample.shape
    assert d == D_MODEL and seq % MIX_TOKENS == 0 and seq % TAIL_TOKENS == 0
    assert SUBLANES % dec_len == 0 and dec_batch % STATE_SEQS == 0

    g_final2 = g_final.reshape(1, D_MODEL)

    proj_s, w_in_b = _proj_sample(x_sample, g_mix, w_in[0])
    omix_s, ret_s, hg_s = _state_sample(proj_s, state_ret[0], state_hgrn[0], dec_len,
                                        ret_gain, hg_gain, hg_lb)
    x1_s, q_s, w_out_b, w_xq_b = _outq_sample(x_sample, omix_s, w_out[0], g_xa, w_xq[0])

    mk, mv, mkb, mvb = _memkv(mem_prompt.reshape(batch * N_MEM, D_MODEL), g_mem, w_xk[0], w_xv[0])
    x1_p, ret_p, hg_p, ox_s = _mix_prompt(x_prompt.reshape(batch * seq, D_MODEL), batch, seq, g_mix,
                                          w_in_b, ret_gain, hg_gain, hg_lb, w_out_b,
                                          q_s, cache_mem_k[0], cache_mem_v[0], dec_len)
    y_s, w_xo_b, w_gate_b, w_up_b, w_down_b = _post_sample(x1_s, ox_s, dec_len, w_xo[0], g_ffn, w_gate[0],
                                                           w_up[0], w_down[0], g_final2)

    y_p = _tail_prompt(x1_p, batch, seq, mkb.reshape(batch, N_MEM, D_MODEL),
                       mvb.reshape(batch, N_MEM, D_MODEL), g_xa, w_xq_b, w_xo_b, g_ffn,
                       w_gate_b, w_up_b, w_down_b, g_final2)

    kv_shape = (depth, batch, N_MEM, XA_HEADS, XA_HD)
    return (y_p.reshape(batch, seq, D_MODEL), y_s,
            ret_p[None], hg_p[None], mk.reshape(kv_shape), mv.reshape(kv_shape),
            ret_s[None], hg_s[None])
```

```python
import functools

import jax
import jax.numpy as jnp
import numpy as np
from jax import lax
from jax.experimental import pallas as pl
from jax.experimental.pallas import tpu as pltpu

D_MODEL = 1024
RET_HEADS = 4
RET_DK = 128
RET_DV = 128
RET_W = RET_HEADS * RET_DV
HG_HEADS = 4
HG_DK = 128
HG_DV = 128
HG_W = HG_HEADS * HG_DV
MIX_W = RET_W + HG_W
IN_W = 2 * RET_HEADS * RET_DK + 2 * RET_W + 2 * HG_HEADS * HG_DK + 2 * HG_W
N_MEM = 256
XA_HEADS = 4
XA_HD = D_MODEL // XA_HEADS
PAST_LEN = 16384
ROPE_BASE = 10000.0
EPS = 1e-6

_RQ, _RK, _RV, _RG = 0, 512, 1024, 1536
_HQ, _HF, _HI, _HGATE = 2048, 2560, 3072, 3584

LANE = 128
SUBLANES = 8
MIX_TOKENS = 512
RET_CHUNK_LEN = 128
HG_CHUNK_LEN = 64
HG_SUB = 16
TAIL_TOKENS = 512
MEMKV_ROWS = 512
STATE_SEQS = 16
POST_FF_CHUNK = 256
VMEM_LIMIT_BYTES = 56 * 1024 * 1024

F32 = jnp.float32
BF16 = jnp.bfloat16


def _dot(a, b):
    return jnp.dot(a, b, preferred_element_type=F32)


def _dot_nt(a, b):
    return lax.dot_general(a, b, (((1,), (1,)), ((), ())), preferred_element_type=F32)


def _dot_tn(a, b):
    return lax.dot_general(a, b, (((0,), (0,)), ((), ())), preferred_element_type=F32)


def _rms(x, g):
    ms = jnp.mean(x * x, axis=-1, keepdims=True)
    return x * lax.rsqrt(ms + EPS) * g


def _sigmoid(x):
    return 1.0 / (1.0 + jnp.exp(-x))


def _silu(x):
    return x * _sigmoid(x)


def _head_norm(o):
    return o * lax.rsqrt(jnp.mean(o * o, axis=-1, keepdims=True) + EPS)


def _softmax_rows(s):
    p = jnp.exp(s - jnp.max(s, axis=-1, keepdims=True))
    return p / jnp.sum(p, axis=-1, keepdims=True)


def _rope(x, cos2, sin2):
    return x * cos2 + pltpu.roll(x, x.shape[-1] // 2, 1) * sin2


def _lower_bound(hglb):
    m = jnp.max(hglb, axis=0, keepdims=True)
    e = jnp.exp(hglb - m)
    return e[0:1, :] / jnp.sum(e, axis=0, keepdims=True)


def _cumsum_rows(x, period):
    row = lax.broadcasted_iota(jnp.int32, x.shape, 0) & (period - 1)
    s = 1
    while s < period:
        x = x + jnp.where(row >= s, pltpu.roll(x, s, 0), 0.0)
        s *= 2
    return x


def _resident(shape):
    zeros = (0,) * len(shape)
    return pl.BlockSpec(shape, lambda *_: zeros, pipeline_mode=pl.Buffered(1))


def _memkv_kernel(mem_ref, g_ref, wk_ref, wv_ref, k_ref, v_ref, kb_ref, vb_ref):
    m = _rms(mem_ref[...], g_ref[...]).astype(BF16)
    k = _dot(m, wk_ref[...].astype(BF16))
    v = _dot(m, wv_ref[...].astype(BF16))
    for hd in range(XA_HEADS):
        cols = slice(hd * XA_HD, (hd + 1) * XA_HD)
        k_ref[:, hd, :] = k[:, cols]
        v_ref[:, hd, :] = v[:, cols]
    kb_ref[...] = k.astype(BF16)
    vb_ref[...] = v.astype(BF16)


def _memkv(mem2d, g_mem, w_xk, w_xv):
    n = mem2d.shape[0]
    full = lambda i: (0, 0)
    row = lambda i: (i, 0)
    blk = pl.BlockSpec((MEMKV_ROWS, D_MODEL), row)
    hblk = pl.BlockSpec((MEMKV_ROWS, XA_HEADS, XA_HD), lambda i: (i, 0, 0))
    return pl.pallas_call(
        _memkv_kernel,
        grid=(n // MEMKV_ROWS,),
        in_specs=[blk, pl.BlockSpec((1, D_MODEL), full),
                  _resident((D_MODEL, D_MODEL)), _resident((D_MODEL, D_MODEL))],
        out_specs=[hblk, hblk, blk, blk],
        out_shape=[jax.ShapeDtypeStruct((n, XA_HEADS, XA_HD), F32),
                   jax.ShapeDtypeStruct((n, XA_HEADS, XA_HD), F32),
                   jax.ShapeDtypeStruct((n, D_MODEL), BF16), jax.ShapeDtypeStruct((n, D_MODEL), BF16)],
        compiler_params=pltpu.CompilerParams(dimension_semantics=("arbitrary",),
                                             vmem_limit_bytes=VMEM_LIMIT_BYTES),
        name="memkv",
    )(mem2d, g_mem, w_xk, w_xv)


def _sample_xattn_stages(dec_len, n_seqs, q_ref, k_ref, v_ref, o_ref):
    pair_rows = SUBLANES
    seqs_per_pair = pair_rows // dec_len
    n_kv = N_MEM * XA_HEADS
    q_rows = XA_HEADS * pair_rows
    own_head = (lax.broadcasted_iota(jnp.int32, (q_rows, n_kv), 1) % XA_HEADS
                == lax.broadcasted_iota(jnp.int32, (q_rows, n_kv), 0) // pair_rows)
    row8 = lax.broadcasted_iota(jnp.int32, (q_rows, XA_HD), 0) % pair_rows
    probs, outs = {}, {}

    def score(seq):
        prow = slice((seq // seqs_per_pair) * pair_rows, (seq // seqs_per_pair + 1) * pair_rows)
        q8 = q_ref[prow, :]
        qs = jnp.concatenate([q8[:, hd * XA_HD:(hd + 1) * XA_HD] for hd in range(XA_HEADS)],
                             axis=0).astype(BF16)
        kb = k_ref[seq].reshape(n_kv, XA_HD).astype(BF16)
        probs[seq] = _softmax_rows(jnp.where(own_head, _dot_nt(qs, kb), -jnp.inf)).astype(BF16)

    def attend(seq):
        p, j = divmod(seq, seqs_per_pair)
        oj = _dot(probs.pop(seq), v_ref[seq].reshape(n_kv, XA_HD).astype(BF16))
        mine = (row8 >= j * dec_len) & (row8 < (j + 1) * dec_len)
        outs[p] = jnp.where(mine, oj, outs[p]) if p in outs else oj
        if j == seqs_per_pair - 1:
            o = outs.pop(p)
            for hd in range(XA_HEADS):
                o_ref[p * pair_rows:(p + 1) * pair_rows, hd * XA_HD:(hd + 1) * XA_HD] = \
                    o[hd * pair_rows:(hd + 1) * pair_rows, :]

    return ([functools.partial(score, s) for s in range(n_seqs)]
            + [functools.partial(attend, s) for s in range(n_seqs)])


def _gate_store(o, gain, gate, out_ref, rows, cols):
    out_ref[rows, cols] = (_head_norm(o) * gain * _silu(gate)).astype(BF16)


def _mix_kernel(dec_len, x_ref, gmix_ref, win_ref, rgain_ref, hgain_ref, hglb_ref, wout_ref,
                cos_ref, sin_ref, dec_ref, qdec_ref, kdec_ref, sdec_ref, qs_ref, ck_ref, cv_ref,
                x1_ref, sret_ref, shg_ref, oxs_ref, omix_scr):
    t = pl.program_id(1)

    @pl.when(t == 0)
    def _():
        sret_ref[...] = jnp.zeros_like(sret_ref)
        shg_ref[...] = jnp.zeros_like(shg_ref)

    x = x_ref[...]
    h = _rms(x, gmix_ref[...]).astype(BF16)
    side_work = _sample_xattn_stages(dec_len, ck_ref.shape[0], qs_ref, ck_ref, cv_ref, oxs_ref)
    groups = {}
    for g0 in (_HF, _HQ, _RQ, _RK, _RV, _HI, _RG, _HGATE):
        groups[g0] = _dot(h, win_ref[:, g0:g0 + RET_W])
        if side_work:
            side_work.pop(0)()
    while side_work:
        side_work.pop(0)()

    class _Proj:
        def __getitem__(self, idx):
            rows, cols = idx
            g0 = (cols.start // RET_W) * RET_W
            return groups[g0][rows, cols.start - g0:cols.stop - g0]

    proj = _Proj()

    ret_units = []
    for c in range(MIX_TOKENS // RET_CHUNK_LEN):
        rows = slice(c * RET_CHUNK_LEN, (c + 1) * RET_CHUNK_LEN)
        cos2 = cos_ref[rows, :]
        sin2 = sin_ref[rows, :]
        for hd in range(RET_HEADS):
            q = _rope(proj[rows, _RQ + hd * LANE:_RQ + (hd + 1) * LANE], cos2, sin2)
            k = _rope(proj[rows, _RK + hd * LANE:_RK + (hd + 1) * LANE], cos2, sin2) * (RET_DK ** -0.5)
            vb = proj[rows, _RV + hd * LANE:_RV + (hd + 1) * LANE].astype(BF16)
            att = (_dot_nt(q.astype(BF16), k.astype(BF16)) * dec_ref[hd]).astype(BF16)
            kv = _dot_tn((k * kdec_ref[hd]).astype(BF16), vb)
            lhs = jnp.concatenate([att, (q * qdec_ref[hd]).astype(BF16)], axis=1)
            ret_units.append((rows, hd, lhs, vb, kv))

    lower = _lower_bound(hglb_ref[...])
    n_sub = HG_CHUNK_LEN // HG_SUB
    crow = lax.broadcasted_iota(jnp.int32, (HG_CHUNK_LEN, HG_CHUNK_LEN), 0)
    ccol = lax.broadcasted_iota(jnp.int32, (HG_CHUNK_LEN, HG_CHUNK_LEN), 1)
    causal = crow >= ccol
    hg_units = []
    for c in range(MIX_TOKENS // HG_CHUNK_LEN):
        rows = slice(c * HG_CHUNK_LEN, (c + 1) * HG_CHUNK_LEN)
        f = lower + (1.0 - lower) * _sigmoid(proj[rows, _HF:_HF + HG_W])
        kk = 1.0 - f
        qq = _silu(proj[rows, _HQ:_HQ + HG_W])
        b = _cumsum_rows(jnp.log(f), HG_CHUNK_LEN)
        b_last = b[HG_CHUNK_LEN - 1:HG_CHUNK_LEN, :]
        q_inter = (qq * jnp.exp(b)).astype(BF16)
        k_upd = (kk * jnp.exp(b_last - b)).astype(BF16)
        d_last = jnp.exp(b_last)
        for hd in range(HG_HEADS):
            cols = slice(hd * LANE, (hd + 1) * LANE)
            bh = b[:, cols]
            qh = qq[:, cols]
            kh = kk[:, cols]
            vb = proj[rows, _HI + hd * LANE:_HI + (hd + 1) * LANE].astype(BF16)
            q_parts, k_parts = [], []
            for j in range(n_sub):
                lo, hi = j * HG_SUB, (j + 1) * HG_SUB
                ref = bh[lo + HG_SUB // 2 - 1:lo + HG_SUB // 2, :]
                qt = qh[lo:, :] * jnp.exp(bh[lo:, :] - ref)
                kt = kh[lo:hi, :] * jnp.exp(ref - bh[lo:hi, :])
                if lo:
                    qt = jnp.concatenate([jnp.zeros((lo, LANE), F32), qt], axis=0)
                    kt = jnp.concatenate([jnp.zeros((lo, LANE), F32), kt], axis=0)
                if hi < HG_CHUNK_LEN:
                    kt = jnp.concatenate([kt, jnp.zeros((HG_CHUNK_LEN - hi, LANE), F32)], axis=0)
                q_parts.append(qt.astype(BF16))
                k_parts.append(kt.astype(BF16))
            qcat = jnp.concatenate(q_parts, axis=1)
            kcat = jnp.concatenate(k_parts, axis=1)
            att = jnp.where(causal, _dot_nt(qcat, kcat), 0.0).astype(BF16)
            kv = _dot_tn(k_upd[:, cols], vb)
            lhs = jnp.concatenate([q_inter[:, cols], att], axis=1)
            hg_units.append((rows, hd, lhs, vb, kv, d_last[:, cols]))

    for hd in range(RET_HEADS):
        cols = slice(hd * LANE, (hd + 1) * LANE)
        s = sret_ref[0, hd]
        for rows, uh, lhs, vb, kv in ret_units:
            if uh != hd:
                continue
            o = _dot(lhs, jnp.concatenate([vb, s.astype(BF16)], axis=0))
            s = sdec_ref[hd] * s + kv
            gate = proj[rows, _RG + hd * LANE:_RG + (hd + 1) * LANE]
            _gate_store(o, rgain_ref[:, cols], gate, omix_scr, rows, cols)
        sret_ref[0, hd] = s

    for hd in range(HG_HEADS):
        cols = slice(hd * LANE, (hd + 1) * LANE)
        mine = [u for u in hg_units if u[1] == hd]
        dl = jnp.concatenate([u[5] for u in mine] + [jnp.zeros((LANE - len(mine), LANE), F32)], axis=0).T
        s = shg_ref[0, hd]
        for i, (rows, _, lhs, vb, kv, _) in enumerate(mine):
            o = _dot(lhs, jnp.concatenate([s.astype(BF16), vb], axis=0))
            s = s * dl[:, i:i + 1] + kv
            gate = proj[rows, _HGATE + hd * LANE:_HGATE + (hd + 1) * LANE]
            _gate_store(o, hgain_ref[:, cols], gate, omix_scr, rows,
                        slice(RET_W + hd * LANE, RET_W + (hd + 1) * LANE))
        shg_ref[0, hd] = s

    x1_ref[...] = x + _dot(omix_scr[...], wout_ref[...])


def _ret_tables(length, period):
    log_g = np.log(1.0 - 2.0 ** (-5.0 - np.arange(RET_HEADS, dtype=np.float64)))
    idx = np.arange(length)
    pos = idx % period
    rel = (idx[:, None] - idx[None, :]).astype(np.float64)
    same = (idx[:, None] // period) == (idx[None, :] // period)
    valid = (rel >= 0) & same
    dec = np.where(valid[None], np.exp(log_g[:, None, None] * np.where(valid, rel, 0.0)[None]), 0.0)
    qdec = np.exp(log_g[:, None] * (pos + 1.0))[:, :, None] * np.ones((1, 1, LANE))
    kdec = np.exp(log_g[:, None] * (period - 1.0 - pos))[:, :, None] * np.ones((1, 1, LANE))
    sdec = np.exp(log_g * period)[:, None, None] * np.ones((1, 1, LANE))
    as32 = lambda a: jnp.asarray(a, dtype=F32)
    return as32(dec), as32(qdec), as32(kdec), as32(sdec)


def _rope_tables(pos):
    half = RET_DK // 2
    inv_freq = ROPE_BASE ** (-np.arange(half, dtype=np.float64) / half)
    ang = np.asarray(pos, dtype=np.float64)[:, None] * inv_freq[None, :]
    cos, sin = np.cos(ang), np.sin(ang)
    return (jnp.asarray(np.concatenate([cos, cos], axis=-1), dtype=F32),
            jnp.asarray(np.concatenate([-sin, sin], axis=-1), dtype=F32))


def _mix_prompt(x2d, batch, seq, g_mix, w_in, ret_gain, hg_gain, hg_lb, w_out,
                q_s, cache_k, cache_v, dec_len):
    nt = seq // MIX_TOKENS
    n_seq = cache_k.shape[0]
    seqs_per_step = n_seq // (batch * nt)
    assert seqs_per_step * batch * nt == n_seq and (seqs_per_step * dec_len) % SUBLANES == 0
    srows = pl.BlockSpec((seqs_per_step * dec_len, D_MODEL), lambda b, t: (b * nt + t, 0))
    cblk = pl.BlockSpec((seqs_per_step, N_MEM, XA_HEADS, XA_HD), lambda b, t: (b * nt + t, 0, 0, 0))
    cos2, sin2 = _rope_tables(np.arange(seq))
    dec, qdec, kdec, sdec = _ret_tables(RET_CHUNK_LEN, RET_CHUNK_LEN)
    c2 = lambda b, t: (0, 0)
    c3 = lambda b, t: (0, 0, 0)
    tok = pl.BlockSpec((MIX_TOKENS, D_MODEL), lambda b, t: (b * nt + t, 0))
    state = pl.BlockSpec((1, RET_HEADS, RET_DK, RET_DV), lambda b, t: (b, 0, 0, 0))
    return pl.pallas_call(
        functools.partial(_mix_kernel, dec_len),
        grid=(batch, nt),
        in_specs=[tok, pl.BlockSpec((1, D_MODEL), c2), _resident((D_MODEL, IN_W)),
                  pl.BlockSpec((1, RET_W), c2), pl.BlockSpec((1, HG_W), c2),
                  pl.BlockSpec(hg_lb.shape, c2), _resident((MIX_W, D_MODEL)),
                  pl.BlockSpec((MIX_TOKENS, LANE), lambda b, t: (t, 0)),
                  pl.BlockSpec((MIX_TOKENS, LANE), lambda b, t: (t, 0)),
                  pl.BlockSpec(dec.shape, c3), pl.BlockSpec(qdec.shape, c3),
                  pl.BlockSpec(kdec.shape, c3), pl.BlockSpec(sdec.shape, c3), srows, cblk, cblk],
        out_specs=[tok, state, state, srows],
        out_shape=[jax.ShapeDtypeStruct(x2d.shape, F32),
                   jax.ShapeDtypeStruct((batch, RET_HEADS, RET_DK, RET_DV), F32),
                   jax.ShapeDtypeStruct((batch, HG_HEADS, HG_DK, HG_DV), F32),
                   jax.ShapeDtypeStruct(q_s.shape, F32)],
        scratch_shapes=[pltpu.VMEM((MIX_TOKENS, MIX_W), BF16)],
        compiler_params=pltpu.CompilerParams(dimension_semantics=("arbitrary", "arbitrary"),
                                             vmem_limit_bytes=VMEM_LIMIT_BYTES),
        name="mix_prompt",
    )(x2d, g_mix, w_in, ret_gain, hg_gain, hg_lb, w_out, cos2, sin2, dec, qdec, kdec, sdec,
      q_s, cache_k, cache_v)


def _inv_rms(x):
    return lax.rsqrt(jnp.mean(x * x, axis=-1, keepdims=True) + EPS)


def _query(x1, gxa, wxq):
    return _dot((x1 * gxa).astype(BF16), wxq) * (_inv_rms(x1) * (XA_HD ** -0.5))


def _ffn_final(x1, ox, wxo, gffn, wgate, wup, wdown, gfinal):
    x2 = x1 + _dot(ox, wxo)
    hb = (x2 * gffn).astype(BF16)
    r = _inv_rms(x2)
    a = (_silu(_dot(hb, wgate) * r) * (_dot(hb, wup) * r)).astype(BF16)
    half = x2.shape[0] // 2
    return jnp.concatenate([_rms(x2[rows] + _dot(a[rows], wdown), gfinal)
                            for rows in (slice(0, half), slice(half, None))], axis=0)


def _tail_kernel(x1_ref, mk_ref, mv_ref, gxa_ref, wxq_ref, wxo_ref, gffn_ref, wgate_ref, wup_ref,
                 wdown_ref, gfinal_ref, y_ref, ox_scr):
    x1 = x1_ref[...]
    q = _query(x1, gxa_ref[...], wxq_ref[...]).astype(BF16)
    heads = [slice(hd * XA_HD, (hd + 1) * XA_HD) for hd in range(XA_HEADS)]
    scores = [_dot_nt(q[:, cols], mk_ref[0, :, cols]) for cols in heads]
    probs = [_softmax_rows(s).astype(BF16) for s in scores]
    for cols, p in zip(heads, probs):
        ox_scr[:, cols] = _dot(p, mv_ref[0, :, cols]).astype(BF16)
    y_ref[...] = _ffn_final(x1, ox_scr[...], wxo_ref[...], gffn_ref[...], wgate_ref[...],
                            wup_ref[...], wdown_ref[...], gfinal_ref[...])


def _tail_prompt(x1, batch, seq, mkb, mvb, g_xa, w_xq, w_xo, g_ffn, w_gate, w_up, w_down, g_final):
    nt = seq // TAIL_TOKENS
    d_ff = w_gate.shape[1]
    tok = pl.BlockSpec((TAIL_TOKENS, D_MODEL), lambda b, t: (b * nt + t, 0))
    mem = pl.BlockSpec((1, N_MEM, D_MODEL), lambda b, t: (b, 0, 0))
    return pl.pallas_call(
        _tail_kernel,
        grid=(batch, nt),
        in_specs=[tok, mem, mem, _resident((1, D_MODEL)), _resident((D_MODEL, D_MODEL)),
                  _resident((D_MODEL, D_MODEL)), _resident((1, D_MODEL)),
                  _resident((D_MODEL, d_ff)), _resident((D_MODEL, d_ff)),
                  _resident((d_ff, D_MODEL)), _resident((1, D_MODEL))],
        out_specs=tok,
        out_shape=jax.ShapeDtypeStruct(x1.shape, F32),
        scratch_shapes=[pltpu.VMEM((TAIL_TOKENS, D_MODEL), BF16)],
        compiler_params=pltpu.CompilerParams(dimension_semantics=("arbitrary", "arbitrary"),
                                             vmem_limit_bytes=VMEM_LIMIT_BYTES),
        name="tail_prompt",
    )(x1, mkb, mvb, g_xa, w_xq, w_xo, g_ffn, w_gate, w_up, w_down, g_final)


def _proj_kernel(x_ref, g_ref, w_ref, o_ref, wb_ref):
    wb = w_ref[...].astype(BF16)
    wb_ref[...] = wb
    x = x_ref[...].reshape(-1, D_MODEL)
    o_ref[...] = _dot(_rms(x, g_ref[...]).astype(BF16), wb)


def _proj_sample(x3d, g_mix, w_in):
    n = x3d.shape[0] * x3d.shape[1]
    nb = IN_W // D_MODEL
    wblk = pl.BlockSpec((D_MODEL, D_MODEL), lambda j: (0, j))
    return pl.pallas_call(
        _proj_kernel,
        grid=(nb,),
        in_specs=[pl.BlockSpec(x3d.shape, lambda j: (0, 0, 0)),
                  pl.BlockSpec((1, D_MODEL), lambda j: (0, 0)), wblk],
        out_specs=[pl.BlockSpec((n, D_MODEL), lambda j: (0, j)), wblk],
        out_shape=[jax.ShapeDtypeStruct((n, IN_W), F32), jax.ShapeDtypeStruct(w_in.shape, BF16)],
        compiler_params=pltpu.CompilerParams(dimension_semantics=("arbitrary",),
                                             vmem_limit_bytes=VMEM_LIMIT_BYTES),
        name="proj_sample",
    )(x3d, g_mix, w_in)


def _state_kernel(dec_len, proj_ref, sret_ref, shg_ref, rgain_ref, hgain_ref, hglb_ref,
                  cos_ref, sin_ref, dec_ref, qdec_ref, kdec_ref, sdec_ref,
                  omix_ref, nret_ref, nhg_ref):
    n_rows = STATE_SEQS * dec_len
    pair_rows = SUBLANES
    seqs_per_pair = pair_rows // dec_len
    row8 = lax.broadcasted_iota(jnp.int32, (pair_rows, LANE), 0)
    rown = lax.broadcasted_iota(jnp.int32, (n_rows, n_rows), 0)
    coln = lax.broadcasted_iota(jnp.int32, (n_rows, n_rows), 1)
    same_seq_causal = (rown >= coln) & ((rown // dec_len) == (coln // dec_len))
    proj = proj_ref[...]
    cos2 = cos_ref[...]
    sin2 = sin_ref[...]

    def per_sequence(q_all, kd_all, vb_all, state_ref, new_ref, hd, scale_of):
        outs = []
        for p in range(n_rows // pair_rows):
            prow = slice(p * pair_rows, (p + 1) * pair_rows)
            q8, kd8, v8 = q_all[prow], kd_all[prow], vb_all[prow]
            acc = jnp.zeros((pair_rows, LANE), F32)
            for j in range(seqs_per_pair):
                seq = p * seqs_per_pair + j
                mine = (row8 >= j * dec_len) & (row8 < (j + 1) * dec_len)
                s0 = state_ref[seq, hd]
                acc = jnp.where(mine, _dot(q8, s0.astype(BF16)), acc)
                kz = jnp.where(mine, kd8, jnp.zeros_like(kd8))
                new_ref[seq, hd] = scale_of(seq) * s0 + _dot_tn(kz, v8)
            outs.append(acc)
        return jnp.concatenate(outs, axis=0)

    for hd in range(RET_HEADS):
        cols = slice(hd * LANE, (hd + 1) * LANE)
        q = _rope(proj[:, _RQ + hd * LANE:_RQ + (hd + 1) * LANE], cos2, sin2)
        k = _rope(proj[:, _RK + hd * LANE:_RK + (hd + 1) * LANE], cos2, sin2) * (RET_DK ** -0.5)
        vb = proj[:, _RV + hd * LANE:_RV + (hd + 1) * LANE].astype(BF16)
        gate = proj[:, _RG + hd * LANE:_RG + (hd + 1) * LANE]
        qb = q.astype(BF16)
        att = _dot_nt(qb, k.astype(BF16)) * dec_ref[hd]
        kd = (k * kdec_ref[hd]).astype(BF16)
        sdec = sdec_ref[hd]
        qs = per_sequence(qb, kd, vb, sret_ref, nret_ref, hd, lambda seq: sdec)
        o = _dot(att.astype(BF16), vb) + qdec_ref[hd] * qs
        _gate_store(o, rgain_ref[:, cols], gate, omix_ref, slice(None), cols)

    lower = _lower_bound(hglb_ref[...])
    f = lower + (1.0 - lower) * _sigmoid(proj[:, _HF:_HF + HG_W])
    kk = 1.0 - f
    qq = _silu(proj[:, _HQ:_HQ + HG_W])
    b = _cumsum_rows(jnp.log(f), dec_len)
    pos = lax.broadcasted_iota(jnp.int32, b.shape, 0) & (dec_len - 1)

    def spread(row_in_seq):
        picked = jnp.where(pos == row_in_seq, b, 0.0)
        out = picked
        for s in range(1, dec_len):
            out = out + jnp.where(pos == (row_in_seq + s), pltpu.roll(picked, s, 0), 0.0)
            out = out + jnp.where(pos == (row_in_seq - s), pltpu.roll(picked, n_rows - s, 0), 0.0)
        return out

    ref = spread(dec_len // 2 - 1)
    b_last = spread(dec_len - 1)
    q_intra = (qq * jnp.exp(b - ref)).astype(BF16)
    k_intra = (kk * jnp.exp(ref - b)).astype(BF16)
    q_inter = (qq * jnp.exp(b)).astype(BF16)
    k_upd = (kk * jnp.exp(b_last - b)).astype(BF16)
    d_last = jnp.exp(b_last)
    for hd in range(HG_HEADS):
        cols = slice(hd * LANE, (hd + 1) * LANE)
        vb = proj[:, _HI + hd * LANE:_HI + (hd + 1) * LANE].astype(BF16)
        gate = proj[:, _HGATE + hd * LANE:_HGATE + (hd + 1) * LANE]
        att = jnp.where(same_seq_causal, _dot_nt(q_intra[:, cols], k_intra[:, cols]), 0.0)
        dpad = jnp.concatenate([d_last[:, cols], jnp.zeros((LANE - n_rows, LANE), F32)], axis=0)
        dcol = dpad.T
        scale_of = lambda seq: dcol[:, seq * dec_len:seq * dec_len + 1]
        qs = per_sequence(q_inter[:, cols], k_upd[:, cols], vb, shg_ref, nhg_ref, hd, scale_of)
        o = _dot(att.astype(BF16), vb) + qs
        _gate_store(o, hgain_ref[:, cols], gate, omix_ref, slice(None),
                    slice(RET_W + hd * LANE, RET_W + (hd + 1) * LANE))


def _state_sample(proj, state_ret, state_hgrn, dec_len, ret_gain, hg_gain, hg_lb):
    n_seq = state_ret.shape[0]
    n_rows = STATE_SEQS * dec_len
    cos2, sin2 = _rope_tables(np.tile(np.arange(dec_len) + PAST_LEN, STATE_SEQS))
    dec, qdec, kdec, sdec = _ret_tables(n_rows, dec_len)
    c2 = lambda i: (0, 0)
    c3 = lambda i: (0, 0, 0)
    state = pl.BlockSpec((STATE_SEQS, RET_HEADS, RET_DK, RET_DV), lambda i: (i, 0, 0, 0))
    return pl.pallas_call(
        functools.partial(_state_kernel, dec_len),
        grid=(n_seq // STATE_SEQS,),
        in_specs=[pl.BlockSpec((n_rows, IN_W), lambda i: (i, 0)), state, state,
                  pl.BlockSpec((1, RET_W), c2), pl.BlockSpec((1, HG_W), c2),
                  pl.BlockSpec(hg_lb.shape, c2),
                  pl.BlockSpec((n_rows, LANE), c2), pl.BlockSpec((n_rows, LANE), c2),
                  pl.BlockSpec(dec.shape, c3), pl.BlockSpec(qdec.shape, c3),
                  pl.BlockSpec(kdec.shape, c3), pl.BlockSpec(sdec.shape, c3)],
        out_specs=[pl.BlockSpec((n_rows, MIX_W), lambda i: (i, 0)), state, state],
        out_shape=[jax.ShapeDtypeStruct((proj.shape[0], MIX_W), BF16),
                   jax.ShapeDtypeStruct(state_ret.shape, F32),
                   jax.ShapeDtypeStruct(state_hgrn.shape, F32)],
        compiler_params=pltpu.CompilerParams(dimension_semantics=("arbitrary",),
                                             vmem_limit_bytes=VMEM_LIMIT_BYTES),
        name="state_sample",
    )(proj, state_ret, state_hgrn, ret_gain, hg_gain, hg_lb, cos2, sin2, dec, qdec, kdec, sdec)


def _outq_kernel(x_ref, omix_ref, wout_ref, gxa_ref, wxq_ref, x1_ref, q_ref, woutb_ref, wxqb_ref):
    wout = wout_ref[...].astype(BF16)
    wxq = wxq_ref[...].astype(BF16)
    woutb_ref[...] = wout
    wxqb_ref[...] = wxq
    x1 = x_ref[...].reshape(-1, D_MODEL) + _dot(omix_ref[...], wout)
    x1_ref[...] = x1
    q_ref[...] = _query(x1, gxa_ref[...], wxq)


def _outq_sample(x3d, omix, w_out, g_xa, w_xq):
    n = x3d.shape[0] * x3d.shape[1]
    z = lambda i: (0, 0)
    return pl.pallas_call(
        _outq_kernel,
        grid=(1,),
        in_specs=[pl.BlockSpec(x3d.shape, lambda i: (0, 0, 0)), pl.BlockSpec((n, MIX_W), z),
                  pl.BlockSpec((MIX_W, D_MODEL), z), pl.BlockSpec((1, D_MODEL), z),
                  pl.BlockSpec((D_MODEL, D_MODEL), z)],
        out_specs=[pl.BlockSpec((n, D_MODEL), z), pl.BlockSpec((n, D_MODEL), z),
                   pl.BlockSpec((MIX_W, D_MODEL), z), pl.BlockSpec((D_MODEL, D_MODEL), z)],
        out_shape=[jax.ShapeDtypeStruct((n, D_MODEL), F32), jax.ShapeDtypeStruct((n, D_MODEL), F32),
                   jax.ShapeDtypeStruct((MIX_W, D_MODEL), BF16),
                   jax.ShapeDtypeStruct((D_MODEL, D_MODEL), BF16)],
        compiler_params=pltpu.CompilerParams(dimension_semantics=("arbitrary",),
                                             vmem_limit_bytes=VMEM_LIMIT_BYTES),
        name="outq_sample",
    )(x3d, omix, w_out, g_xa, w_xq)


def _post_kernel(x1_ref, ox_ref, wxo_ref, gffn_ref, wgate_ref, wup_ref, wdown_ref, gfinal_ref,
                 y_ref, wxob_ref, wgateb_ref, wupb_ref, wdownb_ref, x2_scr, hb_scr, r_scr, acc_scr):
    c = pl.program_id(0)

    @pl.when(c == 0)
    def _():
        wxo = wxo_ref[...].astype(BF16)
        wxob_ref[...] = wxo
        x2 = x1_ref[...] + _dot(ox_ref[...].astype(BF16), wxo)
        x2_scr[...] = x2
        hb_scr[...] = (x2 * gffn_ref[...]).astype(BF16)
        r_scr[...] = jnp.broadcast_to(_inv_rms(x2), r_scr.shape)
        acc_scr[...] = jnp.zeros_like(acc_scr)

    wgate = wgate_ref[...].astype(BF16)
    wup = wup_ref[...].astype(BF16)
    wdown = wdown_ref[...].astype(BF16)
    wgateb_ref[...] = wgate
    wupb_ref[...] = wup
    wdownb_ref[...] = wdown
    hb = hb_scr[...]
    r = r_scr[:, 0:1]
    a = (_silu(_dot(hb, wgate) * r) * (_dot(hb, wup) * r)).astype(BF16)
    acc_scr[...] += _dot(a, wdown)

    @pl.when(c == pl.num_programs(0) - 1)
    def _():
        y_ref[...] = _rms(x2_scr[...] + acc_scr[...], gfinal_ref[...]).reshape(y_ref.shape)


def _post_sample(x1, ox, dec_len, w_xo, g_ffn, w_gate, w_up, w_down, g_final):
    n = x1.shape[0]
    d_ff = w_gate.shape[1]
    chunk = POST_FF_CHUNK
    assert d_ff % chunk == 0
    z = lambda c: (0, 0)
    colblk = pl.BlockSpec((D_MODEL, chunk), lambda c: (0, c))
    rowblk = pl.BlockSpec((chunk, D_MODEL), lambda c: (c, 0))
    full = pl.BlockSpec((n, D_MODEL), z)
    sq = pl.BlockSpec((D_MODEL, D_MODEL), z)
    vec = pl.BlockSpec((1, D_MODEL), z)
    return pl.pallas_call(
        _post_kernel,
        grid=(d_ff // chunk,),
        in_specs=[full, full, sq, vec, colblk, colblk, rowblk, vec],
        out_specs=[pl.BlockSpec((n // dec_len, dec_len, D_MODEL), lambda c: (0, 0, 0)),
                   sq, colblk, colblk, rowblk],
        out_shape=[jax.ShapeDtypeStruct((n // dec_len, dec_len, D_MODEL), F32),
                   jax.ShapeDtypeStruct(w_xo.shape, BF16), jax.ShapeDtypeStruct(w_gate.shape, BF16),
                   jax.ShapeDtypeStruct(w_up.shape, BF16), jax.ShapeDtypeStruct(w_down.shape, BF16)],
        scratch_shapes=[pltpu.VMEM((n, D_MODEL), F32), pltpu.VMEM((n, D_MODEL), BF16),
                        pltpu.VMEM((n, LANE), F32), pltpu.VMEM((n, D_MODEL), F32)],
        compiler_params=pltpu.CompilerParams(dimension_semantics=("arbitrary",),
                                             vmem_limit_bytes=VMEM_LIMIT_BYTES),
        name="post_sample",
    )(x1, ox, w_xo, g_ffn, w_gate, w_up, w_down, g_final)


def kernel(x_prompt, x_sample, mem_prompt, state_ret, state_hgrn, cache_mem_k, cache_mem_v, g_mix, w_in,
           ret_gain, hg_gain, hg_lb, w_out, g_xa, g_mem, w_xq, w_xk, w_xv, w_xo, g_ffn, w_gate, w_up,
           w_down, g_final):
    depth = w_in.shape[0]
    assert depth == 1, "single-layer step"
    batch, seq, d = x_prompt.shape
    dec_batch, dec_len, _ = x_sample.shape
    assert d == D_MODEL and seq % MIX_TOKENS == 0 and seq % TAIL_TOKENS == 0
    assert SUBLANES % dec_len == 0 and dec_batch % STATE_SEQS == 0

    g_final2 = g_final.reshape(1, D_MODEL)

    proj_s, w_in_b = _proj_sample(x_sample, g_mix, w_in[0])
    omix_s, ret_s, hg_s = _state_sample(proj_s, state_ret[0], state_hgrn[0], dec_len,
                                        ret_gain, hg_gain, hg_lb)
    x1_s, q_s, w_out_b, w_xq_b = _outq_sample(x_sample, omix_s, w_out[0], g_xa, w_xq[0])

    mk, mv, mkb, mvb = _memkv(mem_prompt.reshape(batch * N_MEM, D_MODEL), g_mem, w_xk[0], w_xv[0])
    x1_p, ret_p, hg_p, ox_s = _mix_prompt(x_prompt.reshape(batch * seq, D_MODEL), batch, seq, g_mix,
                                          w_in_b, ret_gain, hg_gain, hg_lb, w_out_b,
                                          q_s, cache_mem_k[0], cache_mem_v[0], dec_len)
    y_s, w_xo_b, w_gate_b, w_up_b, w_down_b = _post_sample(x1_s, ox_s, dec_len, w_xo[0], g_ffn, w_gate[0],
                                                           w_up[0], w_down[0], g_final2)

    y_p = _tail_prompt(x1_p, batch, seq, mkb.reshape(batch, N_MEM, D_MODEL),
                       mvb.reshape(batch, N_MEM, D_MODEL), g_xa, w_xq_b, w_xo_b, g_ffn,
                       w_gate_b, w_up_b, w_down_b, g_final2)

    kv_shape = (depth, batch, N_MEM, XA_HEADS, XA_HD)
    return (y_p.reshape(batch, seq, D_MODEL), y_s,
            ret_p[None], hg_p[None], mk.reshape(kv_shape), mv.reshape(kv_shape),
            ret_s[None], hg_s[None])
```

```python
import functools

import jax
import jax.numpy as jnp
import numpy as np
from jax import lax
from jax.experimental import pallas as pl
from jax.experimental.pallas import tpu as pltpu

D_MODEL = 1024
RET_HEADS = 4
RET_DK = 128
RET_DV = 128
RET_W = RET_HEADS * RET_DV
HG_HEADS = 4
HG_DK = 128
HG_DV = 128
HG_W = HG_HEADS * HG_DV
MIX_W = RET_W + HG_W
IN_W = 2 * RET_HEADS * RET_DK + 2 * RET_W + 2 * HG_HEADS * HG_DK + 2 * HG_W
N_MEM = 256
XA_HEADS = 4
XA_HD = D_MODEL // XA_HEADS
PAST_LEN = 16384
ROPE_BASE = 10000.0
EPS = 1e-6
LOG2E = 1.4426950408889634

_RQ, _RK, _RV, _RG = 0, 512, 1024, 1536
_HQ, _HF, _HI, _HGATE = 2048, 2560, 3072, 3584

LANE = 128
SUBLANES = 8
MIX_TOKENS = 512
RET_CHUNK_LEN = 128
HG_CHUNK_LEN = 64
HG_SUB = 16
TAIL_TOKENS = 512
MEMKV_ROWS = 512
STATE_SEQS = 16
POST_FF_CHUNK = 256
VMEM_LIMIT_BYTES = 56 * 1024 * 1024

F32 = jnp.float32
BF16 = jnp.bfloat16


def _dot(a, b):
    return jnp.dot(a, b, preferred_element_type=F32)


def _dot_nt(a, b):
    return lax.dot_general(a, b, (((1,), (1,)), ((), ())), preferred_element_type=F32)


def _dot_tn(a, b):
    return lax.dot_general(a, b, (((0,), (0,)), ((), ())), preferred_element_type=F32)


def _rms(x, g):
    ms = jnp.mean(x * x, axis=-1, keepdims=True)
    return x * lax.rsqrt(ms + EPS) * g


def _sigmoid(x):
    return 1.0 / (1.0 + jnp.exp2(x * (-LOG2E)))


def _silu(x):
    return x * _sigmoid(x)


def _head_norm(o):
    return o * lax.rsqrt(jnp.mean(o * o, axis=-1, keepdims=True) + EPS)


def _softmax_rows(s):
    p = jnp.exp(s - jnp.max(s, axis=-1, keepdims=True))
    return p / jnp.sum(p, axis=-1, keepdims=True)


def _rope(x, cos2, sin2):
    return x * cos2 + pltpu.roll(x, x.shape[-1] // 2, 1) * sin2


def _lower_bound(hglb):
    m = jnp.max(hglb, axis=0, keepdims=True)
    e = jnp.exp(hglb - m)
    return e[0:1, :] / jnp.sum(e, axis=0, keepdims=True)


def _cumsum_rows(x, period):
    row = lax.broadcasted_iota(jnp.int32, x.shape, 0) & (period - 1)
    s = 1
    while s < period:
        x = x + jnp.where(row >= s, pltpu.roll(x, s, 0), 0.0)
        s *= 2
    return x


def _resident(shape):
    zeros = (0,) * len(shape)
    return pl.BlockSpec(shape, lambda *_: zeros, pipeline_mode=pl.Buffered(1))


def _memkv_kernel(mem_ref, g_ref, wk_ref, wv_ref, k_ref, v_ref, kb_ref, vb_ref):
    m = _rms(mem_ref[...], g_ref[...]).astype(BF16)
    k = _dot(m, wk_ref[...].astype(BF16))
    v = _dot(m, wv_ref[...].astype(BF16))
    for hd in range(XA_HEADS):
        cols = slice(hd * XA_HD, (hd + 1) * XA_HD)
        k_ref[:, hd, :] = k[:, cols]
        v_ref[:, hd, :] = v[:, cols]
    kb_ref[...] = k.astype(BF16)
    vb_ref[...] = v.astype(BF16)


def _memkv(mem2d, g_mem, w_xk, w_xv):
    n = mem2d.shape[0]
    full = lambda i: (0, 0)
    row = lambda i: (i, 0)
    blk = pl.BlockSpec((MEMKV_ROWS, D_MODEL), row)
    hblk = pl.BlockSpec((MEMKV_ROWS, XA_HEADS, XA_HD), lambda i: (i, 0, 0))
    return pl.pallas_call(
        _memkv_kernel,
        grid=(n // MEMKV_ROWS,),
        in_specs=[blk, pl.BlockSpec((1, D_MODEL), full),
                  _resident((D_MODEL, D_MODEL)), _resident((D_MODEL, D_MODEL))],
        out_specs=[hblk, hblk, blk, blk],
        out_shape=[jax.ShapeDtypeStruct((n, XA_HEADS, XA_HD), F32),
                   jax.ShapeDtypeStruct((n, XA_HEADS, XA_HD), F32),
                   jax.ShapeDtypeStruct((n, D_MODEL), BF16), jax.ShapeDtypeStruct((n, D_MODEL), BF16)],
        compiler_params=pltpu.CompilerParams(dimension_semantics=("arbitrary",),
                                             vmem_limit_bytes=VMEM_LIMIT_BYTES),
        name="memkv",
    )(mem2d, g_mem, w_xk, w_xv)


def _sample_xattn_stages(dec_len, n_seqs, q_ref, k_ref, v_ref, o_ref):
    pair_rows = SUBLANES
    seqs_per_pair = pair_rows // dec_len
    n_kv = N_MEM * XA_HEADS
    q_rows = XA_HEADS * pair_rows
    own_head = (lax.broadcasted_iota(jnp.int32, (q_rows, n_kv), 1) % XA_HEADS
                == lax.broadcasted_iota(jnp.int32, (q_rows, n_kv), 0) // pair_rows)
    row8 = lax.broadcasted_iota(jnp.int32, (q_rows, XA_HD), 0) % pair_rows
    probs, outs = {}, {}

    def score(seq):
        prow = slice((seq // seqs_per_pair) * pair_rows, (seq // seqs_per_pair + 1) * pair_rows)
        q8 = q_ref[prow, :]
        qs = jnp.concatenate([q8[:, hd * XA_HD:(hd + 1) * XA_HD] for hd in range(XA_HEADS)],
                             axis=0).astype(BF16)
        kb = k_ref[seq].reshape(n_kv, XA_HD).astype(BF16)
        probs[seq] = _softmax_rows(jnp.where(own_head, _dot_nt(qs, kb), -jnp.inf)).astype(BF16)

    def attend(seq):
        p, j = divmod(seq, seqs_per_pair)
        oj = _dot(probs.pop(seq), v_ref[seq].reshape(n_kv, XA_HD).astype(BF16))
        mine = (row8 >= j * dec_len) & (row8 < (j + 1) * dec_len)
        outs[p] = jnp.where(mine, oj, outs[p]) if p in outs else oj
        if j == seqs_per_pair - 1:
            o = outs.pop(p)
            for hd in range(XA_HEADS):
                o_ref[p * pair_rows:(p + 1) * pair_rows, hd * XA_HD:(hd + 1) * XA_HD] = \
                    o[hd * pair_rows:(hd + 1) * pair_rows, :]

    return ([functools.partial(score, s) for s in range(n_seqs)]
            + [functools.partial(attend, s) for s in range(n_seqs)])


def _gate_store(o, gain, gate, out_ref, rows, cols):
    out_ref[rows, cols] = (_head_norm(o) * gain * _silu(gate)).astype(BF16)


def _mix_kernel(dec_len, x_ref, gmix_ref, win_ref, rgain_ref, hgain_ref, hglb_ref, wout_ref,
                cos_ref, sin_ref, dec_ref, qdec_ref, kdec_ref, sdec_ref, qs_ref, ck_ref, cv_ref,
                x1_ref, sret_ref, shg_ref, oxs_ref, omix_scr):
    t = pl.program_id(1)

    @pl.when(t == 0)
    def _():
        sret_ref[...] = jnp.zeros_like(sret_ref)
        shg_ref[...] = jnp.zeros_like(shg_ref)

    x = x_ref[...]
    h = _rms(x, gmix_ref[...]).astype(BF16)
    side_work = _sample_xattn_stages(dec_len, ck_ref.shape[0], qs_ref, ck_ref, cv_ref, oxs_ref)
    groups = {}
    for g0 in (_HF, _HQ, _RQ, _RK, _RV, _HI, _RG, _HGATE):
        groups[g0] = _dot(h, win_ref[:, g0:g0 + RET_W])
        if side_work:
            side_work.pop(0)()
    while side_work:
        side_work.pop(0)()

    class _Proj:
        def __getitem__(self, idx):
            rows, cols = idx
            g0 = (cols.start // RET_W) * RET_W
            return groups[g0][rows, cols.start - g0:cols.stop - g0]

    proj = _Proj()

    ret_units = []
    for c in range(MIX_TOKENS // RET_CHUNK_LEN):
        rows = slice(c * RET_CHUNK_LEN, (c + 1) * RET_CHUNK_LEN)
        cos2 = cos_ref[rows, :]
        sin2 = sin_ref[rows, :]
        for hd in range(RET_HEADS):
            q = _rope(proj[rows, _RQ + hd * LANE:_RQ + (hd + 1) * LANE], cos2, sin2)
            k = _rope(proj[rows, _RK + hd * LANE:_RK + (hd + 1) * LANE], cos2, sin2) * (RET_DK ** -0.5)
            vb = proj[rows, _RV + hd * LANE:_RV + (hd + 1) * LANE].astype(BF16)
            att = (_dot_nt(q.astype(BF16), k.astype(BF16)) * dec_ref[hd]).astype(BF16)
            kv = _dot_tn((k * kdec_ref[hd]).astype(BF16), vb)
            lhs = jnp.concatenate([att, (q * qdec_ref[hd]).astype(BF16)], axis=1)
            ret_units.append((rows, hd, lhs, vb, kv))

    lower = _lower_bound(hglb_ref[...])
    n_sub = HG_CHUNK_LEN // HG_SUB
    crow = lax.broadcasted_iota(jnp.int32, (HG_CHUNK_LEN, HG_CHUNK_LEN), 0)
    ccol = lax.broadcasted_iota(jnp.int32, (HG_CHUNK_LEN, HG_CHUNK_LEN), 1)
    causal = crow >= ccol
    hg_units = []
    for c in range(MIX_TOKENS // HG_CHUNK_LEN):
        rows = slice(c * HG_CHUNK_LEN, (c + 1) * HG_CHUNK_LEN)
        f = lower + (1.0 - lower) * _sigmoid(proj[rows, _HF:_HF + HG_W])
        kk = 1.0 - f
        qq = _silu(proj[rows, _HQ:_HQ + HG_W])
        b = _cumsum_rows(jnp.log2(f), HG_CHUNK_LEN)
        b_last = b[HG_CHUNK_LEN - 1:HG_CHUNK_LEN, :]
        q_inter = (qq * jnp.exp2(b)).astype(BF16)
        k_upd = (kk * jnp.exp2(b_last - b)).astype(BF16)
        d_last = jnp.exp2(b_last)
        for hd in range(HG_HEADS):
            cols = slice(hd * LANE, (hd + 1) * LANE)
            bh = b[:, cols]
            qh = qq[:, cols]
            kh = kk[:, cols]
            vb = proj[rows, _HI + hd * LANE:_HI + (hd + 1) * LANE].astype(BF16)
            q_parts, k_parts = [], []
            for j in range(n_sub):
                lo, hi = j * HG_SUB, (j + 1) * HG_SUB
                ref = bh[lo + HG_SUB // 2 - 1:lo + HG_SUB // 2, :]
                qt = qh[lo:, :] * jnp.exp2(bh[lo:, :] - ref)
                kt = kh[lo:hi, :] * jnp.exp2(ref - bh[lo:hi, :])
                if lo:
                    qt = jnp.concatenate([jnp.zeros((lo, LANE), F32), qt], axis=0)
                    kt = jnp.concatenate([jnp.zeros((lo, LANE), F32), kt], axis=0)
                if hi < HG_CHUNK_LEN:
                    kt = jnp.concatenate([kt, jnp.zeros((HG_CHUNK_LEN - hi, LANE), F32)], axis=0)
                q_parts.append(qt.astype(BF16))
                k_parts.append(kt.astype(BF16))
            qcat = jnp.concatenate(q_parts, axis=1)
            kcat = jnp.concatenate(k_parts, axis=1)
            att = jnp.where(causal, _dot_nt(qcat, kcat), 0.0).astype(BF16)
            kv = _dot_tn(k_upd[:, cols], vb)
            lhs = jnp.concatenate([q_inter[:, cols], att], axis=1)
            hg_units.append((rows, hd, lhs, vb, kv, d_last[:, cols]))

    for hd in range(RET_HEADS):
        cols = slice(hd * LANE, (hd + 1) * LANE)
        s = sret_ref[0, hd]
        for rows, uh, lhs, vb, kv in ret_units:
            if uh != hd:
                continue
            o = _dot(lhs, jnp.concatenate([vb, s.astype(BF16)], axis=0))
            s = sdec_ref[hd] * s + kv
            gate = proj[rows, _RG + hd * LANE:_RG + (hd + 1) * LANE]
            _gate_store(o, rgain_ref[:, cols], gate, omix_scr, rows, cols)
        sret_ref[0, hd] = s

    for hd in range(HG_HEADS):
        cols = slice(hd * LANE, (hd + 1) * LANE)
        mine = [u for u in hg_units if u[1] == hd]
        dl = jnp.concatenate([u[5] for u in mine] + [jnp.zeros((LANE - len(mine), LANE), F32)], axis=0).T
        s = shg_ref[0, hd]
        for i, (rows, _, lhs, vb, kv, _) in enumerate(mine):
            o = _dot(lhs, jnp.concatenate([s.astype(BF16), vb], axis=0))
            s = s * dl[:, i:i + 1] + kv
            gate = proj[rows, _HGATE + hd * LANE:_HGATE + (hd + 1) * LANE]
            _gate_store(o, hgain_ref[:, cols], gate, omix_scr, rows,
                        slice(RET_W + hd * LANE, RET_W + (hd + 1) * LANE))
        shg_ref[0, hd] = s

    x1_ref[...] = x + _dot(omix_scr[...], wout_ref[...])


def _ret_tables(length, period):
    log_g = np.log(1.0 - 2.0 ** (-5.0 - np.arange(RET_HEADS, dtype=np.float64)))
    idx = np.arange(length)
    pos = idx % period
    rel = (idx[:, None] - idx[None, :]).astype(np.float64)
    same = (idx[:, None] // period) == (idx[None, :] // period)
    valid = (rel >= 0) & same
    dec = np.where(valid[None], np.exp(log_g[:, None, None] * np.where(valid, rel, 0.0)[None]), 0.0)
    qdec = np.exp(log_g[:, None] * (pos + 1.0))[:, :, None] * np.ones((1, 1, LANE))
    kdec = np.exp(log_g[:, None] * (period - 1.0 - pos))[:, :, None] * np.ones((1, 1, LANE))
    sdec = np.exp(log_g * period)[:, None, None] * np.ones((1, 1, LANE))
    as32 = lambda a: jnp.asarray(a, dtype=F32)
    return as32(dec), as32(qdec), as32(kdec), as32(sdec)


def _rope_tables(pos):
    half = RET_DK // 2
    inv_freq = ROPE_BASE ** (-np.arange(half, dtype=np.float64) / half)
    ang = np.asarray(pos, dtype=np.float64)[:, None] * inv_freq[None, :]
    cos, sin = np.cos(ang), np.sin(ang)
    return (jnp.asarray(np.concatenate([cos, cos], axis=-1), dtype=F32),
            jnp.asarray(np.concatenate([-sin, sin], axis=-1), dtype=F32))


def _mix_prompt(x2d, batch, seq, g_mix, w_in, ret_gain, hg_gain, hg_lb, w_out,
                q_s, cache_k, cache_v, dec_len):
    nt = seq // MIX_TOKENS
    n_seq = cache_k.shape[0]
    seqs_per_step = n_seq // (batch * nt)
    assert seqs_per_step * batch * nt == n_seq and (seqs_per_step * dec_len) % SUBLANES == 0
    srows = pl.BlockSpec((seqs_per_step * dec_len, D_MODEL), lambda b, t: (b * nt + t, 0))
    cblk = pl.BlockSpec((seqs_per_step, N_MEM, XA_HEADS, XA_HD), lambda b, t: (b * nt + t, 0, 0, 0))
    cos2, sin2 = _rope_tables(np.arange(seq))
    dec, qdec, kdec, sdec = _ret_tables(RET_CHUNK_LEN, RET_CHUNK_LEN)
    c2 = lambda b, t: (0, 0)
    c3 = lambda b, t: (0, 0, 0)
    tok = pl.BlockSpec((MIX_TOKENS, D_MODEL), lambda b, t: (b * nt + t, 0))
    state = pl.BlockSpec((1, RET_HEADS, RET_DK, RET_DV), lambda b, t: (b, 0, 0, 0))
    return pl.pallas_call(
        functools.partial(_mix_kernel, dec_len),
        grid=(batch, nt),
        in_specs=[tok, pl.BlockSpec((1, D_MODEL), c2), _resident((D_MODEL, IN_W)),
                  pl.BlockSpec((1, RET_W), c2), pl.BlockSpec((1, HG_W), c2),
                  pl.BlockSpec(hg_lb.shape, c2), _resident((MIX_W, D_MODEL)),
                  pl.BlockSpec((MIX_TOKENS, LANE), lambda b, t: (t, 0)),
                  pl.BlockSpec((MIX_TOKENS, LANE), lambda b, t: (t, 0)),
                  pl.BlockSpec(dec.shape, c3), pl.BlockSpec(qdec.shape, c3),
                  pl.BlockSpec(kdec.shape, c3), pl.BlockSpec(sdec.shape, c3), srows, cblk, cblk],
        out_specs=[tok, state, state, srows],
        out_shape=[jax.ShapeDtypeStruct(x2d.shape, F32),
                   jax.ShapeDtypeStruct((batch, RET_HEADS, RET_DK, RET_DV), F32),
                   jax.ShapeDtypeStruct((batch, HG_HEADS, HG_DK, HG_DV), F32),
                   jax.ShapeDtypeStruct(q_s.shape, F32)],
        scratch_shapes=[pltpu.VMEM((MIX_TOKENS, MIX_W), BF16)],
        compiler_params=pltpu.CompilerParams(dimension_semantics=("arbitrary", "arbitrary"),
                                             vmem_limit_bytes=VMEM_LIMIT_BYTES),
        name="mix_prompt",
    )(x2d, g_mix, w_in, ret_gain, hg_gain, hg_lb, w_out, cos2, sin2, dec, qdec, kdec, sdec,
      q_s, cache_k, cache_v)


def _inv_rms(x):
    return lax.rsqrt(jnp.mean(x * x, axis=-1, keepdims=True) + EPS)


def _query(x1, gxa, wxq):
    return _dot((x1 * gxa).astype(BF16), wxq) * (_inv_rms(x1) * (XA_HD ** -0.5))


def _ffn_final(x1, ox, wxo, gffn, wgate, wup, wdown, gfinal):
    x2 = x1 + _dot(ox, wxo)
    hb = (x2 * gffn).astype(BF16)
    r = _inv_rms(x2)
    a = (_silu(_dot(hb, wgate) * r) * (_dot(hb, wup) * r)).astype(BF16)
    half = x2.shape[0] // 2
    return jnp.concatenate([_rms(x2[rows] + _dot(a[rows], wdown), gfinal)
                            for rows in (slice(0, half), slice(half, None))], axis=0)


def _tail_kernel(x1_ref, mk_ref, mv_ref, gxa_ref, wxq_ref, wxo_ref, gffn_ref, wgate_ref, wup_ref,
                 wdown_ref, gfinal_ref, y_ref, ox_scr):
    x1 = x1_ref[...]
    q = _query(x1, gxa_ref[...], wxq_ref[...]).astype(BF16)
    heads = [slice(hd * XA_HD, (hd + 1) * XA_HD) for hd in range(XA_HEADS)]
    scores = [_dot_nt(q[:, cols], mk_ref[0, :, cols]) for cols in heads]
    probs = [_softmax_rows(s).astype(BF16) for s in scores]
    for cols, p in zip(heads, probs):
        ox_scr[:, cols] = _dot(p, mv_ref[0, :, cols]).astype(BF16)
    y_ref[...] = _ffn_final(x1, ox_scr[...], wxo_ref[...], gffn_ref[...], wgate_ref[...],
                            wup_ref[...], wdown_ref[...], gfinal_ref[...])


def _tail_prompt(x1, batch, seq, mkb, mvb, g_xa, w_xq, w_xo, g_ffn, w_gate, w_up, w_down, g_final):
    nt = seq // TAIL_TOKENS
    d_ff = w_gate.shape[1]
    tok = pl.BlockSpec((TAIL_TOKENS, D_MODEL), lambda b, t: (b * nt + t, 0))
    mem = pl.BlockSpec((1, N_MEM, D_MODEL), lambda b, t: (b, 0, 0))
    return pl.pallas_call(
        _tail_kernel,
        grid=(batch, nt),
        in_specs=[tok, mem, mem, _resident((1, D_MODEL)), _resident((D_MODEL, D_MODEL)),
                  _resident((D_MODEL, D_MODEL)), _resident((1, D_MODEL)),
                  _resident((D_MODEL, d_ff)), _resident((D_MODEL, d_ff)),
                  _resident((d_ff, D_MODEL)), _resident((1, D_MODEL))],
        out_specs=tok,
        out_shape=jax.ShapeDtypeStruct(x1.shape, F32),
        scratch_shapes=[pltpu.VMEM((TAIL_TOKENS, D_MODEL), BF16)],
        compiler_params=pltpu.CompilerParams(dimension_semantics=("arbitrary", "arbitrary"),
                                             vmem_limit_bytes=VMEM_LIMIT_BYTES),
        name="tail_prompt",
    )(x1, mkb, mvb, g_xa, w_xq, w_xo, g_ffn, w_gate, w_up, w_down, g_final)


def _proj_kernel(x_ref, g_ref, w_ref, o_ref, wb_ref):
    wb = w_ref[...].astype(BF16)
    wb_ref[...] = wb
    x = x_ref[...].reshape(-1, D_MODEL)
    o_ref[...] = _dot(_rms(x, g_ref[...]).astype(BF16), wb)


def _proj_sample(x3d, g_mix, w_in):
    n = x3d.shape[0] * x3d.shape[1]
    nb = IN_W // D_MODEL
    wblk = pl.BlockSpec((D_MODEL, D_MODEL), lambda j: (0, j))
    return pl.pallas_call(
        _proj_kernel,
        grid=(nb,),
        in_specs=[pl.BlockSpec(x3d.shape, lambda j: (0, 0, 0)),
                  pl.BlockSpec((1, D_MODEL), lambda j: (0, 0)), wblk],
        out_specs=[pl.BlockSpec((n, D_MODEL), lambda j: (0, j)), wblk],
        out_shape=[jax.ShapeDtypeStruct((n, IN_W), F32), jax.ShapeDtypeStruct(w_in.shape, BF16)],
        compiler_params=pltpu.CompilerParams(dimension_semantics=("arbitrary",),
                                             vmem_limit_bytes=VMEM_LIMIT_BYTES),
        name="proj_sample",
    )(x3d, g_mix, w_in)


def _state_kernel(dec_len, proj_ref, sret_ref, shg_ref, rgain_ref, hgain_ref, hglb_ref,
                  cos_ref, sin_ref, dec_ref, qdec_ref, kdec_ref, sdec_ref,
                  omix_ref, nret_ref, nhg_ref):
    n_rows = STATE_SEQS * dec_len
    pair_rows = SUBLANES
    seqs_per_pair = pair_rows // dec_len
    row8 = lax.broadcasted_iota(jnp.int32, (pair_rows, LANE), 0)
    rown = lax.broadcasted_iota(jnp.int32, (n_rows, n_rows), 0)
    coln = lax.broadcasted_iota(jnp.int32, (n_rows, n_rows), 1)
    same_seq_causal = (rown >= coln) & ((rown // dec_len) == (coln // dec_len))
    proj = proj_ref[...]
    cos2 = cos_ref[...]
    sin2 = sin_ref[...]

    def per_sequence(q_all, kd_all, vb_all, state_ref, new_ref, hd, scale_of):
        outs = []
        for p in range(n_rows // pair_rows):
            prow = slice(p * pair_rows, (p + 1) * pair_rows)
            q8, kd8, v8 = q_all[prow], kd_all[prow], vb_all[prow]
            acc = jnp.zeros((pair_rows, LANE), F32)
            for j in range(seqs_per_pair):
                seq = p * seqs_per_pair + j
                mine = (row8 >= j * dec_len) & (row8 < (j + 1) * dec_len)
                s0 = state_ref[seq, hd]
                acc = jnp.where(mine, _dot(q8, s0.astype(BF16)), acc)
                kz = jnp.where(mine, kd8, jnp.zeros_like(kd8))
                new_ref[seq, hd] = scale_of(seq) * s0 + _dot_tn(kz, v8)
            outs.append(acc)
        return jnp.concatenate(outs, axis=0)

    for hd in range(RET_HEADS):
        cols = slice(hd * LANE, (hd + 1) * LANE)
        q = _rope(proj[:, _RQ + hd * LANE:_RQ + (hd + 1) * LANE], cos2, sin2)
        k = _rope(proj[:, _RK + hd * LANE:_RK + (hd + 1) * LANE], cos2, sin2) * (RET_DK ** -0.5)
        vb = proj[:, _RV + hd * LANE:_RV + (hd + 1) * LANE].astype(BF16)
        gate = proj[:, _RG + hd * LANE:_RG + (hd + 1) * LANE]
        qb = q.astype(BF16)
        att = _dot_nt(qb, k.astype(BF16)) * dec_ref[hd]
        kd = (k * kdec_ref[hd]).astype(BF16)
        sdec = sdec_ref[hd]
        qs = per_sequence(qb, kd, vb, sret_ref, nret_ref, hd, lambda seq: sdec)
        o = _dot(att.astype(BF16), vb) + qdec_ref[hd] * qs
        _gate_store(o, rgain_ref[:, cols], gate, omix_ref, slice(None), cols)

    lower = _lower_bound(hglb_ref[...])
    f = lower + (1.0 - lower) * _sigmoid(proj[:, _HF:_HF + HG_W])
    kk = 1.0 - f
    qq = _silu(proj[:, _HQ:_HQ + HG_W])
    b = _cumsum_rows(jnp.log2(f), dec_len)
    pos = lax.broadcasted_iota(jnp.int32, b.shape, 0) & (dec_len - 1)

    def spread(row_in_seq):
        picked = jnp.where(pos == row_in_seq, b, 0.0)
        out = picked
        for s in range(1, dec_len):
            out = out + jnp.where(pos == (row_in_seq + s), pltpu.roll(picked, s, 0), 0.0)
            out = out + jnp.where(pos == (row_in_seq - s), pltpu.roll(picked, n_rows - s, 0), 0.0)
        return out

    ref = spread(dec_len // 2 - 1)
    b_last = spread(dec_len - 1)
    q_intra = (qq * jnp.exp2(b - ref)).astype(BF16)
    k_intra = (kk * jnp.exp2(ref - b)).astype(BF16)
    q_inter = (qq * jnp.exp2(b)).astype(BF16)
    k_upd = (kk * jnp.exp2(b_last - b)).astype(BF16)
    d_last = jnp.exp2(b_last)
    for hd in range(HG_HEADS):
        cols = slice(hd * LANE, (hd + 1) * LANE)
        vb = proj[:, _HI + hd * LANE:_HI + (hd + 1) * LANE].astype(BF16)
        gate = proj[:, _HGATE + hd * LANE:_HGATE + (hd + 1) * LANE]
        att = jnp.where(same_seq_causal, _dot_nt(q_intra[:, cols], k_intra[:, cols]), 0.0)
        dpad = jnp.concatenate([d_last[:, cols], jnp.zeros((LANE - n_rows, LANE), F32)], axis=0)
        dcol = dpad.T
        scale_of = lambda seq: dcol[:, seq * dec_len:seq * dec_len + 1]
        qs = per_sequence(q_inter[:, cols], k_upd[:, cols], vb, shg_ref, nhg_ref, hd, scale_of)
        o = _dot(att.astype(BF16), vb) + qs
        _gate_store(o, hgain_ref[:, cols], gate, omix_ref, slice(None),
                    slice(RET_W + hd * LANE, RET_W + (hd + 1) * LANE))


def _state_sample(proj, state_ret, state_hgrn, dec_len, ret_gain, hg_gain, hg_lb):
    n_seq = state_ret.shape[0]
    n_rows = STATE_SEQS * dec_len
    cos2, sin2 = _rope_tables(np.tile(np.arange(dec_len) + PAST_LEN, STATE_SEQS))
    dec, qdec, kdec, sdec = _ret_tables(n_rows, dec_len)
    c2 = lambda i: (0, 0)
    c3 = lambda i: (0, 0, 0)
    state = pl.BlockSpec((STATE_SEQS, RET_HEADS, RET_DK, RET_DV), lambda i: (i, 0, 0, 0))
    return pl.pallas_call(
        functools.partial(_state_kernel, dec_len),
        grid=(n_seq // STATE_SEQS,),
        in_specs=[pl.BlockSpec((n_rows, IN_W), lambda i: (i, 0)), state, state,
                  pl.BlockSpec((1, RET_W), c2), pl.BlockSpec((1, HG_W), c2),
                  pl.BlockSpec(hg_lb.shape, c2),
                  pl.BlockSpec((n_rows, LANE), c2), pl.BlockSpec((n_rows, LANE), c2),
                  pl.BlockSpec(dec.shape, c3), pl.BlockSpec(qdec.shape, c3),
                  pl.BlockSpec(kdec.shape, c3), pl.BlockSpec(sdec.shape, c3)],
        out_specs=[pl.BlockSpec((n_rows, MIX_W), lambda i: (i, 0)), state, state],
        out_shape=[jax.ShapeDtypeStruct((proj.shape[0], MIX_W), BF16),
                   jax.ShapeDtypeStruct(state_ret.shape, F32),
                   jax.ShapeDtypeStruct(state_hgrn.shape, F32)],
        compiler_params=pltpu.CompilerParams(dimension_semantics=("arbitrary",),
                                             vmem_limit_bytes=VMEM_LIMIT_BYTES),
        name="state_sample",
    )(proj, state_ret, state_hgrn, ret_gain, hg_gain, hg_lb, cos2, sin2, dec, qdec, kdec, sdec)


def _outq_kernel(x_ref, omix_ref, wout_ref, gxa_ref, wxq_ref, x1_ref, q_ref, woutb_ref, wxqb_ref):
    wout = wout_ref[...].astype(BF16)
    wxq = wxq_ref[...].astype(BF16)
    woutb_ref[...] = wout
    wxqb_ref[...] = wxq
    x1 = x_ref[...].reshape(-1, D_MODEL) + _dot(omix_ref[...], wout)
    x1_ref[...] = x1
    q_ref[...] = _query(x1, gxa_ref[...], wxq)


def _outq_sample(x3d, omix, w_out, g_xa, w_xq):
    n = x3d.shape[0] * x3d.shape[1]
    z = lambda i: (0, 0)
    return pl.pallas_call(
        _outq_kernel,
        grid=(1,),
        in_specs=[pl.BlockSpec(x3d.shape, lambda i: (0, 0, 0)), pl.BlockSpec((n, MIX_W), z),
                  pl.BlockSpec((MIX_W, D_MODEL), z), pl.BlockSpec((1, D_MODEL), z),
                  pl.BlockSpec((D_MODEL, D_MODEL), z)],
        out_specs=[pl.BlockSpec((n, D_MODEL), z), pl.BlockSpec((n, D_MODEL), z),
                   pl.BlockSpec((MIX_W, D_MODEL), z), pl.BlockSpec((D_MODEL, D_MODEL), z)],
        out_shape=[jax.ShapeDtypeStruct((n, D_MODEL), F32), jax.ShapeDtypeStruct((n, D_MODEL), F32),
                   jax.ShapeDtypeStruct((MIX_W, D_MODEL), BF16),
                   jax.ShapeDtypeStruct((D_MODEL, D_MODEL), BF16)],
        compiler_params=pltpu.CompilerParams(dimension_semantics=("arbitrary",),
                                             vmem_limit_bytes=VMEM_LIMIT_BYTES),
        name="outq_sample",
    )(x3d, omix, w_out, g_xa, w_xq)


def _post_kernel(x1_ref, ox_ref, wxo_ref, gffn_ref, wgate_ref, wup_ref, wdown_ref, gfinal_ref,
                 y_ref, wxob_ref, wgateb_ref, wupb_ref, wdownb_ref, x2_scr, hb_scr, r_scr, acc_scr):
    c = pl.program_id(0)

    @pl.when(c == 0)
    def _():
        wxo = wxo_ref[...].astype(BF16)
        wxob_ref[...] = wxo
        x2 = x1_ref[...] + _dot(ox_ref[...].astype(BF16), wxo)
        x2_scr[...] = x2
        hb_scr[...] = (x2 * gffn_ref[...]).astype(BF16)
        r_scr[...] = jnp.broadcast_to(_inv_rms(x2), r_scr.shape)
        acc_scr[...] = jnp.zeros_like(acc_scr)

    wgate = wgate_ref[...].astype(BF16)
    wup = wup_ref[...].astype(BF16)
    wdown = wdown_ref[...].astype(BF16)
    wgateb_ref[...] = wgate
    wupb_ref[...] = wup
    wdownb_ref[...] = wdown
    hb = hb_scr[...]
    r = r_scr[:, 0:1]
    a = (_silu(_dot(hb, wgate) * r) * (_dot(hb, wup) * r)).astype(BF16)
    acc_scr[...] += _dot(a, wdown)

    @pl.when(c == pl.num_programs(0) - 1)
    def _():
        y_ref[...] = _rms(x2_scr[...] + acc_scr[...], gfinal_ref[...]).reshape(y_ref.shape)


def _post_sample(x1, ox, dec_len, w_xo, g_ffn, w_gate, w_up, w_down, g_final):
    n = x1.shape[0]
    d_ff = w_gate.shape[1]
    chunk = POST_FF_CHUNK
    assert d_ff % chunk == 0
    z = lambda c: (0, 0)
    colblk = pl.BlockSpec((D_MODEL, chunk), lambda c: (0, c))
    rowblk = pl.BlockSpec((chunk, D_MODEL), lambda c: (c, 0))
    full = pl.BlockSpec((n, D_MODEL), z)
    sq = pl.BlockSpec((D_MODEL, D_MODEL), z)
    vec = pl.BlockSpec((1, D_MODEL), z)
    return pl.pallas_call(
        _post_kernel,
        grid=(d_ff // chunk,),
        in_specs=[full, full, sq, vec, colblk, colblk, rowblk, vec],
        out_specs=[pl.BlockSpec((n // dec_len, dec_len, D_MODEL), lambda c: (0, 0, 0)),
                   sq, colblk, colblk, rowblk],
        out_shape=[jax.ShapeDtypeStruct((n // dec_len, dec_len, D_MODEL), F32),
                   jax.ShapeDtypeStruct(w_xo.shape, BF16), jax.ShapeDtypeStruct(w_gate.shape, BF16),
                   jax.ShapeDtypeStruct(w_up.shape, BF16), jax.ShapeDtypeStruct(w_down.shape, BF16)],
        scratch_shapes=[pltpu.VMEM((n, D_MODEL), F32), pltpu.VMEM((n, D_MODEL), BF16),
                        pltpu.VMEM((n, LANE), F32), pltpu.VMEM((n, D_MODEL), F32)],
        compiler_params=pltpu.CompilerParams(dimension_semantics=("arbitrary",),
                                             vmem_limit_bytes=VMEM_LIMIT_BYTES),
        name="post_sample",
    )(x1, ox, w_xo, g_ffn, w_gate, w_up, w_down, g_final)


def kernel(x_prompt, x_sample, mem_prompt, state_ret, state_hgrn, cache_mem_k, cache_mem_v, g_mix, w_in,
           ret_gain, hg_gain, hg_lb, w_out, g_xa, g_mem, w_xq, w_xk, w_xv, w_xo, g_ffn, w_gate, w_up,
           w_down, g_final):
    depth = w_in.shape[0]
    assert depth == 1, "single-layer step"
    batch, seq, d = x_prompt.shape
    dec_batch, dec_len, _ = x_sample.shape
    assert d == D_MODEL and seq % MIX_TOKENS == 0 and seq % TAIL_TOKENS == 0
    assert SUBLANES % dec_len == 0 and dec_batch % STATE_SEQS == 0

    g_final2 = g_final.reshape(1, D_MODEL)

    proj_s, w_in_b = _proj_sample(x_sample, g_mix, w_in[0])
    omix_s, ret_s, hg_s = _state_sample(proj_s, state_ret[0], state_hgrn[0], dec_len,
                                        ret_gain, hg_gain, hg_lb)
    x1_s, q_s, w_out_b, w_xq_b = _outq_sample(x_sample, omix_s, w_out[0], g_xa, w_xq[0])

    mk, mv, mkb, mvb = _memkv(mem_prompt.reshape(batch * N_MEM, D_MODEL), g_mem, w_xk[0], w_xv[0])
    x1_p, ret_p, hg_p, ox_s = _mix_prompt(x_prompt.reshape(batch * seq, D_MODEL), batch, seq, g_mix,
                                          w_in_b, ret_gain, hg_gain, hg_lb, w_out_b,
                                          q_s, cache_mem_k[0], cache_mem_v[0], dec_len)
    y_s, w_xo_b, w_gate_b, w_up_b, w_down_b = _post_sample(x1_s, ox_s, dec_len, w_xo[0], g_ffn, w_gate[0],
                                                           w_up[0], w_down[0], g_final2)

    y_p = _tail_prompt(x1_p, batch, seq, mkb.reshape(batch, N_MEM, D_MODEL),
                       mvb.reshape(batch, N_MEM, D_MODEL), g_xa, w_xq_b, w_xo_b, g_ffn,
                       w_gate_b, w_up_b, w_down_b, g_final2)

    kv_shape = (depth, batch, N_MEM, XA_HEADS, XA_HD)
    return (y_p.reshape(batch, seq, D_MODEL), y_s,
            ret_p[None], hg_p[None], mk.reshape(kv_shape), mv.reshape(kv_shape),
            ret_s[None], hg_s[None])
```

```python
import functools

import jax
import jax.numpy as jnp
import numpy as np
from jax import lax
from jax.experimental import pallas as pl
from jax.experimental.pallas import tpu as pltpu

D_MODEL = 1024
RET_HEADS = 4
RET_DK = 128
RET_DV = 128
RET_W = RET_HEADS * RET_DV
HG_HEADS = 4
HG_DK = 128
HG_DV = 128
HG_W = HG_HEADS * HG_DV
MIX_W = RET_W + HG_W
IN_W = 2 * RET_HEADS * RET_DK + 2 * RET_W + 2 * HG_HEADS * HG_DK + 2 * HG_W
N_MEM = 256
XA_HEADS = 4
XA_HD = D_MODEL // XA_HEADS
PAST_LEN = 16384
ROPE_BASE = 10000.0
EPS = 1e-6
LOG2E = 1.4426950408889634

_RQ, _RK, _RV, _RG = 0, 512, 1024, 1536
_HQ, _HF, _HI, _HGATE = 2048, 2560, 3072, 3584

LANE = 128
SUBLANES = 8
MIX_TOKENS = 512
RET_CHUNK_LEN = 128
HG_CHUNK_LEN = 64
HG_SUB = 16
TAIL_TOKENS = 512
MEMKV_ROWS = 512
STATE_SEQS = 16
CAST_DOWN_ROWS = 128
VMEM_LIMIT_BYTES = 56 * 1024 * 1024

F32 = jnp.float32
BF16 = jnp.bfloat16


def _dot(a, b):
    return jnp.dot(a, b, preferred_element_type=F32)


def _dot_nt(a, b):
    return lax.dot_general(a, b, (((1,), (1,)), ((), ())), preferred_element_type=F32)


def _dot_tn(a, b):
    return lax.dot_general(a, b, (((0,), (0,)), ((), ())), preferred_element_type=F32)


def _rms(x, g):
    ms = jnp.mean(x * x, axis=-1, keepdims=True)
    return x * lax.rsqrt(ms + EPS) * g


def _sigmoid(x):
    return 1.0 / (1.0 + jnp.exp2(x * (-LOG2E)))


def _silu(x):
    return x * _sigmoid(x)


def _head_norm(o):
    return o * lax.rsqrt(jnp.mean(o * o, axis=-1, keepdims=True) + EPS)


def _softmax_rows(s):
    p = jnp.exp(s - jnp.max(s, axis=-1, keepdims=True))
    return p / jnp.sum(p, axis=-1, keepdims=True)


def _rope(x, cos2, sin2):
    return x * cos2 + pltpu.roll(x, x.shape[-1] // 2, 1) * sin2


def _lower_bound(hglb):
    m = jnp.max(hglb, axis=0, keepdims=True)
    e = jnp.exp(hglb - m)
    return e[0:1, :] / jnp.sum(e, axis=0, keepdims=True)


def _cumsum_rows(x, period):
    row = lax.broadcasted_iota(jnp.int32, x.shape, 0) & (period - 1)
    s = 1
    while s < period:
        x = x + jnp.where(row >= s, pltpu.roll(x, s, 0), 0.0)
        s *= 2
    return x


def _resident(shape):
    zeros = (0,) * len(shape)
    return pl.BlockSpec(shape, lambda *_: zeros, pipeline_mode=pl.Buffered(1))


def _memkv_kernel(mem_ref, g_ref, wk_ref, wv_ref, k_ref, v_ref, kb_ref, vb_ref):
    m = _rms(mem_ref[...], g_ref[...]).astype(BF16)
    k = _dot(m, wk_ref[...].astype(BF16))
    v = _dot(m, wv_ref[...].astype(BF16))
    for hd in range(XA_HEADS):
        cols = slice(hd * XA_HD, (hd + 1) * XA_HD)
        k_ref[:, hd, :] = k[:, cols]
        v_ref[:, hd, :] = v[:, cols]
    kb_ref[...] = k.astype(BF16)
    vb_ref[...] = v.astype(BF16)


def _memkv(mem2d, g_mem, w_xk, w_xv):
    n = mem2d.shape[0]
    full = lambda i: (0, 0)
    row = lambda i: (i, 0)
    blk = pl.BlockSpec((MEMKV_ROWS, D_MODEL), row)
    hblk = pl.BlockSpec((MEMKV_ROWS, XA_HEADS, XA_HD), lambda i: (i, 0, 0))
    return pl.pallas_call(
        _memkv_kernel,
        grid=(n // MEMKV_ROWS,),
        in_specs=[blk, pl.BlockSpec((1, D_MODEL), full),
                  _resident((D_MODEL, D_MODEL)), _resident((D_MODEL, D_MODEL))],
        out_specs=[hblk, hblk, blk, blk],
        out_shape=[jax.ShapeDtypeStruct((n, XA_HEADS, XA_HD), F32),
                   jax.ShapeDtypeStruct((n, XA_HEADS, XA_HD), F32),
                   jax.ShapeDtypeStruct((n, D_MODEL), BF16), jax.ShapeDtypeStruct((n, D_MODEL), BF16)],
        compiler_params=pltpu.CompilerParams(dimension_semantics=("arbitrary",),
                                             vmem_limit_bytes=VMEM_LIMIT_BYTES),
        name="memkv",
    )(mem2d, g_mem, w_xk, w_xv)


def _sample_xattn_stages(dec_len, n_seqs, q_ref, k_ref, v_ref, o_ref):
    pair_rows = SUBLANES
    seqs_per_pair = pair_rows // dec_len
    n_kv = N_MEM * XA_HEADS
    q_rows = XA_HEADS * pair_rows
    own_head = (lax.broadcasted_iota(jnp.int32, (q_rows, n_kv), 1) % XA_HEADS
                == lax.broadcasted_iota(jnp.int32, (q_rows, n_kv), 0) // pair_rows)
    row8 = lax.broadcasted_iota(jnp.int32, (q_rows, XA_HD), 0) % pair_rows
    probs, outs = {}, {}

    def score(seq):
        prow = slice((seq // seqs_per_pair) * pair_rows, (seq // seqs_per_pair + 1) * pair_rows)
        q8 = q_ref[prow, :]
        qs = jnp.concatenate([q8[:, hd * XA_HD:(hd + 1) * XA_HD] for hd in range(XA_HEADS)],
                             axis=0).astype(BF16)
        kb = k_ref[seq].reshape(n_kv, XA_HD).astype(BF16)
        probs[seq] = _softmax_rows(jnp.where(own_head, _dot_nt(qs, kb), -jnp.inf)).astype(BF16)

    def attend(seq):
        p, j = divmod(seq, seqs_per_pair)
        oj = _dot(probs.pop(seq), v_ref[seq].reshape(n_kv, XA_HD).astype(BF16))
        mine = (row8 >= j * dec_len) & (row8 < (j + 1) * dec_len)
        outs[p] = jnp.where(mine, oj, outs[p]) if p in outs else oj
        if j == seqs_per_pair - 1:
            o = outs.pop(p)
            for hd in range(XA_HEADS):
                o_ref[p * pair_rows:(p + 1) * pair_rows, hd * XA_HD:(hd + 1) * XA_HD] = \
                    o[hd * pair_rows:(hd + 1) * pair_rows, :]

    return ([functools.partial(score, s) for s in range(n_seqs)]
            + [functools.partial(attend, s) for s in range(n_seqs)])


def _gate_store(o, gain, gate, out_ref, rows, cols):
    out_ref[rows, cols] = (_head_norm(o) * gain * _silu(gate)).astype(BF16)


def _mix_kernel(dec_len, x_ref, gmix_ref, win_ref, rgain_ref, hgain_ref, hglb_ref, wout_ref,
                cos_ref, sin_ref, dec_ref, qdec_ref, kdec_ref, sdec_ref, qs_ref, ck_ref, cv_ref,
                wxo_ref, wgate_ref, wup_ref, wdown_ref,
                x1_ref, sret_ref, shg_ref, oxs_ref, wxob_ref, wgateb_ref, wupb_ref, wdownb_ref,
                omix_scr):
    t = pl.program_id(1)

    @pl.when(t == 0)
    def _():
        sret_ref[...] = jnp.zeros_like(sret_ref)
        shg_ref[...] = jnp.zeros_like(shg_ref)

    for src, dst in ((wxo_ref, wxob_ref), (wgate_ref, wgateb_ref), (wup_ref, wupb_ref),
                     (wdown_ref, wdownb_ref)):
        dst[...] = src[...].astype(BF16)

    x = x_ref[...]
    h = _rms(x, gmix_ref[...]).astype(BF16)
    side_work = _sample_xattn_stages(dec_len, ck_ref.shape[0], qs_ref, ck_ref, cv_ref, oxs_ref)
    groups = {}
    for g0 in (_HF, _HQ, _RQ, _RK, _RV, _HI, _RG, _HGATE):
        groups[g0] = _dot(h, win_ref[:, g0:g0 + RET_W])
        if side_work:
            side_work.pop(0)()
    while side_work:
        side_work.pop(0)()

    class _Proj:
        def __getitem__(self, idx):
            rows, cols = idx
            g0 = (cols.start // RET_W) * RET_W
            return groups[g0][rows, cols.start - g0:cols.stop - g0]

    proj = _Proj()

    ret_units = []
    for c in range(MIX_TOKENS // RET_CHUNK_LEN):
        rows = slice(c * RET_CHUNK_LEN, (c + 1) * RET_CHUNK_LEN)
        cos2 = cos_ref[rows, :]
        sin2 = sin_ref[rows, :]
        for hd in range(RET_HEADS):
            q = _rope(proj[rows, _RQ + hd * LANE:_RQ + (hd + 1) * LANE], cos2, sin2)
            k = _rope(proj[rows, _RK + hd * LANE:_RK + (hd + 1) * LANE], cos2, sin2) * (RET_DK ** -0.5)
            vb = proj[rows, _RV + hd * LANE:_RV + (hd + 1) * LANE].astype(BF16)
            att = (_dot_nt(q.astype(BF16), k.astype(BF16)) * dec_ref[hd]).astype(BF16)
            kv = _dot_tn((k * kdec_ref[hd]).astype(BF16), vb)
            lhs = jnp.concatenate([att, (q * qdec_ref[hd]).astype(BF16)], axis=1)
            ret_units.append((rows, hd, lhs, vb, kv))

    lower = _lower_bound(hglb_ref[...])
    n_sub = HG_CHUNK_LEN // HG_SUB
    crow = lax.broadcasted_iota(jnp.int32, (HG_CHUNK_LEN, HG_CHUNK_LEN), 0)
    ccol = lax.broadcasted_iota(jnp.int32, (HG_CHUNK_LEN, HG_CHUNK_LEN), 1)
    causal = crow >= ccol
    hg_units = []
    for c in range(MIX_TOKENS // HG_CHUNK_LEN):
        rows = slice(c * HG_CHUNK_LEN, (c + 1) * HG_CHUNK_LEN)
        f = lower + (1.0 - lower) * _sigmoid(proj[rows, _HF:_HF + HG_W])
        kk = 1.0 - f
        qq = _silu(proj[rows, _HQ:_HQ + HG_W])
        b = _cumsum_rows(jnp.log2(f), HG_CHUNK_LEN)
        b_last = b[HG_CHUNK_LEN - 1:HG_CHUNK_LEN, :]
        q_inter = (qq * jnp.exp2(b)).astype(BF16)
        k_upd = (kk * jnp.exp2(b_last - b)).astype(BF16)
        d_last = jnp.exp2(b_last)
        for hd in range(HG_HEADS):
            cols = slice(hd * LANE, (hd + 1) * LANE)
            bh = b[:, cols]
            qh = qq[:, cols]
            kh = kk[:, cols]
            vb = proj[rows, _HI + hd * LANE:_HI + (hd + 1) * LANE].astype(BF16)
            q_parts, k_parts = [], []
            for j in range(n_sub):
                lo, hi = j * HG_SUB, (j + 1) * HG_SUB
                ref = bh[lo + HG_SUB // 2 - 1:lo + HG_SUB // 2, :]
                qt = qh[lo:, :] * jnp.exp2(bh[lo:, :] - ref)
                kt = kh[lo:hi, :] * jnp.exp2(ref - bh[lo:hi, :])
                if lo:
                    qt = jnp.concatenate([jnp.zeros((lo, LANE), F32), qt], axis=0)
                    kt = jnp.concatenate([jnp.zeros((lo, LANE), F32), kt], axis=0)
                if hi < HG_CHUNK_LEN:
                    kt = jnp.concatenate([kt, jnp.zeros((HG_CHUNK_LEN - hi, LANE), F32)], axis=0)
                q_parts.append(qt.astype(BF16))
                k_parts.append(kt.astype(BF16))
            qcat = jnp.concatenate(q_parts, axis=1)
            kcat = jnp.concatenate(k_parts, axis=1)
            att = jnp.where(causal, _dot_nt(qcat, kcat), 0.0).astype(BF16)
            kv = _dot_tn(k_upd[:, cols], vb)
            lhs = jnp.concatenate([q_inter[:, cols], att], axis=1)
            hg_units.append((rows, hd, lhs, vb, kv, d_last[:, cols]))

    for hd in range(RET_HEADS):
        cols = slice(hd * LANE, (hd + 1) * LANE)
        s = sret_ref[0, hd]
        for rows, uh, lhs, vb, kv in ret_units:
            if uh != hd:
                continue
            o = _dot(lhs, jnp.concatenate([vb, s.astype(BF16)], axis=0))
            s = sdec_ref[hd] * s + kv
            gate = proj[rows, _RG + hd * LANE:_RG + (hd + 1) * LANE]
            _gate_store(o, rgain_ref[:, cols], gate, omix_scr, rows, cols)
        sret_ref[0, hd] = s

    for hd in range(HG_HEADS):
        cols = slice(hd * LANE, (hd + 1) * LANE)
        mine = [u for u in hg_units if u[1] == hd]
        dl = jnp.concatenate([u[5] for u in mine] + [jnp.zeros((LANE - len(mine), LANE), F32)], axis=0).T
        s = shg_ref[0, hd]
        for i, (rows, _, lhs, vb, kv, _) in enumerate(mine):
            o = _dot(lhs, jnp.concatenate([s.astype(BF16), vb], axis=0))
            s = s * dl[:, i:i + 1] + kv
            gate = proj[rows, _HGATE + hd * LANE:_HGATE + (hd + 1) * LANE]
            _gate_store(o, hgain_ref[:, cols], gate, omix_scr, rows,
                        slice(RET_W + hd * LANE, RET_W + (hd + 1) * LANE))
        shg_ref[0, hd] = s

    x1_ref[...] = x + _dot(omix_scr[...], wout_ref[...])


def _ret_tables(length, period):
    log_g = np.log(1.0 - 2.0 ** (-5.0 - np.arange(RET_HEADS, dtype=np.float64)))
    idx = np.arange(length)
    pos = idx % period
    rel = (idx[:, None] - idx[None, :]).astype(np.float64)
    same = (idx[:, None] // period) == (idx[None, :] // period)
    valid = (rel >= 0) & same
    dec = np.where(valid[None], np.exp(log_g[:, None, None] * np.where(valid, rel, 0.0)[None]), 0.0)
    qdec = np.exp(log_g[:, None] * (pos + 1.0))[:, :, None] * np.ones((1, 1, LANE))
    kdec = np.exp(log_g[:, None] * (period - 1.0 - pos))[:, :, None] * np.ones((1, 1, LANE))
    sdec = np.exp(log_g * period)[:, None, None] * np.ones((1, 1, LANE))
    as32 = lambda a: jnp.asarray(a, dtype=F32)
    return as32(dec), as32(qdec), as32(kdec), as32(sdec)


def _rope_tables(pos):
    half = RET_DK // 2
    inv_freq = ROPE_BASE ** (-np.arange(half, dtype=np.float64) / half)
    ang = np.asarray(pos, dtype=np.float64)[:, None] * inv_freq[None, :]
    cos, sin = np.cos(ang), np.sin(ang)
    return (jnp.asarray(np.concatenate([cos, cos], axis=-1), dtype=F32),
            jnp.asarray(np.concatenate([-sin, sin], axis=-1), dtype=F32))


def _mix_prompt(x2d, batch, seq, g_mix, w_in, ret_gain, hg_gain, hg_lb, w_out,
                q_s, cache_k, cache_v, dec_len, w_xo, w_gate, w_up, w_down):
    nt = seq // MIX_TOKENS
    n_steps = batch * nt
    d_ff = w_gate.shape[1]
    w_rows = D_MODEL // n_steps
    down_blocks = d_ff // CAST_DOWN_ROWS
    assert w_rows * n_steps == D_MODEL and w_rows % 16 == 0
    assert down_blocks * CAST_DOWN_ROWS == d_ff and down_blocks <= n_steps
    step = lambda b, t: b * nt + t
    sq_rows = pl.BlockSpec((w_rows, D_MODEL), lambda b, t: (step(b, t), 0))
    ff_rows = pl.BlockSpec((w_rows, d_ff), lambda b, t: (step(b, t), 0))
    down_rows = pl.BlockSpec((CAST_DOWN_ROWS, D_MODEL),
                             lambda b, t: (jnp.minimum(step(b, t), down_blocks - 1), 0))
    n_seq = cache_k.shape[0]
    seqs_per_step = n_seq // (batch * nt)
    assert seqs_per_step * batch * nt == n_seq and (seqs_per_step * dec_len) % SUBLANES == 0
    srows = pl.BlockSpec((seqs_per_step * dec_len, D_MODEL), lambda b, t: (b * nt + t, 0))
    cblk = pl.BlockSpec((seqs_per_step, N_MEM, XA_HEADS, XA_HD), lambda b, t: (b * nt + t, 0, 0, 0))
    cos2, sin2 = _rope_tables(np.arange(seq))
    dec, qdec, kdec, sdec = _ret_tables(RET_CHUNK_LEN, RET_CHUNK_LEN)
    c2 = lambda b, t: (0, 0)
    c3 = lambda b, t: (0, 0, 0)
    tok = pl.BlockSpec((MIX_TOKENS, D_MODEL), lambda b, t: (b * nt + t, 0))
    state = pl.BlockSpec((1, RET_HEADS, RET_DK, RET_DV), lambda b, t: (b, 0, 0, 0))
    return pl.pallas_call(
        functools.partial(_mix_kernel, dec_len),
        grid=(batch, nt),
        in_specs=[tok, pl.BlockSpec((1, D_MODEL), c2), _resident((D_MODEL, IN_W)),
                  pl.BlockSpec((1, RET_W), c2), pl.BlockSpec((1, HG_W), c2),
                  pl.BlockSpec(hg_lb.shape, c2), _resident((MIX_W, D_MODEL)),
                  pl.BlockSpec((MIX_TOKENS, LANE), lambda b, t: (t, 0)),
                  pl.BlockSpec((MIX_TOKENS, LANE), lambda b, t: (t, 0)),
                  pl.BlockSpec(dec.shape, c3), pl.BlockSpec(qdec.shape, c3),
                  pl.BlockSpec(kdec.shape, c3), pl.BlockSpec(sdec.shape, c3), srows, cblk, cblk,
                  sq_rows, ff_rows, ff_rows, down_rows],
        out_specs=[tok, state, state, srows, sq_rows, ff_rows, ff_rows, down_rows],
        out_shape=[jax.ShapeDtypeStruct(x2d.shape, F32),
                   jax.ShapeDtypeStruct((batch, RET_HEADS, RET_DK, RET_DV), F32),
                   jax.ShapeDtypeStruct((batch, HG_HEADS, HG_DK, HG_DV), F32),
                   jax.ShapeDtypeStruct(q_s.shape, F32),
                   jax.ShapeDtypeStruct(w_xo.shape, BF16), jax.ShapeDtypeStruct(w_gate.shape, BF16),
                   jax.ShapeDtypeStruct(w_up.shape, BF16), jax.ShapeDtypeStruct(w_down.shape, BF16)],
        scratch_shapes=[pltpu.VMEM((MIX_TOKENS, MIX_W), BF16)],
        compiler_params=pltpu.CompilerParams(dimension_semantics=("arbitrary", "arbitrary"),
                                             vmem_limit_bytes=VMEM_LIMIT_BYTES),
        name="mix_prompt",
    )(x2d, g_mix, w_in, ret_gain, hg_gain, hg_lb, w_out, cos2, sin2, dec, qdec, kdec, sdec,
      q_s, cache_k, cache_v, w_xo, w_gate, w_up, w_down)


def _inv_rms(x):
    return lax.rsqrt(jnp.mean(x * x, axis=-1, keepdims=True) + EPS)


def _query(x1, gxa, wxq):
    return _dot((x1 * gxa).astype(BF16), wxq) * (_inv_rms(x1) * (XA_HD ** -0.5))


def _ffn_final(x1, ox, wxo, gffn, wgate, wup, wdown, gfinal):
    x2 = x1 + _dot(ox, wxo)
    hb = (x2 * gffn).astype(BF16)
    r = _inv_rms(x2)
    a = (_silu(_dot(hb, wgate) * r) * (_dot(hb, wup) * r)).astype(BF16)
    half = x2.shape[0] // 2
    return jnp.concatenate([_rms(x2[rows] + _dot(a[rows], wdown), gfinal)
                            for rows in (slice(0, half), slice(half, None))], axis=0)


def _tail_kernel(x1_ref, mk_ref, mv_ref, gxa_ref, wxq_ref, wxo_ref, gffn_ref, wgate_ref, wup_ref,
                 wdown_ref, gfinal_ref, y_ref, ox_scr):
    x1 = x1_ref[...]
    q = _query(x1, gxa_ref[...], wxq_ref[...]).astype(BF16)
    heads = [slice(hd * XA_HD, (hd + 1) * XA_HD) for hd in range(XA_HEADS)]
    scores = [_dot_nt(q[:, cols], mk_ref[0, :, cols]) for cols in heads]
    probs = [_softmax_rows(s).astype(BF16) for s in scores]
    for cols, p in zip(heads, probs):
        ox_scr[:, cols] = _dot(p, mv_ref[0, :, cols]).astype(BF16)
    y_ref[...] = _ffn_final(x1, ox_scr[...], wxo_ref[...], gffn_ref[...], wgate_ref[...],
                            wup_ref[...], wdown_ref[...], gfinal_ref[...])


def _tail_prompt(x1, batch, seq, mkb, mvb, g_xa, w_xq, w_xo, g_ffn, w_gate, w_up, w_down, g_final):
    nt = seq // TAIL_TOKENS
    d_ff = w_gate.shape[1]
    tok = pl.BlockSpec((TAIL_TOKENS, D_MODEL), lambda b, t: (b * nt + t, 0))
    mem = pl.BlockSpec((1, N_MEM, D_MODEL), lambda b, t: (b, 0, 0))
    return pl.pallas_call(
        _tail_kernel,
        grid=(batch, nt),
        in_specs=[tok, mem, mem, _resident((1, D_MODEL)), _resident((D_MODEL, D_MODEL)),
                  _resident((D_MODEL, D_MODEL)), _resident((1, D_MODEL)),
                  _resident((D_MODEL, d_ff)), _resident((D_MODEL, d_ff)),
                  _resident((d_ff, D_MODEL)), _resident((1, D_MODEL))],
        out_specs=tok,
        out_shape=jax.ShapeDtypeStruct(x1.shape, F32),
        scratch_shapes=[pltpu.VMEM((TAIL_TOKENS, D_MODEL), BF16)],
        compiler_params=pltpu.CompilerParams(dimension_semantics=("arbitrary", "arbitrary"),
                                             vmem_limit_bytes=VMEM_LIMIT_BYTES),
        name="tail_prompt",
    )(x1, mkb, mvb, g_xa, w_xq, w_xo, g_ffn, w_gate, w_up, w_down, g_final)


def _proj_kernel(x_ref, g_ref, w_ref, o_ref, wb_ref):
    wb = w_ref[...].astype(BF16)
    wb_ref[...] = wb
    x = x_ref[...].reshape(-1, D_MODEL)
    o_ref[...] = _dot(_rms(x, g_ref[...]).astype(BF16), wb)


def _proj_sample(x3d, g_mix, w_in):
    n = x3d.shape[0] * x3d.shape[1]
    nb = IN_W // D_MODEL
    wblk = pl.BlockSpec((D_MODEL, D_MODEL), lambda j: (0, j))
    return pl.pallas_call(
        _proj_kernel,
        grid=(nb,),
        in_specs=[pl.BlockSpec(x3d.shape, lambda j: (0, 0, 0)),
                  pl.BlockSpec((1, D_MODEL), lambda j: (0, 0)), wblk],
        out_specs=[pl.BlockSpec((n, D_MODEL), lambda j: (0, j)), wblk],
        out_shape=[jax.ShapeDtypeStruct((n, IN_W), F32), jax.ShapeDtypeStruct(w_in.shape, BF16)],
        compiler_params=pltpu.CompilerParams(dimension_semantics=("arbitrary",),
                                             vmem_limit_bytes=VMEM_LIMIT_BYTES),
        name="proj_sample",
    )(x3d, g_mix, w_in)


def _state_kernel(dec_len, proj_ref, sret_ref, shg_ref, rgain_ref, hgain_ref, hglb_ref,
                  cos_ref, sin_ref, dec_ref, qdec_ref, kdec_ref, sdec_ref,
                  omix_ref, nret_ref, nhg_ref):
    n_rows = STATE_SEQS * dec_len
    pair_rows = SUBLANES
    seqs_per_pair = pair_rows // dec_len
    row8 = lax.broadcasted_iota(jnp.int32, (pair_rows, LANE), 0)
    rown = lax.broadcasted_iota(jnp.int32, (n_rows, n_rows), 0)
    coln = lax.broadcasted_iota(jnp.int32, (n_rows, n_rows), 1)
    same_seq_causal = (rown >= coln) & ((rown // dec_len) == (coln // dec_len))
    proj = proj_ref[...]
    cos2 = cos_ref[...]
    sin2 = sin_ref[...]

    def per_sequence(q_all, kd_all, vb_all, state_ref, new_ref, hd, scale_of):
        outs = []
        for p in range(n_rows // pair_rows):
            prow = slice(p * pair_rows, (p + 1) * pair_rows)
            q8, kd8, v8 = q_all[prow], kd_all[prow], vb_all[prow]
            acc = jnp.zeros((pair_rows, LANE), F32)
            for j in range(seqs_per_pair):
                seq = p * seqs_per_pair + j
                mine = (row8 >= j * dec_len) & (row8 < (j + 1) * dec_len)
                s0 = state_ref[seq, hd]
                acc = jnp.where(mine, _dot(q8, s0.astype(BF16)), acc)
                kz = jnp.where(mine, kd8, jnp.zeros_like(kd8))
                new_ref[seq, hd] = scale_of(seq) * s0 + _dot_tn(kz, v8)
            outs.append(acc)
        return jnp.concatenate(outs, axis=0)

    for hd in range(RET_HEADS):
        cols = slice(hd * LANE, (hd + 1) * LANE)
        q = _rope(proj[:, _RQ + hd * LANE:_RQ + (hd + 1) * LANE], cos2, sin2)
        k = _rope(proj[:, _RK + hd * LANE:_RK + (hd + 1) * LANE], cos2, sin2) * (RET_DK ** -0.5)
        vb = proj[:, _RV + hd * LANE:_RV + (hd + 1) * LANE].astype(BF16)
        gate = proj[:, _RG + hd * LANE:_RG + (hd + 1) * LANE]
        qb = q.astype(BF16)
        att = _dot_nt(qb, k.astype(BF16)) * dec_ref[hd]
        kd = (k * kdec_ref[hd]).astype(BF16)
        sdec = sdec_ref[hd]
        qs = per_sequence(qb, kd, vb, sret_ref, nret_ref, hd, lambda seq: sdec)
        o = _dot(att.astype(BF16), vb) + qdec_ref[hd] * qs
        _gate_store(o, rgain_ref[:, cols], gate, omix_ref, slice(None), cols)

    lower = _lower_bound(hglb_ref[...])
    f = lower + (1.0 - lower) * _sigmoid(proj[:, _HF:_HF + HG_W])
    kk = 1.0 - f
    qq = _silu(proj[:, _HQ:_HQ + HG_W])
    b = _cumsum_rows(jnp.log2(f), dec_len)
    pos = lax.broadcasted_iota(jnp.int32, b.shape, 0) & (dec_len - 1)

    def spread(row_in_seq):
        picked = jnp.where(pos == row_in_seq, b, 0.0)
        out = picked
        for s in range(1, dec_len):
            out = out + jnp.where(pos == (row_in_seq + s), pltpu.roll(picked, s, 0), 0.0)
            out = out + jnp.where(pos == (row_in_seq - s), pltpu.roll(picked, n_rows - s, 0), 0.0)
        return out

    ref = spread(dec_len // 2 - 1)
    b_last = spread(dec_len - 1)
    q_intra = (qq * jnp.exp2(b - ref)).astype(BF16)
    k_intra = (kk * jnp.exp2(ref - b)).astype(BF16)
    q_inter = (qq * jnp.exp2(b)).astype(BF16)
    k_upd = (kk * jnp.exp2(b_last - b)).astype(BF16)
    d_last = jnp.exp2(b_last)
    for hd in range(HG_HEADS):
        cols = slice(hd * LANE, (hd + 1) * LANE)
        vb = proj[:, _HI + hd * LANE:_HI + (hd + 1) * LANE].astype(BF16)
        gate = proj[:, _HGATE + hd * LANE:_HGATE + (hd + 1) * LANE]
        att = jnp.where(same_seq_causal, _dot_nt(q_intra[:, cols], k_intra[:, cols]), 0.0)
        dpad = jnp.concatenate([d_last[:, cols], jnp.zeros((LANE - n_rows, LANE), F32)], axis=0)
        dcol = dpad.T
        scale_of = lambda seq: dcol[:, seq * dec_len:seq * dec_len + 1]
        qs = per_sequence(q_inter[:, cols], k_upd[:, cols], vb, shg_ref, nhg_ref, hd, scale_of)
        o = _dot(att.astype(BF16), vb) + qs
        _gate_store(o, hgain_ref[:, cols], gate, omix_ref, slice(None),
                    slice(RET_W + hd * LANE, RET_W + (hd + 1) * LANE))


def _state_sample(proj, state_ret, state_hgrn, dec_len, ret_gain, hg_gain, hg_lb):
    n_seq = state_ret.shape[0]
    n_rows = STATE_SEQS * dec_len
    cos2, sin2 = _rope_tables(np.tile(np.arange(dec_len) + PAST_LEN, STATE_SEQS))
    dec, qdec, kdec, sdec = _ret_tables(n_rows, dec_len)
    c2 = lambda i: (0, 0)
    c3 = lambda i: (0, 0, 0)
    state = pl.BlockSpec((STATE_SEQS, RET_HEADS, RET_DK, RET_DV), lambda i: (i, 0, 0, 0))
    return pl.pallas_call(
        functools.partial(_state_kernel, dec_len),
        grid=(n_seq // STATE_SEQS,),
        in_specs=[pl.BlockSpec((n_rows, IN_W), lambda i: (i, 0)), state, state,
                  pl.BlockSpec((1, RET_W), c2), pl.BlockSpec((1, HG_W), c2),
                  pl.BlockSpec(hg_lb.shape, c2),
                  pl.BlockSpec((n_rows, LANE), c2), pl.BlockSpec((n_rows, LANE), c2),
                  pl.BlockSpec(dec.shape, c3), pl.BlockSpec(qdec.shape, c3),
                  pl.BlockSpec(kdec.shape, c3), pl.BlockSpec(sdec.shape, c3)],
        out_specs=[pl.BlockSpec((n_rows, MIX_W), lambda i: (i, 0)), state, state],
        out_shape=[jax.ShapeDtypeStruct((proj.shape[0], MIX_W), BF16),
                   jax.ShapeDtypeStruct(state_ret.shape, F32),
                   jax.ShapeDtypeStruct(state_hgrn.shape, F32)],
        compiler_params=pltpu.CompilerParams(dimension_semantics=("arbitrary",),
                                             vmem_limit_bytes=VMEM_LIMIT_BYTES),
        name="state_sample",
    )(proj, state_ret, state_hgrn, ret_gain, hg_gain, hg_lb, cos2, sin2, dec, qdec, kdec, sdec)


def _outq_kernel(x_ref, omix_ref, wout_ref, gxa_ref, wxq_ref, x1_ref, q_ref, woutb_ref, wxqb_ref):
    wout = wout_ref[...].astype(BF16)
    wxq = wxq_ref[...].astype(BF16)
    woutb_ref[...] = wout
    wxqb_ref[...] = wxq
    x1 = x_ref[...].reshape(-1, D_MODEL) + _dot(omix_ref[...], wout)
    x1_ref[...] = x1
    q_ref[...] = _query(x1, gxa_ref[...], wxq)


def _outq_sample(x3d, omix, w_out, g_xa, w_xq):
    n = x3d.shape[0] * x3d.shape[1]
    z = lambda i: (0, 0)
    return pl.pallas_call(
        _outq_kernel,
        grid=(1,),
        in_specs=[pl.BlockSpec(x3d.shape, lambda i: (0, 0, 0)), pl.BlockSpec((n, MIX_W), z),
                  pl.BlockSpec((MIX_W, D_MODEL), z), pl.BlockSpec((1, D_MODEL), z),
                  pl.BlockSpec((D_MODEL, D_MODEL), z)],
        out_specs=[pl.BlockSpec((n, D_MODEL), z), pl.BlockSpec((n, D_MODEL), z),
                   pl.BlockSpec((MIX_W, D_MODEL), z), pl.BlockSpec((D_MODEL, D_MODEL), z)],
        out_shape=[jax.ShapeDtypeStruct((n, D_MODEL), F32), jax.ShapeDtypeStruct((n, D_MODEL), F32),
                   jax.ShapeDtypeStruct((MIX_W, D_MODEL), BF16),
                   jax.ShapeDtypeStruct((D_MODEL, D_MODEL), BF16)],
        compiler_params=pltpu.CompilerParams(dimension_semantics=("arbitrary",),
                                             vmem_limit_bytes=VMEM_LIMIT_BYTES),
        name="outq_sample",
    )(x3d, omix, w_out, g_xa, w_xq)


def _post_kernel(x1_ref, ox_ref, wxo_ref, gffn_ref, wgate_ref, wup_ref, wdown_ref, gfinal_ref, y_ref):
    y = _ffn_final(x1_ref[...], ox_ref[...].astype(BF16), wxo_ref[...], gffn_ref[...],
                   wgate_ref[...], wup_ref[...], wdown_ref[...], gfinal_ref[...])
    y_ref[...] = y.reshape(y_ref.shape)


def _post_sample(x1, ox, dec_len, w_xo, g_ffn, w_gate, w_up, w_down, g_final):
    n = x1.shape[0]
    d_ff = w_gate.shape[1]
    return pl.pallas_call(
        _post_kernel,
        grid=(1,),
        in_specs=[_resident((n, D_MODEL)), _resident((n, D_MODEL)), _resident((D_MODEL, D_MODEL)),
                  _resident((1, D_MODEL)), _resident((D_MODEL, d_ff)), _resident((D_MODEL, d_ff)),
                  _resident((d_ff, D_MODEL)), _resident((1, D_MODEL))],
        out_specs=pl.BlockSpec((n // dec_len, dec_len, D_MODEL), lambda i: (0, 0, 0)),
        out_shape=jax.ShapeDtypeStruct((n // dec_len, dec_len, D_MODEL), F32),
        compiler_params=pltpu.CompilerParams(dimension_semantics=("arbitrary",),
                                             vmem_limit_bytes=VMEM_LIMIT_BYTES),
        name="post_sample",
    )(x1, ox, w_xo, g_ffn, w_gate, w_up, w_down, g_final)


def kernel(x_prompt, x_sample, mem_prompt, state_ret, state_hgrn, cache_mem_k, cache_mem_v, g_mix, w_in,
           ret_gain, hg_gain, hg_lb, w_out, g_xa, g_mem, w_xq, w_xk, w_xv, w_xo, g_ffn, w_gate, w_up,
           w_down, g_final):
    depth = w_in.shape[0]
    assert depth == 1, "single-layer step"
    batch, seq, d = x_prompt.shape
    dec_batch, dec_len, _ = x_sample.shape
    assert d == D_MODEL and seq % MIX_TOKENS == 0 and seq % TAIL_TOKENS == 0
    assert SUBLANES % dec_len == 0 and dec_batch % STATE_SEQS == 0

    g_final2 = g_final.reshape(1, D_MODEL)

    proj_s, w_in_b = _proj_sample(x_sample, g_mix, w_in[0])
    omix_s, ret_s, hg_s = _state_sample(proj_s, state_ret[0], state_hgrn[0], dec_len,
                                        ret_gain, hg_gain, hg_lb)
    x1_s, q_s, w_out_b, w_xq_b = _outq_sample(x_sample, omix_s, w_out[0], g_xa, w_xq[0])

    mk, mv, mkb, mvb = _memkv(mem_prompt.reshape(batch * N_MEM, D_MODEL), g_mem, w_xk[0], w_xv[0])
    x1_p, ret_p, hg_p, ox_s, w_xo_b, w_gate_b, w_up_b, w_down_b = _mix_prompt(
        x_prompt.reshape(batch * seq, D_MODEL), batch, seq, g_mix, w_in_b, ret_gain, hg_gain, hg_lb,
        w_out_b, q_s, cache_mem_k[0], cache_mem_v[0], dec_len, w_xo[0], w_gate[0], w_up[0], w_down[0])
    y_s = _post_sample(x1_s, ox_s, dec_len, w_xo_b, g_ffn, w_gate_b, w_up_b, w_down_b, g_final2)

    y_p = _tail_prompt(x1_p, batch, seq, mkb.reshape(batch, N_MEM, D_MODEL),
                       mvb.reshape(batch, N_MEM, D_MODEL), g_xa, w_xq_b, w_xo_b, g_ffn,
                       w_gate_b, w_up_b, w_down_b, g_final2)

    kv_shape = (depth, batch, N_MEM, XA_HEADS, XA_HD)
    return (y_p.reshape(batch, seq, D_MODEL), y_s,
            ret_p[None], hg_p[None], mk.reshape(kv_shape), mv.reshape(kv_shape),
            ret_s[None], hg_s[None])
```

```python
import functools

import jax
import jax.numpy as jnp
import numpy as np
from jax import lax
from jax.experimental import pallas as pl
from jax.experimental.pallas import tpu as pltpu

D_MODEL = 1024
RET_HEADS = 4
RET_DK = 128
RET_DV = 128
RET_W = RET_HEADS * RET_DV
HG_HEADS = 4
HG_DK = 128
HG_DV = 128
HG_W = HG_HEADS * HG_DV
MIX_W = RET_W + HG_W
IN_W = 2 * RET_HEADS * RET_DK + 2 * RET_W + 2 * HG_HEADS * HG_DK + 2 * HG_W
N_MEM = 256
XA_HEADS = 4
XA_HD = D_MODEL // XA_HEADS
PAST_LEN = 16384
ROPE_BASE = 10000.0
EPS = 1e-6
LOG2E = 1.4426950408889634

_RQ, _RK, _RV, _RG = 0, 512, 1024, 1536
_HQ, _HF, _HI, _HGATE = 2048, 2560, 3072, 3584

LANE = 128
SUBLANES = 8
MIX_TOKENS = 512
RET_CHUNK_LEN = 128
HG_CHUNK_LEN = 64
HG_SUB = 16
TAIL_TOKENS = 512
MEMKV_ROWS = 512
STATE_SEQS = 16
CAST_DOWN_ROWS = 128
POST_FF_CHUNK = 1408
VMEM_LIMIT_BYTES = 56 * 1024 * 1024

F32 = jnp.float32
BF16 = jnp.bfloat16


def _dot(a, b):
    return jnp.dot(a, b, preferred_element_type=F32)


def _dot_nt(a, b):
    return lax.dot_general(a, b, (((1,), (1,)), ((), ())), preferred_element_type=F32)


def _dot_tn(a, b):
    return lax.dot_general(a, b, (((0,), (0,)), ((), ())), preferred_element_type=F32)


def _rms(x, g):
    ms = jnp.mean(x * x, axis=-1, keepdims=True)
    return x * lax.rsqrt(ms + EPS) * g


def _sigmoid(x):
    return 1.0 / (1.0 + jnp.exp2(x * (-LOG2E)))


def _silu(x):
    return x * _sigmoid(x)


def _head_norm(o):
    return o * lax.rsqrt(jnp.mean(o * o, axis=-1, keepdims=True) + EPS)


def _softmax_rows(s):
    p = jnp.exp(s - jnp.max(s, axis=-1, keepdims=True))
    return p / jnp.sum(p, axis=-1, keepdims=True)


def _rope(x, cos2, sin2):
    return x * cos2 + pltpu.roll(x, x.shape[-1] // 2, 1) * sin2


def _lower_bound(hglb):
    m = jnp.max(hglb, axis=0, keepdims=True)
    e = jnp.exp(hglb - m)
    return e[0:1, :] / jnp.sum(e, axis=0, keepdims=True)


def _cumsum_rows(x, period):
    row = lax.broadcasted_iota(jnp.int32, x.shape, 0) & (period - 1)
    s = 1
    while s < period:
        x = x + jnp.where(row >= s, pltpu.roll(x, s, 0), 0.0)
        s *= 2
    return x


def _resident(shape):
    zeros = (0,) * len(shape)
    return pl.BlockSpec(shape, lambda *_: zeros, pipeline_mode=pl.Buffered(1))


def _memkv_kernel(mem_ref, g_ref, wk_ref, wv_ref, k_ref, v_ref, kb_ref, vb_ref):
    m = _rms(mem_ref[...], g_ref[...]).astype(BF16)
    k = _dot(m, wk_ref[...].astype(BF16))
    v = _dot(m, wv_ref[...].astype(BF16))
    for hd in range(XA_HEADS):
        cols = slice(hd * XA_HD, (hd + 1) * XA_HD)
        k_ref[:, hd, :] = k[:, cols]
        v_ref[:, hd, :] = v[:, cols]
    kb_ref[...] = k.astype(BF16)
    vb_ref[...] = v.astype(BF16)


def _memkv(mem2d, g_mem, w_xk, w_xv):
    n = mem2d.shape[0]
    full = lambda i: (0, 0)
    row = lambda i: (i, 0)
    blk = pl.BlockSpec((MEMKV_ROWS, D_MODEL), row)
    hblk = pl.BlockSpec((MEMKV_ROWS, XA_HEADS, XA_HD), lambda i: (i, 0, 0))
    return pl.pallas_call(
        _memkv_kernel,
        grid=(n // MEMKV_ROWS,),
        in_specs=[blk, pl.BlockSpec((1, D_MODEL), full),
                  _resident((D_MODEL, D_MODEL)), _resident((D_MODEL, D_MODEL))],
        out_specs=[hblk, hblk, blk, blk],
        out_shape=[jax.ShapeDtypeStruct((n, XA_HEADS, XA_HD), F32),
                   jax.ShapeDtypeStruct((n, XA_HEADS, XA_HD), F32),
                   jax.ShapeDtypeStruct((n, D_MODEL), BF16), jax.ShapeDtypeStruct((n, D_MODEL), BF16)],
        compiler_params=pltpu.CompilerParams(dimension_semantics=("arbitrary",),
                                             vmem_limit_bytes=VMEM_LIMIT_BYTES),
        name="memkv",
    )(mem2d, g_mem, w_xk, w_xv)


def _sample_xattn_stages(dec_len, n_seqs, q_ref, k_ref, v_ref, o_ref):
    pair_rows = SUBLANES
    seqs_per_pair = pair_rows // dec_len
    n_kv = N_MEM * XA_HEADS
    q_rows = XA_HEADS * pair_rows
    own_head = (lax.broadcasted_iota(jnp.int32, (q_rows, n_kv), 1) % XA_HEADS
                == lax.broadcasted_iota(jnp.int32, (q_rows, n_kv), 0) // pair_rows)
    row8 = lax.broadcasted_iota(jnp.int32, (q_rows, XA_HD), 0) % pair_rows
    probs, outs = {}, {}

    def score(seq):
        prow = slice((seq // seqs_per_pair) * pair_rows, (seq // seqs_per_pair + 1) * pair_rows)
        q8 = q_ref[prow, :]
        qs = jnp.concatenate([q8[:, hd * XA_HD:(hd + 1) * XA_HD] for hd in range(XA_HEADS)],
                             axis=0).astype(BF16)
        kb = k_ref[seq].reshape(n_kv, XA_HD).astype(BF16)
        probs[seq] = _softmax_rows(jnp.where(own_head, _dot_nt(qs, kb), -jnp.inf)).astype(BF16)

    def attend(seq):
        p, j = divmod(seq, seqs_per_pair)
        oj = _dot(probs.pop(seq), v_ref[seq].reshape(n_kv, XA_HD).astype(BF16))
        mine = (row8 >= j * dec_len) & (row8 < (j + 1) * dec_len)
        outs[p] = jnp.where(mine, oj, outs[p]) if p in outs else oj
        if j == seqs_per_pair - 1:
            o = outs.pop(p)
            for hd in range(XA_HEADS):
                o_ref[p * pair_rows:(p + 1) * pair_rows, hd * XA_HD:(hd + 1) * XA_HD] = \
                    o[hd * pair_rows:(hd + 1) * pair_rows, :]

    return ([functools.partial(score, s) for s in range(n_seqs)]
            + [functools.partial(attend, s) for s in range(n_seqs)])


def _gate_store(o, gain, gate, out_ref, rows, cols):
    out_ref[rows, cols] = (_head_norm(o) * gain * _silu(gate)).astype(BF16)


def _mix_kernel(dec_len, x_ref, gmix_ref, win_ref, rgain_ref, hgain_ref, hglb_ref, wout_ref,
                cos_ref, sin_ref, dec_ref, qdec_ref, kdec_ref, sdec_ref, qs_ref, ck_ref, cv_ref,
                wxo_ref, wgate_ref, wup_ref, wdown_ref,
                x1_ref, sret_ref, shg_ref, oxs_ref, wxob_ref, wgateb_ref, wupb_ref, wdownb_ref,
                omix_scr):
    t = pl.program_id(1)

    @pl.when(t == 0)
    def _():
        sret_ref[...] = jnp.zeros_like(sret_ref)
        shg_ref[...] = jnp.zeros_like(shg_ref)

    for src, dst in ((wxo_ref, wxob_ref), (wgate_ref, wgateb_ref), (wup_ref, wupb_ref),
                     (wdown_ref, wdownb_ref)):
        dst[...] = src[...].astype(BF16)

    x = x_ref[...]
    h = _rms(x, gmix_ref[...]).astype(BF16)
    side_work = _sample_xattn_stages(dec_len, ck_ref.shape[0], qs_ref, ck_ref, cv_ref, oxs_ref)
    groups = {}
    for g0 in (_HF, _HQ, _RQ, _RK, _RV, _HI, _RG, _HGATE):
        groups[g0] = _dot(h, win_ref[:, g0:g0 + RET_W])
        if side_work:
            side_work.pop(0)()
    while side_work:
        side_work.pop(0)()

    class _Proj:
        def __getitem__(self, idx):
            rows, cols = idx
            g0 = (cols.start // RET_W) * RET_W
            return groups[g0][rows, cols.start - g0:cols.stop - g0]

    proj = _Proj()

    ret_units = []
    for c in range(MIX_TOKENS // RET_CHUNK_LEN):
        rows = slice(c * RET_CHUNK_LEN, (c + 1) * RET_CHUNK_LEN)
        cos2 = cos_ref[rows, :]
        sin2 = sin_ref[rows, :]
        for hd in range(RET_HEADS):
            q = _rope(proj[rows, _RQ + hd * LANE:_RQ + (hd + 1) * LANE], cos2, sin2)
            k = _rope(proj[rows, _RK + hd * LANE:_RK + (hd + 1) * LANE], cos2, sin2) * (RET_DK ** -0.5)
            vb = proj[rows, _RV + hd * LANE:_RV + (hd + 1) * LANE].astype(BF16)
            att = (_dot_nt(q.astype(BF16), k.astype(BF16)) * dec_ref[hd]).astype(BF16)
            kv = _dot_tn((k * kdec_ref[hd]).astype(BF16), vb)
            lhs = jnp.concatenate([att, (q * qdec_ref[hd]).astype(BF16)], axis=1)
            ret_units.append((rows, hd, lhs, vb, kv))

    lower = _lower_bound(hglb_ref[...])
    n_sub = HG_CHUNK_LEN // HG_SUB
    crow = lax.broadcasted_iota(jnp.int32, (HG_CHUNK_LEN, HG_CHUNK_LEN), 0)
    ccol = lax.broadcasted_iota(jnp.int32, (HG_CHUNK_LEN, HG_CHUNK_LEN), 1)
    causal = crow >= ccol
    hg_units = []
    for c in range(MIX_TOKENS // HG_CHUNK_LEN):
        rows = slice(c * HG_CHUNK_LEN, (c + 1) * HG_CHUNK_LEN)
        f = lower + (1.0 - lower) * _sigmoid(proj[rows, _HF:_HF + HG_W])
        kk = 1.0 - f
        qq = _silu(proj[rows, _HQ:_HQ + HG_W])
        b = _cumsum_rows(jnp.log2(f), HG_CHUNK_LEN)
        b_last = b[HG_CHUNK_LEN - 1:HG_CHUNK_LEN, :]
        q_inter = (qq * jnp.exp2(b)).astype(BF16)
        k_upd = (kk * jnp.exp2(b_last - b)).astype(BF16)
        d_last = jnp.exp2(b_last)
        for hd in range(HG_HEADS):
            cols = slice(hd * LANE, (hd + 1) * LANE)
            bh = b[:, cols]
            qh = qq[:, cols]
            kh = kk[:, cols]
            vb = proj[rows, _HI + hd * LANE:_HI + (hd + 1) * LANE].astype(BF16)
            q_parts, k_parts = [], []
            for j in range(n_sub):
                lo, hi = j * HG_SUB, (j + 1) * HG_SUB
                ref = bh[lo + HG_SUB // 2 - 1:lo + HG_SUB // 2, :]
                qt = qh[lo:, :] * jnp.exp2(bh[lo:, :] - ref)
                kt = kh[lo:hi, :] * jnp.exp2(ref - bh[lo:hi, :])
                if lo:
                    qt = jnp.concatenate([jnp.zeros((lo, LANE), F32), qt], axis=0)
                    kt = jnp.concatenate([jnp.zeros((lo, LANE), F32), kt], axis=0)
                if hi < HG_CHUNK_LEN:
                    kt = jnp.concatenate([kt, jnp.zeros((HG_CHUNK_LEN - hi, LANE), F32)], axis=0)
                q_parts.append(qt.astype(BF16))
                k_parts.append(kt.astype(BF16))
            qcat = jnp.concatenate(q_parts, axis=1)
            kcat = jnp.concatenate(k_parts, axis=1)
            att = jnp.where(causal, _dot_nt(qcat, kcat), 0.0).astype(BF16)
            kv = _dot_tn(k_upd[:, cols], vb)
            lhs = jnp.concatenate([q_inter[:, cols], att], axis=1)
            hg_units.append((rows, hd, lhs, vb, kv, d_last[:, cols]))

    for hd in range(RET_HEADS):
        cols = slice(hd * LANE, (hd + 1) * LANE)
        s = sret_ref[0, hd]
        for rows, uh, lhs, vb, kv in ret_units:
            if uh != hd:
                continue
            o = _dot(lhs, jnp.concatenate([vb, s.astype(BF16)], axis=0))
            s = sdec_ref[hd] * s + kv
            gate = proj[rows, _RG + hd * LANE:_RG + (hd + 1) * LANE]
            _gate_store(o, rgain_ref[:, cols], gate, omix_scr, rows, cols)
        sret_ref[0, hd] = s

    for hd in range(HG_HEADS):
        cols = slice(hd * LANE, (hd + 1) * LANE)
        mine = [u for u in hg_units if u[1] == hd]
        dl = jnp.concatenate([u[5] for u in mine] + [jnp.zeros((LANE - len(mine), LANE), F32)], axis=0).T
        s = shg_ref[0, hd]
        for i, (rows, _, lhs, vb, kv, _) in enumerate(mine):
            o = _dot(lhs, jnp.concatenate([s.astype(BF16), vb], axis=0))
            s = s * dl[:, i:i + 1] + kv
            gate = proj[rows, _HGATE + hd * LANE:_HGATE + (hd + 1) * LANE]
            _gate_store(o, hgain_ref[:, cols], gate, omix_scr, rows,
                        slice(RET_W + hd * LANE, RET_W + (hd + 1) * LANE))
        shg_ref[0, hd] = s

    x1_ref[...] = x + _dot(omix_scr[...], wout_ref[...])


def _ret_tables(length, period):
    log_g = np.log(1.0 - 2.0 ** (-5.0 - np.arange(RET_HEADS, dtype=np.float64)))
    idx = np.arange(length)
    pos = idx % period
    rel = (idx[:, None] - idx[None, :]).astype(np.float64)
    same = (idx[:, None] // period) == (idx[None, :] // period)
    valid = (rel >= 0) & same
    dec = np.where(valid[None], np.exp(log_g[:, None, None] * np.where(valid, rel, 0.0)[None]), 0.0)
    qdec = np.exp(log_g[:, None] * (pos + 1.0))[:, :, None] * np.ones((1, 1, LANE))
    kdec = np.exp(log_g[:, None] * (period - 1.0 - pos))[:, :, None] * np.ones((1, 1, LANE))
    sdec = np.exp(log_g * period)[:, None, None] * np.ones((1, 1, LANE))
    as32 = lambda a: jnp.asarray(a, dtype=F32)
    return as32(dec), as32(qdec), as32(kdec), as32(sdec)


def _rope_tables(pos):
    half = RET_DK // 2
    inv_freq = ROPE_BASE ** (-np.arange(half, dtype=np.float64) / half)
    ang = np.asarray(pos, dtype=np.float64)[:, None] * inv_freq[None, :]
    cos, sin = np.cos(ang), np.sin(ang)
    return (jnp.asarray(np.concatenate([cos, cos], axis=-1), dtype=F32),
            jnp.asarray(np.concatenate([-sin, sin], axis=-1), dtype=F32))


def _mix_prompt(x2d, batch, seq, g_mix, w_in, ret_gain, hg_gain, hg_lb, w_out,
                q_s, cache_k, cache_v, dec_len, w_xo, w_gate, w_up, w_down):
    nt = seq // MIX_TOKENS
    n_steps = batch * nt
    d_ff = w_gate.shape[1]
    w_rows = D_MODEL // n_steps
    down_blocks = d_ff // CAST_DOWN_ROWS
    assert w_rows * n_steps == D_MODEL and w_rows % 16 == 0
    assert down_blocks * CAST_DOWN_ROWS == d_ff and down_blocks <= n_steps
    step = lambda b, t: b * nt + t
    sq_rows = pl.BlockSpec((w_rows, D_MODEL), lambda b, t: (step(b, t), 0))
    ff_rows = pl.BlockSpec((w_rows, d_ff), lambda b, t: (step(b, t), 0))
    down_rows = pl.BlockSpec((CAST_DOWN_ROWS, D_MODEL),
                             lambda b, t: (jnp.minimum(step(b, t), down_blocks - 1), 0))
    n_seq = cache_k.shape[0]
    seqs_per_step = n_seq // (batch * nt)
    assert seqs_per_step * batch * nt == n_seq and (seqs_per_step * dec_len) % SUBLANES == 0
    srows = pl.BlockSpec((seqs_per_step * dec_len, D_MODEL), lambda b, t: (b * nt + t, 0))
    cblk = pl.BlockSpec((seqs_per_step, N_MEM, XA_HEADS, XA_HD), lambda b, t: (b * nt + t, 0, 0, 0))
    cos2, sin2 = _rope_tables(np.arange(seq))
    dec, qdec, kdec, sdec = _ret_tables(RET_CHUNK_LEN, RET_CHUNK_LEN)
    c2 = lambda b, t: (0, 0)
    c3 = lambda b, t: (0, 0, 0)
    tok = pl.BlockSpec((MIX_TOKENS, D_MODEL), lambda b, t: (b * nt + t, 0))
    state = pl.BlockSpec((1, RET_HEADS, RET_DK, RET_DV), lambda b, t: (b, 0, 0, 0))
    return pl.pallas_call(
        functools.partial(_mix_kernel, dec_len),
        grid=(batch, nt),
        in_specs=[tok, pl.BlockSpec((1, D_MODEL), c2), _resident((D_MODEL, IN_W)),
                  pl.BlockSpec((1, RET_W), c2), pl.BlockSpec((1, HG_W), c2),
                  pl.BlockSpec(hg_lb.shape, c2), _resident((MIX_W, D_MODEL)),
                  pl.BlockSpec((MIX_TOKENS, LANE), lambda b, t: (t, 0)),
                  pl.BlockSpec((MIX_TOKENS, LANE), lambda b, t: (t, 0)),
                  pl.BlockSpec(dec.shape, c3), pl.BlockSpec(qdec.shape, c3),
                  pl.BlockSpec(kdec.shape, c3), pl.BlockSpec(sdec.shape, c3), srows, cblk, cblk,
                  sq_rows, ff_rows, ff_rows, down_rows],
        out_specs=[tok, state, state, srows, sq_rows, ff_rows, ff_rows, down_rows],
        out_shape=[jax.ShapeDtypeStruct(x2d.shape, F32),
                   jax.ShapeDtypeStruct((batch, RET_HEADS, RET_DK, RET_DV), F32),
                   jax.ShapeDtypeStruct((batch, HG_HEADS, HG_DK, HG_DV), F32),
                   jax.ShapeDtypeStruct(q_s.shape, F32),
                   jax.ShapeDtypeStruct(w_xo.shape, BF16), jax.ShapeDtypeStruct(w_gate.shape, BF16),
                   jax.ShapeDtypeStruct(w_up.shape, BF16), jax.ShapeDtypeStruct(w_down.shape, BF16)],
        scratch_shapes=[pltpu.VMEM((MIX_TOKENS, MIX_W), BF16)],
        compiler_params=pltpu.CompilerParams(dimension_semantics=("arbitrary", "arbitrary"),
                                             vmem_limit_bytes=VMEM_LIMIT_BYTES),
        name="mix_prompt",
    )(x2d, g_mix, w_in, ret_gain, hg_gain, hg_lb, w_out, cos2, sin2, dec, qdec, kdec, sdec,
      q_s, cache_k, cache_v, w_xo, w_gate, w_up, w_down)


def _inv_rms(x):
    return lax.rsqrt(jnp.mean(x * x, axis=-1, keepdims=True) + EPS)


def _query(x1, gxa, wxq):
    return _dot((x1 * gxa).astype(BF16), wxq) * (_inv_rms(x1) * (XA_HD ** -0.5))


def _ffn_final(x1, ox, wxo, gffn, wgate, wup, wdown, gfinal):
    x2 = x1 + _dot(ox, wxo)
    hb = (x2 * gffn).astype(BF16)
    r = _inv_rms(x2)
    a = (_silu(_dot(hb, wgate) * r) * (_dot(hb, wup) * r)).astype(BF16)
    half = x2.shape[0] // 2
    return jnp.concatenate([_rms(x2[rows] + _dot(a[rows], wdown), gfinal)
                            for rows in (slice(0, half), slice(half, None))], axis=0)


def _tail_kernel(x1_ref, mk_ref, mv_ref, gxa_ref, wxq_ref, wxo_ref, gffn_ref, wgate_ref, wup_ref,
                 wdown_ref, gfinal_ref, y_ref, ox_scr):
    x1 = x1_ref[...]
    q = _query(x1, gxa_ref[...], wxq_ref[...]).astype(BF16)
    heads = [slice(hd * XA_HD, (hd + 1) * XA_HD) for hd in range(XA_HEADS)]
    scores = [_dot_nt(q[:, cols], mk_ref[0, :, cols]) for cols in heads]
    probs = [_softmax_rows(s).astype(BF16) for s in scores]
    for cols, p in zip(heads, probs):
        ox_scr[:, cols] = _dot(p, mv_ref[0, :, cols]).astype(BF16)
    y_ref[...] = _ffn_final(x1, ox_scr[...], wxo_ref[...], gffn_ref[...], wgate_ref[...],
                            wup_ref[...], wdown_ref[...], gfinal_ref[...])


def _tail_prompt(x1, batch, seq, mkb, mvb, g_xa, w_xq, w_xo, g_ffn, w_gate, w_up, w_down, g_final):
    nt = seq // TAIL_TOKENS
    d_ff = w_gate.shape[1]
    tok = pl.BlockSpec((TAIL_TOKENS, D_MODEL), lambda b, t: (b * nt + t, 0))
    mem = pl.BlockSpec((1, N_MEM, D_MODEL), lambda b, t: (b, 0, 0))
    return pl.pallas_call(
        _tail_kernel,
        grid=(batch, nt),
        in_specs=[tok, mem, mem, _resident((1, D_MODEL)), _resident((D_MODEL, D_MODEL)),
                  _resident((D_MODEL, D_MODEL)), _resident((1, D_MODEL)),
                  _resident((D_MODEL, d_ff)), _resident((D_MODEL, d_ff)),
                  _resident((d_ff, D_MODEL)), _resident((1, D_MODEL))],
        out_specs=tok,
        out_shape=jax.ShapeDtypeStruct(x1.shape, F32),
        scratch_shapes=[pltpu.VMEM((TAIL_TOKENS, D_MODEL), BF16)],
        compiler_params=pltpu.CompilerParams(dimension_semantics=("arbitrary", "arbitrary"),
                                             vmem_limit_bytes=VMEM_LIMIT_BYTES),
        name="tail_prompt",
    )(x1, mkb, mvb, g_xa, w_xq, w_xo, g_ffn, w_gate, w_up, w_down, g_final)


def _proj_kernel(x_ref, g_ref, w_ref, o_ref, wb_ref):
    wb = w_ref[...].astype(BF16)
    wb_ref[...] = wb
    x = x_ref[...].reshape(-1, D_MODEL)
    o_ref[...] = _dot(_rms(x, g_ref[...]).astype(BF16), wb)


def _proj_sample(x3d, g_mix, w_in):
    n = x3d.shape[0] * x3d.shape[1]
    nb = IN_W // D_MODEL
    wblk = pl.BlockSpec((D_MODEL, D_MODEL), lambda j: (0, j))
    return pl.pallas_call(
        _proj_kernel,
        grid=(nb,),
        in_specs=[pl.BlockSpec(x3d.shape, lambda j: (0, 0, 0)),
                  pl.BlockSpec((1, D_MODEL), lambda j: (0, 0)), wblk],
        out_specs=[pl.BlockSpec((n, D_MODEL), lambda j: (0, j)), wblk],
        out_shape=[jax.ShapeDtypeStruct((n, IN_W), F32), jax.ShapeDtypeStruct(w_in.shape, BF16)],
        compiler_params=pltpu.CompilerParams(dimension_semantics=("arbitrary",),
                                             vmem_limit_bytes=VMEM_LIMIT_BYTES),
        name="proj_sample",
    )(x3d, g_mix, w_in)


def _state_kernel(dec_len, proj_ref, sret_ref, shg_ref, rgain_ref, hgain_ref, hglb_ref,
                  cos_ref, sin_ref, dec_ref, qdec_ref, kdec_ref, sdec_ref,
                  omix_ref, nret_ref, nhg_ref):
    n_rows = STATE_SEQS * dec_len
    pair_rows = SUBLANES
    seqs_per_pair = pair_rows // dec_len
    row8 = lax.broadcasted_iota(jnp.int32, (pair_rows, LANE), 0)
    rown = lax.broadcasted_iota(jnp.int32, (n_rows, n_rows), 0)
    coln = lax.broadcasted_iota(jnp.int32, (n_rows, n_rows), 1)
    same_seq_causal = (rown >= coln) & ((rown // dec_len) == (coln // dec_len))
    proj = proj_ref[...]
    cos2 = cos_ref[...]
    sin2 = sin_ref[...]

    def per_sequence(q_all, kd_all, vb_all, state_ref, new_ref, hd, scale_of):
        outs = []
        for p in range(n_rows // pair_rows):
            prow = slice(p * pair_rows, (p + 1) * pair_rows)
            q8, kd8, v8 = q_all[prow], kd_all[prow], vb_all[prow]
            acc = jnp.zeros((pair_rows, LANE), F32)
            for j in range(seqs_per_pair):
                seq = p * seqs_per_pair + j
                mine = (row8 >= j * dec_len) & (row8 < (j + 1) * dec_len)
                s0 = state_ref[seq, hd]
                acc = jnp.where(mine, _dot(q8, s0.astype(BF16)), acc)
                kz = jnp.where(mine, kd8, jnp.zeros_like(kd8))
                new_ref[seq, hd] = scale_of(seq) * s0 + _dot_tn(kz, v8)
            outs.append(acc)
        return jnp.concatenate(outs, axis=0)

    for hd in range(RET_HEADS):
        cols = slice(hd * LANE, (hd + 1) * LANE)
        q = _rope(proj[:, _RQ + hd * LANE:_RQ + (hd + 1) * LANE], cos2, sin2)
        k = _rope(proj[:, _RK + hd * LANE:_RK + (hd + 1) * LANE], cos2, sin2) * (RET_DK ** -0.5)
        vb = proj[:, _RV + hd * LANE:_RV + (hd + 1) * LANE].astype(BF16)
        gate = proj[:, _RG + hd * LANE:_RG + (hd + 1) * LANE]
        qb = q.astype(BF16)
        att = _dot_nt(qb, k.astype(BF16)) * dec_ref[hd]
        kd = (k * kdec_ref[hd]).astype(BF16)
        sdec = sdec_ref[hd]
        qs = per_sequence(qb, kd, vb, sret_ref, nret_ref, hd, lambda seq: sdec)
        o = _dot(att.astype(BF16), vb) + qdec_ref[hd] * qs
        _gate_store(o, rgain_ref[:, cols], gate, omix_ref, slice(None), cols)

    lower = _lower_bound(hglb_ref[...])
    f = lower + (1.0 - lower) * _sigmoid(proj[:, _HF:_HF + HG_W])
    kk = 1.0 - f
    qq = _silu(proj[:, _HQ:_HQ + HG_W])
    b = _cumsum_rows(jnp.log2(f), dec_len)
    pos = lax.broadcasted_iota(jnp.int32, b.shape, 0) & (dec_len - 1)

    def spread(row_in_seq):
        picked = jnp.where(pos == row_in_seq, b, 0.0)
        out = picked
        for s in range(1, dec_len):
            out = out + jnp.where(pos == (row_in_seq + s), pltpu.roll(picked, s, 0), 0.0)
            out = out + jnp.where(pos == (row_in_seq - s), pltpu.roll(picked, n_rows - s, 0), 0.0)
        return out

    ref = spread(dec_len // 2 - 1)
    b_last = spread(dec_len - 1)
    q_intra = (qq * jnp.exp2(b - ref)).astype(BF16)
    k_intra = (kk * jnp.exp2(ref - b)).astype(BF16)
    q_inter = (qq * jnp.exp2(b)).astype(BF16)
    k_upd = (kk * jnp.exp2(b_last - b)).astype(BF16)
    d_last = jnp.exp2(b_last)
    for hd in range(HG_HEADS):
        cols = slice(hd * LANE, (hd + 1) * LANE)
        vb = proj[:, _HI + hd * LANE:_HI + (hd + 1) * LANE].astype(BF16)
        gate = proj[:, _HGATE + hd * LANE:_HGATE + (hd + 1) * LANE]
        att = jnp.where(same_seq_causal, _dot_nt(q_intra[:, cols], k_intra[:, cols]), 0.0)
        dpad = jnp.concatenate([d_last[:, cols], jnp.zeros((LANE - n_rows, LANE), F32)], axis=0)
        dcol = dpad.T
        scale_of = lambda seq: dcol[:, seq * dec_len:seq * dec_len + 1]
        qs = per_sequence(q_inter[:, cols], k_upd[:, cols], vb, shg_ref, nhg_ref, hd, scale_of)
        o = _dot(att.astype(BF16), vb) + qs
        _gate_store(o, hgain_ref[:, cols], gate, omix_ref, slice(None),
                    slice(RET_W + hd * LANE, RET_W + (hd + 1) * LANE))


def _state_sample(proj, state_ret, state_hgrn, dec_len, ret_gain, hg_gain, hg_lb):
    n_seq = state_ret.shape[0]
    n_rows = STATE_SEQS * dec_len
    cos2, sin2 = _rope_tables(np.tile(np.arange(dec_len) + PAST_LEN, STATE_SEQS))
    dec, qdec, kdec, sdec = _ret_tables(n_rows, dec_len)
    c2 = lambda i: (0, 0)
    c3 = lambda i: (0, 0, 0)
    state = pl.BlockSpec((STATE_SEQS, RET_HEADS, RET_DK, RET_DV), lambda i: (i, 0, 0, 0))
    return pl.pallas_call(
        functools.partial(_state_kernel, dec_len),
        grid=(n_seq // STATE_SEQS,),
        in_specs=[pl.BlockSpec((n_rows, IN_W), lambda i: (i, 0)), state, state,
                  pl.BlockSpec((1, RET_W), c2), pl.BlockSpec((1, HG_W), c2),
                  pl.BlockSpec(hg_lb.shape, c2),
                  pl.BlockSpec((n_rows, LANE), c2), pl.BlockSpec((n_rows, LANE), c2),
                  pl.BlockSpec(dec.shape, c3), pl.BlockSpec(qdec.shape, c3),
                  pl.BlockSpec(kdec.shape, c3), pl.BlockSpec(sdec.shape, c3)],
        out_specs=[pl.BlockSpec((n_rows, MIX_W), lambda i: (i, 0)), state, state],
        out_shape=[jax.ShapeDtypeStruct((proj.shape[0], MIX_W), BF16),
                   jax.ShapeDtypeStruct(state_ret.shape, F32),
                   jax.ShapeDtypeStruct(state_hgrn.shape, F32)],
        compiler_params=pltpu.CompilerParams(dimension_semantics=("arbitrary",),
                                             vmem_limit_bytes=VMEM_LIMIT_BYTES),
        name="state_sample",
    )(proj, state_ret, state_hgrn, ret_gain, hg_gain, hg_lb, cos2, sin2, dec, qdec, kdec, sdec)


def _outq_kernel(x_ref, omix_ref, wout_ref, gxa_ref, wxq_ref, x1_ref, q_ref, woutb_ref, wxqb_ref):
    wout = wout_ref[...].astype(BF16)
    wxq = wxq_ref[...].astype(BF16)
    woutb_ref[...] = wout
    wxqb_ref[...] = wxq
    x1 = x_ref[...].reshape(-1, D_MODEL) + _dot(omix_ref[...], wout)
    x1_ref[...] = x1
    q_ref[...] = _query(x1, gxa_ref[...], wxq)


def _outq_sample(x3d, omix, w_out, g_xa, w_xq):
    n = x3d.shape[0] * x3d.shape[1]
    z = lambda i: (0, 0)
    return pl.pallas_call(
        _outq_kernel,
        grid=(1,),
        in_specs=[pl.BlockSpec(x3d.shape, lambda i: (0, 0, 0)), pl.BlockSpec((n, MIX_W), z),
                  pl.BlockSpec((MIX_W, D_MODEL), z), pl.BlockSpec((1, D_MODEL), z),
                  pl.BlockSpec((D_MODEL, D_MODEL), z)],
        out_specs=[pl.BlockSpec((n, D_MODEL), z), pl.BlockSpec((n, D_MODEL), z),
                   pl.BlockSpec((MIX_W, D_MODEL), z), pl.BlockSpec((D_MODEL, D_MODEL), z)],
        out_shape=[jax.ShapeDtypeStruct((n, D_MODEL), F32), jax.ShapeDtypeStruct((n, D_MODEL), F32),
                   jax.ShapeDtypeStruct((MIX_W, D_MODEL), BF16),
                   jax.ShapeDtypeStruct((D_MODEL, D_MODEL), BF16)],
        compiler_params=pltpu.CompilerParams(dimension_semantics=("arbitrary",),
                                             vmem_limit_bytes=VMEM_LIMIT_BYTES),
        name="outq_sample",
    )(x3d, omix, w_out, g_xa, w_xq)


def _post_kernel(x1_ref, ox_ref, wxo_ref, gffn_ref, wgate_ref, wup_ref, wdown_ref, gfinal_ref,
                 y_ref, x2_scr, hb_scr, r_scr, acc_scr):
    c = pl.program_id(0)

    @pl.when(c == 0)
    def _():
        x2 = x1_ref[...] + _dot(ox_ref[...].astype(BF16), wxo_ref[...])
        x2_scr[...] = x2
        hb_scr[...] = (x2 * gffn_ref[...]).astype(BF16)
        r_scr[...] = jnp.broadcast_to(_inv_rms(x2), r_scr.shape)
        acc_scr[...] = jnp.zeros_like(acc_scr)

    hb = hb_scr[...]
    r = r_scr[:, 0:1]
    a = (_silu(_dot(hb, wgate_ref[...]) * r) * (_dot(hb, wup_ref[...]) * r)).astype(BF16)
    acc_scr[...] += _dot(a, wdown_ref[...])

    @pl.when(c == pl.num_programs(0) - 1)
    def _():
        y_ref[...] = _rms(x2_scr[...] + acc_scr[...], gfinal_ref[...]).reshape(y_ref.shape)


def _post_sample(x1, ox, dec_len, w_xo, g_ffn, w_gate, w_up, w_down, g_final):
    n = x1.shape[0]
    d_ff = w_gate.shape[1]
    chunk = POST_FF_CHUNK
    assert d_ff % chunk == 0
    z = lambda c: (0, 0)
    colblk = pl.BlockSpec((D_MODEL, chunk), lambda c: (0, c))
    rowblk = pl.BlockSpec((chunk, D_MODEL), lambda c: (c, 0))
    full = pl.BlockSpec((n, D_MODEL), z)
    vec = pl.BlockSpec((1, D_MODEL), z)
    return pl.pallas_call(
        _post_kernel,
        grid=(d_ff // chunk,),
        in_specs=[full, full, pl.BlockSpec((D_MODEL, D_MODEL), z), vec, colblk, colblk, rowblk, vec],
        out_specs=pl.BlockSpec((n // dec_len, dec_len, D_MODEL), lambda c: (0, 0, 0)),
        out_shape=jax.ShapeDtypeStruct((n // dec_len, dec_len, D_MODEL), F32),
        scratch_shapes=[pltpu.VMEM((n, D_MODEL), F32), pltpu.VMEM((n, D_MODEL), BF16),
                        pltpu.VMEM((n, LANE), F32), pltpu.VMEM((n, D_MODEL), F32)],
        compiler_params=pltpu.CompilerParams(dimension_semantics=("arbitrary",),
                                             vmem_limit_bytes=VMEM_LIMIT_BYTES),
        name="post_sample",
    )(x1, ox, w_xo, g_ffn, w_gate, w_up, w_down, g_final)


def kernel(x_prompt, x_sample, mem_prompt, state_ret, state_hgrn, cache_mem_k, cache_mem_v, g_mix, w_in,
           ret_gain, hg_gain, hg_lb, w_out, g_xa, g_mem, w_xq, w_xk, w_xv, w_xo, g_ffn, w_gate, w_up,
           w_down, g_final):
    depth = w_in.shape[0]
    assert depth == 1, "single-layer step"
    batch, seq, d = x_prompt.shape
    dec_batch, dec_len, _ = x_sample.shape
    assert d == D_MODEL and seq % MIX_TOKENS == 0 and seq % TAIL_TOKENS == 0
    assert SUBLANES % dec_len == 0 and dec_batch % STATE_SEQS == 0

    g_final2 = g_final.reshape(1, D_MODEL)

    proj_s, w_in_b = _proj_sample(x_sample, g_mix, w_in[0])
    omix_s, ret_s, hg_s = _state_sample(proj_s, state_ret[0], state_hgrn[0], dec_len,
                                        ret_gain, hg_gain, hg_lb)
    x1_s, q_s, w_out_b, w_xq_b = _outq_sample(x_sample, omix_s, w_out[0], g_xa, w_xq[0])

    mk, mv, mkb, mvb = _memkv(mem_prompt.reshape(batch * N_MEM, D_MODEL), g_mem, w_xk[0], w_xv[0])
    x1_p, ret_p, hg_p, ox_s, w_xo_b, w_gate_b, w_up_b, w_down_b = _mix_prompt(
        x_prompt.reshape(batch * seq, D_MODEL), batch, seq, g_mix, w_in_b, ret_gain, hg_gain, hg_lb,
        w_out_b, q_s, cache_mem_k[0], cache_mem_v[0], dec_len, w_xo[0], w_gate[0], w_up[0], w_down[0])
    y_s = _post_sample(x1_s, ox_s, dec_len, w_xo_b, g_ffn, w_gate_b, w_up_b, w_down_b, g_final2)

    y_p = _tail_prompt(x1_p, batch, seq, mkb.reshape(batch, N_MEM, D_MODEL),
                       mvb.reshape(batch, N_MEM, D_MODEL), g_xa, w_xq_b, w_xo_b, g_ffn,
                       w_gate_b, w_up_b, w_down_b, g_final2)

    kv_shape = (depth, batch, N_MEM, XA_HEADS, XA_HD)
    return (y_p.reshape(batch, seq, D_MODEL), y_s,
            ret_p[None], hg_p[None], mk.reshape(kv_shape), mv.reshape(kv_shape),
            ret_s[None], hg_s[None])
```

```python
import functools

import jax
import jax.numpy as jnp
import numpy as np
from jax import lax
from jax.experimental import pallas as pl
from jax.experimental.pallas import tpu as pltpu

D_MODEL = 1024
RET_HEADS = 4
RET_DK = 128
RET_DV = 128
RET_W = RET_HEADS * RET_DV
HG_HEADS = 4
HG_DK = 128
HG_DV = 128
HG_W = HG_HEADS * HG_DV
MIX_W = RET_W + HG_W
IN_W = 2 * RET_HEADS * RET_DK + 2 * RET_W + 2 * HG_HEADS * HG_DK + 2 * HG_W
N_MEM = 256
XA_HEADS = 4
XA_HD = D_MODEL // XA_HEADS
PAST_LEN = 16384
ROPE_BASE = 10000.0
EPS = 1e-6
LOG2E = 1.4426950408889634

_RQ, _RK, _RV, _RG = 0, 512, 1024, 1536
_HQ, _HF, _HI, _HGATE = 2048, 2560, 3072, 3584

LANE = 128
SUBLANES = 8
MIX_TOKENS = 512
RET_CHUNK_LEN = 128
HG_CHUNK_LEN = 64
HG_SUB = 16
TAIL_TOKENS = 512
MEMKV_ROWS = 512
STATE_SEQS = 16
CAST_DOWN_ROWS = 128
VMEM_LIMIT_BYTES = 56 * 1024 * 1024

F32 = jnp.float32
BF16 = jnp.bfloat16


def _dot(a, b):
    return jnp.dot(a, b, preferred_element_type=F32)


def _dot_nt(a, b):
    return lax.dot_general(a, b, (((1,), (1,)), ((), ())), preferred_element_type=F32)


def _dot_tn(a, b):
    return lax.dot_general(a, b, (((0,), (0,)), ((), ())), preferred_element_type=F32)


def _rms(x, g):
    ms = jnp.mean(x * x, axis=-1, keepdims=True)
    return x * lax.rsqrt(ms + EPS) * g


def _sigmoid(x):
    return 1.0 / (1.0 + jnp.exp2(x * (-LOG2E)))


def _silu(x):
    return x * _sigmoid(x)


def _head_norm(o):
    return o * lax.rsqrt(jnp.mean(o * o, axis=-1, keepdims=True) + EPS)


def _softmax_rows(s):
    p = jnp.exp(s - jnp.max(s, axis=-1, keepdims=True))
    return p / jnp.sum(p, axis=-1, keepdims=True)


def _rope(x, cos2, sin2):
    return x * cos2 + pltpu.roll(x, x.shape[-1] // 2, 1) * sin2


def _lower_bound(hglb):
    m = jnp.max(hglb, axis=0, keepdims=True)
    e = jnp.exp(hglb - m)
    return e[0:1, :] / jnp.sum(e, axis=0, keepdims=True)


def _cumsum_rows(x, period):
    row = lax.broadcasted_iota(jnp.int32, x.shape, 0) & (period - 1)
    s = 1
    while s < period:
        x = x + jnp.where(row >= s, pltpu.roll(x, s, 0), 0.0)
        s *= 2
    return x


def _resident(shape):
    zeros = (0,) * len(shape)
    return pl.BlockSpec(shape, lambda *_: zeros, pipeline_mode=pl.Buffered(1))


def _memkv_kernel(mem_ref, g_ref, wk_ref, wv_ref, k_ref, v_ref, kb_ref, vb_ref):
    m = _rms(mem_ref[...], g_ref[...]).astype(BF16)
    k = _dot(m, wk_ref[...].astype(BF16))
    v = _dot(m, wv_ref[...].astype(BF16))
    for hd in range(XA_HEADS):
        cols = slice(hd * XA_HD, (hd + 1) * XA_HD)
        k_ref[:, hd, :] = k[:, cols]
        v_ref[:, hd, :] = v[:, cols]
    kb_ref[...] = k.astype(BF16)
    vb_ref[...] = v.astype(BF16)


def _memkv(mem2d, g_mem, w_xk, w_xv):
    n = mem2d.shape[0]
    full = lambda i: (0, 0)
    row = lambda i: (i, 0)
    blk = pl.BlockSpec((MEMKV_ROWS, D_MODEL), row)
    hblk = pl.BlockSpec((MEMKV_ROWS, XA_HEADS, XA_HD), lambda i: (i, 0, 0))
    return pl.pallas_call(
        _memkv_kernel,
        grid=(n // MEMKV_ROWS,),
        in_specs=[blk, pl.BlockSpec((1, D_MODEL), full),
                  _resident((D_MODEL, D_MODEL)), _resident((D_MODEL, D_MODEL))],
        out_specs=[hblk, hblk, blk, blk],
        out_shape=[jax.ShapeDtypeStruct((n, XA_HEADS, XA_HD), F32),
                   jax.ShapeDtypeStruct((n, XA_HEADS, XA_HD), F32),
                   jax.ShapeDtypeStruct((n, D_MODEL), BF16), jax.ShapeDtypeStruct((n, D_MODEL), BF16)],
        compiler_params=pltpu.CompilerParams(dimension_semantics=("arbitrary",),
                                             vmem_limit_bytes=VMEM_LIMIT_BYTES),
        name="memkv",
    )(mem2d, g_mem, w_xk, w_xv)


def _sample_xattn_stages(dec_len, n_seqs, q_ref, k_ref, v_ref, o_ref):
    pair_rows = SUBLANES
    seqs_per_pair = pair_rows // dec_len
    n_kv = N_MEM * XA_HEADS
    q_rows = XA_HEADS * pair_rows
    own_head = (lax.broadcasted_iota(jnp.int32, (q_rows, n_kv), 1) % XA_HEADS
                == lax.broadcasted_iota(jnp.int32, (q_rows, n_kv), 0) // pair_rows)
    row8 = lax.broadcasted_iota(jnp.int32, (q_rows, XA_HD), 0) % pair_rows
    probs, outs = {}, {}

    def score(seq):
        prow = slice((seq // seqs_per_pair) * pair_rows, (seq // seqs_per_pair + 1) * pair_rows)
        q8 = q_ref[prow, :]
        qs = jnp.concatenate([q8[:, hd * XA_HD:(hd + 1) * XA_HD] for hd in range(XA_HEADS)],
                             axis=0).astype(BF16)
        kb = k_ref[seq].reshape(n_kv, XA_HD).astype(BF16)
        probs[seq] = _softmax_rows(jnp.where(own_head, _dot_nt(qs, kb), -jnp.inf)).astype(BF16)

    def attend(seq):
        p, j = divmod(seq, seqs_per_pair)
        oj = _dot(probs.pop(seq), v_ref[seq].reshape(n_kv, XA_HD).astype(BF16))
        mine = (row8 >= j * dec_len) & (row8 < (j + 1) * dec_len)
        outs[p] = jnp.where(mine, oj, outs[p]) if p in outs else oj
        if j == seqs_per_pair - 1:
            o = outs.pop(p)
            for hd in range(XA_HEADS):
                o_ref[p * pair_rows:(p + 1) * pair_rows, hd * XA_HD:(hd + 1) * XA_HD] = \
                    o[hd * pair_rows:(hd + 1) * pair_rows, :]

    return ([functools.partial(score, s) for s in range(n_seqs)]
            + [functools.partial(attend, s) for s in range(n_seqs)])


def _gate_store(o, gain, gate, out_ref, rows, cols):
    out_ref[rows, cols] = (_head_norm(o) * gain * _silu(gate)).astype(BF16)


def _mix_kernel(dec_len, x_ref, gmix_ref, win_ref, rgain_ref, hgain_ref, hglb_ref, wout_ref,
                cos_ref, sin_ref, dec_ref, qdec_ref, kdec_ref, sdec_ref, qs_ref, ck_ref, cv_ref,
                wxo_ref, wgate_ref, wup_ref, wdown_ref,
                x1_ref, sret_ref, shg_ref, oxs_ref, wxob_ref, wgateb_ref, wupb_ref, wdownb_ref,
                omix_scr):
    t = pl.program_id(1)

    @pl.when(t == 0)
    def _():
        sret_ref[...] = jnp.zeros_like(sret_ref)
        shg_ref[...] = jnp.zeros_like(shg_ref)

    for src, dst in ((wxo_ref, wxob_ref), (wgate_ref, wgateb_ref), (wup_ref, wupb_ref),
                     (wdown_ref, wdownb_ref)):
        dst[...] = src[...].astype(BF16)

    x = x_ref[...]
    h = _rms(x, gmix_ref[...]).astype(BF16)
    side_work = _sample_xattn_stages(dec_len, ck_ref.shape[0], qs_ref, ck_ref, cv_ref, oxs_ref)
    groups = {}
    for g0 in (_HF, _HQ, _RQ, _RK, _RV, _HI, _RG, _HGATE):
        groups[g0] = _dot(h, win_ref[:, g0:g0 + RET_W])
        if side_work:
            side_work.pop(0)()
    while side_work:
        side_work.pop(0)()

    class _Proj:
        def __getitem__(self, idx):
            rows, cols = idx
            g0 = (cols.start // RET_W) * RET_W
            return groups[g0][rows, cols.start - g0:cols.stop - g0]

    proj = _Proj()

    ret_units = []
    for c in range(MIX_TOKENS // RET_CHUNK_LEN):
        rows = slice(c * RET_CHUNK_LEN, (c + 1) * RET_CHUNK_LEN)
        cos2 = cos_ref[rows, :]
        sin2 = sin_ref[rows, :]
        for hd in range(RET_HEADS):
            q = _rope(proj[rows, _RQ + hd * LANE:_RQ + (hd + 1) * LANE], cos2, sin2)
            k = _rope(proj[rows, _RK + hd * LANE:_RK + (hd + 1) * LANE], cos2, sin2) * (RET_DK ** -0.5)
            vb = proj[rows, _RV + hd * LANE:_RV + (hd + 1) * LANE].astype(BF16)
            att = (_dot_nt(q.astype(BF16), k.astype(BF16)) * dec_ref[hd]).astype(BF16)
            kv = _dot_tn((k * kdec_ref[hd]).astype(BF16), vb)
            lhs = jnp.concatenate([att, (q * qdec_ref[hd]).astype(BF16)], axis=1)
            ret_units.append((rows, hd, lhs, vb, kv))

    lower = _lower_bound(hglb_ref[...])
    n_sub = HG_CHUNK_LEN // HG_SUB
    crow = lax.broadcasted_iota(jnp.int32, (HG_CHUNK_LEN, HG_CHUNK_LEN), 0)
    ccol = lax.broadcasted_iota(jnp.int32, (HG_CHUNK_LEN, HG_CHUNK_LEN), 1)
    causal = crow >= ccol
    hg_units = []
    for c in range(MIX_TOKENS // HG_CHUNK_LEN):
        rows = slice(c * HG_CHUNK_LEN, (c + 1) * HG_CHUNK_LEN)
        f = lower + (1.0 - lower) * _sigmoid(proj[rows, _HF:_HF + HG_W])
        kk = 1.0 - f
        qq = _silu(proj[rows, _HQ:_HQ + HG_W])
        b = _cumsum_rows(jnp.log2(f), HG_CHUNK_LEN)
        b_last = b[HG_CHUNK_LEN - 1:HG_CHUNK_LEN, :]
        q_inter = (qq * jnp.exp2(b)).astype(BF16)
        k_upd = (kk * jnp.exp2(b_last - b)).astype(BF16)
        d_last = jnp.exp2(b_last)
        for hd in range(HG_HEADS):
            cols = slice(hd * LANE, (hd + 1) * LANE)
            bh = b[:, cols]
            qh = qq[:, cols]
            kh = kk[:, cols]
            vb = proj[rows, _HI + hd * LANE:_HI + (hd + 1) * LANE].astype(BF16)
            q_parts, k_parts = [], []
            for j in range(n_sub):
                lo, hi = j * HG_SUB, (j + 1) * HG_SUB
                ref = bh[lo + HG_SUB // 2 - 1:lo + HG_SUB // 2, :]
                qt = qh[lo:, :] * jnp.exp2(bh[lo:, :] - ref)
                kt = kh[lo:hi, :] * jnp.exp2(ref - bh[lo:hi, :])
                if lo:
                    qt = jnp.concatenate([jnp.zeros((lo, LANE), F32), qt], axis=0)
                    kt = jnp.concatenate([jnp.zeros((lo, LANE), F32), kt], axis=0)
                if hi < HG_CHUNK_LEN:
                    kt = jnp.concatenate([kt, jnp.zeros((HG_CHUNK_LEN - hi, LANE), F32)], axis=0)
                q_parts.append(qt.astype(BF16))
                k_parts.append(kt.astype(BF16))
            qcat = jnp.concatenate(q_parts, axis=1)
            kcat = jnp.concatenate(k_parts, axis=1)
            att = jnp.where(causal, _dot_nt(qcat, kcat), 0.0).astype(BF16)
            kv = _dot_tn(k_upd[:, cols], vb)
            lhs = jnp.concatenate([q_inter[:, cols], att], axis=1)
            hg_units.append((rows, hd, lhs, vb, kv, d_last[:, cols]))

    for hd in range(RET_HEADS):
        cols = slice(hd * LANE, (hd + 1) * LANE)
        s = sret_ref[0, hd]
        for rows, uh, lhs, vb, kv in ret_units:
            if uh != hd:
                continue
            o = _dot(lhs, jnp.concatenate([vb, s.astype(BF16)], axis=0))
            s = sdec_ref[hd] * s + kv
            gate = proj[rows, _RG + hd * LANE:_RG + (hd + 1) * LANE]
            _gate_store(o, rgain_ref[:, cols], gate, omix_scr, rows, cols)
        sret_ref[0, hd] = s

    for hd in range(HG_HEADS):
        cols = slice(hd * LANE, (hd + 1) * LANE)
        mine = [u for u in hg_units if u[1] == hd]
        dl = jnp.concatenate([u[5] for u in mine] + [jnp.zeros((LANE - len(mine), LANE), F32)], axis=0).T
        s = shg_ref[0, hd]
        for i, (rows, _, lhs, vb, kv, _) in enumerate(mine):
            o = _dot(lhs, jnp.concatenate([s.astype(BF16), vb], axis=0))
            s = s * dl[:, i:i + 1] + kv
            gate = proj[rows, _HGATE + hd * LANE:_HGATE + (hd + 1) * LANE]
            _gate_store(o, hgain_ref[:, cols], gate, omix_scr, rows,
                        slice(RET_W + hd * LANE, RET_W + (hd + 1) * LANE))
        shg_ref[0, hd] = s

    x1_ref[...] = x + _dot(omix_scr[...], wout_ref[...])


def _ret_tables(length, period):
    log_g = np.log(1.0 - 2.0 ** (-5.0 - np.arange(RET_HEADS, dtype=np.float64)))
    idx = np.arange(length)
    pos = idx % period
    rel = (idx[:, None] - idx[None, :]).astype(np.float64)
    same = (idx[:, None] // period) == (idx[None, :] // period)
    valid = (rel >= 0) & same
    dec = np.where(valid[None], np.exp(log_g[:, None, None] * np.where(valid, rel, 0.0)[None]), 0.0)
    qdec = np.exp(log_g[:, None] * (pos + 1.0))[:, :, None] * np.ones((1, 1, LANE))
    kdec = np.exp(log_g[:, None] * (period - 1.0 - pos))[:, :, None] * np.ones((1, 1, LANE))
    sdec = np.exp(log_g * period)[:, None, None] * np.ones((1, 1, LANE))
    as32 = lambda a: jnp.asarray(a, dtype=F32)
    return as32(dec), as32(qdec), as32(kdec), as32(sdec)


def _rope_tables(pos):
    half = RET_DK // 2
    inv_freq = ROPE_BASE ** (-np.arange(half, dtype=np.float64) / half)
    ang = np.asarray(pos, dtype=np.float64)[:, None] * inv_freq[None, :]
    cos, sin = np.cos(ang), np.sin(ang)
    return (jnp.asarray(np.concatenate([cos, cos], axis=-1), dtype=F32),
            jnp.asarray(np.concatenate([-sin, sin], axis=-1), dtype=F32))


def _mix_prompt(x2d, batch, seq, g_mix, w_in, ret_gain, hg_gain, hg_lb, w_out,
                q_s, cache_k, cache_v, dec_len, w_xo, w_gate, w_up, w_down):
    nt = seq // MIX_TOKENS
    n_steps = batch * nt
    d_ff = w_gate.shape[1]
    w_rows = D_MODEL // n_steps
    down_blocks = d_ff // CAST_DOWN_ROWS
    assert w_rows * n_steps == D_MODEL and w_rows % 16 == 0
    assert down_blocks * CAST_DOWN_ROWS == d_ff and down_blocks <= n_steps
    step = lambda b, t: b * nt + t
    sq_rows = pl.BlockSpec((w_rows, D_MODEL), lambda b, t: (step(b, t), 0))
    ff_rows = pl.BlockSpec((w_rows, d_ff), lambda b, t: (step(b, t), 0))
    down_rows = pl.BlockSpec((CAST_DOWN_ROWS, D_MODEL),
                             lambda b, t: (jnp.minimum(step(b, t), down_blocks - 1), 0))
    n_seq = cache_k.shape[0]
    seqs_per_step = n_seq // (batch * nt)
    assert seqs_per_step * batch * nt == n_seq and (seqs_per_step * dec_len) % SUBLANES == 0
    srows = pl.BlockSpec((seqs_per_step * dec_len, D_MODEL), lambda b, t: (b * nt + t, 0))
    cblk = pl.BlockSpec((seqs_per_step, N_MEM, XA_HEADS, XA_HD), lambda b, t: (b * nt + t, 0, 0, 0))
    cos2, sin2 = _rope_tables(np.arange(seq))
    dec, qdec, kdec, sdec = _ret_tables(RET_CHUNK_LEN, RET_CHUNK_LEN)
    c2 = lambda b, t: (0, 0)
    c3 = lambda b, t: (0, 0, 0)
    tok = pl.BlockSpec((MIX_TOKENS, D_MODEL), lambda b, t: (b * nt + t, 0))
    state = pl.BlockSpec((1, RET_HEADS, RET_DK, RET_DV), lambda b, t: (b, 0, 0, 0))
    return pl.pallas_call(
        functools.partial(_mix_kernel, dec_len),
        grid=(batch, nt),
        in_specs=[tok, pl.BlockSpec((1, D_MODEL), c2), _resident((D_MODEL, IN_W)),
                  pl.BlockSpec((1, RET_W), c2), pl.BlockSpec((1, HG_W), c2),
                  pl.BlockSpec(hg_lb.shape, c2), _resident((MIX_W, D_MODEL)),
                  pl.BlockSpec((MIX_TOKENS, LANE), lambda b, t: (t, 0)),
                  pl.BlockSpec((MIX_TOKENS, LANE), lambda b, t: (t, 0)),
                  pl.BlockSpec(dec.shape, c3), pl.BlockSpec(qdec.shape, c3),
                  pl.BlockSpec(kdec.shape, c3), pl.BlockSpec(sdec.shape, c3), srows, cblk, cblk,
                  sq_rows, ff_rows, ff_rows, down_rows],
        out_specs=[tok, state, state, srows, sq_rows, ff_rows, ff_rows, down_rows],
        out_shape=[jax.ShapeDtypeStruct(x2d.shape, F32),
                   jax.ShapeDtypeStruct((batch, RET_HEADS, RET_DK, RET_DV), F32),
                   jax.ShapeDtypeStruct((batch, HG_HEADS, HG_DK, HG_DV), F32),
                   jax.ShapeDtypeStruct(q_s.shape, F32),
                   jax.ShapeDtypeStruct(w_xo.shape, BF16), jax.ShapeDtypeStruct(w_gate.shape, BF16),
                   jax.ShapeDtypeStruct(w_up.shape, BF16), jax.ShapeDtypeStruct(w_down.shape, BF16)],
        scratch_shapes=[pltpu.VMEM((MIX_TOKENS, MIX_W), BF16)],
        compiler_params=pltpu.CompilerParams(dimension_semantics=("arbitrary", "arbitrary"),
                                             vmem_limit_bytes=VMEM_LIMIT_BYTES),
        name="mix_prompt",
    )(x2d, g_mix, w_in, ret_gain, hg_gain, hg_lb, w_out, cos2, sin2, dec, qdec, kdec, sdec,
      q_s, cache_k, cache_v, w_xo, w_gate, w_up, w_down)


def _inv_rms(x):
    return lax.rsqrt(jnp.mean(x * x, axis=-1, keepdims=True) + EPS)


def _query(x1, gxa, wxq):
    return _dot((x1 * gxa).astype(BF16), wxq) * (_inv_rms(x1) * (XA_HD ** -0.5))


def _ffn_final(x1, ox, wxo, gffn, wgate, wup, wdown, gfinal):
    x2 = x1 + _dot(ox, wxo)
    hb = (x2 * gffn).astype(BF16)
    r = _inv_rms(x2)
    a = (_silu(_dot(hb, wgate) * r) * (_dot(hb, wup) * r)).astype(BF16)
    half = x2.shape[0] // 2
    return jnp.concatenate([_rms(x2[rows] + _dot(a[rows], wdown), gfinal)
                            for rows in (slice(0, half), slice(half, None))], axis=0)


def _tail_kernel(x1_ref, x1s_ref, oxs_ref, mk_ref, mv_ref, gxa_ref, wxq_ref, wxo_ref, gffn_ref,
                 wgate_ref, wup_ref, wdown_ref, gfinal_ref, y_ref, ys_ref, ox_scr):
    is_sample = pl.program_id(0) == pl.num_programs(0) - 1
    x1 = jnp.where(is_sample, x1s_ref[...], x1_ref[...])
    q = _query(x1, gxa_ref[...], wxq_ref[...]).astype(BF16)
    heads = [slice(hd * XA_HD, (hd + 1) * XA_HD) for hd in range(XA_HEADS)]
    scores = [_dot_nt(q[:, cols], mk_ref[0, :, cols]) for cols in heads]
    probs = [_softmax_rows(s).astype(BF16) for s in scores]
    for cols, p in zip(heads, probs):
        ox_scr[:, cols] = _dot(p, mv_ref[0, :, cols]).astype(BF16)
    ox = jnp.where(is_sample, oxs_ref[...].astype(BF16), ox_scr[...])
    y = _ffn_final(x1, ox, wxo_ref[...], gffn_ref[...], wgate_ref[...], wup_ref[...],
                   wdown_ref[...], gfinal_ref[...])

    @pl.when(jnp.logical_not(is_sample))
    def _():
        y_ref[...] = y

    @pl.when(is_sample)
    def _():
        ys_ref[...] = y.reshape(ys_ref.shape)


def _tail(x1, batch, seq, mkb, mvb, x1_s, ox_s, dec_len, g_xa, w_xq, w_xo, g_ffn, w_gate, w_up, w_down,
          g_final):
    nt = seq // TAIL_TOKENS
    n_tiles = batch * nt
    d_ff = w_gate.shape[1]
    n_s = x1_s.shape[0]
    assert n_s == TAIL_TOKENS, "the sample tokens must fill exactly one tail tile"
    tile = lambda i: jnp.minimum(i, n_tiles - 1)
    tok = pl.BlockSpec((TAIL_TOKENS, D_MODEL), lambda i: (tile(i), 0))
    mem = pl.BlockSpec((1, N_MEM, D_MODEL), lambda i: (tile(i) // nt, 0, 0))
    ys_shape = (n_s // dec_len, dec_len, D_MODEL)
    return pl.pallas_call(
        _tail_kernel,
        grid=(n_tiles + 1,),
        in_specs=[tok, _resident((n_s, D_MODEL)), _resident((n_s, D_MODEL)), mem, mem,
                  _resident((1, D_MODEL)), _resident((D_MODEL, D_MODEL)),
                  _resident((D_MODEL, D_MODEL)), _resident((1, D_MODEL)),
                  _resident((D_MODEL, d_ff)), _resident((D_MODEL, d_ff)),
                  _resident((d_ff, D_MODEL)), _resident((1, D_MODEL))],
        out_specs=[tok, pl.BlockSpec(ys_shape, lambda i: (0, 0, 0))],
        out_shape=[jax.ShapeDtypeStruct(x1.shape, F32), jax.ShapeDtypeStruct(ys_shape, F32)],
        scratch_shapes=[pltpu.VMEM((TAIL_TOKENS, D_MODEL), BF16)],
        compiler_params=pltpu.CompilerParams(dimension_semantics=("arbitrary",),
                                             vmem_limit_bytes=VMEM_LIMIT_BYTES),
        name="tail",
    )(x1, x1_s, ox_s, mkb, mvb, g_xa, w_xq, w_xo, g_ffn, w_gate, w_up, w_down, g_final)


def _proj_kernel(x_ref, g_ref, w_ref, o_ref, wb_ref):
    wb = w_ref[...].astype(BF16)
    wb_ref[...] = wb
    x = x_ref[...].reshape(-1, D_MODEL)
    o_ref[...] = _dot(_rms(x, g_ref[...]).astype(BF16), wb)


def _proj_sample(x3d, g_mix, w_in):
    n = x3d.shape[0] * x3d.shape[1]
    nb = IN_W // D_MODEL
    wblk = pl.BlockSpec((D_MODEL, D_MODEL), lambda j: (0, j))
    return pl.pallas_call(
        _proj_kernel,
        grid=(nb,),
        in_specs=[pl.BlockSpec(x3d.shape, lambda j: (0, 0, 0)),
                  pl.BlockSpec((1, D_MODEL), lambda j: (0, 0)), wblk],
        out_specs=[pl.BlockSpec((n, D_MODEL), lambda j: (0, j)), wblk],
        out_shape=[jax.ShapeDtypeStruct((n, IN_W), F32), jax.ShapeDtypeStruct(w_in.shape, BF16)],
        compiler_params=pltpu.CompilerParams(dimension_semantics=("arbitrary",),
                                             vmem_limit_bytes=VMEM_LIMIT_BYTES),
        name="proj_sample",
    )(x3d, g_mix, w_in)


def _state_kernel(dec_len, proj_ref, sret_ref, shg_ref, rgain_ref, hgain_ref, hglb_ref,
                  cos_ref, sin_ref, dec_ref, qdec_ref, kdec_ref, sdec_ref,
                  omix_ref, nret_ref, nhg_ref):
    n_rows = STATE_SEQS * dec_len
    pair_rows = SUBLANES
    seqs_per_pair = pair_rows // dec_len
    row8 = lax.broadcasted_iota(jnp.int32, (pair_rows, LANE), 0)
    rown = lax.broadcasted_iota(jnp.int32, (n_rows, n_rows), 0)
    coln = lax.broadcasted_iota(jnp.int32, (n_rows, n_rows), 1)
    same_seq_causal = (rown >= coln) & ((rown // dec_len) == (coln // dec_len))
    proj = proj_ref[...]
    cos2 = cos_ref[...]
    sin2 = sin_ref[...]

    def per_sequence(q_all, kd_all, vb_all, state_ref, new_ref, hd, scale_of):
        outs = []
        for p in range(n_rows // pair_rows):
            prow = slice(p * pair_rows, (p + 1) * pair_rows)
            q8, kd8, v8 = q_all[prow], kd_all[prow], vb_all[prow]
            acc = jnp.zeros((pair_rows, LANE), F32)
            for j in range(seqs_per_pair):
                seq = p * seqs_per_pair + j
                mine = (row8 >= j * dec_len) & (row8 < (j + 1) * dec_len)
                s0 = state_ref[seq, hd]
                acc = jnp.where(mine, _dot(q8, s0.astype(BF16)), acc)
                kz = jnp.where(mine, kd8, jnp.zeros_like(kd8))
                new_ref[seq, hd] = scale_of(seq) * s0 + _dot_tn(kz, v8)
            outs.append(acc)
        return jnp.concatenate(outs, axis=0)

    for hd in range(RET_HEADS):
        cols = slice(hd * LANE, (hd + 1) * LANE)
        q = _rope(proj[:, _RQ + hd * LANE:_RQ + (hd + 1) * LANE], cos2, sin2)
        k = _rope(proj[:, _RK + hd * LANE:_RK + (hd + 1) * LANE], cos2, sin2) * (RET_DK ** -0.5)
        vb = proj[:, _RV + hd * LANE:_RV + (hd + 1) * LANE].astype(BF16)
        gate = proj[:, _RG + hd * LANE:_RG + (hd + 1) * LANE]
        qb = q.astype(BF16)
        att = _dot_nt(qb, k.astype(BF16)) * dec_ref[hd]
        kd = (k * kdec_ref[hd]).astype(BF16)
        sdec = sdec_ref[hd]
        qs = per_sequence(qb, kd, vb, sret_ref, nret_ref, hd, lambda seq: sdec)
        o = _dot(att.astype(BF16), vb) + qdec_ref[hd] * qs
        _gate_store(o, rgain_ref[:, cols], gate, omix_ref, slice(None), cols)

    lower = _lower_bound(hglb_ref[...])
    f = lower + (1.0 - lower) * _sigmoid(proj[:, _HF:_HF + HG_W])
    kk = 1.0 - f
    qq = _silu(proj[:, _HQ:_HQ + HG_W])
    b = _cumsum_rows(jnp.log2(f), dec_len)
    pos = lax.broadcasted_iota(jnp.int32, b.shape, 0) & (dec_len - 1)

    def spread(row_in_seq):
        picked = jnp.where(pos == row_in_seq, b, 0.0)
        out = picked
        for s in range(1, dec_len):
            out = out + jnp.where(pos == (row_in_seq + s), pltpu.roll(picked, s, 0), 0.0)
            out = out + jnp.where(pos == (row_in_seq - s), pltpu.roll(picked, n_rows - s, 0), 0.0)
        return out

    ref = spread(dec_len // 2 - 1)
    b_last = spread(dec_len - 1)
    q_intra = (qq * jnp.exp2(b - ref)).astype(BF16)
    k_intra = (kk * jnp.exp2(ref - b)).astype(BF16)
    q_inter = (qq * jnp.exp2(b)).astype(BF16)
    k_upd = (kk * jnp.exp2(b_last - b)).astype(BF16)
    d_last = jnp.exp2(b_last)
    for hd in range(HG_HEADS):
        cols = slice(hd * LANE, (hd + 1) * LANE)
        vb = proj[:, _HI + hd * LANE:_HI + (hd + 1) * LANE].astype(BF16)
        gate = proj[:, _HGATE + hd * LANE:_HGATE + (hd + 1) * LANE]
        att = jnp.where(same_seq_causal, _dot_nt(q_intra[:, cols], k_intra[:, cols]), 0.0)
        dpad = jnp.concatenate([d_last[:, cols], jnp.zeros((LANE - n_rows, LANE), F32)], axis=0)
        dcol = dpad.T
        scale_of = lambda seq: dcol[:, seq * dec_len:seq * dec_len + 1]
        qs = per_sequence(q_inter[:, cols], k_upd[:, cols], vb, shg_ref, nhg_ref, hd, scale_of)
        o = _dot(att.astype(BF16), vb) + qs
        _gate_store(o, hgain_ref[:, cols], gate, omix_ref, slice(None),
                    slice(RET_W + hd * LANE, RET_W + (hd + 1) * LANE))


def _state_sample(proj, state_ret, state_hgrn, dec_len, ret_gain, hg_gain, hg_lb):
    n_seq = state_ret.shape[0]
    n_rows = STATE_SEQS * dec_len
    cos2, sin2 = _rope_tables(np.tile(np.arange(dec_len) + PAST_LEN, STATE_SEQS))
    dec, qdec, kdec, sdec = _ret_tables(n_rows, dec_len)
    c2 = lambda i: (0, 0)
    c3 = lambda i: (0, 0, 0)
    state = pl.BlockSpec((STATE_SEQS, RET_HEADS, RET_DK, RET_DV), lambda i: (i, 0, 0, 0))
    return pl.pallas_call(
        functools.partial(_state_kernel, dec_len),
        grid=(n_seq // STATE_SEQS,),
        in_specs=[pl.BlockSpec((n_rows, IN_W), lambda i: (i, 0)), state, state,
                  pl.BlockSpec((1, RET_W), c2), pl.BlockSpec((1, HG_W), c2),
                  pl.BlockSpec(hg_lb.shape, c2),
                  pl.BlockSpec((n_rows, LANE), c2), pl.BlockSpec((n_rows, LANE), c2),
                  pl.BlockSpec(dec.shape, c3), pl.BlockSpec(qdec.shape, c3),
                  pl.BlockSpec(kdec.shape, c3), pl.BlockSpec(sdec.shape, c3)],
        out_specs=[pl.BlockSpec((n_rows, MIX_W), lambda i: (i, 0)), state, state],
        out_shape=[jax.ShapeDtypeStruct((proj.shape[0], MIX_W), BF16),
                   jax.ShapeDtypeStruct(state_ret.shape, F32),
                   jax.ShapeDtypeStruct(state_hgrn.shape, F32)],
        compiler_params=pltpu.CompilerParams(dimension_semantics=("arbitrary",),
                                             vmem_limit_bytes=VMEM_LIMIT_BYTES),
        name="state_sample",
    )(proj, state_ret, state_hgrn, ret_gain, hg_gain, hg_lb, cos2, sin2, dec, qdec, kdec, sdec)


def _outq_kernel(x_ref, omix_ref, wout_ref, gxa_ref, wxq_ref, x1_ref, q_ref, woutb_ref, wxqb_ref):
    wout = wout_ref[...].astype(BF16)
    wxq = wxq_ref[...].astype(BF16)
    woutb_ref[...] = wout
    wxqb_ref[...] = wxq
    x1 = x_ref[...].reshape(-1, D_MODEL) + _dot(omix_ref[...], wout)
    x1_ref[...] = x1
    q_ref[...] = _query(x1, gxa_ref[...], wxq)


def _outq_sample(x3d, omix, w_out, g_xa, w_xq):
    n = x3d.shape[0] * x3d.shape[1]
    z = lambda i: (0, 0)
    return pl.pallas_call(
        _outq_kernel,
        grid=(1,),
        in_specs=[pl.BlockSpec(x3d.shape, lambda i: (0, 0, 0)), pl.BlockSpec((n, MIX_W), z),
                  pl.BlockSpec((MIX_W, D_MODEL), z), pl.BlockSpec((1, D_MODEL), z),
                  pl.BlockSpec((D_MODEL, D_MODEL), z)],
        out_specs=[pl.BlockSpec((n, D_MODEL), z), pl.BlockSpec((n, D_MODEL), z),
                   pl.BlockSpec((MIX_W, D_MODEL), z), pl.BlockSpec((D_MODEL, D_MODEL), z)],
        out_shape=[jax.ShapeDtypeStruct((n, D_MODEL), F32), jax.ShapeDtypeStruct((n, D_MODEL), F32),
                   jax.ShapeDtypeStruct((MIX_W, D_MODEL), BF16),
                   jax.ShapeDtypeStruct((D_MODEL, D_MODEL), BF16)],
        compiler_params=pltpu.CompilerParams(dimension_semantics=("arbitrary",),
                                             vmem_limit_bytes=VMEM_LIMIT_BYTES),
        name="outq_sample",
    )(x3d, omix, w_out, g_xa, w_xq)


def kernel(x_prompt, x_sample, mem_prompt, state_ret, state_hgrn, cache_mem_k, cache_mem_v, g_mix, w_in,
           ret_gain, hg_gain, hg_lb, w_out, g_xa, g_mem, w_xq, w_xk, w_xv, w_xo, g_ffn, w_gate, w_up,
           w_down, g_final):
    depth = w_in.shape[0]
    assert depth == 1, "single-layer step"
    batch, seq, d = x_prompt.shape
    dec_batch, dec_len, _ = x_sample.shape
    assert d == D_MODEL and seq % MIX_TOKENS == 0 and seq % TAIL_TOKENS == 0
    assert SUBLANES % dec_len == 0 and dec_batch % STATE_SEQS == 0

    g_final2 = g_final.reshape(1, D_MODEL)

    proj_s, w_in_b = _proj_sample(x_sample, g_mix, w_in[0])
    omix_s, ret_s, hg_s = _state_sample(proj_s, state_ret[0], state_hgrn[0], dec_len,
                                        ret_gain, hg_gain, hg_lb)
    x1_s, q_s, w_out_b, w_xq_b = _outq_sample(x_sample, omix_s, w_out[0], g_xa, w_xq[0])

    mk, mv, mkb, mvb = _memkv(mem_prompt.reshape(batch * N_MEM, D_MODEL), g_mem, w_xk[0], w_xv[0])
    x1_p, ret_p, hg_p, ox_s, w_xo_b, w_gate_b, w_up_b, w_down_b = _mix_prompt(
        x_prompt.reshape(batch * seq, D_MODEL), batch, seq, g_mix, w_in_b, ret_gain, hg_gain, hg_lb,
        w_out_b, q_s, cache_mem_k[0], cache_mem_v[0], dec_len, w_xo[0], w_gate[0], w_up[0], w_down[0])

    y_p, y_s = _tail(x1_p, batch, seq, mkb.reshape(batch, N_MEM, D_MODEL),
                     mvb.reshape(batch, N_MEM, D_MODEL), x1_s, ox_s, dec_len, g_xa, w_xq_b, w_xo_b,
                     g_ffn, w_gate_b, w_up_b, w_down_b, g_final2)

    kv_shape = (depth, batch, N_MEM, XA_HEADS, XA_HD)
    return (y_p.reshape(batch, seq, D_MODEL), y_s,
            ret_p[None], hg_p[None], mk.reshape(kv_shape), mv.reshape(kv_shape),
            ret_s[None], hg_s[None])
```

```python
import functools

import jax
import jax.numpy as jnp
import numpy as np
from jax import lax
from jax.experimental import pallas as pl
from jax.experimental.pallas import tpu as pltpu

D_MODEL = 1024
RET_HEADS = 4
RET_DK = 128
RET_DV = 128
RET_W = RET_HEADS * RET_DV
HG_HEADS = 4
HG_DK = 128
HG_DV = 128
HG_W = HG_HEADS * HG_DV
MIX_W = RET_W + HG_W
IN_W = 2 * RET_HEADS * RET_DK + 2 * RET_W + 2 * HG_HEADS * HG_DK + 2 * HG_W
N_MEM = 256
XA_HEADS = 4
XA_HD = D_MODEL // XA_HEADS
PAST_LEN = 16384
ROPE_BASE = 10000.0
EPS = 1e-6
LOG2E = 1.4426950408889634

_RQ, _RK, _RV, _RG = 0, 512, 1024, 1536
_HQ, _HF, _HI, _HGATE = 2048, 2560, 3072, 3584

LANE = 128
SUBLANES = 8
MIX_TOKENS = 512
RET_CHUNK_LEN = 128
HG_CHUNK_LEN = 64
HG_SUB = 16
TAIL_TOKENS = 512
MEMKV_ROWS = 256
PROJ_COLS = 512
STATE_SEQS = 16
CAST_DOWN_ROWS = 128
VMEM_LIMIT_BYTES = 56 * 1024 * 1024

F32 = jnp.float32
BF16 = jnp.bfloat16


def _dot(a, b):
    return jnp.dot(a, b, preferred_element_type=F32)


def _dot_nt(a, b):
    return lax.dot_general(a, b, (((1,), (1,)), ((), ())), preferred_element_type=F32)


def _dot_tn(a, b):
    return lax.dot_general(a, b, (((0,), (0,)), ((), ())), preferred_element_type=F32)


def _rms(x, g):
    ms = jnp.mean(x * x, axis=-1, keepdims=True)
    return x * lax.rsqrt(ms + EPS) * g


def _sigmoid(x):
    return 1.0 / (1.0 + jnp.exp2(x * (-LOG2E)))


def _silu(x):
    return x * _sigmoid(x)


def _head_norm(o):
    return o * lax.rsqrt(jnp.mean(o * o, axis=-1, keepdims=True) + EPS)


def _softmax_rows(s):
    p = jnp.exp(s - jnp.max(s, axis=-1, keepdims=True))
    return p / jnp.sum(p, axis=-1, keepdims=True)


def _rope(x, cos2, sin2):
    return x * cos2 + pltpu.roll(x, x.shape[-1] // 2, 1) * sin2


def _lower_bound(hglb):
    m = jnp.max(hglb, axis=0, keepdims=True)
    e = jnp.exp(hglb - m)
    return e[0:1, :] / jnp.sum(e, axis=0, keepdims=True)


def _cumsum_rows(x, period):
    row = lax.broadcasted_iota(jnp.int32, x.shape, 0) & (period - 1)
    s = 1
    while s < period:
        x = x + jnp.where(row >= s, pltpu.roll(x, s, 0), 0.0)
        s *= 2
    return x


def _resident(shape):
    zeros = (0,) * len(shape)
    return pl.BlockSpec(shape, lambda *_: zeros, pipeline_mode=pl.Buffered(1))


def _memkv_kernel(mem_ref, g_ref, wk_ref, wv_ref, k_ref, v_ref, kb_ref, vb_ref):
    m = _rms(mem_ref[...], g_ref[...]).astype(BF16)
    k = _dot(m, wk_ref[...].astype(BF16))
    v = _dot(m, wv_ref[...].astype(BF16))
    for hd in range(XA_HEADS):
        cols = slice(hd * XA_HD, (hd + 1) * XA_HD)
        k_ref[:, hd, :] = k[:, cols]
        v_ref[:, hd, :] = v[:, cols]
    kb_ref[...] = k.astype(BF16)
    vb_ref[...] = v.astype(BF16)


def _memkv(mem2d, g_mem, w_xk, w_xv):
    n = mem2d.shape[0]
    full = lambda i: (0, 0)
    row = lambda i: (i, 0)
    blk = pl.BlockSpec((MEMKV_ROWS, D_MODEL), row)
    hblk = pl.BlockSpec((MEMKV_ROWS, XA_HEADS, XA_HD), lambda i: (i, 0, 0))
    return pl.pallas_call(
        _memkv_kernel,
        grid=(n // MEMKV_ROWS,),
        in_specs=[blk, pl.BlockSpec((1, D_MODEL), full),
                  _resident((D_MODEL, D_MODEL)), _resident((D_MODEL, D_MODEL))],
        out_specs=[hblk, hblk, blk, blk],
        out_shape=[jax.ShapeDtypeStruct((n, XA_HEADS, XA_HD), F32),
                   jax.ShapeDtypeStruct((n, XA_HEADS, XA_HD), F32),
                   jax.ShapeDtypeStruct((n, D_MODEL), BF16), jax.ShapeDtypeStruct((n, D_MODEL), BF16)],
        compiler_params=pltpu.CompilerParams(dimension_semantics=("arbitrary",),
                                             vmem_limit_bytes=VMEM_LIMIT_BYTES),
        name="memkv",
    )(mem2d, g_mem, w_xk, w_xv)


def _sample_xattn_stages(dec_len, n_seqs, q_ref, k_ref, v_ref, o_ref):
    pair_rows = SUBLANES
    seqs_per_pair = pair_rows // dec_len
    n_kv = N_MEM * XA_HEADS
    q_rows = XA_HEADS * pair_rows
    own_head = (lax.broadcasted_iota(jnp.int32, (q_rows, n_kv), 1) % XA_HEADS
                == lax.broadcasted_iota(jnp.int32, (q_rows, n_kv), 0) // pair_rows)
    row8 = lax.broadcasted_iota(jnp.int32, (q_rows, XA_HD), 0) % pair_rows
    probs, outs = {}, {}

    def score(seq):
        prow = slice((seq // seqs_per_pair) * pair_rows, (seq // seqs_per_pair + 1) * pair_rows)
        q8 = q_ref[prow, :]
        qs = jnp.concatenate([q8[:, hd * XA_HD:(hd + 1) * XA_HD] for hd in range(XA_HEADS)],
                             axis=0).astype(BF16)
        kb = k_ref[seq].reshape(n_kv, XA_HD).astype(BF16)
        probs[seq] = _softmax_rows(jnp.where(own_head, _dot_nt(qs, kb), -jnp.inf)).astype(BF16)

    def attend(seq):
        p, j = divmod(seq, seqs_per_pair)
        oj = _dot(probs.pop(seq), v_ref[seq].reshape(n_kv, XA_HD).astype(BF16))
        mine = (row8 >= j * dec_len) & (row8 < (j + 1) * dec_len)
        outs[p] = jnp.where(mine, oj, outs[p]) if p in outs else oj
        if j == seqs_per_pair - 1:
            o = outs.pop(p)
            for hd in range(XA_HEADS):
                o_ref[p * pair_rows:(p + 1) * pair_rows, hd * XA_HD:(hd + 1) * XA_HD] = \
                    o[hd * pair_rows:(hd + 1) * pair_rows, :]

    return ([functools.partial(score, s) for s in range(n_seqs)]
            + [functools.partial(attend, s) for s in range(n_seqs)])


def _gate_store(o, gain, gate, out_ref, rows, cols):
    out_ref[rows, cols] = (_head_norm(o) * gain * _silu(gate)).astype(BF16)


def _mix_kernel(dec_len, x_ref, gmix_ref, win_ref, rgain_ref, hgain_ref, hglb_ref, wout_ref,
                cos_ref, sin_ref, dec_ref, qdec_ref, kdec_ref, sdec_ref, qs_ref, ck_ref, cv_ref,
                wxo_ref, wgate_ref, wup_ref, wdown_ref,
                x1_ref, sret_ref, shg_ref, oxs_ref, wxob_ref, wgateb_ref, wupb_ref, wdownb_ref,
                omix_scr):
    t = pl.program_id(1)

    @pl.when(t == 0)
    def _():
        sret_ref[...] = jnp.zeros_like(sret_ref)
        shg_ref[...] = jnp.zeros_like(shg_ref)

    for src, dst in ((wxo_ref, wxob_ref), (wgate_ref, wgateb_ref), (wup_ref, wupb_ref),
                     (wdown_ref, wdownb_ref)):
        dst[...] = src[...].astype(BF16)

    x = x_ref[...]
    h = _rms(x, gmix_ref[...]).astype(BF16)
    side_work = _sample_xattn_stages(dec_len, ck_ref.shape[0], qs_ref, ck_ref, cv_ref, oxs_ref)
    groups = {}
    for g0 in (_HF, _HQ, _RQ, _RK, _RV, _HI, _RG, _HGATE):
        groups[g0] = _dot(h, win_ref[:, g0:g0 + RET_W])
        if side_work:
            side_work.pop(0)()
    while side_work:
        side_work.pop(0)()

    class _Proj:
        def __getitem__(self, idx):
            rows, cols = idx
            g0 = (cols.start // RET_W) * RET_W
            return groups[g0][rows, cols.start - g0:cols.stop - g0]

    proj = _Proj()

    ret_units = []
    for c in range(MIX_TOKENS // RET_CHUNK_LEN):
        rows = slice(c * RET_CHUNK_LEN, (c + 1) * RET_CHUNK_LEN)
        cos2 = cos_ref[rows, :]
        sin2 = sin_ref[rows, :]
        for hd in range(RET_HEADS):
            q = _rope(proj[rows, _RQ + hd * LANE:_RQ + (hd + 1) * LANE], cos2, sin2)
            k = _rope(proj[rows, _RK + hd * LANE:_RK + (hd + 1) * LANE], cos2, sin2) * (RET_DK ** -0.5)
            vb = proj[rows, _RV + hd * LANE:_RV + (hd + 1) * LANE].astype(BF16)
            att = (_dot_nt(q.astype(BF16), k.astype(BF16)) * dec_ref[hd]).astype(BF16)
            kv = _dot_tn((k * kdec_ref[hd]).astype(BF16), vb)
            lhs = jnp.concatenate([att, (q * qdec_ref[hd]).astype(BF16)], axis=1)
            ret_units.append((rows, hd, lhs, vb, kv))

    lower = _lower_bound(hglb_ref[...])
    n_sub = HG_CHUNK_LEN // HG_SUB
    crow = lax.broadcasted_iota(jnp.int32, (HG_CHUNK_LEN, HG_CHUNK_LEN), 0)
    ccol = lax.broadcasted_iota(jnp.int32, (HG_CHUNK_LEN, HG_CHUNK_LEN), 1)
    causal = crow >= ccol
    hg_units = []
    for c in range(MIX_TOKENS // HG_CHUNK_LEN):
        rows = slice(c * HG_CHUNK_LEN, (c + 1) * HG_CHUNK_LEN)
        f = lower + (1.0 - lower) * _sigmoid(proj[rows, _HF:_HF + HG_W])
        kk = 1.0 - f
        qq = _silu(proj[rows, _HQ:_HQ + HG_W])
        b = _cumsum_rows(jnp.log2(f), HG_CHUNK_LEN)
        b_last = b[HG_CHUNK_LEN - 1:HG_CHUNK_LEN, :]
        q_inter = (qq * jnp.exp2(b)).astype(BF16)
        k_upd = (kk * jnp.exp2(b_last - b)).astype(BF16)
        d_last = jnp.exp2(b_last)
        for hd in range(HG_HEADS):
            cols = slice(hd * LANE, (hd + 1) * LANE)
            bh = b[:, cols]
            qh = qq[:, cols]
            kh = kk[:, cols]
            vb = proj[rows, _HI + hd * LANE:_HI + (hd + 1) * LANE].astype(BF16)
            q_parts, k_parts = [], []
            for j in range(n_sub):
                lo, hi = j * HG_SUB, (j + 1) * HG_SUB
                ref = bh[lo + HG_SUB // 2 - 1:lo + HG_SUB // 2, :]
                qt = qh[lo:, :] * jnp.exp2(bh[lo:, :] - ref)
                kt = kh[lo:hi, :] * jnp.exp2(ref - bh[lo:hi, :])
                if lo:
                    qt = jnp.concatenate([jnp.zeros((lo, LANE), F32), qt], axis=0)
                    kt = jnp.concatenate([jnp.zeros((lo, LANE), F32), kt], axis=0)
                if hi < HG_CHUNK_LEN:
                    kt = jnp.concatenate([kt, jnp.zeros((HG_CHUNK_LEN - hi, LANE), F32)], axis=0)
                q_parts.append(qt.astype(BF16))
                k_parts.append(kt.astype(BF16))
            qcat = jnp.concatenate(q_parts, axis=1)
            kcat = jnp.concatenate(k_parts, axis=1)
            att = jnp.where(causal, _dot_nt(qcat, kcat), 0.0).astype(BF16)
            kv = _dot_tn(k_upd[:, cols], vb)
            lhs = jnp.concatenate([q_inter[:, cols], att], axis=1)
            hg_units.append((rows, hd, lhs, vb, kv, d_last[:, cols]))

    for hd in range(RET_HEADS):
        cols = slice(hd * LANE, (hd + 1) * LANE)
        s = sret_ref[0, hd]
        for rows, uh, lhs, vb, kv in ret_units:
            if uh != hd:
                continue
            o = _dot(lhs, jnp.concatenate([vb, s.astype(BF16)], axis=0))
            s = sdec_ref[hd] * s + kv
            gate = proj[rows, _RG + hd * LANE:_RG + (hd + 1) * LANE]
            _gate_store(o, rgain_ref[:, cols], gate, omix_scr, rows, cols)
        sret_ref[0, hd] = s

    for hd in range(HG_HEADS):
        cols = slice(hd * LANE, (hd + 1) * LANE)
        mine = [u for u in hg_units if u[1] == hd]
        dl = jnp.concatenate([u[5] for u in mine] + [jnp.zeros((LANE - len(mine), LANE), F32)], axis=0).T
        s = shg_ref[0, hd]
        for i, (rows, _, lhs, vb, kv, _) in enumerate(mine):
            o = _dot(lhs, jnp.concatenate([s.astype(BF16), vb], axis=0))
            s = s * dl[:, i:i + 1] + kv
            gate = proj[rows, _HGATE + hd * LANE:_HGATE + (hd + 1) * LANE]
            _gate_store(o, hgain_ref[:, cols], gate, omix_scr, rows,
                        slice(RET_W + hd * LANE, RET_W + (hd + 1) * LANE))
        shg_ref[0, hd] = s

    x1_ref[...] = x + _dot(omix_scr[...], wout_ref[...])


def _ret_tables(length, period):
    log_g = np.log(1.0 - 2.0 ** (-5.0 - np.arange(RET_HEADS, dtype=np.float64)))
    idx = np.arange(length)
    pos = idx % period
    rel = (idx[:, None] - idx[None, :]).astype(np.float64)
    same = (idx[:, None] // period) == (idx[None, :] // period)
    valid = (rel >= 0) & same
    dec = np.where(valid[None], np.exp(log_g[:, None, None] * np.where(valid, rel, 0.0)[None]), 0.0)
    qdec = np.exp(log_g[:, None] * (pos + 1.0))[:, :, None] * np.ones((1, 1, LANE))
    kdec = np.exp(log_g[:, None] * (period - 1.0 - pos))[:, :, None] * np.ones((1, 1, LANE))
    sdec = np.exp(log_g * period)[:, None, None] * np.ones((1, 1, LANE))
    as32 = lambda a: jnp.asarray(a, dtype=F32)
    return as32(dec), as32(qdec), as32(kdec), as32(sdec)


def _rope_tables(pos):
    half = RET_DK // 2
    inv_freq = ROPE_BASE ** (-np.arange(half, dtype=np.float64) / half)
    ang = np.asarray(pos, dtype=np.float64)[:, None] * inv_freq[None, :]
    cos, sin = np.cos(ang), np.sin(ang)
    return (jnp.asarray(np.concatenate([cos, cos], axis=-1), dtype=F32),
            jnp.asarray(np.concatenate([-sin, sin], axis=-1), dtype=F32))


def _mix_prompt(x2d, batch, seq, g_mix, w_in, ret_gain, hg_gain, hg_lb, w_out,
                q_s, cache_k, cache_v, dec_len, w_xo, w_gate, w_up, w_down):
    nt = seq // MIX_TOKENS
    n_steps = batch * nt
    d_ff = w_gate.shape[1]
    w_rows = D_MODEL // n_steps
    down_blocks = d_ff // CAST_DOWN_ROWS
    assert w_rows * n_steps == D_MODEL and w_rows % 16 == 0
    assert down_blocks * CAST_DOWN_ROWS == d_ff and down_blocks <= n_steps
    step = lambda b, t: b * nt + t
    sq_rows = pl.BlockSpec((w_rows, D_MODEL), lambda b, t: (step(b, t), 0))
    ff_rows = pl.BlockSpec((w_rows, d_ff), lambda b, t: (step(b, t), 0))
    down_rows = pl.BlockSpec((CAST_DOWN_ROWS, D_MODEL),
                             lambda b, t: (jnp.minimum(step(b, t), down_blocks - 1), 0))
    n_seq = cache_k.shape[0]
    seqs_per_step = n_seq // (batch * nt)
    assert seqs_per_step * batch * nt == n_seq and (seqs_per_step * dec_len) % SUBLANES == 0
    srows = pl.BlockSpec((seqs_per_step * dec_len, D_MODEL), lambda b, t: (b * nt + t, 0))
    cblk = pl.BlockSpec((seqs_per_step, N_MEM, XA_HEADS, XA_HD), lambda b, t: (b * nt + t, 0, 0, 0))
    cos2, sin2 = _rope_tables(np.arange(seq))
    dec, qdec, kdec, sdec = _ret_tables(RET_CHUNK_LEN, RET_CHUNK_LEN)
    c2 = lambda b, t: (0, 0)
    c3 = lambda b, t: (0, 0, 0)
    tok = pl.BlockSpec((MIX_TOKENS, D_MODEL), lambda b, t: (b * nt + t, 0))
    state = pl.BlockSpec((1, RET_HEADS, RET_DK, RET_DV), lambda b, t: (b, 0, 0, 0))
    return pl.pallas_call(
        functools.partial(_mix_kernel, dec_len),
        grid=(batch, nt),
        in_specs=[tok, pl.BlockSpec((1, D_MODEL), c2), _resident((D_MODEL, IN_W)),
                  pl.BlockSpec((1, RET_W), c2), pl.BlockSpec((1, HG_W), c2),
                  pl.BlockSpec(hg_lb.shape, c2), _resident((MIX_W, D_MODEL)),
                  pl.BlockSpec((MIX_TOKENS, LANE), lambda b, t: (t, 0)),
                  pl.BlockSpec((MIX_TOKENS, LANE), lambda b, t: (t, 0)),
                  pl.BlockSpec(dec.shape, c3), pl.BlockSpec(qdec.shape, c3),
                  pl.BlockSpec(kdec.shape, c3), pl.BlockSpec(sdec.shape, c3), srows, cblk, cblk,
                  sq_rows, ff_rows, ff_rows, down_rows],
        out_specs=[tok, state, state, srows, sq_rows, ff_rows, ff_rows, down_rows],
        out_shape=[jax.ShapeDtypeStruct(x2d.shape, F32),
                   jax.ShapeDtypeStruct((batch, RET_HEADS, RET_DK, RET_DV), F32),
                   jax.ShapeDtypeStruct((batch, HG_HEADS, HG_DK, HG_DV), F32),
                   jax.ShapeDtypeStruct(q_s.shape, F32),
                   jax.ShapeDtypeStruct(w_xo.shape, BF16), jax.ShapeDtypeStruct(w_gate.shape, BF16),
                   jax.ShapeDtypeStruct(w_up.shape, BF16), jax.ShapeDtypeStruct(w_down.shape, BF16)],
        scratch_shapes=[pltpu.VMEM((MIX_TOKENS, MIX_W), BF16)],
        compiler_params=pltpu.CompilerParams(dimension_semantics=("arbitrary", "arbitrary"),
                                             vmem_limit_bytes=VMEM_LIMIT_BYTES),
        name="mix_prompt",
    )(x2d, g_mix, w_in, ret_gain, hg_gain, hg_lb, w_out, cos2, sin2, dec, qdec, kdec, sdec,
      q_s, cache_k, cache_v, w_xo, w_gate, w_up, w_down)


def _inv_rms(x):
    return lax.rsqrt(jnp.mean(x * x, axis=-1, keepdims=True) + EPS)


def _query(x1, gxa, wxq):
    return _dot((x1 * gxa).astype(BF16), wxq) * (_inv_rms(x1) * (XA_HD ** -0.5))


def _ffn_final(x2, gffn, wgate, wup, wdown, gfinal):
    hb = (x2 * gffn).astype(BF16)
    r = _inv_rms(x2)
    a = (_silu(_dot(hb, wgate) * r) * (_dot(hb, wup) * r)).astype(BF16)
    half = x2.shape[0] // 2
    return jnp.concatenate([_rms(x2[rows] + _dot(a[rows], wdown), gfinal)
                            for rows in (slice(0, half), slice(half, None))], axis=0)


def _tail_kernel(x1_ref, mk_ref, mv_ref, gxa_ref, wxq_ref, wxo_ref, gffn_ref, wgate_ref, wup_ref,
                 wdown_ref, gfinal_ref, y_ref):
    x1 = x1_ref[...]
    q = _query(x1, gxa_ref[...], wxq_ref[...]).astype(BF16)
    heads = [slice(hd * XA_HD, (hd + 1) * XA_HD) for hd in range(XA_HEADS)]
    scores = [_dot_nt(q[:, cols], mk_ref[0, :, cols]) for cols in heads]
    probs = [_softmax_rows(s).astype(BF16) for s in scores]
    x2 = x1
    outs = []
    for cols, p in zip(heads, probs):
        outs.append((cols, _dot(p, mv_ref[0, :, cols]).astype(BF16)))
        if len(outs) > 1:
            pc, po = outs[-2]
            x2 = x2 + _dot(po, wxo_ref[pc, :])
    pc, po = outs[-1]
    x2 = x2 + _dot(po, wxo_ref[pc, :])
    y_ref[...] = _ffn_final(x2, gffn_ref[...], wgate_ref[...], wup_ref[...], wdown_ref[...],
                            gfinal_ref[...])


def _tail_prompt(x1, batch, seq, mkb, mvb, g_xa, w_xq, w_xo, g_ffn, w_gate, w_up, w_down, g_final):
    nt = seq // TAIL_TOKENS
    d_ff = w_gate.shape[1]
    tok = pl.BlockSpec((TAIL_TOKENS, D_MODEL), lambda b, t: (b * nt + t, 0))
    mem = pl.BlockSpec((1, N_MEM, D_MODEL), lambda b, t: (b, 0, 0))
    return pl.pallas_call(
        _tail_kernel,
        grid=(batch, nt),
        in_specs=[tok, mem, mem, _resident((1, D_MODEL)), _resident((D_MODEL, D_MODEL)),
                  _resident((D_MODEL, D_MODEL)), _resident((1, D_MODEL)),
                  _resident((D_MODEL, d_ff)), _resident((D_MODEL, d_ff)),
                  _resident((d_ff, D_MODEL)), _resident((1, D_MODEL))],
        out_specs=tok,
        out_shape=jax.ShapeDtypeStruct(x1.shape, F32),
        compiler_params=pltpu.CompilerParams(dimension_semantics=("arbitrary", "arbitrary"),
                                             vmem_limit_bytes=VMEM_LIMIT_BYTES),
        name="tail_prompt",
    )(x1, mkb, mvb, g_xa, w_xq, w_xo, g_ffn, w_gate, w_up, w_down, g_final)


def _proj_kernel(x_ref, g_ref, w_ref, o_ref, wb_ref):
    wb = w_ref[...].astype(BF16)
    wb_ref[...] = wb
    x = x_ref[...].reshape(-1, D_MODEL)
    o_ref[...] = _dot(_rms(x, g_ref[...]).astype(BF16), wb)


def _proj_sample(x3d, g_mix, w_in):
    n = x3d.shape[0] * x3d.shape[1]
    nb = IN_W // PROJ_COLS
    wblk = pl.BlockSpec((D_MODEL, PROJ_COLS), lambda j: (0, j))
    return pl.pallas_call(
        _proj_kernel,
        grid=(nb,),
        in_specs=[pl.BlockSpec(x3d.shape, lambda j: (0, 0, 0)),
                  pl.BlockSpec((1, D_MODEL), lambda j: (0, 0)), wblk],
        out_specs=[pl.BlockSpec((n, PROJ_COLS), lambda j: (0, j)), wblk],
        out_shape=[jax.ShapeDtypeStruct((n, IN_W), F32), jax.ShapeDtypeStruct(w_in.shape, BF16)],
        compiler_params=pltpu.CompilerParams(dimension_semantics=("arbitrary",),
                                             vmem_limit_bytes=VMEM_LIMIT_BYTES),
        name="proj_sample",
    )(x3d, g_mix, w_in)


def _state_kernel(dec_len, proj_ref, sret_ref, shg_ref, rgain_ref, hgain_ref, hglb_ref,
                  cos_ref, sin_ref, dec_ref, qdec_ref, kdec_ref, sdec_ref,
                  omix_ref, nret_ref, nhg_ref):
    n_rows = STATE_SEQS * dec_len
    pair_rows = SUBLANES
    seqs_per_pair = pair_rows // dec_len
    row8 = lax.broadcasted_iota(jnp.int32, (pair_rows, LANE), 0)
    rown = lax.broadcasted_iota(jnp.int32, (n_rows, n_rows), 0)
    coln = lax.broadcasted_iota(jnp.int32, (n_rows, n_rows), 1)
    same_seq_causal = (rown >= coln) & ((rown // dec_len) == (coln // dec_len))
    proj = proj_ref[...]
    cos2 = cos_ref[...]
    sin2 = sin_ref[...]

    def per_sequence(q_all, kd_all, vb_all, state_ref, new_ref, hd, scale_of):
        outs = []
        for p in range(n_rows // pair_rows):
            prow = slice(p * pair_rows, (p + 1) * pair_rows)
            q8, kd8, v8 = q_all[prow], kd_all[prow], vb_all[prow]
            acc = jnp.zeros((pair_rows, LANE), F32)
            for j in range(seqs_per_pair):
                seq = p * seqs_per_pair + j
                mine = (row8 >= j * dec_len) & (row8 < (j + 1) * dec_len)
                s0 = state_ref[seq, hd]
                acc = jnp.where(mine, _dot(q8, s0.astype(BF16)), acc)
                kz = jnp.where(mine, kd8, jnp.zeros_like(kd8))
                new_ref[seq, hd] = scale_of(seq) * s0 + _dot_tn(kz, v8)
            outs.append(acc)
        return jnp.concatenate(outs, axis=0)

    for hd in range(RET_HEADS):
        cols = slice(hd * LANE, (hd + 1) * LANE)
        q = _rope(proj[:, _RQ + hd * LANE:_RQ + (hd + 1) * LANE], cos2, sin2)
        k = _rope(proj[:, _RK + hd * LANE:_RK + (hd + 1) * LANE], cos2, sin2) * (RET_DK ** -0.5)
        vb = proj[:, _RV + hd * LANE:_RV + (hd + 1) * LANE].astype(BF16)
        gate = proj[:, _RG + hd * LANE:_RG + (hd + 1) * LANE]
        qb = q.astype(BF16)
        att = _dot_nt(qb, k.astype(BF16)) * dec_ref[hd]
        kd = (k * kdec_ref[hd]).astype(BF16)
        sdec = sdec_ref[hd]
        qs = per_sequence(qb, kd, vb, sret_ref, nret_ref, hd, lambda seq: sdec)
        o = _dot(att.astype(BF16), vb) + qdec_ref[hd] * qs
        _gate_store(o, rgain_ref[:, cols], gate, omix_ref, slice(None), cols)

    lower = _lower_bound(hglb_ref[...])
    f = lower + (1.0 - lower) * _sigmoid(proj[:, _HF:_HF + HG_W])
    kk = 1.0 - f
    qq = _silu(proj[:, _HQ:_HQ + HG_W])
    b = _cumsum_rows(jnp.log2(f), dec_len)
    pos = lax.broadcasted_iota(jnp.int32, b.shape, 0) & (dec_len - 1)

    def spread(row_in_seq):
        picked = jnp.where(pos == row_in_seq, b, 0.0)
        out = picked
        for s in range(1, dec_len):
            out = out + jnp.where(pos == (row_in_seq + s), pltpu.roll(picked, s, 0), 0.0)
            out = out + jnp.where(pos == (row_in_seq - s), pltpu.roll(picked, n_rows - s, 0), 0.0)
        return out

    ref = spread(dec_len // 2 - 1)
    b_last = spread(dec_len - 1)
    q_intra = (qq * jnp.exp2(b - ref)).astype(BF16)
    k_intra = (kk * jnp.exp2(ref - b)).astype(BF16)
    q_inter = (qq * jnp.exp2(b)).astype(BF16)
    k_upd = (kk * jnp.exp2(b_last - b)).astype(BF16)
    d_last = jnp.exp2(b_last)
    for hd in range(HG_HEADS):
        cols = slice(hd * LANE, (hd + 1) * LANE)
        vb = proj[:, _HI + hd * LANE:_HI + (hd + 1) * LANE].astype(BF16)
        gate = proj[:, _HGATE + hd * LANE:_HGATE + (hd + 1) * LANE]
        att = jnp.where(same_seq_causal, _dot_nt(q_intra[:, cols], k_intra[:, cols]), 0.0)
        dpad = jnp.concatenate([d_last[:, cols], jnp.zeros((LANE - n_rows, LANE), F32)], axis=0)
        dcol = dpad.T
        scale_of = lambda seq: dcol[:, seq * dec_len:seq * dec_len + 1]
        qs = per_sequence(q_inter[:, cols], k_upd[:, cols], vb, shg_ref, nhg_ref, hd, scale_of)
        o = _dot(att.astype(BF16), vb) + qs
        _gate_store(o, hgain_ref[:, cols], gate, omix_ref, slice(None),
                    slice(RET_W + hd * LANE, RET_W + (hd + 1) * LANE))


def _state_sample(proj, state_ret, state_hgrn, dec_len, ret_gain, hg_gain, hg_lb):
    n_seq = state_ret.shape[0]
    n_rows = STATE_SEQS * dec_len
    cos2, sin2 = _rope_tables(np.tile(np.arange(dec_len) + PAST_LEN, STATE_SEQS))
    dec, qdec, kdec, sdec = _ret_tables(n_rows, dec_len)
    c2 = lambda i: (0, 0)
    c3 = lambda i: (0, 0, 0)
    state = pl.BlockSpec((STATE_SEQS, RET_HEADS, RET_DK, RET_DV), lambda i: (i, 0, 0, 0))
    return pl.pallas_call(
        functools.partial(_state_kernel, dec_len),
        grid=(n_seq // STATE_SEQS,),
        in_specs=[pl.BlockSpec((n_rows, IN_W), lambda i: (i, 0)), state, state,
                  pl.BlockSpec((1, RET_W), c2), pl.BlockSpec((1, HG_W), c2),
                  pl.BlockSpec(hg_lb.shape, c2),
                  pl.BlockSpec((n_rows, LANE), c2), pl.BlockSpec((n_rows, LANE), c2),
                  pl.BlockSpec(dec.shape, c3), pl.BlockSpec(qdec.shape, c3),
                  pl.BlockSpec(kdec.shape, c3), pl.BlockSpec(sdec.shape, c3)],
        out_specs=[pl.BlockSpec((n_rows, MIX_W), lambda i: (i, 0)), state, state],
        out_shape=[jax.ShapeDtypeStruct((proj.shape[0], MIX_W), BF16),
                   jax.ShapeDtypeStruct(state_ret.shape, F32),
                   jax.ShapeDtypeStruct(state_hgrn.shape, F32)],
        compiler_params=pltpu.CompilerParams(dimension_semantics=("arbitrary",),
                                             vmem_limit_bytes=VMEM_LIMIT_BYTES),
        name="state_sample",
    )(proj, state_ret, state_hgrn, ret_gain, hg_gain, hg_lb, cos2, sin2, dec, qdec, kdec, sdec)


def _outq_kernel(x_ref, omix_ref, wout_ref, gxa_ref, wxq_ref, x1_ref, q_ref, woutb_ref, wxqb_ref):
    wout = wout_ref[...].astype(BF16)
    wxq = wxq_ref[...].astype(BF16)
    woutb_ref[...] = wout
    wxqb_ref[...] = wxq
    x1 = x_ref[...].reshape(-1, D_MODEL) + _dot(omix_ref[...], wout)
    x1_ref[...] = x1
    q_ref[...] = _query(x1, gxa_ref[...], wxq)


def _outq_sample(x3d, omix, w_out, g_xa, w_xq):
    n = x3d.shape[0] * x3d.shape[1]
    z = lambda i: (0, 0)
    return pl.pallas_call(
        _outq_kernel,
        grid=(1,),
        in_specs=[pl.BlockSpec(x3d.shape, lambda i: (0, 0, 0)), pl.BlockSpec((n, MIX_W), z),
                  pl.BlockSpec((MIX_W, D_MODEL), z), pl.BlockSpec((1, D_MODEL), z),
                  pl.BlockSpec((D_MODEL, D_MODEL), z)],
        out_specs=[pl.BlockSpec((n, D_MODEL), z), pl.BlockSpec((n, D_MODEL), z),
                   pl.BlockSpec((MIX_W, D_MODEL), z), pl.BlockSpec((D_MODEL, D_MODEL), z)],
        out_shape=[jax.ShapeDtypeStruct((n, D_MODEL), F32), jax.ShapeDtypeStruct((n, D_MODEL), F32),
                   jax.ShapeDtypeStruct((MIX_W, D_MODEL), BF16),
                   jax.ShapeDtypeStruct((D_MODEL, D_MODEL), BF16)],
        compiler_params=pltpu.CompilerParams(dimension_semantics=("arbitrary",),
                                             vmem_limit_bytes=VMEM_LIMIT_BYTES),
        name="outq_sample",
    )(x3d, omix, w_out, g_xa, w_xq)


def _post_kernel(x1_ref, ox_ref, wxo_ref, gffn_ref, wgate_ref, wup_ref, wdown_ref, gfinal_ref, y_ref):
    x2 = x1_ref[...] + _dot(ox_ref[...].astype(BF16), wxo_ref[...])
    y = _ffn_final(x2, gffn_ref[...], wgate_ref[...], wup_ref[...], wdown_ref[...], gfinal_ref[...])
    y_ref[...] = y.reshape(y_ref.shape)


def _post_sample(x1, ox, dec_len, w_xo, g_ffn, w_gate, w_up, w_down, g_final):
    n = x1.shape[0]
    d_ff = w_gate.shape[1]
    return pl.pallas_call(
        _post_kernel,
        grid=(1,),
        in_specs=[_resident((n, D_MODEL)), _resident((n, D_MODEL)), _resident((D_MODEL, D_MODEL)),
                  _resident((1, D_MODEL)), _resident((D_MODEL, d_ff)), _resident((D_MODEL, d_ff)),
                  _resident((d_ff, D_MODEL)), _resident((1, D_MODEL))],
        out_specs=pl.BlockSpec((n // dec_len, dec_len, D_MODEL), lambda i: (0, 0, 0)),
        out_shape=jax.ShapeDtypeStruct((n // dec_len, dec_len, D_MODEL), F32),
        compiler_params=pltpu.CompilerParams(dimension_semantics=("arbitrary",),
                                             vmem_limit_bytes=VMEM_LIMIT_BYTES),
        name="post_sample",
    )(x1, ox, w_xo, g_ffn, w_gate, w_up, w_down, g_final)


def kernel(x_prompt, x_sample, mem_prompt, state_ret, state_hgrn, cache_mem_k, cache_mem_v, g_mix, w_in,
           ret_gain, hg_gain, hg_lb, w_out, g_xa, g_mem, w_xq, w_xk, w_xv, w_xo, g_ffn, w_gate, w_up,
           w_down, g_final):
    depth = w_in.shape[0]
    assert depth == 1, "single-layer step"
    batch, seq, d = x_prompt.shape
    dec_batch, dec_len, _ = x_sample.shape
    assert d == D_MODEL and seq % MIX_TOKENS == 0 and seq % TAIL_TOKENS == 0
    assert SUBLANES % dec_len == 0 and dec_batch % STATE_SEQS == 0

    g_final2 = g_final.reshape(1, D_MODEL)

    proj_s, w_in_b = _proj_sample(x_sample, g_mix, w_in[0])
    omix_s, ret_s, hg_s = _state_sample(proj_s, state_ret[0], state_hgrn[0], dec_len,
                                        ret_gain, hg_gain, hg_lb)
    x1_s, q_s, w_out_b, w_xq_b = _outq_sample(x_sample, omix_s, w_out[0], g_xa, w_xq[0])

    mk, mv, mkb, mvb = _memkv(mem_prompt.reshape(batch * N_MEM, D_MODEL), g_mem, w_xk[0], w_xv[0])
    x1_p, ret_p, hg_p, ox_s, w_xo_b, w_gate_b, w_up_b, w_down_b = _mix_prompt(
        x_prompt.reshape(batch * seq, D_MODEL), batch, seq, g_mix, w_in_b, ret_gain, hg_gain, hg_lb,
        w_out_b, q_s, cache_mem_k[0], cache_mem_v[0], dec_len, w_xo[0], w_gate[0], w_up[0], w_down[0])
    y_s = _post_sample(x1_s, ox_s, dec_len, w_xo_b, g_ffn, w_gate_b, w_up_b, w_down_b, g_final2)

    y_p = _tail_prompt(x1_p, batch, seq, mkb.reshape(batch, N_MEM, D_MODEL),
                       mvb.reshape(batch, N_MEM, D_MODEL), g_xa, w_xq_b, w_xo_b, g_ffn,
                       w_gate_b, w_up_b, w_down_b, g_final2)

    kv_shape = (depth, batch, N_MEM, XA_HEADS, XA_HD)
    return (y_p.reshape(batch, seq, D_MODEL), y_s,
            ret_p[None], hg_p[None], mk.reshape(kv_shape), mv.reshape(kv_shape),
            ret_s[None], hg_s[None])
```

```python
import functools

import jax
import jax.numpy as jnp
import numpy as np
from jax import lax
from jax.experimental import pallas as pl
from jax.experimental.pallas import tpu as pltpu

D_MODEL = 1024
RET_HEADS = 4
RET_DK = 128
RET_DV = 128
RET_W = RET_HEADS * RET_DV
HG_HEADS = 4
HG_DK = 128
HG_DV = 128
HG_W = HG_HEADS * HG_DV
MIX_W = RET_W + HG_W
IN_W = 2 * RET_HEADS * RET_DK + 2 * RET_W + 2 * HG_HEADS * HG_DK + 2 * HG_W
N_MEM = 256
XA_HEADS = 4
XA_HD = D_MODEL // XA_HEADS
PAST_LEN = 16384
ROPE_BASE = 10000.0
EPS = 1e-6
LOG2E = 1.4426950408889634

_RQ, _RK, _RV, _RG = 0, 512, 1024, 1536
_HQ, _HF, _HI, _HGATE = 2048, 2560, 3072, 3584

LANE = 128
SUBLANES = 8
MIX_TOKENS = 512
RET_CHUNK_LEN = 128
HG_CHUNK_LEN = 64
HG_SUB = 16
TAIL_TOKENS = 512
MEMKV_ROWS = 512
CAST_DOWN_ROWS = 128
VMEM_LIMIT_BYTES = 56 * 1024 * 1024
TAIL_VMEM_LIMIT_BYTES = 60 * 1024 * 1024

F32 = jnp.float32
BF16 = jnp.bfloat16


def _dot(a, b):
    return jnp.dot(a, b, preferred_element_type=F32)


def _dot_nt(a, b):
    return lax.dot_general(a, b, (((1,), (1,)), ((), ())), preferred_element_type=F32)


def _dot_tn(a, b):
    return lax.dot_general(a, b, (((0,), (0,)), ((), ())), preferred_element_type=F32)


def _rms(x, g):
    ms = jnp.mean(x * x, axis=-1, keepdims=True)
    return x * lax.rsqrt(ms + EPS) * g


def _sigmoid(x):
    return 1.0 / (1.0 + jnp.exp2(x * (-LOG2E)))


def _silu(x):
    return x * _sigmoid(x)


def _head_norm(o):
    return o * lax.rsqrt(jnp.mean(o * o, axis=-1, keepdims=True) + EPS)


def _softmax_rows(s):
    p = jnp.exp(s - jnp.max(s, axis=-1, keepdims=True))
    return p / jnp.sum(p, axis=-1, keepdims=True)


def _rope(x, cos2, sin2):
    return x * cos2 + pltpu.roll(x, x.shape[-1] // 2, 1) * sin2


def _lower_bound(hglb):
    m = jnp.max(hglb, axis=0, keepdims=True)
    e = jnp.exp(hglb - m)
    return e[0:1, :] / jnp.sum(e, axis=0, keepdims=True)


def _cumsum_rows(x, period):
    row = lax.broadcasted_iota(jnp.int32, x.shape, 0) & (period - 1)
    s = 1
    while s < period:
        x = x + jnp.where(row >= s, pltpu.roll(x, s, 0), 0.0)
        s *= 2
    return x


def _resident(shape):
    zeros = (0,) * len(shape)
    return pl.BlockSpec(shape, lambda *_: zeros, pipeline_mode=pl.Buffered(1))


def _memkv_kernel(mem_ref, g_ref, wk_ref, wv_ref, k_ref, v_ref, kb_ref, vb_ref):
    m = _rms(mem_ref[...], g_ref[...]).astype(BF16)
    k = _dot(m, wk_ref[...].astype(BF16))
    v = _dot(m, wv_ref[...].astype(BF16))
    for hd in range(XA_HEADS):
        cols = slice(hd * XA_HD, (hd + 1) * XA_HD)
        k_ref[:, hd, :] = k[:, cols]
        v_ref[:, hd, :] = v[:, cols]
    kb_ref[...] = k.astype(BF16)
    vb_ref[...] = v.astype(BF16)


def _memkv(mem2d, g_mem, w_xk, w_xv):
    n = mem2d.shape[0]
    full = lambda i: (0, 0)
    row = lambda i: (i, 0)
    blk = pl.BlockSpec((MEMKV_ROWS, D_MODEL), row)
    hblk = pl.BlockSpec((MEMKV_ROWS, XA_HEADS, XA_HD), lambda i: (i, 0, 0))
    return pl.pallas_call(
        _memkv_kernel,
        grid=(n // MEMKV_ROWS,),
        in_specs=[blk, pl.BlockSpec((1, D_MODEL), full),
                  _resident((D_MODEL, D_MODEL)), _resident((D_MODEL, D_MODEL))],
        out_specs=[hblk, hblk, blk, blk],
        out_shape=[jax.ShapeDtypeStruct((n, XA_HEADS, XA_HD), F32),
                   jax.ShapeDtypeStruct((n, XA_HEADS, XA_HD), F32),
                   jax.ShapeDtypeStruct((n, D_MODEL), BF16), jax.ShapeDtypeStruct((n, D_MODEL), BF16)],
        compiler_params=pltpu.CompilerParams(dimension_semantics=("arbitrary",),
                                             vmem_limit_bytes=VMEM_LIMIT_BYTES),
        name="memkv",
    )(mem2d, g_mem, w_xk, w_xv)


def _sample_xattn_stages(dec_len, n_seqs, q_ref, k_ref, v_ref, o_ref):
    pair_rows = SUBLANES
    seqs_per_pair = pair_rows // dec_len
    n_kv = N_MEM * XA_HEADS
    q_rows = XA_HEADS * pair_rows
    own_head = (lax.broadcasted_iota(jnp.int32, (q_rows, n_kv), 1) % XA_HEADS
                == lax.broadcasted_iota(jnp.int32, (q_rows, n_kv), 0) // pair_rows)
    row8 = lax.broadcasted_iota(jnp.int32, (q_rows, XA_HD), 0) % pair_rows
    probs, outs = {}, {}

    def score(seq):
        prow = slice((seq // seqs_per_pair) * pair_rows, (seq // seqs_per_pair + 1) * pair_rows)
        q8 = q_ref[prow, :]
        qs = jnp.concatenate([q8[:, hd * XA_HD:(hd + 1) * XA_HD] for hd in range(XA_HEADS)],
                             axis=0).astype(BF16)
        kb = k_ref[seq].reshape(n_kv, XA_HD).astype(BF16)
        probs[seq] = _softmax_rows(jnp.where(own_head, _dot_nt(qs, kb), -jnp.inf)).astype(BF16)

    def attend(seq):
        p, j = divmod(seq, seqs_per_pair)
        oj = _dot(probs.pop(seq), v_ref[seq].reshape(n_kv, XA_HD).astype(BF16))
        mine = (row8 >= j * dec_len) & (row8 < (j + 1) * dec_len)
        outs[p] = jnp.where(mine, oj, outs[p]) if p in outs else oj
        if j == seqs_per_pair - 1:
            o = outs.pop(p)
            for hd in range(XA_HEADS):
                o_ref[p * pair_rows:(p + 1) * pair_rows, hd * XA_HD:(hd + 1) * XA_HD] = \
                    o[hd * pair_rows:(hd + 1) * pair_rows, :]

    return ([functools.partial(score, s) for s in range(n_seqs)]
            + [functools.partial(attend, s) for s in range(n_seqs)])


def _gate_store(o, gain, gate, out_ref, rows, cols):
    out_ref[rows, cols] = (_head_norm(o) * gain * _silu(gate)).astype(BF16)


def _mix_kernel(dec_len, x_ref, gmix_ref, win_ref, rgain_ref, hgain_ref, hglb_ref, wout_ref,
                cos_ref, sin_ref, dec_ref, qdec_ref, kdec_ref, sdec_ref,
                projs_ref, srets_ref, shgs_ref, coss_ref, sins_ref, decs_ref, qdecs_ref, kdecs_ref,
                sdecs_ref, wxo_ref, wgate_ref, wup_ref, wdown_ref,
                x1_ref, sret_ref, shg_ref, omixs_ref, nrets_ref, nhgs_ref,
                wxob_ref, wgateb_ref, wupb_ref, wdownb_ref, omix_scr):
    t = pl.program_id(1)

    @pl.when(t == 0)
    def _():
        sret_ref[...] = jnp.zeros_like(sret_ref)
        shg_ref[...] = jnp.zeros_like(shg_ref)

    for src, dst in ((wxo_ref, wxob_ref), (wgate_ref, wgateb_ref), (wup_ref, wupb_ref),
                     (wdown_ref, wdownb_ref)):
        dst[...] = src[...].astype(BF16)

    x = x_ref[...]
    h = _rms(x, gmix_ref[...]).astype(BF16)
    lower = _lower_bound(hglb_ref[...])
    side_work = _sample_state_stages(dec_len, srets_ref.shape[0], projs_ref, srets_ref, shgs_ref,
                                     rgain_ref, hgain_ref, lower, coss_ref, sins_ref, decs_ref,
                                     qdecs_ref, kdecs_ref, sdecs_ref, omixs_ref, nrets_ref, nhgs_ref)
    groups = {}
    for g0 in (_HF, _HQ, _RQ, _RK, _RV, _HI, _RG, _HGATE):
        groups[g0] = _dot(h, win_ref[:, g0:g0 + RET_W])
        if side_work:
            side_work.pop(0)()
    while side_work:
        side_work.pop(0)()

    class _Proj:
        def __getitem__(self, idx):
            rows, cols = idx
            g0 = (cols.start // RET_W) * RET_W
            return groups[g0][rows, cols.start - g0:cols.stop - g0]

    proj = _Proj()

    ret_units = []
    for c in range(MIX_TOKENS // RET_CHUNK_LEN):
        rows = slice(c * RET_CHUNK_LEN, (c + 1) * RET_CHUNK_LEN)
        cos2 = cos_ref[rows, :]
        sin2 = sin_ref[rows, :]
        for hd in range(RET_HEADS):
            q = _rope(proj[rows, _RQ + hd * LANE:_RQ + (hd + 1) * LANE], cos2, sin2)
            k = _rope(proj[rows, _RK + hd * LANE:_RK + (hd + 1) * LANE], cos2, sin2) * (RET_DK ** -0.5)
            vb = proj[rows, _RV + hd * LANE:_RV + (hd + 1) * LANE].astype(BF16)
            att = (_dot_nt(q.astype(BF16), k.astype(BF16)) * dec_ref[hd]).astype(BF16)
            kv = _dot_tn((k * kdec_ref[hd]).astype(BF16), vb)
            lhs = jnp.concatenate([att, (q * qdec_ref[hd]).astype(BF16)], axis=1)
            ret_units.append((rows, hd, lhs, vb, kv))

    n_sub = HG_CHUNK_LEN // HG_SUB
    crow = lax.broadcasted_iota(jnp.int32, (HG_CHUNK_LEN, HG_CHUNK_LEN), 0)
    ccol = lax.broadcasted_iota(jnp.int32, (HG_CHUNK_LEN, HG_CHUNK_LEN), 1)
    causal = crow >= ccol
    hg_units = []
    for c in range(MIX_TOKENS // HG_CHUNK_LEN):
        rows = slice(c * HG_CHUNK_LEN, (c + 1) * HG_CHUNK_LEN)
        f = lower + (1.0 - lower) * _sigmoid(proj[rows, _HF:_HF + HG_W])
        kk = 1.0 - f
        qq = _silu(proj[rows, _HQ:_HQ + HG_W])
        b = _cumsum_rows(jnp.log2(f), HG_CHUNK_LEN)
        b_last = b[HG_CHUNK_LEN - 1:HG_CHUNK_LEN, :]
        q_inter = (qq * jnp.exp2(b)).astype(BF16)
        k_upd = (kk * jnp.exp2(b_last - b)).astype(BF16)
        d_last = jnp.exp2(b_last)
        for hd in range(HG_HEADS):
            cols = slice(hd * LANE, (hd + 1) * LANE)
            bh = b[:, cols]
            qh = qq[:, cols]
            kh = kk[:, cols]
            vb = proj[rows, _HI + hd * LANE:_HI + (hd + 1) * LANE].astype(BF16)
            q_parts, k_parts = [], []
            for j in range(n_sub):
                lo, hi = j * HG_SUB, (j + 1) * HG_SUB
                ref = bh[lo + HG_SUB // 2 - 1:lo + HG_SUB // 2, :]
                qt = qh[lo:, :] * jnp.exp2(bh[lo:, :] - ref)
                kt = kh[lo:hi, :] * jnp.exp2(ref - bh[lo:hi, :])
                if lo:
                    qt = jnp.concatenate([jnp.zeros((lo, LANE), F32), qt], axis=0)
                    kt = jnp.concatenate([jnp.zeros((lo, LANE), F32), kt], axis=0)
                if hi < HG_CHUNK_LEN:
                    kt = jnp.concatenate([kt, jnp.zeros((HG_CHUNK_LEN - hi, LANE), F32)], axis=0)
                q_parts.append(qt.astype(BF16))
                k_parts.append(kt.astype(BF16))
            qcat = jnp.concatenate(q_parts, axis=1)
            kcat = jnp.concatenate(k_parts, axis=1)
            att = jnp.where(causal, _dot_nt(qcat, kcat), 0.0).astype(BF16)
            kv = _dot_tn(k_upd[:, cols], vb)
            lhs = jnp.concatenate([q_inter[:, cols], att], axis=1)
            hg_units.append((rows, hd, lhs, vb, kv, d_last[:, cols]))

    for hd in range(RET_HEADS):
        cols = slice(hd * LANE, (hd + 1) * LANE)
        s = sret_ref[0, hd]
        for rows, uh, lhs, vb, kv in ret_units:
            if uh != hd:
                continue
            o = _dot(lhs, jnp.concatenate([vb, s.astype(BF16)], axis=0))
            s = sdec_ref[hd] * s + kv
            gate = proj[rows, _RG + hd * LANE:_RG + (hd + 1) * LANE]
            _gate_store(o, rgain_ref[:, cols], gate, omix_scr, rows, cols)
        sret_ref[0, hd] = s

    for hd in range(HG_HEADS):
        cols = slice(hd * LANE, (hd + 1) * LANE)
        mine = [u for u in hg_units if u[1] == hd]
        dl = jnp.concatenate([u[5] for u in mine] + [jnp.zeros((LANE - len(mine), LANE), F32)], axis=0).T
        s = shg_ref[0, hd]
        for i, (rows, _, lhs, vb, kv, _) in enumerate(mine):
            o = _dot(lhs, jnp.concatenate([s.astype(BF16), vb], axis=0))
            s = s * dl[:, i:i + 1] + kv
            gate = proj[rows, _HGATE + hd * LANE:_HGATE + (hd + 1) * LANE]
            _gate_store(o, hgain_ref[:, cols], gate, omix_scr, rows,
                        slice(RET_W + hd * LANE, RET_W + (hd + 1) * LANE))
        shg_ref[0, hd] = s

    x1_ref[...] = x + _dot(omix_scr[...], wout_ref[...])


def _ret_tables(length, period):
    log_g = np.log(1.0 - 2.0 ** (-5.0 - np.arange(RET_HEADS, dtype=np.float64)))
    idx = np.arange(length)
    pos = idx % period
    rel = (idx[:, None] - idx[None, :]).astype(np.float64)
    same = (idx[:, None] // period) == (idx[None, :] // period)
    valid = (rel >= 0) & same
    dec = np.where(valid[None], np.exp(log_g[:, None, None] * np.where(valid, rel, 0.0)[None]), 0.0)
    qdec = np.exp(log_g[:, None] * (pos + 1.0))[:, :, None] * np.ones((1, 1, LANE))
    kdec = np.exp(log_g[:, None] * (period - 1.0 - pos))[:, :, None] * np.ones((1, 1, LANE))
    sdec = np.exp(log_g * period)[:, None, None] * np.ones((1, 1, LANE))
    as32 = lambda a: jnp.asarray(a, dtype=F32)
    return as32(dec), as32(qdec), as32(kdec), as32(sdec)


def _rope_tables(pos):
    half = RET_DK // 2
    inv_freq = ROPE_BASE ** (-np.arange(half, dtype=np.float64) / half)
    ang = np.asarray(pos, dtype=np.float64)[:, None] * inv_freq[None, :]
    cos, sin = np.cos(ang), np.sin(ang)
    return (jnp.asarray(np.concatenate([cos, cos], axis=-1), dtype=F32),
            jnp.asarray(np.concatenate([-sin, sin], axis=-1), dtype=F32))


def _mix_prompt(x2d, batch, seq, g_mix, w_in, ret_gain, hg_gain, hg_lb, w_out,
                proj_s, state_ret, state_hgrn, dec_len, w_xo, w_gate, w_up, w_down):
    nt = seq // MIX_TOKENS
    n_steps = batch * nt
    d_ff = w_gate.shape[1]
    w_rows = D_MODEL // n_steps
    down_blocks = d_ff // CAST_DOWN_ROWS
    assert w_rows * n_steps == D_MODEL and w_rows % 16 == 0
    assert down_blocks * CAST_DOWN_ROWS == d_ff and down_blocks <= n_steps
    step = lambda b, t: b * nt + t
    sq_rows = pl.BlockSpec((w_rows, D_MODEL), lambda b, t: (step(b, t), 0))
    ff_rows = pl.BlockSpec((w_rows, d_ff), lambda b, t: (step(b, t), 0))
    down_rows = pl.BlockSpec((CAST_DOWN_ROWS, D_MODEL),
                             lambda b, t: (jnp.minimum(step(b, t), down_blocks - 1), 0))
    n_seq = state_ret.shape[0]
    seqs_per_step = n_seq // (batch * nt)
    s_rows = seqs_per_step * dec_len
    assert seqs_per_step * batch * nt == n_seq and s_rows % 16 == 0
    sproj = pl.BlockSpec((s_rows, IN_W), lambda b, t: (b * nt + t, 0))
    somix = pl.BlockSpec((s_rows, MIX_W), lambda b, t: (b * nt + t, 0))
    sstate = pl.BlockSpec((seqs_per_step, RET_HEADS, RET_DK, RET_DV), lambda b, t: (b * nt + t, 0, 0, 0))
    cos2, sin2 = _rope_tables(np.arange(seq))
    dec, qdec, kdec, sdec = _ret_tables(RET_CHUNK_LEN, RET_CHUNK_LEN)
    cos_s, sin_s = _rope_tables(np.tile(np.arange(dec_len) + PAST_LEN, seqs_per_step))
    dec_s, qdec_s, kdec_s, sdec_s = _ret_tables(s_rows, dec_len)
    c2 = lambda b, t: (0, 0)
    c3 = lambda b, t: (0, 0, 0)
    tok = pl.BlockSpec((MIX_TOKENS, D_MODEL), lambda b, t: (b * nt + t, 0))
    state = pl.BlockSpec((1, RET_HEADS, RET_DK, RET_DV), lambda b, t: (b, 0, 0, 0))
    return pl.pallas_call(
        functools.partial(_mix_kernel, dec_len),
        grid=(batch, nt),
        in_specs=[tok, pl.BlockSpec((1, D_MODEL), c2), _resident((D_MODEL, IN_W)),
                  pl.BlockSpec((1, RET_W), c2), pl.BlockSpec((1, HG_W), c2),
                  pl.BlockSpec(hg_lb.shape, c2), _resident((MIX_W, D_MODEL)),
                  pl.BlockSpec((MIX_TOKENS, LANE), lambda b, t: (t, 0)),
                  pl.BlockSpec((MIX_TOKENS, LANE), lambda b, t: (t, 0)),
                  pl.BlockSpec(dec.shape, c3), pl.BlockSpec(qdec.shape, c3),
                  pl.BlockSpec(kdec.shape, c3), pl.BlockSpec(sdec.shape, c3),
                  sproj, sstate, sstate, pl.BlockSpec((s_rows, LANE), c2), pl.BlockSpec((s_rows, LANE), c2),
                  pl.BlockSpec(dec_s.shape, c3), pl.BlockSpec(qdec_s.shape, c3),
                  pl.BlockSpec(kdec_s.shape, c3), pl.BlockSpec(sdec_s.shape, c3),
                  sq_rows, ff_rows, ff_rows, down_rows],
        out_specs=[tok, state, state, somix, sstate, sstate, sq_rows, ff_rows, ff_rows, down_rows],
        out_shape=[jax.ShapeDtypeStruct(x2d.shape, F32),
                   jax.ShapeDtypeStruct((batch, RET_HEADS, RET_DK, RET_DV), F32),
                   jax.ShapeDtypeStruct((batch, HG_HEADS, HG_DK, HG_DV), F32),
                   jax.ShapeDtypeStruct((proj_s.shape[0], MIX_W), BF16),
                   jax.ShapeDtypeStruct(state_ret.shape, F32),
                   jax.ShapeDtypeStruct(state_hgrn.shape, F32),
                   jax.ShapeDtypeStruct(w_xo.shape, BF16), jax.ShapeDtypeStruct(w_gate.shape, BF16),
                   jax.ShapeDtypeStruct(w_up.shape, BF16), jax.ShapeDtypeStruct(w_down.shape, BF16)],
        scratch_shapes=[pltpu.VMEM((MIX_TOKENS, MIX_W), BF16)],
        compiler_params=pltpu.CompilerParams(dimension_semantics=("arbitrary", "arbitrary"),
                                             vmem_limit_bytes=VMEM_LIMIT_BYTES),
        name="mix_prompt",
    )(x2d, g_mix, w_in, ret_gain, hg_gain, hg_lb, w_out, cos2, sin2, dec, qdec, kdec, sdec,
      proj_s, state_ret, state_hgrn, cos_s, sin_s, dec_s, qdec_s, kdec_s, sdec_s,
      w_xo, w_gate, w_up, w_down)


def _inv_rms(x):
    return lax.rsqrt(jnp.mean(x * x, axis=-1, keepdims=True) + EPS)


def _query(x1, gxa, wxq):
    return _dot((x1 * gxa).astype(BF16), wxq) * (_inv_rms(x1) * (XA_HD ** -0.5))


def _ffn_final(x1, ox, wxo, gffn, wgate, wup, wdown, gfinal, side_work=()):
    side_work = list(side_work)
    pop = lambda: side_work.pop(0)() if side_work else None
    x2 = x1 + _dot(ox, wxo)
    pop()
    hb = (x2 * gffn).astype(BF16)
    r = _inv_rms(x2)
    g = _dot(hb, wgate)
    pop()
    u = _dot(hb, wup)
    pop()
    a = (_silu(g * r) * (u * r)).astype(BF16)
    half = x2.shape[0] // 2
    return jnp.concatenate([_rms(x2[rows] + _dot(a[rows], wdown), gfinal)
                            for rows in (slice(0, half), slice(half, None))], axis=0)


def _tail_kernel(dec_len, x1_ref, mk_ref, mv_ref, gxa_ref, wxq_ref, wxo_ref, gffn_ref, wgate_ref,
                 wup_ref, wdown_ref, gfinal_ref, qs_ref, ck_ref, cv_ref, y_ref, oxs_ref, ox_scr):
    side_work = _sample_xattn_stages(dec_len, ck_ref.shape[0], qs_ref, ck_ref, cv_ref, oxs_ref)
    x1 = x1_ref[...]
    q = _query(x1, gxa_ref[...], wxq_ref[...]).astype(BF16)
    side_work.pop(0)()
    heads = [slice(hd * XA_HD, (hd + 1) * XA_HD) for hd in range(XA_HEADS)]
    scores = []
    for cols in heads:
        scores.append(_dot_nt(q[:, cols], mk_ref[0, :, cols]))
        if len(side_work) > 4:
            side_work.pop(0)()
    probs = [_softmax_rows(s).astype(BF16) for s in scores]
    for cols, p in zip(heads, probs):
        ox_scr[:, cols] = _dot(p, mv_ref[0, :, cols]).astype(BF16)
    while len(side_work) > 3:
        side_work.pop(0)()
    y_ref[...] = _ffn_final(x1, ox_scr[...], wxo_ref[...], gffn_ref[...], wgate_ref[...],
                            wup_ref[...], wdown_ref[...], gfinal_ref[...], side_work)


def _tail_prompt(x1, batch, seq, mkb, mvb, g_xa, w_xq, w_xo, g_ffn, w_gate, w_up, w_down, g_final,
                 q_s, cache_k, cache_v, dec_len):
    nt = seq // TAIL_TOKENS
    d_ff = w_gate.shape[1]
    n_seq = cache_k.shape[0]
    seqs_per_step = n_seq // (batch * nt)
    assert seqs_per_step * batch * nt == n_seq and (seqs_per_step * dec_len) % SUBLANES == 0
    srows = pl.BlockSpec((seqs_per_step * dec_len, D_MODEL), lambda b, t: (b * nt + t, 0))
    cblk = pl.BlockSpec((seqs_per_step, N_MEM, XA_HEADS, XA_HD), lambda b, t: (b * nt + t, 0, 0, 0))
    tok = pl.BlockSpec((TAIL_TOKENS, D_MODEL), lambda b, t: (b * nt + t, 0))
    mem = pl.BlockSpec((1, N_MEM, D_MODEL), lambda b, t: (b, 0, 0))
    return pl.pallas_call(
        functools.partial(_tail_kernel, dec_len),
        grid=(batch, nt),
        in_specs=[tok, mem, mem, _resident((1, D_MODEL)), _resident((D_MODEL, D_MODEL)),
                  _resident((D_MODEL, D_MODEL)), _resident((1, D_MODEL)),
                  _resident((D_MODEL, d_ff)), _resident((D_MODEL, d_ff)),
                  _resident((d_ff, D_MODEL)), _resident((1, D_MODEL)), srows, cblk, cblk],
        out_specs=[tok, srows],
        out_shape=[jax.ShapeDtypeStruct(x1.shape, F32), jax.ShapeDtypeStruct(q_s.shape, F32)],
        scratch_shapes=[pltpu.VMEM((TAIL_TOKENS, D_MODEL), BF16)],
        compiler_params=pltpu.CompilerParams(dimension_semantics=("arbitrary", "arbitrary"),
                                             vmem_limit_bytes=TAIL_VMEM_LIMIT_BYTES),
        name="tail_prompt",
    )(x1, mkb, mvb, g_xa, w_xq, w_xo, g_ffn, w_gate, w_up, w_down, g_final, q_s, cache_k, cache_v)


def _proj_kernel(x_ref, g_ref, w_ref, wout_ref, o_ref, wb_ref, woutb_ref):
    wb = w_ref[...].astype(BF16)
    wb_ref[...] = wb
    woutb_ref[...] = wout_ref[...].astype(BF16)
    x = x_ref[...].reshape(-1, D_MODEL)
    o_ref[...] = _dot(_rms(x, g_ref[...]).astype(BF16), wb)


def _proj_sample(x3d, g_mix, w_in, w_out):
    n = x3d.shape[0] * x3d.shape[1]
    nb = IN_W // D_MODEL
    wblk = pl.BlockSpec((D_MODEL, D_MODEL), lambda j: (0, j))
    oblk = pl.BlockSpec((MIX_W // nb, D_MODEL), lambda j: (j, 0))
    return pl.pallas_call(
        _proj_kernel,
        grid=(nb,),
        in_specs=[pl.BlockSpec(x3d.shape, lambda j: (0, 0, 0)),
                  pl.BlockSpec((1, D_MODEL), lambda j: (0, 0)), wblk, oblk],
        out_specs=[pl.BlockSpec((n, D_MODEL), lambda j: (0, j)), wblk, oblk],
        out_shape=[jax.ShapeDtypeStruct((n, IN_W), F32), jax.ShapeDtypeStruct(w_in.shape, BF16),
                   jax.ShapeDtypeStruct(w_out.shape, BF16)],
        compiler_params=pltpu.CompilerParams(dimension_semantics=("arbitrary",),
                                             vmem_limit_bytes=VMEM_LIMIT_BYTES),
        name="proj_sample",
    )(x3d, g_mix, w_in, w_out)


def _sample_state_stages(dec_len, n_seqs, proj_ref, sret_ref, shg_ref, rgain_ref, hgain_ref, lower,
                         cos_ref, sin_ref, dec_ref, qdec_ref, kdec_ref, sdec_ref,
                         omix_ref, nret_ref, nhg_ref):
    n_rows = n_seqs * dec_len
    pair_rows = SUBLANES
    seqs_per_pair = pair_rows // dec_len
    row8 = lax.broadcasted_iota(jnp.int32, (pair_rows, LANE), 0)
    rown = lax.broadcasted_iota(jnp.int32, (n_rows, n_rows), 0)
    coln = lax.broadcasted_iota(jnp.int32, (n_rows, n_rows), 1)
    same_seq_causal = (rown >= coln) & ((rown // dec_len) == (coln // dec_len))

    def per_sequence(q_all, kd_all, vb_all, state_ref, new_ref, hd, scale_of):
        outs = []
        for p in range(n_rows // pair_rows):
            prow = slice(p * pair_rows, (p + 1) * pair_rows)
            q8, kd8, v8 = q_all[prow], kd_all[prow], vb_all[prow]
            acc = jnp.zeros((pair_rows, LANE), F32)
            for j in range(seqs_per_pair):
                seq = p * seqs_per_pair + j
                mine = (row8 >= j * dec_len) & (row8 < (j + 1) * dec_len)
                s0 = state_ref[seq, hd]
                acc = jnp.where(mine, _dot(q8, s0.astype(BF16)), acc)
                kz = jnp.where(mine, kd8, jnp.zeros_like(kd8))
                new_ref[seq, hd] = scale_of(seq) * s0 + _dot_tn(kz, v8)
            outs.append(acc)
        return jnp.concatenate(outs, axis=0)

    def retention_head(hd):
        cols = slice(hd * LANE, (hd + 1) * LANE)
        cos2, sin2 = cos_ref[...], sin_ref[...]
        q = _rope(proj_ref[:, _RQ + hd * LANE:_RQ + (hd + 1) * LANE], cos2, sin2)
        k = _rope(proj_ref[:, _RK + hd * LANE:_RK + (hd + 1) * LANE], cos2, sin2) * (RET_DK ** -0.5)
        vb = proj_ref[:, _RV + hd * LANE:_RV + (hd + 1) * LANE].astype(BF16)
        gate = proj_ref[:, _RG + hd * LANE:_RG + (hd + 1) * LANE]
        qb = q.astype(BF16)
        att = _dot_nt(qb, k.astype(BF16)) * dec_ref[hd]
        kd = (k * kdec_ref[hd]).astype(BF16)
        sdec = sdec_ref[hd]
        qs = per_sequence(qb, kd, vb, sret_ref, nret_ref, hd, lambda seq: sdec)
        o = _dot(att.astype(BF16), vb) + qdec_ref[hd] * qs
        _gate_store(o, rgain_ref[:, cols], gate, omix_ref, slice(None), cols)

    shared = {}

    def hgrn_prepare():
        f = lower + (1.0 - lower) * _sigmoid(proj_ref[:, _HF:_HF + HG_W])
        kk = 1.0 - f
        qq = _silu(proj_ref[:, _HQ:_HQ + HG_W])
        b = _cumsum_rows(jnp.log2(f), dec_len)
        pos = lax.broadcasted_iota(jnp.int32, b.shape, 0) & (dec_len - 1)

        def spread(row_in_seq):
            picked = jnp.where(pos == row_in_seq, b, 0.0)
            out = picked
            for s in range(1, dec_len):
                out = out + jnp.where(pos == (row_in_seq + s), pltpu.roll(picked, s, 0), 0.0)
                out = out + jnp.where(pos == (row_in_seq - s), pltpu.roll(picked, n_rows - s, 0), 0.0)
            return out

        ref = spread(dec_len // 2 - 1)
        b_last = spread(dec_len - 1)
        shared.update(q_intra=(qq * jnp.exp2(b - ref)).astype(BF16),
                      k_intra=(kk * jnp.exp2(ref - b)).astype(BF16),
                      q_inter=(qq * jnp.exp2(b)).astype(BF16),
                      k_upd=(kk * jnp.exp2(b_last - b)).astype(BF16),
                      d_last=jnp.exp2(b_last))

    def hgrn_head(hd):
        cols = slice(hd * LANE, (hd + 1) * LANE)
        vb = proj_ref[:, _HI + hd * LANE:_HI + (hd + 1) * LANE].astype(BF16)
        gate = proj_ref[:, _HGATE + hd * LANE:_HGATE + (hd + 1) * LANE]
        att = jnp.where(same_seq_causal,
                        _dot_nt(shared["q_intra"][:, cols], shared["k_intra"][:, cols]), 0.0)
        dpad = jnp.concatenate([shared["d_last"][:, cols],
                                jnp.zeros((LANE - n_rows, LANE), F32)], axis=0)
        dcol = dpad.T
        scale_of = lambda seq: dcol[:, seq * dec_len:seq * dec_len + 1]
        qs = per_sequence(shared["q_inter"][:, cols], shared["k_upd"][:, cols], vb, shg_ref, nhg_ref,
                          hd, scale_of)
        o = _dot(att.astype(BF16), vb) + qs
        _gate_store(o, hgain_ref[:, cols], gate, omix_ref, slice(None),
                    slice(RET_W + hd * LANE, RET_W + (hd + 1) * LANE))

    return ([functools.partial(retention_head, hd) for hd in range(RET_HEADS)] + [hgrn_prepare]
            + [functools.partial(hgrn_head, hd) for hd in range(HG_HEADS)])


def _outq_kernel(x_ref, omix_ref, wout_ref, gxa_ref, wxq_ref, x1_ref, q_ref, wxqb_ref):
    wxq = wxq_ref[...].astype(BF16)
    wxqb_ref[...] = wxq
    x1 = x_ref[...].reshape(-1, D_MODEL) + _dot(omix_ref[...], wout_ref[...])
    x1_ref[...] = x1
    q_ref[...] = _query(x1, gxa_ref[...], wxq)


def _outq_sample(x3d, omix, w_out, g_xa, w_xq):
    n = x3d.shape[0] * x3d.shape[1]
    z = lambda i: (0, 0)
    return pl.pallas_call(
        _outq_kernel,
        grid=(1,),
        in_specs=[pl.BlockSpec(x3d.shape, lambda i: (0, 0, 0)), pl.BlockSpec((n, MIX_W), z),
                  pl.BlockSpec((MIX_W, D_MODEL), z), pl.BlockSpec((1, D_MODEL), z),
                  pl.BlockSpec((D_MODEL, D_MODEL), z)],
        out_specs=[pl.BlockSpec((n, D_MODEL), z), pl.BlockSpec((n, D_MODEL), z),
                   pl.BlockSpec((D_MODEL, D_MODEL), z)],
        out_shape=[jax.ShapeDtypeStruct((n, D_MODEL), F32), jax.ShapeDtypeStruct((n, D_MODEL), F32),
                   jax.ShapeDtypeStruct((D_MODEL, D_MODEL), BF16)],
        compiler_params=pltpu.CompilerParams(dimension_semantics=("arbitrary",),
                                             vmem_limit_bytes=VMEM_LIMIT_BYTES),
        name="outq_sample",
    )(x3d, omix, w_out, g_xa, w_xq)


def _post_kernel(x1_ref, ox_ref, wxo_ref, gffn_ref, wgate_ref, wup_ref, wdown_ref, gfinal_ref, y_ref):
    y = _ffn_final(x1_ref[...], ox_ref[...].astype(BF16), wxo_ref[...], gffn_ref[...],
                   wgate_ref[...], wup_ref[...], wdown_ref[...], gfinal_ref[...])
    y_ref[...] = y.reshape(y_ref.shape)


def _post_sample(x1, ox, dec_len, w_xo, g_ffn, w_gate, w_up, w_down, g_final):
    n = x1.shape[0]
    d_ff = w_gate.shape[1]
    return pl.pallas_call(
        _post_kernel,
        grid=(1,),
        in_specs=[_resident((n, D_MODEL)), _resident((n, D_MODEL)), _resident((D_MODEL, D_MODEL)),
                  _resident((1, D_MODEL)), _resident((D_MODEL, d_ff)), _resident((D_MODEL, d_ff)),
                  _resident((d_ff, D_MODEL)), _resident((1, D_MODEL))],
        out_specs=pl.BlockSpec((n // dec_len, dec_len, D_MODEL), lambda i: (0, 0, 0)),
        out_shape=jax.ShapeDtypeStruct((n // dec_len, dec_len, D_MODEL), F32),
        compiler_params=pltpu.CompilerParams(dimension_semantics=("arbitrary",),
                                             vmem_limit_bytes=VMEM_LIMIT_BYTES),
        name="post_sample",
    )(x1, ox, w_xo, g_ffn, w_gate, w_up, w_down, g_final)


def kernel(x_prompt, x_sample, mem_prompt, state_ret, state_hgrn, cache_mem_k, cache_mem_v, g_mix, w_in,
           ret_gain, hg_gain, hg_lb, w_out, g_xa, g_mem, w_xq, w_xk, w_xv, w_xo, g_ffn, w_gate, w_up,
           w_down, g_final):
    depth = w_in.shape[0]
    assert depth == 1, "single-layer step"
    batch, seq, d = x_prompt.shape
    dec_batch, dec_len, _ = x_sample.shape
    assert d == D_MODEL and seq % MIX_TOKENS == 0 and seq % TAIL_TOKENS == 0
    assert SUBLANES % dec_len == 0

    g_final2 = g_final.reshape(1, D_MODEL)

    proj_s, w_in_b, w_out_b = _proj_sample(x_sample, g_mix, w_in[0], w_out[0])
    mk, mv, mkb, mvb = _memkv(mem_prompt.reshape(batch * N_MEM, D_MODEL), g_mem, w_xk[0], w_xv[0])

    x1_p, ret_p, hg_p, omix_s, ret_s, hg_s, w_xo_b, w_gate_b, w_up_b, w_down_b = _mix_prompt(
        x_prompt.reshape(batch * seq, D_MODEL), batch, seq, g_mix, w_in_b, ret_gain, hg_gain, hg_lb,
        w_out_b, proj_s, state_ret[0], state_hgrn[0], dec_len, w_xo[0], w_gate[0], w_up[0], w_down[0])
    x1_s, q_s, w_xq_b = _outq_sample(x_sample, omix_s, w_out_b, g_xa, w_xq[0])

    y_p, ox_s = _tail_prompt(x1_p, batch, seq, mkb.reshape(batch, N_MEM, D_MODEL),
                             mvb.reshape(batch, N_MEM, D_MODEL), g_xa, w_xq_b, w_xo_b, g_ffn,
                             w_gate_b, w_up_b, w_down_b, g_final2,
                             q_s, cache_mem_k[0], cache_mem_v[0], dec_len)
    y_s = _post_sample(x1_s, ox_s, dec_len, w_xo_b, g_ffn, w_gate_b, w_up_b, w_down_b, g_final2)

    kv_shape = (depth, batch, N_MEM, XA_HEADS, XA_HD)
    return (y_p.reshape(batch, seq, D_MODEL), y_s,
            ret_p[None], hg_p[None], mk.reshape(kv_shape), mv.reshape(kv_shape),
            ret_s[None], hg_s[None])
```

```python
import functools

import jax
import jax.numpy as jnp
import numpy as np
from jax import lax
from jax.experimental import pallas as pl
from jax.experimental.pallas import tpu as pltpu

D_MODEL = 1024
RET_HEADS = 4
RET_DK = 128
RET_DV = 128
RET_W = RET_HEADS * RET_DV
HG_HEADS = 4
HG_DK = 128
HG_DV = 128
HG_W = HG_HEADS * HG_DV
MIX_W = RET_W + HG_W
IN_W = 2 * RET_HEADS * RET_DK + 2 * RET_W + 2 * HG_HEADS * HG_DK + 2 * HG_W
N_MEM = 256
XA_HEADS = 4
XA_HD = D_MODEL // XA_HEADS
PAST_LEN = 16384
ROPE_BASE = 10000.0
EPS = 1e-6
LOG2E = 1.4426950408889634

_RQ, _RK, _RV, _RG = 0, 512, 1024, 1536
_HQ, _HF, _HI, _HGATE = 2048, 2560, 3072, 3584

LANE = 128
SUBLANES = 8
MIX_TOKENS = 512
RET_CHUNK_LEN = 128
HG_CHUNK_LEN = 64
HG_SUB = 16
HG_LOG2_F_MIN = -14.0
TAIL_TOKENS = 512
MEMKV_ROWS = 512
CAST_DOWN_ROWS = 128
VMEM_LIMIT_BYTES = 56 * 1024 * 1024
TAIL_VMEM_LIMIT_BYTES = 60 * 1024 * 1024

F32 = jnp.float32
BF16 = jnp.bfloat16


def _dot(a, b):
    return jnp.dot(a, b, preferred_element_type=F32)


def _dot_nt(a, b):
    return lax.dot_general(a, b, (((1,), (1,)), ((), ())), preferred_element_type=F32)


def _dot_tn(a, b):
    return lax.dot_general(a, b, (((0,), (0,)), ((), ())), preferred_element_type=F32)


def _rms(x, g):
    ms = jnp.mean(x * x, axis=-1, keepdims=True)
    return x * lax.rsqrt(ms + EPS) * g


def _sigmoid(x):
    return 1.0 / (1.0 + jnp.exp2(x * (-LOG2E)))


def _silu(x):
    return x * _sigmoid(x)


def _head_norm(o):
    return o * lax.rsqrt(jnp.mean(o * o, axis=-1, keepdims=True) + EPS)


def _softmax_rows(s):
    p = jnp.exp(s - jnp.max(s, axis=-1, keepdims=True))
    return p / jnp.sum(p, axis=-1, keepdims=True)


def _rope(x, cos2, sin2):
    return x * cos2 + pltpu.roll(x, x.shape[-1] // 2, 1) * sin2


def _lower_bound(hglb):
    m = jnp.max(hglb, axis=0, keepdims=True)
    e = jnp.exp(hglb - m)
    return e[0:1, :] / jnp.sum(e, axis=0, keepdims=True)


def _cumsum_rows(x, period):
    row = lax.broadcasted_iota(jnp.int32, x.shape, 0) & (period - 1)
    s = 1
    while s < period:
        x = x + jnp.where(row >= s, pltpu.roll(x, s, 0), 0.0)
        s *= 2
    return x


def _resident(shape):
    zeros = (0,) * len(shape)
    return pl.BlockSpec(shape, lambda *_: zeros, pipeline_mode=pl.Buffered(1))


def _memkv_kernel(mem_ref, g_ref, wk_ref, wv_ref, k_ref, v_ref, kb_ref, vb_ref):
    m = _rms(mem_ref[...], g_ref[...]).astype(BF16)
    k = _dot(m, wk_ref[...].astype(BF16))
    v = _dot(m, wv_ref[...].astype(BF16))
    for hd in range(XA_HEADS):
        cols = slice(hd * XA_HD, (hd + 1) * XA_HD)
        k_ref[:, hd, :] = k[:, cols]
        v_ref[:, hd, :] = v[:, cols]
    kb_ref[...] = k.astype(BF16)
    vb_ref[...] = v.astype(BF16)


def _memkv(mem2d, g_mem, w_xk, w_xv):
    n = mem2d.shape[0]
    full = lambda i: (0, 0)
    row = lambda i: (i, 0)
    blk = pl.BlockSpec((MEMKV_ROWS, D_MODEL), row)
    hblk = pl.BlockSpec((MEMKV_ROWS, XA_HEADS, XA_HD), lambda i: (i, 0, 0))
    return pl.pallas_call(
        _memkv_kernel,
        grid=(n // MEMKV_ROWS,),
        in_specs=[blk, pl.BlockSpec((1, D_MODEL), full),
                  _resident((D_MODEL, D_MODEL)), _resident((D_MODEL, D_MODEL))],
        out_specs=[hblk, hblk, blk, blk],
        out_shape=[jax.ShapeDtypeStruct((n, XA_HEADS, XA_HD), F32),
                   jax.ShapeDtypeStruct((n, XA_HEADS, XA_HD), F32),
                   jax.ShapeDtypeStruct((n, D_MODEL), BF16), jax.ShapeDtypeStruct((n, D_MODEL), BF16)],
        compiler_params=pltpu.CompilerParams(dimension_semantics=("arbitrary",),
                                             vmem_limit_bytes=VMEM_LIMIT_BYTES),
        name="memkv",
    )(mem2d, g_mem, w_xk, w_xv)


def _sample_xattn_stages(dec_len, n_seqs, q_ref, k_ref, v_ref, o_ref):
    pair_rows = SUBLANES
    seqs_per_pair = pair_rows // dec_len
    n_kv = N_MEM * XA_HEADS
    q_rows = XA_HEADS * pair_rows
    own_head = (lax.broadcasted_iota(jnp.int32, (q_rows, n_kv), 1) % XA_HEADS
                == lax.broadcasted_iota(jnp.int32, (q_rows, n_kv), 0) // pair_rows)
    row8 = lax.broadcasted_iota(jnp.int32, (q_rows, XA_HD), 0) % pair_rows
    probs, outs = {}, {}

    def score(seq):
        prow = slice((seq // seqs_per_pair) * pair_rows, (seq // seqs_per_pair + 1) * pair_rows)
        q8 = q_ref[prow, :]
        qs = jnp.concatenate([q8[:, hd * XA_HD:(hd + 1) * XA_HD] for hd in range(XA_HEADS)],
                             axis=0).astype(BF16)
        kb = k_ref[seq].reshape(n_kv, XA_HD).astype(BF16)
        probs[seq] = _softmax_rows(jnp.where(own_head, _dot_nt(qs, kb), -jnp.inf)).astype(BF16)

    def attend(seq):
        p, j = divmod(seq, seqs_per_pair)
        oj = _dot(probs.pop(seq), v_ref[seq].reshape(n_kv, XA_HD).astype(BF16))
        mine = (row8 >= j * dec_len) & (row8 < (j + 1) * dec_len)
        outs[p] = jnp.where(mine, oj, outs[p]) if p in outs else oj
        if j == seqs_per_pair - 1:
            o = outs.pop(p)
            for hd in range(XA_HEADS):
                o_ref[p * pair_rows:(p + 1) * pair_rows, hd * XA_HD:(hd + 1) * XA_HD] = \
                    o[hd * pair_rows:(hd + 1) * pair_rows, :]

    return ([functools.partial(score, s) for s in range(n_seqs)]
            + [functools.partial(attend, s) for s in range(n_seqs)])


def _gate_store(o, gain, gate, out_ref, rows, cols):
    out_ref[rows, cols] = (_head_norm(o) * gain * _silu(gate)).astype(BF16)


def _mix_kernel(dec_len, x_ref, gmix_ref, win_ref, rgain_ref, hgain_ref, hglb_ref, wout_ref,
                cos_ref, sin_ref, dec_ref, qdec_ref, kdec_ref, sdec_ref,
                projs_ref, srets_ref, shgs_ref, coss_ref, sins_ref, decs_ref, qdecs_ref, kdecs_ref,
                sdecs_ref, wxo_ref, wgate_ref, wup_ref, wdown_ref,
                x1_ref, sret_ref, shg_ref, omixs_ref, nrets_ref, nhgs_ref,
                wxob_ref, wgateb_ref, wupb_ref, wdownb_ref, omix_scr):
    t = pl.program_id(1)

    @pl.when(t == 0)
    def _():
        sret_ref[...] = jnp.zeros_like(sret_ref)
        shg_ref[...] = jnp.zeros_like(shg_ref)

    for src, dst in ((wxo_ref, wxob_ref), (wgate_ref, wgateb_ref), (wup_ref, wupb_ref),
                     (wdown_ref, wdownb_ref)):
        dst[...] = src[...].astype(BF16)

    x = x_ref[...]
    h = _rms(x, gmix_ref[...]).astype(BF16)
    lower = _lower_bound(hglb_ref[...])
    side_work = _sample_state_stages(dec_len, srets_ref.shape[0], projs_ref, srets_ref, shgs_ref,
                                     rgain_ref, hgain_ref, lower, coss_ref, sins_ref, decs_ref,
                                     qdecs_ref, kdecs_ref, sdecs_ref, omixs_ref, nrets_ref, nhgs_ref)
    groups = {}
    for g0 in (_HF, _HQ, _RQ, _RK, _RV, _HI, _RG, _HGATE):
        groups[g0] = _dot(h, win_ref[:, g0:g0 + RET_W])
        if side_work:
            side_work.pop(0)()
    while side_work:
        side_work.pop(0)()

    class _Proj:
        def __getitem__(self, idx):
            rows, cols = idx
            g0 = (cols.start // RET_W) * RET_W
            return groups[g0][rows, cols.start - g0:cols.stop - g0]

    proj = _Proj()

    ret_units = []
    for c in range(MIX_TOKENS // RET_CHUNK_LEN):
        rows = slice(c * RET_CHUNK_LEN, (c + 1) * RET_CHUNK_LEN)
        cos2 = cos_ref[rows, :]
        sin2 = sin_ref[rows, :]
        for hd in range(RET_HEADS):
            q = _rope(proj[rows, _RQ + hd * LANE:_RQ + (hd + 1) * LANE], cos2, sin2)
            k = _rope(proj[rows, _RK + hd * LANE:_RK + (hd + 1) * LANE], cos2, sin2) * (RET_DK ** -0.5)
            vb = proj[rows, _RV + hd * LANE:_RV + (hd + 1) * LANE].astype(BF16)
            att = (_dot_nt(q.astype(BF16), k.astype(BF16)) * dec_ref[hd]).astype(BF16)
            kv = _dot_tn((k * kdec_ref[hd]).astype(BF16), vb)
            lhs = jnp.concatenate([att, (q * qdec_ref[hd]).astype(BF16)], axis=1)
            ret_units.append((rows, hd, lhs, vb, kv))

    n_sub = HG_CHUNK_LEN // HG_SUB
    crow = lax.broadcasted_iota(jnp.int32, (HG_CHUNK_LEN, HG_CHUNK_LEN), 0)
    ccol = lax.broadcasted_iota(jnp.int32, (HG_CHUNK_LEN, HG_CHUNK_LEN), 1)
    causal = crow >= ccol
    hg_units = []
    for c in range(MIX_TOKENS // HG_CHUNK_LEN):
        rows = slice(c * HG_CHUNK_LEN, (c + 1) * HG_CHUNK_LEN)
        f = lower + (1.0 - lower) * _sigmoid(proj[rows, _HF:_HF + HG_W])
        kk = 1.0 - f
        qq = _silu(proj[rows, _HQ:_HQ + HG_W])
        b = _cumsum_rows(jnp.maximum(jnp.log2(f), HG_LOG2_F_MIN), HG_CHUNK_LEN)
        b_last = b[HG_CHUNK_LEN - 1:HG_CHUNK_LEN, :]
        q_inter = (qq * jnp.exp2(b)).astype(BF16)
        k_upd = (kk * jnp.exp2(b_last - b)).astype(BF16)
        d_last = jnp.exp2(b_last)
        for hd in range(HG_HEADS):
            cols = slice(hd * LANE, (hd + 1) * LANE)
            bh = b[:, cols]
            qh = qq[:, cols]
            kh = kk[:, cols]
            vb = proj[rows, _HI + hd * LANE:_HI + (hd + 1) * LANE].astype(BF16)
            q_parts, k_parts = [], []
            for j in range(n_sub):
                lo, hi = j * HG_SUB, (j + 1) * HG_SUB
                ref = bh[lo + HG_SUB // 2 - 1:lo + HG_SUB // 2, :]
                qt = qh[lo:, :] * jnp.exp2(bh[lo:, :] - ref)
                kt = kh[lo:hi, :] * jnp.exp2(ref - bh[lo:hi, :])
                if lo:
                    qt = jnp.concatenate([jnp.zeros((lo, LANE), F32), qt], axis=0)
                    kt = jnp.concatenate([jnp.zeros((lo, LANE), F32), kt], axis=0)
                if hi < HG_CHUNK_LEN:
                    kt = jnp.concatenate([kt, jnp.zeros((HG_CHUNK_LEN - hi, LANE), F32)], axis=0)
                q_parts.append(qt.astype(BF16))
                k_parts.append(kt.astype(BF16))
            qcat = jnp.concatenate(q_parts, axis=1)
            kcat = jnp.concatenate(k_parts, axis=1)
            att = jnp.where(causal, _dot_nt(qcat, kcat), 0.0).astype(BF16)
            kv = _dot_tn(k_upd[:, cols], vb)
            lhs = jnp.concatenate([q_inter[:, cols], att], axis=1)
            hg_units.append((rows, hd, lhs, vb, kv, d_last[:, cols]))

    for hd in range(RET_HEADS):
        cols = slice(hd * LANE, (hd + 1) * LANE)
        s = sret_ref[0, hd]
        for rows, uh, lhs, vb, kv in ret_units:
            if uh != hd:
                continue
            o = _dot(lhs, jnp.concatenate([vb, s.astype(BF16)], axis=0))
            s = sdec_ref[hd] * s + kv
            gate = proj[rows, _RG + hd * LANE:_RG + (hd + 1) * LANE]
            _gate_store(o, rgain_ref[:, cols], gate, omix_scr, rows, cols)
        sret_ref[0, hd] = s

    for hd in range(HG_HEADS):
        cols = slice(hd * LANE, (hd + 1) * LANE)
        mine = [u for u in hg_units if u[1] == hd]
        dl = jnp.concatenate([u[5] for u in mine] + [jnp.zeros((LANE - len(mine), LANE), F32)], axis=0).T
        s = shg_ref[0, hd]
        for i, (rows, _, lhs, vb, kv, _) in enumerate(mine):
            o = _dot(lhs, jnp.concatenate([s.astype(BF16), vb], axis=0))
            s = s * dl[:, i:i + 1] + kv
            gate = proj[rows, _HGATE + hd * LANE:_HGATE + (hd + 1) * LANE]
            _gate_store(o, hgain_ref[:, cols], gate, omix_scr, rows,
                        slice(RET_W + hd * LANE, RET_W + (hd + 1) * LANE))
        shg_ref[0, hd] = s

    x1_ref[...] = x + _dot(omix_scr[...], wout_ref[...])


def _ret_tables(length, period):
    log_g = np.log(1.0 - 2.0 ** (-5.0 - np.arange(RET_HEADS, dtype=np.float64)))
    idx = np.arange(length)
    pos = idx % period
    rel = (idx[:, None] - idx[None, :]).astype(np.float64)
    same = (idx[:, None] // period) == (idx[None, :] // period)
    valid = (rel >= 0) & same
    dec = np.where(valid[None], np.exp(log_g[:, None, None] * np.where(valid, rel, 0.0)[None]), 0.0)
    qdec = np.exp(log_g[:, None] * (pos + 1.0))[:, :, None] * np.ones((1, 1, LANE))
    kdec = np.exp(log_g[:, None] * (period - 1.0 - pos))[:, :, None] * np.ones((1, 1, LANE))
    sdec = np.exp(log_g * period)[:, None, None] * np.ones((1, 1, LANE))
    as32 = lambda a: jnp.asarray(a, dtype=F32)
    return as32(dec), as32(qdec), as32(kdec), as32(sdec)


def _rope_tables(pos):
    half = RET_DK // 2
    inv_freq = ROPE_BASE ** (-np.arange(half, dtype=np.float64) / half)
    ang = np.asarray(pos, dtype=np.float64)[:, None] * inv_freq[None, :]
    cos, sin = np.cos(ang), np.sin(ang)
    return (jnp.asarray(np.concatenate([cos, cos], axis=-1), dtype=F32),
            jnp.asarray(np.concatenate([-sin, sin], axis=-1), dtype=F32))


def _mix_prompt(x2d, batch, seq, g_mix, w_in, ret_gain, hg_gain, hg_lb, w_out,
                proj_s, state_ret, state_hgrn, dec_len, w_xo, w_gate, w_up, w_down):
    nt = seq // MIX_TOKENS
    n_steps = batch * nt
    d_ff = w_gate.shape[1]
    w_rows = D_MODEL // n_steps
    down_blocks = d_ff // CAST_DOWN_ROWS
    assert w_rows * n_steps == D_MODEL and w_rows % 16 == 0
    assert down_blocks * CAST_DOWN_ROWS == d_ff and down_blocks <= n_steps
    step = lambda b, t: b * nt + t
    sq_rows = pl.BlockSpec((w_rows, D_MODEL), lambda b, t: (step(b, t), 0))
    ff_rows = pl.BlockSpec((w_rows, d_ff), lambda b, t: (step(b, t), 0))
    down_rows = pl.BlockSpec((CAST_DOWN_ROWS, D_MODEL),
                             lambda b, t: (jnp.minimum(step(b, t), down_blocks - 1), 0))
    n_seq = state_ret.shape[0]
    seqs_per_step = n_seq // (batch * nt)
    s_rows = seqs_per_step * dec_len
    assert seqs_per_step * batch * nt == n_seq and s_rows % 16 == 0
    sproj = pl.BlockSpec((s_rows, IN_W), lambda b, t: (b * nt + t, 0))
    somix = pl.BlockSpec((s_rows, MIX_W), lambda b, t: (b * nt + t, 0))
    sstate = pl.BlockSpec((seqs_per_step, RET_HEADS, RET_DK, RET_DV), lambda b, t: (b * nt + t, 0, 0, 0))
    cos2, sin2 = _rope_tables(np.arange(seq))
    dec, qdec, kdec, sdec = _ret_tables(RET_CHUNK_LEN, RET_CHUNK_LEN)
    cos_s, sin_s = _rope_tables(np.tile(np.arange(dec_len) + PAST_LEN, seqs_per_step))
    dec_s, qdec_s, kdec_s, sdec_s = _ret_tables(s_rows, dec_len)
    c2 = lambda b, t: (0, 0)
    c3 = lambda b, t: (0, 0, 0)
    tok = pl.BlockSpec((MIX_TOKENS, D_MODEL), lambda b, t: (b * nt + t, 0))
    state = pl.BlockSpec((1, RET_HEADS, RET_DK, RET_DV), lambda b, t: (b, 0, 0, 0))
    return pl.pallas_call(
        functools.partial(_mix_kernel, dec_len),
        grid=(batch, nt),
        in_specs=[tok, pl.BlockSpec((1, D_MODEL), c2), _resident((D_MODEL, IN_W)),
                  pl.BlockSpec((1, RET_W), c2), pl.BlockSpec((1, HG_W), c2),
                  pl.BlockSpec(hg_lb.shape, c2), _resident((MIX_W, D_MODEL)),
                  pl.BlockSpec((MIX_TOKENS, LANE), lambda b, t: (t, 0)),
                  pl.BlockSpec((MIX_TOKENS, LANE), lambda b, t: (t, 0)),
                  pl.BlockSpec(dec.shape, c3), pl.BlockSpec(qdec.shape, c3),
                  pl.BlockSpec(kdec.shape, c3), pl.BlockSpec(sdec.shape, c3),
                  sproj, sstate, sstate, pl.BlockSpec((s_rows, LANE), c2), pl.BlockSpec((s_rows, LANE), c2),
                  pl.BlockSpec(dec_s.shape, c3), pl.BlockSpec(qdec_s.shape, c3),
                  pl.BlockSpec(kdec_s.shape, c3), pl.BlockSpec(sdec_s.shape, c3),
                  sq_rows, ff_rows, ff_rows, down_rows],
        out_specs=[tok, state, state, somix, sstate, sstate, sq_rows, ff_rows, ff_rows, down_rows],
        out_shape=[jax.ShapeDtypeStruct(x2d.shape, F32),
                   jax.ShapeDtypeStruct((batch, RET_HEADS, RET_DK, RET_DV), F32),
                   jax.ShapeDtypeStruct((batch, HG_HEADS, HG_DK, HG_DV), F32),
                   jax.ShapeDtypeStruct((proj_s.shape[0], MIX_W), BF16),
                   jax.ShapeDtypeStruct(state_ret.shape, F32),
                   jax.ShapeDtypeStruct(state_hgrn.shape, F32),
                   jax.ShapeDtypeStruct(w_xo.shape, BF16), jax.ShapeDtypeStruct(w_gate.shape, BF16),
                   jax.ShapeDtypeStruct(w_up.shape, BF16), jax.ShapeDtypeStruct(w_down.shape, BF16)],
        scratch_shapes=[pltpu.VMEM((MIX_TOKENS, MIX_W), BF16)],
        compiler_params=pltpu.CompilerParams(dimension_semantics=("arbitrary", "arbitrary"),
                                             vmem_limit_bytes=VMEM_LIMIT_BYTES),
        name="mix_prompt",
    )(x2d, g_mix, w_in, ret_gain, hg_gain, hg_lb, w_out, cos2, sin2, dec, qdec, kdec, sdec,
      proj_s, state_ret, state_hgrn, cos_s, sin_s, dec_s, qdec_s, kdec_s, sdec_s,
      w_xo, w_gate, w_up, w_down)


def _inv_rms(x):
    return lax.rsqrt(jnp.mean(x * x, axis=-1, keepdims=True) + EPS)


def _query(x1, gxa, wxq):
    return _dot((x1 * gxa).astype(BF16), wxq) * (_inv_rms(x1) * (XA_HD ** -0.5))


def _ffn_final(x1, ox, wxo, gffn, wgate, wup, wdown, gfinal):
    x2 = x1 + _dot(ox, wxo)
    hb = (x2 * gffn).astype(BF16)
    r = _inv_rms(x2)
    a = (_silu(_dot(hb, wgate) * r) * (_dot(hb, wup) * r)).astype(BF16)
    half = x2.shape[0] // 2
    return jnp.concatenate([_rms(x2[rows] + _dot(a[rows], wdown), gfinal)
                            for rows in (slice(0, half), slice(half, None))], axis=0)


def _tail_kernel(dec_len, x1_ref, mk_ref, mv_ref, gxa_ref, wxq_ref, wxo_ref, gffn_ref, wgate_ref,
                 wup_ref, wdown_ref, gfinal_ref, qs_ref, ck_ref, cv_ref, y_ref, oxs_ref, ox_scr):
    n_seqs = ck_ref.shape[0]
    side_work = _sample_xattn_stages(dec_len, n_seqs, qs_ref, ck_ref, cv_ref, oxs_ref)
    score_stages, attend_stages = side_work[:n_seqs], side_work[n_seqs:]

    def pop(stages):
        if stages:
            stages.pop(0)()

    x1 = x1_ref[...]
    q = _query(x1, gxa_ref[...], wxq_ref[...]).astype(BF16)
    pop(score_stages)
    heads = [slice(hd * XA_HD, (hd + 1) * XA_HD) for hd in range(XA_HEADS)]
    scores = []
    for cols in heads:
        scores.append(_dot_nt(q[:, cols], mk_ref[0, :, cols]))
        pop(score_stages)
    while score_stages:
        pop(score_stages)
    probs = [_softmax_rows(s).astype(BF16) for s in scores]
    for cols, p in zip(heads, probs):
        ox_scr[:, cols] = _dot(p, mv_ref[0, :, cols]).astype(BF16)
        pop(attend_stages)
    while attend_stages:
        pop(attend_stages)
    y_ref[...] = _ffn_final(x1, ox_scr[...], wxo_ref[...], gffn_ref[...], wgate_ref[...],
                            wup_ref[...], wdown_ref[...], gfinal_ref[...])


def _tail_prompt(x1, batch, seq, mkb, mvb, g_xa, w_xq, w_xo, g_ffn, w_gate, w_up, w_down, g_final,
                 q_s, cache_k, cache_v, dec_len):
    nt = seq // TAIL_TOKENS
    d_ff = w_gate.shape[1]
    n_seq = cache_k.shape[0]
    seqs_per_step = n_seq // (batch * nt)
    assert seqs_per_step * batch * nt == n_seq and (seqs_per_step * dec_len) % SUBLANES == 0
    srows = pl.BlockSpec((seqs_per_step * dec_len, D_MODEL), lambda b, t: (b * nt + t, 0))
    cblk = pl.BlockSpec((seqs_per_step, N_MEM, XA_HEADS, XA_HD), lambda b, t: (b * nt + t, 0, 0, 0))
    tok = pl.BlockSpec((TAIL_TOKENS, D_MODEL), lambda b, t: (b * nt + t, 0))
    mem = pl.BlockSpec((1, N_MEM, D_MODEL), lambda b, t: (b, 0, 0))
    return pl.pallas_call(
        functools.partial(_tail_kernel, dec_len),
        grid=(batch, nt),
        in_specs=[tok, mem, mem, _resident((1, D_MODEL)), _resident((D_MODEL, D_MODEL)),
                  _resident((D_MODEL, D_MODEL)), _resident((1, D_MODEL)),
                  _resident((D_MODEL, d_ff)), _resident((D_MODEL, d_ff)),
                  _resident((d_ff, D_MODEL)), _resident((1, D_MODEL)), srows, cblk, cblk],
        out_specs=[tok, srows],
        out_shape=[jax.ShapeDtypeStruct(x1.shape, F32), jax.ShapeDtypeStruct(q_s.shape, F32)],
        scratch_shapes=[pltpu.VMEM((TAIL_TOKENS, D_MODEL), BF16)],
        compiler_params=pltpu.CompilerParams(dimension_semantics=("arbitrary", "arbitrary"),
                                             vmem_limit_bytes=TAIL_VMEM_LIMIT_BYTES),
        name="tail_prompt",
    )(x1, mkb, mvb, g_xa, w_xq, w_xo, g_ffn, w_gate, w_up, w_down, g_final, q_s, cache_k, cache_v)


def _proj_kernel(x_ref, g_ref, w_ref, wout_ref, o_ref, wb_ref, woutb_ref):
    wb = w_ref[...].astype(BF16)
    wb_ref[...] = wb
    woutb_ref[...] = wout_ref[...].astype(BF16)
    x = x_ref[...].reshape(-1, D_MODEL)
    o_ref[...] = _dot(_rms(x, g_ref[...]).astype(BF16), wb)


def _proj_sample(x3d, g_mix, w_in, w_out):
    n = x3d.shape[0] * x3d.shape[1]
    nb = IN_W // D_MODEL
    wblk = pl.BlockSpec((D_MODEL, D_MODEL), lambda j: (0, j))
    oblk = pl.BlockSpec((MIX_W // nb, D_MODEL), lambda j: (j, 0))
    return pl.pallas_call(
        _proj_kernel,
        grid=(nb,),
        in_specs=[pl.BlockSpec(x3d.shape, lambda j: (0, 0, 0)),
                  pl.BlockSpec((1, D_MODEL), lambda j: (0, 0)), wblk, oblk],
        out_specs=[pl.BlockSpec((n, D_MODEL), lambda j: (0, j)), wblk, oblk],
        out_shape=[jax.ShapeDtypeStruct((n, IN_W), F32), jax.ShapeDtypeStruct(w_in.shape, BF16),
                   jax.ShapeDtypeStruct(w_out.shape, BF16)],
        compiler_params=pltpu.CompilerParams(dimension_semantics=("arbitrary",),
                                             vmem_limit_bytes=VMEM_LIMIT_BYTES),
        name="proj_sample",
    )(x3d, g_mix, w_in, w_out)


def _sample_state_stages(dec_len, n_seqs, proj_ref, sret_ref, shg_ref, rgain_ref, hgain_ref, lower,
                         cos_ref, sin_ref, dec_ref, qdec_ref, kdec_ref, sdec_ref,
                         omix_ref, nret_ref, nhg_ref):
    n_rows = n_seqs * dec_len
    pair_rows = SUBLANES
    seqs_per_pair = pair_rows // dec_len
    row8 = lax.broadcasted_iota(jnp.int32, (pair_rows, LANE), 0)
    rown = lax.broadcasted_iota(jnp.int32, (n_rows, n_rows), 0)
    coln = lax.broadcasted_iota(jnp.int32, (n_rows, n_rows), 1)
    same_seq_causal = (rown >= coln) & ((rown // dec_len) == (coln // dec_len))

    def per_sequence(q_all, kd_all, vb_all, state_ref, new_ref, hd, scale_of):
        outs = []
        for p in range(n_rows // pair_rows):
            prow = slice(p * pair_rows, (p + 1) * pair_rows)
            q8, kd8, v8 = q_all[prow], kd_all[prow], vb_all[prow]
            acc = jnp.zeros((pair_rows, LANE), F32)
            for j in range(seqs_per_pair):
                seq = p * seqs_per_pair + j
                mine = (row8 >= j * dec_len) & (row8 < (j + 1) * dec_len)
                s0 = state_ref[seq, hd]
                acc = jnp.where(mine, _dot(q8, s0.astype(BF16)), acc)
                kz = jnp.where(mine, kd8, jnp.zeros_like(kd8))
                new_ref[seq, hd] = scale_of(seq) * s0 + _dot_tn(kz, v8)
            outs.append(acc)
        return jnp.concatenate(outs, axis=0)

    def retention_head(hd):
        cols = slice(hd * LANE, (hd + 1) * LANE)
        cos2, sin2 = cos_ref[...], sin_ref[...]
        q = _rope(proj_ref[:, _RQ + hd * LANE:_RQ + (hd + 1) * LANE], cos2, sin2)
        k = _rope(proj_ref[:, _RK + hd * LANE:_RK + (hd + 1) * LANE], cos2, sin2) * (RET_DK ** -0.5)
        vb = proj_ref[:, _RV + hd * LANE:_RV + (hd + 1) * LANE].astype(BF16)
        gate = proj_ref[:, _RG + hd * LANE:_RG + (hd + 1) * LANE]
        qb = q.astype(BF16)
        att = _dot_nt(qb, k.astype(BF16)) * dec_ref[hd]
        kd = (k * kdec_ref[hd]).astype(BF16)
        sdec = sdec_ref[hd]
        qs = per_sequence(qb, kd, vb, sret_ref, nret_ref, hd, lambda seq: sdec)
        o = _dot(att.astype(BF16), vb) + qdec_ref[hd] * qs
        _gate_store(o, rgain_ref[:, cols], gate, omix_ref, slice(None), cols)

    shared = {}

    def hgrn_prepare():
        f = lower + (1.0 - lower) * _sigmoid(proj_ref[:, _HF:_HF + HG_W])
        kk = 1.0 - f
        qq = _silu(proj_ref[:, _HQ:_HQ + HG_W])
        b = _cumsum_rows(jnp.maximum(jnp.log2(f), HG_LOG2_F_MIN), dec_len)
        pos = lax.broadcasted_iota(jnp.int32, b.shape, 0) & (dec_len - 1)

        def spread(row_in_seq):
            picked = jnp.where(pos == row_in_seq, b, 0.0)
            out = picked
            for s in range(1, dec_len):
                out = out + jnp.where(pos == (row_in_seq + s), pltpu.roll(picked, s, 0), 0.0)
                out = out + jnp.where(pos == (row_in_seq - s), pltpu.roll(picked, n_rows - s, 0), 0.0)
            return out

        ref = spread(dec_len // 2 - 1)
        b_last = spread(dec_len - 1)
        shared.update(q_intra=(qq * jnp.exp2(b - ref)).astype(BF16),
                      k_intra=(kk * jnp.exp2(ref - b)).astype(BF16),
                      q_inter=(qq * jnp.exp2(b)).astype(BF16),
                      k_upd=(kk * jnp.exp2(b_last - b)).astype(BF16),
                      d_last=jnp.exp2(b_last))

    def hgrn_head(hd):
        cols = slice(hd * LANE, (hd + 1) * LANE)
        vb = proj_ref[:, _HI + hd * LANE:_HI + (hd + 1) * LANE].astype(BF16)
        gate = proj_ref[:, _HGATE + hd * LANE:_HGATE + (hd + 1) * LANE]
        att = jnp.where(same_seq_causal,
                        _dot_nt(shared["q_intra"][:, cols], shared["k_intra"][:, cols]), 0.0)
        dpad = jnp.concatenate([shared["d_last"][:, cols],
                                jnp.zeros((LANE - n_rows, LANE), F32)], axis=0)
        dcol = dpad.T
        scale_of = lambda seq: dcol[:, seq * dec_len:seq * dec_len + 1]
        qs = per_sequence(shared["q_inter"][:, cols], shared["k_upd"][:, cols], vb, shg_ref, nhg_ref,
                          hd, scale_of)
        o = _dot(att.astype(BF16), vb) + qs
        _gate_store(o, hgain_ref[:, cols], gate, omix_ref, slice(None),
                    slice(RET_W + hd * LANE, RET_W + (hd + 1) * LANE))

    return ([functools.partial(retention_head, hd) for hd in range(RET_HEADS)] + [hgrn_prepare]
            + [functools.partial(hgrn_head, hd) for hd in range(HG_HEADS)])


def _outq_kernel(x_ref, omix_ref, wout_ref, gxa_ref, wxq_ref, x1_ref, q_ref, wxqb_ref):
    wxq = wxq_ref[...].astype(BF16)
    wxqb_ref[...] = wxq
    x1 = x_ref[...].reshape(-1, D_MODEL) + _dot(omix_ref[...], wout_ref[...])
    x1_ref[...] = x1
    q_ref[...] = _query(x1, gxa_ref[...], wxq)


def _outq_sample(x3d, omix, w_out, g_xa, w_xq):
    n = x3d.shape[0] * x3d.shape[1]
    z = lambda i: (0, 0)
    return pl.pallas_call(
        _outq_kernel,
        grid=(1,),
        in_specs=[pl.BlockSpec(x3d.shape, lambda i: (0, 0, 0)), pl.BlockSpec((n, MIX_W), z),
                  pl.BlockSpec((MIX_W, D_MODEL), z), pl.BlockSpec((1, D_MODEL), z),
                  pl.BlockSpec((D_MODEL, D_MODEL), z)],
        out_specs=[pl.BlockSpec((n, D_MODEL), z), pl.BlockSpec((n, D_MODEL), z),
                   pl.BlockSpec((D_MODEL, D_MODEL), z)],
        out_shape=[jax.ShapeDtypeStruct((n, D_MODEL), F32), jax.ShapeDtypeStruct((n, D_MODEL), F32),
                   jax.ShapeDtypeStruct((D_MODEL, D_MODEL), BF16)],
        compiler_params=pltpu.CompilerParams(dimension_semantics=("arbitrary",),
                                             vmem_limit_bytes=VMEM_LIMIT_BYTES),
        name="outq_sample",
    )(x3d, omix, w_out, g_xa, w_xq)


def _post_kernel(x1_ref, ox_ref, wxo_ref, gffn_ref, wgate_ref, wup_ref, wdown_ref, gfinal_ref, y_ref):
    y = _ffn_final(x1_ref[...], ox_ref[...].astype(BF16), wxo_ref[...], gffn_ref[...],
                   wgate_ref[...], wup_ref[...], wdown_ref[...], gfinal_ref[...])
    y_ref[...] = y.reshape(y_ref.shape)


def _post_sample(x1, ox, dec_len, w_xo, g_ffn, w_gate, w_up, w_down, g_final):
    n = x1.shape[0]
    d_ff = w_gate.shape[1]
    return pl.pallas_call(
        _post_kernel,
        grid=(1,),
        in_specs=[_resident((n, D_MODEL)), _resident((n, D_MODEL)), _resident((D_MODEL, D_MODEL)),
                  _resident((1, D_MODEL)), _resident((D_MODEL, d_ff)), _resident((D_MODEL, d_ff)),
                  _resident((d_ff, D_MODEL)), _resident((1, D_MODEL))],
        out_specs=pl.BlockSpec((n // dec_len, dec_len, D_MODEL), lambda i: (0, 0, 0)),
        out_shape=jax.ShapeDtypeStruct((n // dec_len, dec_len, D_MODEL), F32),
        compiler_params=pltpu.CompilerParams(dimension_semantics=("arbitrary",),
                                             vmem_limit_bytes=VMEM_LIMIT_BYTES),
        name="post_sample",
    )(x1, ox, w_xo, g_ffn, w_gate, w_up, w_down, g_final)


def kernel(x_prompt, x_sample, mem_prompt, state_ret, state_hgrn, cache_mem_k, cache_mem_v, g_mix, w_in,
           ret_gain, hg_gain, hg_lb, w_out, g_xa, g_mem, w_xq, w_xk, w_xv, w_xo, g_ffn, w_gate, w_up,
           w_down, g_final):
    depth = w_in.shape[0]
    assert depth == 1, "single-layer step"
    batch, seq, d = x_prompt.shape
    dec_batch, dec_len, _ = x_sample.shape
    assert d == D_MODEL and seq % MIX_TOKENS == 0 and seq % TAIL_TOKENS == 0
    assert SUBLANES % dec_len == 0

    g_final2 = g_final.reshape(1, D_MODEL)

    proj_s, w_in_b, w_out_b = _proj_sample(x_sample, g_mix, w_in[0], w_out[0])
    mk, mv, mkb, mvb = _memkv(mem_prompt.reshape(batch * N_MEM, D_MODEL), g_mem, w_xk[0], w_xv[0])

    x1_p, ret_p, hg_p, omix_s, ret_s, hg_s, w_xo_b, w_gate_b, w_up_b, w_down_b = _mix_prompt(
        x_prompt.reshape(batch * seq, D_MODEL), batch, seq, g_mix, w_in_b, ret_gain, hg_gain, hg_lb,
        w_out_b, proj_s, state_ret[0], state_hgrn[0], dec_len, w_xo[0], w_gate[0], w_up[0], w_down[0])
    x1_s, q_s, w_xq_b = _outq_sample(x_sample, omix_s, w_out_b, g_xa, w_xq[0])

    y_p, ox_s = _tail_prompt(x1_p, batch, seq, mkb.reshape(batch, N_MEM, D_MODEL),
                             mvb.reshape(batch, N_MEM, D_MODEL), g_xa, w_xq_b, w_xo_b, g_ffn,
                             w_gate_b, w_up_b, w_down_b, g_final2,
                             q_s, cache_mem_k[0], cache_mem_v[0], dec_len)
    y_s = _post_sample(x1_s, ox_s, dec_len, w_xo_b, g_ffn, w_gate_b, w_up_b, w_down_b, g_final2)

    kv_shape = (depth, batch, N_MEM, XA_HEADS, XA_HD)
    return (y_p.reshape(batch, seq, D_MODEL), y_s,
            ret_p[None], hg_p[None], mk.reshape(kv_shape), mv.reshape(kv_shape),
            ret_s[None], hg_s[None])
```

```python
import functools

import jax
import jax.numpy as jnp
import numpy as np
from jax import lax
from jax.experimental import pallas as pl
from jax.experimental.pallas import tpu as pltpu

D_MODEL = 1024
RET_HEADS = 4
RET_DK = 128
RET_DV = 128
RET_W = RET_HEADS * RET_DV
HG_HEADS = 4
HG_DK = 128
HG_DV = 128
HG_W = HG_HEADS * HG_DV
MIX_W = RET_W + HG_W
IN_W = 2 * RET_HEADS * RET_DK + 2 * RET_W + 2 * HG_HEADS * HG_DK + 2 * HG_W
N_MEM = 256
XA_HEADS = 4
XA_HD = D_MODEL // XA_HEADS
PAST_LEN = 16384
ROPE_BASE = 10000.0
EPS = 1e-6
LOG2E = 1.4426950408889634

_RQ, _RK, _RV, _RG = 0, 512, 1024, 1536
_HQ, _HF, _HI, _HGATE = 2048, 2560, 3072, 3584

LANE = 128
SUBLANES = 8
MIX_TOKENS = 512
RET_CHUNK_LEN = 128
HG_CHUNK_LEN = 64
HG_SUB = 16
HG_LOG2_F_MIN = -14.0
TAIL_TOKENS = 512
CAST_DOWN_ROWS = 128
VMEM_LIMIT_BYTES = 56 * 1024 * 1024
PROMPT_VMEM_LIMIT_BYTES = 60 * 1024 * 1024

F32 = jnp.float32
BF16 = jnp.bfloat16


def _dot(a, b):
    return jnp.dot(a, b, preferred_element_type=F32)


def _dot_nt(a, b):
    return lax.dot_general(a, b, (((1,), (1,)), ((), ())), preferred_element_type=F32)


def _dot_tn(a, b):
    return lax.dot_general(a, b, (((0,), (0,)), ((), ())), preferred_element_type=F32)


def _rms(x, g):
    ms = jnp.mean(x * x, axis=-1, keepdims=True)
    return x * lax.rsqrt(ms + EPS) * g


def _sigmoid(x):
    return 1.0 / (1.0 + jnp.exp2(x * (-LOG2E)))


def _silu(x):
    return x * _sigmoid(x)


def _head_norm(o):
    return o * lax.rsqrt(jnp.mean(o * o, axis=-1, keepdims=True) + EPS)


def _softmax_rows(s):
    p = jnp.exp(s - jnp.max(s, axis=-1, keepdims=True))
    return p / jnp.sum(p, axis=-1, keepdims=True)


def _rope(x, cos2, sin2):
    return x * cos2 + pltpu.roll(x, x.shape[-1] // 2, 1) * sin2


def _lower_bound(hglb):
    m = jnp.max(hglb, axis=0, keepdims=True)
    e = jnp.exp(hglb - m)
    return e[0:1, :] / jnp.sum(e, axis=0, keepdims=True)


def _cumsum_rows(x, period):
    row = lax.broadcasted_iota(jnp.int32, x.shape, 0) & (period - 1)
    s = 1
    while s < period:
        x = x + jnp.where(row >= s, pltpu.roll(x, s, 0), 0.0)
        s *= 2
    return x


def _resident(shape):
    zeros = (0,) * len(shape)
    return pl.BlockSpec(shape, lambda *_: zeros, pipeline_mode=pl.Buffered(1))


def _project_memory(m, w_ref, o_ref, ob_ref, hd):
    cols = slice(hd * XA_HD, (hd + 1) * XA_HD)
    kv = _dot(m, w_ref[:, cols])
    o_ref[:, hd, :] = kv
    ob_ref[:, cols] = kv.astype(BF16)


def _sample_xattn_stages(dec_len, n_seqs, q_ref, k_ref, v_ref, o_ref):
    pair_rows = SUBLANES
    seqs_per_pair = pair_rows // dec_len
    n_kv = N_MEM * XA_HEADS
    q_rows = XA_HEADS * pair_rows
    own_head = (lax.broadcasted_iota(jnp.int32, (q_rows, n_kv), 1) % XA_HEADS
                == lax.broadcasted_iota(jnp.int32, (q_rows, n_kv), 0) // pair_rows)
    row8 = lax.broadcasted_iota(jnp.int32, (q_rows, XA_HD), 0) % pair_rows
    probs, outs = {}, {}

    def score(seq):
        prow = slice((seq // seqs_per_pair) * pair_rows, (seq // seqs_per_pair + 1) * pair_rows)
        q8 = q_ref[prow, :]
        qs = jnp.concatenate([q8[:, hd * XA_HD:(hd + 1) * XA_HD] for hd in range(XA_HEADS)],
                             axis=0).astype(BF16)
        kb = k_ref[seq].reshape(n_kv, XA_HD).astype(BF16)
        probs[seq] = _softmax_rows(jnp.where(own_head, _dot_nt(qs, kb), -jnp.inf)).astype(BF16)

    def attend(seq):
        p, j = divmod(seq, seqs_per_pair)
        oj = _dot(probs.pop(seq), v_ref[seq].reshape(n_kv, XA_HD).astype(BF16))
        mine = (row8 >= j * dec_len) & (row8 < (j + 1) * dec_len)
        outs[p] = jnp.where(mine, oj, outs[p]) if p in outs else oj
        if j == seqs_per_pair - 1:
            o = outs.pop(p)
            for hd in range(XA_HEADS):
                o_ref[p * pair_rows:(p + 1) * pair_rows, hd * XA_HD:(hd + 1) * XA_HD] = \
                    o[hd * pair_rows:(hd + 1) * pair_rows, :]

    return ([functools.partial(score, s) for s in range(n_seqs)]
            + [functools.partial(attend, s) for s in range(n_seqs)])


def _gate_store(o, gain, gate, out_ref, rows, cols):
    out_ref[rows, cols] = (_head_norm(o) * gain * _silu(gate)).astype(BF16)


def _mix_kernel(dec_len, x_ref, gmix_ref, win_ref, rgain_ref, hgain_ref, hglb_ref, wout_ref,
                cos_ref, sin_ref, dec_ref, qdec_ref, kdec_ref, sdec_ref,
                projs_ref, srets_ref, shgs_ref, coss_ref, sins_ref, decs_ref, qdecs_ref, kdecs_ref,
                sdecs_ref, wxo_ref, wgate_ref, wup_ref, wdown_ref, mem_ref, gmem_ref, wxk_ref, wxv_ref,
                x1_ref, sret_ref, shg_ref, omixs_ref, nrets_ref, nhgs_ref,
                wxob_ref, wgateb_ref, wupb_ref, wdownb_ref, mk_ref, mv_ref, mkb_ref, mvb_ref,
                omix_scr):
    t = pl.program_id(1)

    @pl.when(t == 0)
    def _():
        sret_ref[...] = jnp.zeros_like(sret_ref)
        shg_ref[...] = jnp.zeros_like(shg_ref)

    for src, dst in ((wxo_ref, wxob_ref), (wgate_ref, wgateb_ref), (wup_ref, wupb_ref),
                     (wdown_ref, wdownb_ref)):
        dst[...] = src[...].astype(BF16)

    x = x_ref[...]
    h = _rms(x, gmix_ref[...]).astype(BF16)
    lower = _lower_bound(hglb_ref[...])
    side_work = _sample_state_stages(dec_len, srets_ref.shape[0], projs_ref, srets_ref, shgs_ref,
                                     rgain_ref, hgain_ref, lower, coss_ref, sins_ref, decs_ref,
                                     qdecs_ref, kdecs_ref, sdecs_ref, omixs_ref, nrets_ref, nhgs_ref)
    groups = {}
    for g0 in (_HF, _HQ, _RQ, _RK, _RV, _HI, _RG, _HGATE):
        groups[g0] = _dot(h, win_ref[:, g0:g0 + RET_W])
        if side_work:
            side_work.pop(0)()
    while side_work:
        side_work.pop(0)()

    class _Proj:
        def __getitem__(self, idx):
            rows, cols = idx
            g0 = (cols.start // RET_W) * RET_W
            return groups[g0][rows, cols.start - g0:cols.stop - g0]

    proj = _Proj()

    ret_units = []
    for c in range(MIX_TOKENS // RET_CHUNK_LEN):
        rows = slice(c * RET_CHUNK_LEN, (c + 1) * RET_CHUNK_LEN)
        cos2 = cos_ref[rows, :]
        sin2 = sin_ref[rows, :]
        for hd in range(RET_HEADS):
            q = _rope(proj[rows, _RQ + hd * LANE:_RQ + (hd + 1) * LANE], cos2, sin2)
            k = _rope(proj[rows, _RK + hd * LANE:_RK + (hd + 1) * LANE], cos2, sin2) * (RET_DK ** -0.5)
            vb = proj[rows, _RV + hd * LANE:_RV + (hd + 1) * LANE].astype(BF16)
            att = (_dot_nt(q.astype(BF16), k.astype(BF16)) * dec_ref[hd]).astype(BF16)
            kv = _dot_tn((k * kdec_ref[hd]).astype(BF16), vb)
            lhs = jnp.concatenate([att, (q * qdec_ref[hd]).astype(BF16)], axis=1)
            ret_units.append((rows, hd, lhs, vb, kv))

    n_sub = HG_CHUNK_LEN // HG_SUB
    crow = lax.broadcasted_iota(jnp.int32, (HG_CHUNK_LEN, HG_CHUNK_LEN), 0)
    ccol = lax.broadcasted_iota(jnp.int32, (HG_CHUNK_LEN, HG_CHUNK_LEN), 1)
    causal = crow >= ccol
    hg_units = []
    for c in range(MIX_TOKENS // HG_CHUNK_LEN):
        rows = slice(c * HG_CHUNK_LEN, (c + 1) * HG_CHUNK_LEN)
        f = lower + (1.0 - lower) * _sigmoid(proj[rows, _HF:_HF + HG_W])
        kk = 1.0 - f
        qq = _silu(proj[rows, _HQ:_HQ + HG_W])
        b = _cumsum_rows(jnp.maximum(jnp.log2(f), HG_LOG2_F_MIN), HG_CHUNK_LEN)
        b_last = b[HG_CHUNK_LEN - 1:HG_CHUNK_LEN, :]
        q_inter = (qq * jnp.exp2(b)).astype(BF16)
        k_upd = (kk * jnp.exp2(b_last - b)).astype(BF16)
        d_last = jnp.exp2(b_last)
        for hd in range(HG_HEADS):
            cols = slice(hd * LANE, (hd + 1) * LANE)
            bh = b[:, cols]
            qh = qq[:, cols]
            kh = kk[:, cols]
            vb = proj[rows, _HI + hd * LANE:_HI + (hd + 1) * LANE].astype(BF16)
            q_parts, k_parts = [], []
            for j in range(n_sub):
                lo, hi = j * HG_SUB, (j + 1) * HG_SUB
                ref = bh[lo + HG_SUB // 2 - 1:lo + HG_SUB // 2, :]
                qt = qh[lo:, :] * jnp.exp2(bh[lo:, :] - ref)
                kt = kh[lo:hi, :] * jnp.exp2(ref - bh[lo:hi, :])
                if lo:
                    qt = jnp.concatenate([jnp.zeros((lo, LANE), F32), qt], axis=0)
                    kt = jnp.concatenate([jnp.zeros((lo, LANE), F32), kt], axis=0)
                if hi < HG_CHUNK_LEN:
                    kt = jnp.concatenate([kt, jnp.zeros((HG_CHUNK_LEN - hi, LANE), F32)], axis=0)
                q_parts.append(qt.astype(BF16))
                k_parts.append(kt.astype(BF16))
            qcat = jnp.concatenate(q_parts, axis=1)
            kcat = jnp.concatenate(k_parts, axis=1)
            att = jnp.where(causal, _dot_nt(qcat, kcat), 0.0).astype(BF16)
            kv = _dot_tn(k_upd[:, cols], vb)
            lhs = jnp.concatenate([q_inter[:, cols], att], axis=1)
            hg_units.append((rows, hd, lhs, vb, kv, d_last[:, cols]))

    assert RET_HEADS == HG_HEADS == XA_HEADS
    mem_n = _rms(mem_ref[...], gmem_ref[...]).astype(BF16)
    for hd in range(RET_HEADS):
        cols = slice(hd * LANE, (hd + 1) * LANE)
        s = sret_ref[0, hd]
        for rows, uh, lhs, vb, kv in ret_units:
            if uh != hd:
                continue
            o = _dot(lhs, jnp.concatenate([vb, s.astype(BF16)], axis=0))
            s = sdec_ref[hd] * s + kv
            gate = proj[rows, _RG + hd * LANE:_RG + (hd + 1) * LANE]
            _gate_store(o, rgain_ref[:, cols], gate, omix_scr, rows, cols)
        sret_ref[0, hd] = s
        _project_memory(mem_n, wxk_ref, mk_ref, mkb_ref, hd)

    for hd in range(HG_HEADS):
        cols = slice(hd * LANE, (hd + 1) * LANE)
        mine = [u for u in hg_units if u[1] == hd]
        dl = jnp.concatenate([u[5] for u in mine] + [jnp.zeros((LANE - len(mine), LANE), F32)], axis=0).T
        s = shg_ref[0, hd]
        for i, (rows, _, lhs, vb, kv, _) in enumerate(mine):
            o = _dot(lhs, jnp.concatenate([s.astype(BF16), vb], axis=0))
            s = s * dl[:, i:i + 1] + kv
            gate = proj[rows, _HGATE + hd * LANE:_HGATE + (hd + 1) * LANE]
            _gate_store(o, hgain_ref[:, cols], gate, omix_scr, rows,
                        slice(RET_W + hd * LANE, RET_W + (hd + 1) * LANE))
        shg_ref[0, hd] = s
        _project_memory(mem_n, wxv_ref, mv_ref, mvb_ref, hd)

    x1_ref[...] = x + _dot(omix_scr[...], wout_ref[...])


def _ret_tables(length, period):
    log_g = np.log(1.0 - 2.0 ** (-5.0 - np.arange(RET_HEADS, dtype=np.float64)))
    idx = np.arange(length)
    pos = idx % period
    rel = (idx[:, None] - idx[None, :]).astype(np.float64)
    same = (idx[:, None] // period) == (idx[None, :] // period)
    valid = (rel >= 0) & same
    dec = np.where(valid[None], np.exp(log_g[:, None, None] * np.where(valid, rel, 0.0)[None]), 0.0)
    qdec = np.exp(log_g[:, None] * (pos + 1.0))[:, :, None] * np.ones((1, 1, LANE))
    kdec = np.exp(log_g[:, None] * (period - 1.0 - pos))[:, :, None] * np.ones((1, 1, LANE))
    sdec = np.exp(log_g * period)[:, None, None] * np.ones((1, 1, LANE))
    as32 = lambda a: jnp.asarray(a, dtype=F32)
    return as32(dec), as32(qdec), as32(kdec), as32(sdec)


def _rope_tables(pos):
    half = RET_DK // 2
    inv_freq = ROPE_BASE ** (-np.arange(half, dtype=np.float64) / half)
    ang = np.asarray(pos, dtype=np.float64)[:, None] * inv_freq[None, :]
    cos, sin = np.cos(ang), np.sin(ang)
    return (jnp.asarray(np.concatenate([cos, cos], axis=-1), dtype=F32),
            jnp.asarray(np.concatenate([-sin, sin], axis=-1), dtype=F32))


def _mix_prompt(x2d, batch, seq, g_mix, w_in, ret_gain, hg_gain, hg_lb, w_out,
                proj_s, state_ret, state_hgrn, dec_len, w_xo, w_gate, w_up, w_down,
                mem2d, g_mem, w_xk, w_xv):
    nt = seq // MIX_TOKENS
    n_steps = batch * nt
    d_ff = w_gate.shape[1]
    w_rows = D_MODEL // n_steps
    down_blocks = d_ff // CAST_DOWN_ROWS
    assert w_rows * n_steps == D_MODEL and w_rows % 16 == 0
    assert down_blocks * CAST_DOWN_ROWS == d_ff and down_blocks <= n_steps
    step = lambda b, t: b * nt + t
    sq_rows = pl.BlockSpec((w_rows, D_MODEL), lambda b, t: (step(b, t), 0))
    ff_rows = pl.BlockSpec((w_rows, d_ff), lambda b, t: (step(b, t), 0))
    down_rows = pl.BlockSpec((CAST_DOWN_ROWS, D_MODEL),
                             lambda b, t: (jnp.minimum(step(b, t), down_blocks - 1), 0))
    n_seq = state_ret.shape[0]
    seqs_per_step = n_seq // (batch * nt)
    s_rows = seqs_per_step * dec_len
    assert seqs_per_step * batch * nt == n_seq and s_rows % 16 == 0
    sproj = pl.BlockSpec((s_rows, IN_W), lambda b, t: (b * nt + t, 0))
    somix = pl.BlockSpec((s_rows, MIX_W), lambda b, t: (b * nt + t, 0))
    sstate = pl.BlockSpec((seqs_per_step, RET_HEADS, RET_DK, RET_DV), lambda b, t: (b * nt + t, 0, 0, 0))
    cos2, sin2 = _rope_tables(np.arange(seq))
    dec, qdec, kdec, sdec = _ret_tables(RET_CHUNK_LEN, RET_CHUNK_LEN)
    cos_s, sin_s = _rope_tables(np.tile(np.arange(dec_len) + PAST_LEN, seqs_per_step))
    dec_s, qdec_s, kdec_s, sdec_s = _ret_tables(s_rows, dec_len)
    c2 = lambda b, t: (0, 0)
    c3 = lambda b, t: (0, 0, 0)
    tok = pl.BlockSpec((MIX_TOKENS, D_MODEL), lambda b, t: (b * nt + t, 0))
    n_mem = mem2d.shape[0]
    mem_rows = n_mem // n_steps
    assert mem_rows * n_steps == n_mem and mem_rows % 16 == 0
    mrow = pl.BlockSpec((mem_rows, D_MODEL), lambda b, t: (step(b, t), 0))
    mhead = pl.BlockSpec((mem_rows, XA_HEADS, XA_HD), lambda b, t: (step(b, t), 0, 0))
    state = pl.BlockSpec((1, RET_HEADS, RET_DK, RET_DV), lambda b, t: (b, 0, 0, 0))
    return pl.pallas_call(
        functools.partial(_mix_kernel, dec_len),
        grid=(batch, nt),
        in_specs=[tok, pl.BlockSpec((1, D_MODEL), c2), _resident((D_MODEL, IN_W)),
                  pl.BlockSpec((1, RET_W), c2), pl.BlockSpec((1, HG_W), c2),
                  pl.BlockSpec(hg_lb.shape, c2), _resident((MIX_W, D_MODEL)),
                  pl.BlockSpec((MIX_TOKENS, LANE), lambda b, t: (t, 0)),
                  pl.BlockSpec((MIX_TOKENS, LANE), lambda b, t: (t, 0)),
                  pl.BlockSpec(dec.shape, c3), pl.BlockSpec(qdec.shape, c3),
                  pl.BlockSpec(kdec.shape, c3), pl.BlockSpec(sdec.shape, c3),
                  sproj, sstate, sstate, pl.BlockSpec((s_rows, LANE), c2), pl.BlockSpec((s_rows, LANE), c2),
                  pl.BlockSpec(dec_s.shape, c3), pl.BlockSpec(qdec_s.shape, c3),
                  pl.BlockSpec(kdec_s.shape, c3), pl.BlockSpec(sdec_s.shape, c3),
                  sq_rows, ff_rows, ff_rows, down_rows,
                  mrow, pl.BlockSpec((1, D_MODEL), c2), _resident((D_MODEL, D_MODEL)),
                  _resident((D_MODEL, D_MODEL))],
        out_specs=[tok, state, state, somix, sstate, sstate, sq_rows, ff_rows, ff_rows, down_rows,
                   mhead, mhead, mrow, mrow],
        out_shape=[jax.ShapeDtypeStruct(x2d.shape, F32),
                   jax.ShapeDtypeStruct((batch, RET_HEADS, RET_DK, RET_DV), F32),
                   jax.ShapeDtypeStruct((batch, HG_HEADS, HG_DK, HG_DV), F32),
                   jax.ShapeDtypeStruct((proj_s.shape[0], MIX_W), BF16),
                   jax.ShapeDtypeStruct(state_ret.shape, F32),
                   jax.ShapeDtypeStruct(state_hgrn.shape, F32),
                   jax.ShapeDtypeStruct(w_xo.shape, BF16), jax.ShapeDtypeStruct(w_gate.shape, BF16),
                   jax.ShapeDtypeStruct(w_up.shape, BF16), jax.ShapeDtypeStruct(w_down.shape, BF16),
                   jax.ShapeDtypeStruct((n_mem, XA_HEADS, XA_HD), F32),
                   jax.ShapeDtypeStruct((n_mem, XA_HEADS, XA_HD), F32),
                   jax.ShapeDtypeStruct((n_mem, D_MODEL), BF16),
                   jax.ShapeDtypeStruct((n_mem, D_MODEL), BF16)],
        scratch_shapes=[pltpu.VMEM((MIX_TOKENS, MIX_W), BF16)],
        compiler_params=pltpu.CompilerParams(dimension_semantics=("arbitrary", "arbitrary"),
                                             vmem_limit_bytes=PROMPT_VMEM_LIMIT_BYTES),
        name="mix_prompt",
    )(x2d, g_mix, w_in, ret_gain, hg_gain, hg_lb, w_out, cos2, sin2, dec, qdec, kdec, sdec,
      proj_s, state_ret, state_hgrn, cos_s, sin_s, dec_s, qdec_s, kdec_s, sdec_s,
      w_xo, w_gate, w_up, w_down, mem2d, g_mem, w_xk, w_xv)


def _inv_rms(x):
    return lax.rsqrt(jnp.mean(x * x, axis=-1, keepdims=True) + EPS)


def _query(x1, gxa, wxq):
    return _dot((x1 * gxa).astype(BF16), wxq) * (_inv_rms(x1) * (XA_HD ** -0.5))


def _ffn_final(x1, ox, wxo, gffn, wgate, wup, wdown, gfinal):
    x2 = x1 + _dot(ox, wxo)
    hb = (x2 * gffn).astype(BF16)
    r = _inv_rms(x2)
    a = (_silu(_dot(hb, wgate) * r) * (_dot(hb, wup) * r)).astype(BF16)
    half = x2.shape[0] // 2
    return jnp.concatenate([_rms(x2[rows] + _dot(a[rows], wdown), gfinal)
                            for rows in (slice(0, half), slice(half, None))], axis=0)


def _tail_kernel(dec_len, x1_ref, mk_ref, mv_ref, gxa_ref, wxq_ref, wxo_ref, gffn_ref, wgate_ref,
                 wup_ref, wdown_ref, gfinal_ref, qs_ref, ck_ref, cv_ref, y_ref, oxs_ref, ox_scr):
    n_seqs = ck_ref.shape[0]
    side_work = _sample_xattn_stages(dec_len, n_seqs, qs_ref, ck_ref, cv_ref, oxs_ref)
    score_stages, attend_stages = side_work[:n_seqs], side_work[n_seqs:]

    def pop(stages):
        if stages:
            stages.pop(0)()

    x1 = x1_ref[...]
    q = _query(x1, gxa_ref[...], wxq_ref[...]).astype(BF16)
    pop(score_stages)
    heads = [slice(hd * XA_HD, (hd + 1) * XA_HD) for hd in range(XA_HEADS)]
    scores = []
    for cols in heads:
        scores.append(_dot_nt(q[:, cols], mk_ref[0, :, cols]))
        pop(score_stages)
    while score_stages:
        pop(score_stages)
    probs = [_softmax_rows(s).astype(BF16) for s in scores]
    for cols, p in zip(heads, probs):
        ox_scr[:, cols] = _dot(p, mv_ref[0, :, cols]).astype(BF16)
        pop(attend_stages)
    while attend_stages:
        pop(attend_stages)
    y_ref[...] = _ffn_final(x1, ox_scr[...], wxo_ref[...], gffn_ref[...], wgate_ref[...],
                            wup_ref[...], wdown_ref[...], gfinal_ref[...])


def _tail_prompt(x1, batch, seq, mkb, mvb, g_xa, w_xq, w_xo, g_ffn, w_gate, w_up, w_down, g_final,
                 q_s, cache_k, cache_v, dec_len):
    nt = seq // TAIL_TOKENS
    d_ff = w_gate.shape[1]
    n_seq = cache_k.shape[0]
    seqs_per_step = n_seq // (batch * nt)
    assert seqs_per_step * batch * nt == n_seq and (seqs_per_step * dec_len) % SUBLANES == 0
    srows = pl.BlockSpec((seqs_per_step * dec_len, D_MODEL), lambda b, t: (b * nt + t, 0))
    cblk = pl.BlockSpec((seqs_per_step, N_MEM, XA_HEADS, XA_HD), lambda b, t: (b * nt + t, 0, 0, 0))
    tok = pl.BlockSpec((TAIL_TOKENS, D_MODEL), lambda b, t: (b * nt + t, 0))
    mem = pl.BlockSpec((1, N_MEM, D_MODEL), lambda b, t: (b, 0, 0))
    return pl.pallas_call(
        functools.partial(_tail_kernel, dec_len),
        grid=(batch, nt),
        in_specs=[tok, mem, mem, _resident((1, D_MODEL)), _resident((D_MODEL, D_MODEL)),
                  _resident((D_MODEL, D_MODEL)), _resident((1, D_MODEL)),
                  _resident((D_MODEL, d_ff)), _resident((D_MODEL, d_ff)),
                  _resident((d_ff, D_MODEL)), _resident((1, D_MODEL)), srows, cblk, cblk],
        out_specs=[tok, srows],
        out_shape=[jax.ShapeDtypeStruct(x1.shape, F32), jax.ShapeDtypeStruct(q_s.shape, F32)],
        scratch_shapes=[pltpu.VMEM((TAIL_TOKENS, D_MODEL), BF16)],
        compiler_params=pltpu.CompilerParams(dimension_semantics=("arbitrary", "arbitrary"),
                                             vmem_limit_bytes=PROMPT_VMEM_LIMIT_BYTES),
        name="tail_prompt",
    )(x1, mkb, mvb, g_xa, w_xq, w_xo, g_ffn, w_gate, w_up, w_down, g_final, q_s, cache_k, cache_v)


def _proj_kernel(x_ref, g_ref, w_ref, wout_ref, wxk_ref, wxv_ref, o_ref, wb_ref, woutb_ref,
                 wxkb_ref, wxvb_ref):
    wb = w_ref[...].astype(BF16)
    wb_ref[...] = wb
    for src, dst in ((wout_ref, woutb_ref), (wxk_ref, wxkb_ref), (wxv_ref, wxvb_ref)):
        dst[...] = src[...].astype(BF16)
    x = x_ref[...].reshape(-1, D_MODEL)
    o_ref[...] = _dot(_rms(x, g_ref[...]).astype(BF16), wb)


def _proj_sample(x3d, g_mix, w_in, w_out, w_xk, w_xv):
    n = x3d.shape[0] * x3d.shape[1]
    nb = IN_W // D_MODEL
    wblk = pl.BlockSpec((D_MODEL, D_MODEL), lambda j: (0, j))
    assert MIX_W == D_MODEL
    oblk = pl.BlockSpec((D_MODEL // nb, D_MODEL), lambda j: (j, 0))
    return pl.pallas_call(
        _proj_kernel,
        grid=(nb,),
        in_specs=[pl.BlockSpec(x3d.shape, lambda j: (0, 0, 0)),
                  pl.BlockSpec((1, D_MODEL), lambda j: (0, 0)), wblk, oblk, oblk, oblk],
        out_specs=[pl.BlockSpec((n, D_MODEL), lambda j: (0, j)), wblk, oblk, oblk, oblk],
        out_shape=[jax.ShapeDtypeStruct((n, IN_W), F32), jax.ShapeDtypeStruct(w_in.shape, BF16),
                   jax.ShapeDtypeStruct(w_out.shape, BF16), jax.ShapeDtypeStruct(w_xk.shape, BF16),
                   jax.ShapeDtypeStruct(w_xv.shape, BF16)],
        compiler_params=pltpu.CompilerParams(dimension_semantics=("arbitrary",),
                                             vmem_limit_bytes=VMEM_LIMIT_BYTES),
        name="proj_sample",
    )(x3d, g_mix, w_in, w_out, w_xk, w_xv)


def _sample_state_stages(dec_len, n_seqs, proj_ref, sret_ref, shg_ref, rgain_ref, hgain_ref, lower,
                         cos_ref, sin_ref, dec_ref, qdec_ref, kdec_ref, sdec_ref,
                         omix_ref, nret_ref, nhg_ref):
    n_rows = n_seqs * dec_len
    pair_rows = SUBLANES
    seqs_per_pair = pair_rows // dec_len
    row8 = lax.broadcasted_iota(jnp.int32, (pair_rows, LANE), 0)
    rown = lax.broadcasted_iota(jnp.int32, (n_rows, n_rows), 0)
    coln = lax.broadcasted_iota(jnp.int32, (n_rows, n_rows), 1)
    same_seq_causal = (rown >= coln) & ((rown // dec_len) == (coln // dec_len))

    def per_sequence(q_all, kd_all, vb_all, state_ref, new_ref, hd, scale_of):
        outs = []
        for p in range(n_rows // pair_rows):
            prow = slice(p * pair_rows, (p + 1) * pair_rows)
            q8, kd8, v8 = q_all[prow], kd_all[prow], vb_all[prow]
            acc = jnp.zeros((pair_rows, LANE), F32)
            for j in range(seqs_per_pair):
                seq = p * seqs_per_pair + j
                mine = (row8 >= j * dec_len) & (row8 < (j + 1) * dec_len)
                s0 = state_ref[seq, hd]
                acc = jnp.where(mine, _dot(q8, s0.astype(BF16)), acc)
                kz = jnp.where(mine, kd8, jnp.zeros_like(kd8))
                new_ref[seq, hd] = scale_of(seq) * s0 + _dot_tn(kz, v8)
            outs.append(acc)
        return jnp.concatenate(outs, axis=0)

    def retention_head(hd):
        cols = slice(hd * LANE, (hd + 1) * LANE)
        cos2, sin2 = cos_ref[...], sin_ref[...]
        q = _rope(proj_ref[:, _RQ + hd * LANE:_RQ + (hd + 1) * LANE], cos2, sin2)
        k = _rope(proj_ref[:, _RK + hd * LANE:_RK + (hd + 1) * LANE], cos2, sin2) * (RET_DK ** -0.5)
        vb = proj_ref[:, _RV + hd * LANE:_RV + (hd + 1) * LANE].astype(BF16)
        gate = proj_ref[:, _RG + hd * LANE:_RG + (hd + 1) * LANE]
        qb = q.astype(BF16)
        att = _dot_nt(qb, k.astype(BF16)) * dec_ref[hd]
        kd = (k * kdec_ref[hd]).astype(BF16)
        sdec = sdec_ref[hd]
        qs = per_sequence(qb, kd, vb, sret_ref, nret_ref, hd, lambda seq: sdec)
        o = _dot(att.astype(BF16), vb) + qdec_ref[hd] * qs
        _gate_store(o, rgain_ref[:, cols], gate, omix_ref, slice(None), cols)

    shared = {}

    def hgrn_prepare():
        f = lower + (1.0 - lower) * _sigmoid(proj_ref[:, _HF:_HF + HG_W])
        kk = 1.0 - f
        qq = _silu(proj_ref[:, _HQ:_HQ + HG_W])
        b = _cumsum_rows(jnp.maximum(jnp.log2(f), HG_LOG2_F_MIN), dec_len)
        pos = lax.broadcasted_iota(jnp.int32, b.shape, 0) & (dec_len - 1)

        def spread(row_in_seq):
            picked = jnp.where(pos == row_in_seq, b, 0.0)
            out = picked
            for s in range(1, dec_len):
                out = out + jnp.where(pos == (row_in_seq + s), pltpu.roll(picked, s, 0), 0.0)
                out = out + jnp.where(pos == (row_in_seq - s), pltpu.roll(picked, n_rows - s, 0), 0.0)
            return out

        ref = spread(dec_len // 2 - 1)
        b_last = spread(dec_len - 1)
        shared.update(q_intra=(qq * jnp.exp2(b - ref)).astype(BF16),
                      k_intra=(kk * jnp.exp2(ref - b)).astype(BF16),
                      q_inter=(qq * jnp.exp2(b)).astype(BF16),
                      k_upd=(kk * jnp.exp2(b_last - b)).astype(BF16),
                      d_last=jnp.exp2(b_last))

    def hgrn_head(hd):
        cols = slice(hd * LANE, (hd + 1) * LANE)
        vb = proj_ref[:, _HI + hd * LANE:_HI + (hd + 1) * LANE].astype(BF16)
        gate = proj_ref[:, _HGATE + hd * LANE:_HGATE + (hd + 1) * LANE]
        att = jnp.where(same_seq_causal,
                        _dot_nt(shared["q_intra"][:, cols], shared["k_intra"][:, cols]), 0.0)
        dpad = jnp.concatenate([shared["d_last"][:, cols],
                                jnp.zeros((LANE - n_rows, LANE), F32)], axis=0)
        dcol = dpad.T
        scale_of = lambda seq: dcol[:, seq * dec_len:seq * dec_len + 1]
        qs = per_sequence(shared["q_inter"][:, cols], shared["k_upd"][:, cols], vb, shg_ref, nhg_ref,
                          hd, scale_of)
        o = _dot(att.astype(BF16), vb) + qs
        _gate_store(o, hgain_ref[:, cols], gate, omix_ref, slice(None),
                    slice(RET_W + hd * LANE, RET_W + (hd + 1) * LANE))

    return ([functools.partial(retention_head, hd) for hd in range(RET_HEADS)] + [hgrn_prepare]
            + [functools.partial(hgrn_head, hd) for hd in range(HG_HEADS)])


def _outq_kernel(x_ref, omix_ref, wout_ref, gxa_ref, wxq_ref, x1_ref, q_ref, wxqb_ref):
    wxq = wxq_ref[...].astype(BF16)
    wxqb_ref[...] = wxq
    x1 = x_ref[...].reshape(-1, D_MODEL) + _dot(omix_ref[...], wout_ref[...])
    x1_ref[...] = x1
    q_ref[...] = _query(x1, gxa_ref[...], wxq)


def _outq_sample(x3d, omix, w_out, g_xa, w_xq):
    n = x3d.shape[0] * x3d.shape[1]
    z = lambda i: (0, 0)
    return pl.pallas_call(
        _outq_kernel,
        grid=(1,),
        in_specs=[pl.BlockSpec(x3d.shape, lambda i: (0, 0, 0)), pl.BlockSpec((n, MIX_W), z),
                  pl.BlockSpec((MIX_W, D_MODEL), z), pl.BlockSpec((1, D_MODEL), z),
                  pl.BlockSpec((D_MODEL, D_MODEL), z)],
        out_specs=[pl.BlockSpec((n, D_MODEL), z), pl.BlockSpec((n, D_MODEL), z),
                   pl.BlockSpec((D_MODEL, D_MODEL), z)],
        out_shape=[jax.ShapeDtypeStruct((n, D_MODEL), F32), jax.ShapeDtypeStruct((n, D_MODEL), F32),
                   jax.ShapeDtypeStruct((D_MODEL, D_MODEL), BF16)],
        compiler_params=pltpu.CompilerParams(dimension_semantics=("arbitrary",),
                                             vmem_limit_bytes=VMEM_LIMIT_BYTES),
        name="outq_sample",
    )(x3d, omix, w_out, g_xa, w_xq)


def _post_kernel(x1_ref, ox_ref, wxo_ref, gffn_ref, wgate_ref, wup_ref, wdown_ref, gfinal_ref, y_ref):
    y = _ffn_final(x1_ref[...], ox_ref[...].astype(BF16), wxo_ref[...], gffn_ref[...],
                   wgate_ref[...], wup_ref[...], wdown_ref[...], gfinal_ref[...])
    y_ref[...] = y.reshape(y_ref.shape)


def _post_sample(x1, ox, dec_len, w_xo, g_ffn, w_gate, w_up, w_down, g_final):
    n = x1.shape[0]
    d_ff = w_gate.shape[1]
    return pl.pallas_call(
        _post_kernel,
        grid=(1,),
        in_specs=[_resident((n, D_MODEL)), _resident((n, D_MODEL)), _resident((D_MODEL, D_MODEL)),
                  _resident((1, D_MODEL)), _resident((D_MODEL, d_ff)), _resident((D_MODEL, d_ff)),
                  _resident((d_ff, D_MODEL)), _resident((1, D_MODEL))],
        out_specs=pl.BlockSpec((n // dec_len, dec_len, D_MODEL), lambda i: (0, 0, 0)),
        out_shape=jax.ShapeDtypeStruct((n // dec_len, dec_len, D_MODEL), F32),
        compiler_params=pltpu.CompilerParams(dimension_semantics=("arbitrary",),
                                             vmem_limit_bytes=VMEM_LIMIT_BYTES),
        name="post_sample",
    )(x1, ox, w_xo, g_ffn, w_gate, w_up, w_down, g_final)


def kernel(x_prompt, x_sample, mem_prompt, state_ret, state_hgrn, cache_mem_k, cache_mem_v, g_mix, w_in,
           ret_gain, hg_gain, hg_lb, w_out, g_xa, g_mem, w_xq, w_xk, w_xv, w_xo, g_ffn, w_gate, w_up,
           w_down, g_final):
    depth = w_in.shape[0]
    assert depth == 1, "single-layer step"
    batch, seq, d = x_prompt.shape
    dec_batch, dec_len, _ = x_sample.shape
    assert d == D_MODEL and seq % MIX_TOKENS == 0 and seq % TAIL_TOKENS == 0
    assert SUBLANES % dec_len == 0

    g_final2 = g_final.reshape(1, D_MODEL)

    proj_s, w_in_b, w_out_b, w_xk_b, w_xv_b = _proj_sample(x_sample, g_mix, w_in[0], w_out[0],
                                                           w_xk[0], w_xv[0])

    (x1_p, ret_p, hg_p, omix_s, ret_s, hg_s, w_xo_b, w_gate_b, w_up_b, w_down_b,
     mk, mv, mkb, mvb) = _mix_prompt(
        x_prompt.reshape(batch * seq, D_MODEL), batch, seq, g_mix, w_in_b, ret_gain, hg_gain, hg_lb,
        w_out_b, proj_s, state_ret[0], state_hgrn[0], dec_len, w_xo[0], w_gate[0], w_up[0], w_down[0],
        mem_prompt.reshape(batch * N_MEM, D_MODEL), g_mem, w_xk_b, w_xv_b)
    x1_s, q_s, w_xq_b = _outq_sample(x_sample, omix_s, w_out_b, g_xa, w_xq[0])

    y_p, ox_s = _tail_prompt(x1_p, batch, seq, mkb.reshape(batch, N_MEM, D_MODEL),
                             mvb.reshape(batch, N_MEM, D_MODEL), g_xa, w_xq_b, w_xo_b, g_ffn,
                             w_gate_b, w_up_b, w_down_b, g_final2,
                             q_s, cache_mem_k[0], cache_mem_v[0], dec_len)
    y_s = _post_sample(x1_s, ox_s, dec_len, w_xo_b, g_ffn, w_gate_b, w_up_b, w_down_b, g_final2)

    kv_shape = (depth, batch, N_MEM, XA_HEADS, XA_HD)
    return (y_p.reshape(batch, seq, D_MODEL), y_s,
            ret_p[None], hg_p[None], mk.reshape(kv_shape), mv.reshape(kv_shape),
            ret_s[None], hg_s[None])
```

```python
import functools

import jax
import jax.numpy as jnp
import numpy as np
from jax import lax
from jax.experimental import pallas as pl
from jax.experimental.pallas import tpu as pltpu

D_MODEL = 1024
RET_HEADS = 4
RET_DK = 128
RET_DV = 128
RET_W = RET_HEADS * RET_DV
HG_HEADS = 4
HG_DK = 128
HG_DV = 128
HG_W = HG_HEADS * HG_DV
MIX_W = RET_W + HG_W
IN_W = 2 * RET_HEADS * RET_DK + 2 * RET_W + 2 * HG_HEADS * HG_DK + 2 * HG_W
N_MEM = 256
XA_HEADS = 4
XA_HD = D_MODEL // XA_HEADS
PAST_LEN = 16384
ROPE_BASE = 10000.0
EPS = 1e-6
LOG2E = 1.4426950408889634

_RQ, _RK, _RV, _RG = 0, 512, 1024, 1536
_HQ, _HF, _HI, _HGATE = 2048, 2560, 3072, 3584

LANE = 128
SUBLANES = 8
MIX_TOKENS = 512
RET_CHUNK_LEN = 128
HG_CHUNK_LEN = 64
HG_SUB = 16
HG_LOG2_F_MIN = -14.0
TAIL_TOKENS = 512
MEMKV_ROWS = 512
CAST_DOWN_ROWS = 128
VMEM_LIMIT_BYTES = 56 * 1024 * 1024
TAIL_VMEM_LIMIT_BYTES = 60 * 1024 * 1024

F32 = jnp.float32
BF16 = jnp.bfloat16


def _dot(a, b):
    return jnp.dot(a, b, preferred_element_type=F32)


def _dot_nt(a, b):
    return lax.dot_general(a, b, (((1,), (1,)), ((), ())), preferred_element_type=F32)


def _dot_tn(a, b):
    return lax.dot_general(a, b, (((0,), (0,)), ((), ())), preferred_element_type=F32)


def _rms(x, g):
    ms = jnp.mean(x * x, axis=-1, keepdims=True)
    return x * lax.rsqrt(ms + EPS) * g


def _sigmoid(x):
    return 1.0 / (1.0 + jnp.exp2(x * (-LOG2E)))


def _silu(x):
    return x * _sigmoid(x)


def _head_norm(o):
    return o * lax.rsqrt(jnp.mean(o * o, axis=-1, keepdims=True) + EPS)


def _softmax_rows(s):
    p = jnp.exp(s - jnp.max(s, axis=-1, keepdims=True))
    return p / jnp.sum(p, axis=-1, keepdims=True)


def _rope(x, cos2, sin2):
    return x * cos2 + pltpu.roll(x, x.shape[-1] // 2, 1) * sin2


def _lower_bound(hglb):
    m = jnp.max(hglb, axis=0, keepdims=True)
    e = jnp.exp(hglb - m)
    return e[0:1, :] / jnp.sum(e, axis=0, keepdims=True)


def _cumsum_rows(x, period):
    row = lax.broadcasted_iota(jnp.int32, x.shape, 0) & (period - 1)
    s = 1
    while s < period:
        x = x + jnp.where(row >= s, pltpu.roll(x, s, 0), 0.0)
        s *= 2
    return x


def _resident(shape):
    zeros = (0,) * len(shape)
    return pl.BlockSpec(shape, lambda *_: zeros, pipeline_mode=pl.Buffered(1))


def _memkv_kernel(mem_ref, g_ref, wk_ref, wv_ref, k_ref, v_ref, kb_ref, vb_ref):
    m = _rms(mem_ref[...], g_ref[...]).astype(BF16)
    k = _dot(m, wk_ref[...].astype(BF16))
    v = _dot(m, wv_ref[...].astype(BF16))
    for hd in range(XA_HEADS):
        cols = slice(hd * XA_HD, (hd + 1) * XA_HD)
        k_ref[:, hd, :] = k[:, cols]
        v_ref[:, hd, :] = v[:, cols]
    kb_ref[...] = k.astype(BF16)
    vb_ref[...] = v.astype(BF16)


def _memkv(mem2d, g_mem, w_xk, w_xv):
    n = mem2d.shape[0]
    full = lambda i: (0, 0)
    row = lambda i: (i, 0)
    blk = pl.BlockSpec((MEMKV_ROWS, D_MODEL), row)
    hblk = pl.BlockSpec((MEMKV_ROWS, XA_HEADS, XA_HD), lambda i: (i, 0, 0))
    return pl.pallas_call(
        _memkv_kernel,
        grid=(n // MEMKV_ROWS,),
        in_specs=[blk, pl.BlockSpec((1, D_MODEL), full),
                  _resident((D_MODEL, D_MODEL)), _resident((D_MODEL, D_MODEL))],
        out_specs=[hblk, hblk, blk, blk],
        out_shape=[jax.ShapeDtypeStruct((n, XA_HEADS, XA_HD), F32),
                   jax.ShapeDtypeStruct((n, XA_HEADS, XA_HD), F32),
                   jax.ShapeDtypeStruct((n, D_MODEL), BF16), jax.ShapeDtypeStruct((n, D_MODEL), BF16)],
        compiler_params=pltpu.CompilerParams(dimension_semantics=("arbitrary",),
                                             vmem_limit_bytes=VMEM_LIMIT_BYTES),
        name="memkv",
    )(mem2d, g_mem, w_xk, w_xv)


def _sample_xattn_stages(dec_len, n_seqs, q_ref, k_ref, v_ref, o_ref):
    pair_rows = SUBLANES
    seqs_per_pair = pair_rows // dec_len
    n_kv = N_MEM * XA_HEADS
    q_rows = XA_HEADS * pair_rows
    own_head = (lax.broadcasted_iota(jnp.int32, (q_rows, n_kv), 1) % XA_HEADS
                == lax.broadcasted_iota(jnp.int32, (q_rows, n_kv), 0) // pair_rows)
    row8 = lax.broadcasted_iota(jnp.int32, (q_rows, XA_HD), 0) % pair_rows
    probs, outs = {}, {}

    def score(seq):
        prow = slice((seq // seqs_per_pair) * pair_rows, (seq // seqs_per_pair + 1) * pair_rows)
        q8 = q_ref[prow, :]
        qs = jnp.concatenate([q8[:, hd * XA_HD:(hd + 1) * XA_HD] for hd in range(XA_HEADS)],
                             axis=0).astype(BF16)
        kb = k_ref[seq].reshape(n_kv, XA_HD).astype(BF16)
        probs[seq] = _softmax_rows(jnp.where(own_head, _dot_nt(qs, kb), -jnp.inf)).astype(BF16)

    def attend(seq):
        p, j = divmod(seq, seqs_per_pair)
        oj = _dot(probs.pop(seq), v_ref[seq].reshape(n_kv, XA_HD).astype(BF16))
        mine = (row8 >= j * dec_len) & (row8 < (j + 1) * dec_len)
        outs[p] = jnp.where(mine, oj, outs[p]) if p in outs else oj
        if j == seqs_per_pair - 1:
            o = outs.pop(p)
            for hd in range(XA_HEADS):
                o_ref[p * pair_rows:(p + 1) * pair_rows, hd * XA_HD:(hd + 1) * XA_HD] = \
                    o[hd * pair_rows:(hd + 1) * pair_rows, :]

    return ([functools.partial(score, s) for s in range(n_seqs)]
            + [functools.partial(attend, s) for s in range(n_seqs)])


def _gate_store(o, gain, gate, out_ref, rows, cols):
    out_ref[rows, cols] = (_head_norm(o) * gain * _silu(gate)).astype(BF16)


def _mix_kernel(dec_len, x_ref, gmix_ref, win_ref, rgain_ref, hgain_ref, hglb_ref, wout_ref,
                cos_ref, sin_ref, dec_ref, qdec_ref, kdec_ref, sdec_ref,
                projs_ref, srets_ref, shgs_ref, coss_ref, sins_ref, decs_ref, qdecs_ref, kdecs_ref,
                sdecs_ref, wxo_ref, wgate_ref, wup_ref, wdown_ref,
                x1_ref, sret_ref, shg_ref, omixs_ref, nrets_ref, nhgs_ref,
                wxob_ref, wgateb_ref, wupb_ref, wdownb_ref, omix_scr):
    t = pl.program_id(1)

    @pl.when(t == 0)
    def _():
        sret_ref[...] = jnp.zeros_like(sret_ref)
        shg_ref[...] = jnp.zeros_like(shg_ref)

    for src, dst in ((wxo_ref, wxob_ref), (wgate_ref, wgateb_ref), (wup_ref, wupb_ref),
                     (wdown_ref, wdownb_ref)):
        dst[...] = src[...].astype(BF16)

    x = x_ref[...]
    h = _rms(x, gmix_ref[...]).astype(BF16)
    lower = _lower_bound(hglb_ref[...])
    side_work = _sample_state_stages(dec_len, srets_ref.shape[0], projs_ref, srets_ref, shgs_ref,
                                     rgain_ref, hgain_ref, lower, coss_ref, sins_ref, decs_ref,
                                     qdecs_ref, kdecs_ref, sdecs_ref, omixs_ref, nrets_ref, nhgs_ref)
    groups = {}
    for g0 in (_HF, _HQ, _RQ, _RK, _RV, _HI, _RG, _HGATE):
        groups[g0] = _dot(h, win_ref[:, g0:g0 + RET_W])

    class _Proj:
        def __getitem__(self, idx):
            rows, cols = idx
            g0 = (cols.start // RET_W) * RET_W
            return groups[g0][rows, cols.start - g0:cols.stop - g0]

    proj = _Proj()

    ret_units = []
    for c in range(MIX_TOKENS // RET_CHUNK_LEN):
        rows = slice(c * RET_CHUNK_LEN, (c + 1) * RET_CHUNK_LEN)
        cos2 = cos_ref[rows, :]
        sin2 = sin_ref[rows, :]
        for hd in range(RET_HEADS):
            q = _rope(proj[rows, _RQ + hd * LANE:_RQ + (hd + 1) * LANE], cos2, sin2)
            k = _rope(proj[rows, _RK + hd * LANE:_RK + (hd + 1) * LANE], cos2, sin2) * (RET_DK ** -0.5)
            vb = proj[rows, _RV + hd * LANE:_RV + (hd + 1) * LANE].astype(BF16)
            att = (_dot_nt(q.astype(BF16), k.astype(BF16)) * dec_ref[hd]).astype(BF16)
            kv = _dot_tn((k * kdec_ref[hd]).astype(BF16), vb)
            lhs = jnp.concatenate([att, (q * qdec_ref[hd]).astype(BF16)], axis=1)
            ret_units.append((rows, hd, lhs, vb, kv))

    n_sub = HG_CHUNK_LEN // HG_SUB
    crow = lax.broadcasted_iota(jnp.int32, (HG_CHUNK_LEN, HG_CHUNK_LEN), 0)
    ccol = lax.broadcasted_iota(jnp.int32, (HG_CHUNK_LEN, HG_CHUNK_LEN), 1)
    causal = crow >= ccol
    hg_units = []
    for c in range(MIX_TOKENS // HG_CHUNK_LEN):
        rows = slice(c * HG_CHUNK_LEN, (c + 1) * HG_CHUNK_LEN)
        f = lower + (1.0 - lower) * _sigmoid(proj[rows, _HF:_HF + HG_W])
        kk = 1.0 - f
        qq = _silu(proj[rows, _HQ:_HQ + HG_W])
        b = _cumsum_rows(jnp.maximum(jnp.log2(f), HG_LOG2_F_MIN), HG_CHUNK_LEN)
        b_last = b[HG_CHUNK_LEN - 1:HG_CHUNK_LEN, :]
        q_inter = (qq * jnp.exp2(b)).astype(BF16)
        k_upd = (kk * jnp.exp2(b_last - b)).astype(BF16)
        d_last = jnp.exp2(b_last)
        for hd in range(HG_HEADS):
            cols = slice(hd * LANE, (hd + 1) * LANE)
            bh = b[:, cols]
            qh = qq[:, cols]
            kh = kk[:, cols]
            vb = proj[rows, _HI + hd * LANE:_HI + (hd + 1) * LANE].astype(BF16)
            q_parts, k_parts = [], []
            for j in range(n_sub):
                lo, hi = j * HG_SUB, (j + 1) * HG_SUB
                ref = bh[lo + HG_SUB // 2 - 1:lo + HG_SUB // 2, :]
                qt = qh[lo:, :] * jnp.exp2(bh[lo:, :] - ref)
                kt = kh[lo:hi, :] * jnp.exp2(ref - bh[lo:hi, :])
                if lo:
                    qt = jnp.concatenate([jnp.zeros((lo, LANE), F32), qt], axis=0)
                    kt = jnp.concatenate([jnp.zeros((lo, LANE), F32), kt], axis=0)
                if hi < HG_CHUNK_LEN:
                    kt = jnp.concatenate([kt, jnp.zeros((HG_CHUNK_LEN - hi, LANE), F32)], axis=0)
                q_parts.append(qt.astype(BF16))
                k_parts.append(kt.astype(BF16))
            qcat = jnp.concatenate(q_parts, axis=1)
            kcat = jnp.concatenate(k_parts, axis=1)
            att = jnp.where(causal, _dot_nt(qcat, kcat), 0.0).astype(BF16)
            kv = _dot_tn(k_upd[:, cols], vb)
            lhs = jnp.concatenate([q_inter[:, cols], att], axis=1)
            hg_units.append((rows, hd, lhs, vb, kv, d_last[:, cols]))

    for hd in range(RET_HEADS):
        cols = slice(hd * LANE, (hd + 1) * LANE)
        s = sret_ref[0, hd]
        for rows, uh, lhs, vb, kv in ret_units:
            if uh != hd:
                continue
            o = _dot(lhs, jnp.concatenate([vb, s.astype(BF16)], axis=0))
            s = sdec_ref[hd] * s + kv
            gate = proj[rows, _RG + hd * LANE:_RG + (hd + 1) * LANE]
            _gate_store(o, rgain_ref[:, cols], gate, omix_scr, rows, cols)
        sret_ref[0, hd] = s
        side_work.pop(0)()

    for hd in range(HG_HEADS):
        cols = slice(hd * LANE, (hd + 1) * LANE)
        mine = [u for u in hg_units if u[1] == hd]
        dl = jnp.concatenate([u[5] for u in mine] + [jnp.zeros((LANE - len(mine), LANE), F32)], axis=0).T
        s = shg_ref[0, hd]
        for i, (rows, _, lhs, vb, kv, _) in enumerate(mine):
            o = _dot(lhs, jnp.concatenate([s.astype(BF16), vb], axis=0))
            s = s * dl[:, i:i + 1] + kv
            gate = proj[rows, _HGATE + hd * LANE:_HGATE + (hd + 1) * LANE]
            _gate_store(o, hgain_ref[:, cols], gate, omix_scr, rows,
                        slice(RET_W + hd * LANE, RET_W + (hd + 1) * LANE))
        shg_ref[0, hd] = s
        for _ in range(2 if hd == 0 else 1):
            side_work.pop(0)()
    assert not side_work

    x1_ref[...] = x + _dot(omix_scr[...], wout_ref[...])


def _ret_tables(length, period):
    log_g = np.log(1.0 - 2.0 ** (-5.0 - np.arange(RET_HEADS, dtype=np.float64)))
    idx = np.arange(length)
    pos = idx % period
    rel = (idx[:, None] - idx[None, :]).astype(np.float64)
    same = (idx[:, None] // period) == (idx[None, :] // period)
    valid = (rel >= 0) & same
    dec = np.where(valid[None], np.exp(log_g[:, None, None] * np.where(valid, rel, 0.0)[None]), 0.0)
    qdec = np.exp(log_g[:, None] * (pos + 1.0))[:, :, None] * np.ones((1, 1, LANE))
    kdec = np.exp(log_g[:, None] * (period - 1.0 - pos))[:, :, None] * np.ones((1, 1, LANE))
    sdec = np.exp(log_g * period)[:, None, None] * np.ones((1, 1, LANE))
    as32 = lambda a: jnp.asarray(a, dtype=F32)
    return as32(dec), as32(qdec), as32(kdec), as32(sdec)


def _rope_tables(pos):
    half = RET_DK // 2
    inv_freq = ROPE_BASE ** (-np.arange(half, dtype=np.float64) / half)
    ang = np.asarray(pos, dtype=np.float64)[:, None] * inv_freq[None, :]
    cos, sin = np.cos(ang), np.sin(ang)
    return (jnp.asarray(np.concatenate([cos, cos], axis=-1), dtype=F32),
            jnp.asarray(np.concatenate([-sin, sin], axis=-1), dtype=F32))


def _mix_prompt(x2d, batch, seq, g_mix, w_in, ret_gain, hg_gain, hg_lb, w_out,
                proj_s, state_ret, state_hgrn, dec_len, w_xo, w_gate, w_up, w_down):
    nt = seq // MIX_TOKENS
    n_steps = batch * nt
    d_ff = w_gate.shape[1]
    w_rows = D_MODEL // n_steps
    down_blocks = d_ff // CAST_DOWN_ROWS
    assert w_rows * n_steps == D_MODEL and w_rows % 16 == 0
    assert down_blocks * CAST_DOWN_ROWS == d_ff and down_blocks <= n_steps
    step = lambda b, t: b * nt + t
    sq_rows = pl.BlockSpec((w_rows, D_MODEL), lambda b, t: (step(b, t), 0))
    ff_rows = pl.BlockSpec((w_rows, d_ff), lambda b, t: (step(b, t), 0))
    down_rows = pl.BlockSpec((CAST_DOWN_ROWS, D_MODEL),
                             lambda b, t: (jnp.minimum(step(b, t), down_blocks - 1), 0))
    n_seq = state_ret.shape[0]
    seqs_per_step = n_seq // (batch * nt)
    s_rows = seqs_per_step * dec_len
    assert seqs_per_step * batch * nt == n_seq and s_rows % 16 == 0
    sproj = pl.BlockSpec((s_rows, IN_W), lambda b, t: (b * nt + t, 0))
    somix = pl.BlockSpec((s_rows, MIX_W), lambda b, t: (b * nt + t, 0))
    sstate = pl.BlockSpec((seqs_per_step, RET_HEADS, RET_DK, RET_DV), lambda b, t: (b * nt + t, 0, 0, 0))
    cos2, sin2 = _rope_tables(np.arange(seq))
    dec, qdec, kdec, sdec = _ret_tables(RET_CHUNK_LEN, RET_CHUNK_LEN)
    cos_s, sin_s = _rope_tables(np.tile(np.arange(dec_len) + PAST_LEN, seqs_per_step))
    dec_s, qdec_s, kdec_s, sdec_s = _ret_tables(s_rows, dec_len)
    c2 = lambda b, t: (0, 0)
    c3 = lambda b, t: (0, 0, 0)
    tok = pl.BlockSpec((MIX_TOKENS, D_MODEL), lambda b, t: (b * nt + t, 0))
    state = pl.BlockSpec((1, RET_HEADS, RET_DK, RET_DV), lambda b, t: (b, 0, 0, 0))
    return pl.pallas_call(
        functools.partial(_mix_kernel, dec_len),
        grid=(batch, nt),
        in_specs=[tok, pl.BlockSpec((1, D_MODEL), c2), _resident((D_MODEL, IN_W)),
                  pl.BlockSpec((1, RET_W), c2), pl.BlockSpec((1, HG_W), c2),
                  pl.BlockSpec(hg_lb.shape, c2), _resident((MIX_W, D_MODEL)),
                  pl.BlockSpec((MIX_TOKENS, LANE), lambda b, t: (t, 0)),
                  pl.BlockSpec((MIX_TOKENS, LANE), lambda b, t: (t, 0)),
                  pl.BlockSpec(dec.shape, c3), pl.BlockSpec(qdec.shape, c3),
                  pl.BlockSpec(kdec.shape, c3), pl.BlockSpec(sdec.shape, c3),
                  sproj, sstate, sstate, pl.BlockSpec((s_rows, LANE), c2), pl.BlockSpec((s_rows, LANE), c2),
                  pl.BlockSpec(dec_s.shape, c3), pl.BlockSpec(qdec_s.shape, c3),
                  pl.BlockSpec(kdec_s.shape, c3), pl.BlockSpec(sdec_s.shape, c3),
                  sq_rows, ff_rows, ff_rows, down_rows],
        out_specs=[tok, state, state, somix, sstate, sstate, sq_rows, ff_rows, ff_rows, down_rows],
        out_shape=[jax.ShapeDtypeStruct(x2d.shape, F32),
                   jax.ShapeDtypeStruct((batch, RET_HEADS, RET_DK, RET_DV), F32),
                   jax.ShapeDtypeStruct((batch, HG_HEADS, HG_DK, HG_DV), F32),
                   jax.ShapeDtypeStruct((proj_s.shape[0], MIX_W), BF16),
                   jax.ShapeDtypeStruct(state_ret.shape, F32),
                   jax.ShapeDtypeStruct(state_hgrn.shape, F32),
                   jax.ShapeDtypeStruct(w_xo.shape, BF16), jax.ShapeDtypeStruct(w_gate.shape, BF16),
                   jax.ShapeDtypeStruct(w_up.shape, BF16), jax.ShapeDtypeStruct(w_down.shape, BF16)],
        scratch_shapes=[pltpu.VMEM((MIX_TOKENS, MIX_W), BF16)],
        compiler_params=pltpu.CompilerParams(dimension_semantics=("arbitrary", "arbitrary"),
                                             vmem_limit_bytes=VMEM_LIMIT_BYTES),
        name="mix_prompt",
    )(x2d, g_mix, w_in, ret_gain, hg_gain, hg_lb, w_out, cos2, sin2, dec, qdec, kdec, sdec,
      proj_s, state_ret, state_hgrn, cos_s, sin_s, dec_s, qdec_s, kdec_s, sdec_s,
      w_xo, w_gate, w_up, w_down)


def _inv_rms(x):
    return lax.rsqrt(jnp.mean(x * x, axis=-1, keepdims=True) + EPS)


def _query(x1, gxa, wxq):
    return _dot((x1 * gxa).astype(BF16), wxq) * (_inv_rms(x1) * (XA_HD ** -0.5))


def _ffn_final(x1, ox, wxo, gffn, wgate, wup, wdown, gfinal):
    x2 = x1 + _dot(ox, wxo)
    hb = (x2 * gffn).astype(BF16)
    r = _inv_rms(x2)
    a = (_silu(_dot(hb, wgate) * r) * (_dot(hb, wup) * r)).astype(BF16)
    half = x2.shape[0] // 2
    return jnp.concatenate([_rms(x2[rows] + _dot(a[rows], wdown), gfinal)
                            for rows in (slice(0, half), slice(half, None))], axis=0)


def _tail_kernel(dec_len, x1_ref, mk_ref, mv_ref, gxa_ref, wxq_ref, wxo_ref, gffn_ref, wgate_ref,
                 wup_ref, wdown_ref, gfinal_ref, qs_ref, ck_ref, cv_ref, y_ref, oxs_ref, ox_scr):
    n_seqs = ck_ref.shape[0]
    side_work = _sample_xattn_stages(dec_len, n_seqs, qs_ref, ck_ref, cv_ref, oxs_ref)
    score_stages, attend_stages = side_work[:n_seqs], side_work[n_seqs:]

    def pop(stages):
        if stages:
            stages.pop(0)()

    x1 = x1_ref[...]
    q = _query(x1, gxa_ref[...], wxq_ref[...]).astype(BF16)
    pop(score_stages)
    heads = [slice(hd * XA_HD, (hd + 1) * XA_HD) for hd in range(XA_HEADS)]
    scores = []
    for cols in heads:
        scores.append(_dot_nt(q[:, cols], mk_ref[0, :, cols]))
        pop(score_stages)
    while score_stages:
        pop(score_stages)
    probs = [_softmax_rows(s).astype(BF16) for s in scores]
    for cols, p in zip(heads, probs):
        ox_scr[:, cols] = _dot(p, mv_ref[0, :, cols]).astype(BF16)
        pop(attend_stages)
    while attend_stages:
        pop(attend_stages)
    y_ref[...] = _ffn_final(x1, ox_scr[...], wxo_ref[...], gffn_ref[...], wgate_ref[...],
                            wup_ref[...], wdown_ref[...], gfinal_ref[...])


def _tail_prompt(x1, batch, seq, mkb, mvb, g_xa, w_xq, w_xo, g_ffn, w_gate, w_up, w_down, g_final,
                 q_s, cache_k, cache_v, dec_len):
    nt = seq // TAIL_TOKENS
    d_ff = w_gate.shape[1]
    n_seq = cache_k.shape[0]
    seqs_per_step = n_seq // (batch * nt)
    assert seqs_per_step * batch * nt == n_seq and (seqs_per_step * dec_len) % SUBLANES == 0
    srows = pl.BlockSpec((seqs_per_step * dec_len, D_MODEL), lambda b, t: (b * nt + t, 0))
    cblk = pl.BlockSpec((seqs_per_step, N_MEM, XA_HEADS, XA_HD), lambda b, t: (b * nt + t, 0, 0, 0))
    tok = pl.BlockSpec((TAIL_TOKENS, D_MODEL), lambda b, t: (b * nt + t, 0))
    mem = pl.BlockSpec((1, N_MEM, D_MODEL), lambda b, t: (b, 0, 0))
    return pl.pallas_call(
        functools.partial(_tail_kernel, dec_len),
        grid=(batch, nt),
        in_specs=[tok, mem, mem, _resident((1, D_MODEL)), _resident((D_MODEL, D_MODEL)),
                  _resident((D_MODEL, D_MODEL)), _resident((1, D_MODEL)),
                  _resident((D_MODEL, d_ff)), _resident((D_MODEL, d_ff)),
                  _resident((d_ff, D_MODEL)), _resident((1, D_MODEL)), srows, cblk, cblk],
        out_specs=[tok, srows],
        out_shape=[jax.ShapeDtypeStruct(x1.shape, F32), jax.ShapeDtypeStruct(q_s.shape, F32)],
        scratch_shapes=[pltpu.VMEM((TAIL_TOKENS, D_MODEL), BF16)],
        compiler_params=pltpu.CompilerParams(dimension_semantics=("arbitrary", "arbitrary"),
                                             vmem_limit_bytes=TAIL_VMEM_LIMIT_BYTES),
        name="tail_prompt",
    )(x1, mkb, mvb, g_xa, w_xq, w_xo, g_ffn, w_gate, w_up, w_down, g_final, q_s, cache_k, cache_v)


def _proj_kernel(x_ref, g_ref, w_ref, wout_ref, o_ref, wb_ref, woutb_ref):
    wb = w_ref[...].astype(BF16)
    wb_ref[...] = wb
    woutb_ref[...] = wout_ref[...].astype(BF16)
    x = x_ref[...].reshape(-1, D_MODEL)
    o_ref[...] = _dot(_rms(x, g_ref[...]).astype(BF16), wb)


def _proj_sample(x3d, g_mix, w_in, w_out):
    n = x3d.shape[0] * x3d.shape[1]
    nb = IN_W // D_MODEL
    wblk = pl.BlockSpec((D_MODEL, D_MODEL), lambda j: (0, j))
    oblk = pl.BlockSpec((MIX_W // nb, D_MODEL), lambda j: (j, 0))
    return pl.pallas_call(
        _proj_kernel,
        grid=(nb,),
        in_specs=[pl.BlockSpec(x3d.shape, lambda j: (0, 0, 0)),
                  pl.BlockSpec((1, D_MODEL), lambda j: (0, 0)), wblk, oblk],
        out_specs=[pl.BlockSpec((n, D_MODEL), lambda j: (0, j)), wblk, oblk],
        out_shape=[jax.ShapeDtypeStruct((n, IN_W), F32), jax.ShapeDtypeStruct(w_in.shape, BF16),
                   jax.ShapeDtypeStruct(w_out.shape, BF16)],
        compiler_params=pltpu.CompilerParams(dimension_semantics=("arbitrary",),
                                             vmem_limit_bytes=VMEM_LIMIT_BYTES),
        name="proj_sample",
    )(x3d, g_mix, w_in, w_out)


def _sample_state_stages(dec_len, n_seqs, proj_ref, sret_ref, shg_ref, rgain_ref, hgain_ref, lower,
                         cos_ref, sin_ref, dec_ref, qdec_ref, kdec_ref, sdec_ref,
                         omix_ref, nret_ref, nhg_ref):
    n_rows = n_seqs * dec_len
    pair_rows = SUBLANES
    seqs_per_pair = pair_rows // dec_len
    row8 = lax.broadcasted_iota(jnp.int32, (pair_rows, LANE), 0)
    rown = lax.broadcasted_iota(jnp.int32, (n_rows, n_rows), 0)
    coln = lax.broadcasted_iota(jnp.int32, (n_rows, n_rows), 1)
    same_seq_causal = (rown >= coln) & ((rown // dec_len) == (coln // dec_len))

    def per_sequence(q_all, kd_all, vb_all, state_ref, new_ref, hd, scale_of):
        outs = []
        for p in range(n_rows // pair_rows):
            prow = slice(p * pair_rows, (p + 1) * pair_rows)
            q8, kd8, v8 = q_all[prow], kd_all[prow], vb_all[prow]
            acc = jnp.zeros((pair_rows, LANE), F32)
            for j in range(seqs_per_pair):
                seq = p * seqs_per_pair + j
                mine = (row8 >= j * dec_len) & (row8 < (j + 1) * dec_len)
                s0 = state_ref[seq, hd]
                acc = jnp.where(mine, _dot(q8, s0.astype(BF16)), acc)
                kz = jnp.where(mine, kd8, jnp.zeros_like(kd8))
                new_ref[seq, hd] = scale_of(seq) * s0 + _dot_tn(kz, v8)
            outs.append(acc)
        return jnp.concatenate(outs, axis=0)

    def retention_head(hd):
        cols = slice(hd * LANE, (hd + 1) * LANE)
        cos2, sin2 = cos_ref[...], sin_ref[...]
        q = _rope(proj_ref[:, _RQ + hd * LANE:_RQ + (hd + 1) * LANE], cos2, sin2)
        k = _rope(proj_ref[:, _RK + hd * LANE:_RK + (hd + 1) * LANE], cos2, sin2) * (RET_DK ** -0.5)
        vb = proj_ref[:, _RV + hd * LANE:_RV + (hd + 1) * LANE].astype(BF16)
        gate = proj_ref[:, _RG + hd * LANE:_RG + (hd + 1) * LANE]
        qb = q.astype(BF16)
        att = _dot_nt(qb, k.astype(BF16)) * dec_ref[hd]
        kd = (k * kdec_ref[hd]).astype(BF16)
        sdec = sdec_ref[hd]
        qs = per_sequence(qb, kd, vb, sret_ref, nret_ref, hd, lambda seq: sdec)
        o = _dot(att.astype(BF16), vb) + qdec_ref[hd] * qs
        _gate_store(o, rgain_ref[:, cols], gate, omix_ref, slice(None), cols)

    shared = {}

    def hgrn_prepare():
        f = lower + (1.0 - lower) * _sigmoid(proj_ref[:, _HF:_HF + HG_W])
        kk = 1.0 - f
        qq = _silu(proj_ref[:, _HQ:_HQ + HG_W])
        b = _cumsum_rows(jnp.maximum(jnp.log2(f), HG_LOG2_F_MIN), dec_len)
        pos = lax.broadcasted_iota(jnp.int32, b.shape, 0) & (dec_len - 1)

        def spread(row_in_seq):
            picked = jnp.where(pos == row_in_seq, b, 0.0)
            out = picked
            for s in range(1, dec_len):
                out = out + jnp.where(pos == (row_in_seq + s), pltpu.roll(picked, s, 0), 0.0)
                out = out + jnp.where(pos == (row_in_seq - s), pltpu.roll(picked, n_rows - s, 0), 0.0)
            return out

        ref = spread(dec_len // 2 - 1)
        b_last = spread(dec_len - 1)
        shared.update(q_intra=(qq * jnp.exp2(b - ref)).astype(BF16),
                      k_intra=(kk * jnp.exp2(ref - b)).astype(BF16),
                      q_inter=(qq * jnp.exp2(b)).astype(BF16),
                      k_upd=(kk * jnp.exp2(b_last - b)).astype(BF16),
                      d_last=jnp.exp2(b_last))

    def hgrn_head(hd):
        cols = slice(hd * LANE, (hd + 1) * LANE)
        vb = proj_ref[:, _HI + hd * LANE:_HI + (hd + 1) * LANE].astype(BF16)
        gate = proj_ref[:, _HGATE + hd * LANE:_HGATE + (hd + 1) * LANE]
        att = jnp.where(same_seq_causal,
                        _dot_nt(shared["q_intra"][:, cols], shared["k_intra"][:, cols]), 0.0)
        dpad = jnp.concatenate([shared["d_last"][:, cols],
                                jnp.zeros((LANE - n_rows, LANE), F32)], axis=0)
        dcol = dpad.T
        scale_of = lambda seq: dcol[:, seq * dec_len:seq * dec_len + 1]
        qs = per_sequence(shared["q_inter"][:, cols], shared["k_upd"][:, cols], vb, shg_ref, nhg_ref,
                          hd, scale_of)
        o = _dot(att.astype(BF16), vb) + qs
        _gate_store(o, hgain_ref[:, cols], gate, omix_ref, slice(None),
                    slice(RET_W + hd * LANE, RET_W + (hd + 1) * LANE))

    return ([functools.partial(retention_head, hd) for hd in range(RET_HEADS)] + [hgrn_prepare]
            + [functools.partial(hgrn_head, hd) for hd in range(HG_HEADS)])


def _outq_kernel(x_ref, omix_ref, wout_ref, gxa_ref, wxq_ref, x1_ref, q_ref, wxqb_ref):
    wxq = wxq_ref[...].astype(BF16)
    wxqb_ref[...] = wxq
    x1 = x_ref[...].reshape(-1, D_MODEL) + _dot(omix_ref[...], wout_ref[...])
    x1_ref[...] = x1
    q_ref[...] = _query(x1, gxa_ref[...], wxq)


def _outq_sample(x3d, omix, w_out, g_xa, w_xq):
    n = x3d.shape[0] * x3d.shape[1]
    z = lambda i: (0, 0)
    return pl.pallas_call(
        _outq_kernel,
        grid=(1,),
        in_specs=[pl.BlockSpec(x3d.shape, lambda i: (0, 0, 0)), pl.BlockSpec((n, MIX_W), z),
                  pl.BlockSpec((MIX_W, D_MODEL), z), pl.BlockSpec((1, D_MODEL), z),
                  pl.BlockSpec((D_MODEL, D_MODEL), z)],
        out_specs=[pl.BlockSpec((n, D_MODEL), z), pl.BlockSpec((n, D_MODEL), z),
                   pl.BlockSpec((D_MODEL, D_MODEL), z)],
        out_shape=[jax.ShapeDtypeStruct((n, D_MODEL), F32), jax.ShapeDtypeStruct((n, D_MODEL), F32),
                   jax.ShapeDtypeStruct((D_MODEL, D_MODEL), BF16)],
        compiler_params=pltpu.CompilerParams(dimension_semantics=("arbitrary",),
                                             vmem_limit_bytes=VMEM_LIMIT_BYTES),
        name="outq_sample",
    )(x3d, omix, w_out, g_xa, w_xq)


def _post_kernel(x1_ref, ox_ref, wxo_ref, gffn_ref, wgate_ref, wup_ref, wdown_ref, gfinal_ref, y_ref):
    y = _ffn_final(x1_ref[...], ox_ref[...].astype(BF16), wxo_ref[...], gffn_ref[...],
                   wgate_ref[...], wup_ref[...], wdown_ref[...], gfinal_ref[...])
    y_ref[...] = y.reshape(y_ref.shape)


def _post_sample(x1, ox, dec_len, w_xo, g_ffn, w_gate, w_up, w_down, g_final):
    n = x1.shape[0]
    d_ff = w_gate.shape[1]
    return pl.pallas_call(
        _post_kernel,
        grid=(1,),
        in_specs=[_resident((n, D_MODEL)), _resident((n, D_MODEL)), _resident((D_MODEL, D_MODEL)),
                  _resident((1, D_MODEL)), _resident((D_MODEL, d_ff)), _resident((D_MODEL, d_ff)),
                  _resident((d_ff, D_MODEL)), _resident((1, D_MODEL))],
        out_specs=pl.BlockSpec((n // dec_len, dec_len, D_MODEL), lambda i: (0, 0, 0)),
        out_shape=jax.ShapeDtypeStruct((n // dec_len, dec_len, D_MODEL), F32),
        compiler_params=pltpu.CompilerParams(dimension_semantics=("arbitrary",),
                                             vmem_limit_bytes=VMEM_LIMIT_BYTES),
        name="post_sample",
    )(x1, ox, w_xo, g_ffn, w_gate, w_up, w_down, g_final)


def kernel(x_prompt, x_sample, mem_prompt, state_ret, state_hgrn, cache_mem_k, cache_mem_v, g_mix, w_in,
           ret_gain, hg_gain, hg_lb, w_out, g_xa, g_mem, w_xq, w_xk, w_xv, w_xo, g_ffn, w_gate, w_up,
           w_down, g_final):
    depth = w_in.shape[0]
    assert depth == 1, "single-layer step"
    batch, seq, d = x_prompt.shape
    dec_batch, dec_len, _ = x_sample.shape
    assert d == D_MODEL and seq % MIX_TOKENS == 0 and seq % TAIL_TOKENS == 0
    assert SUBLANES % dec_len == 0

    g_final2 = g_final.reshape(1, D_MODEL)

    proj_s, w_in_b, w_out_b = _proj_sample(x_sample, g_mix, w_in[0], w_out[0])
    mk, mv, mkb, mvb = _memkv(mem_prompt.reshape(batch * N_MEM, D_MODEL), g_mem, w_xk[0], w_xv[0])

    x1_p, ret_p, hg_p, omix_s, ret_s, hg_s, w_xo_b, w_gate_b, w_up_b, w_down_b = _mix_prompt(
        x_prompt.reshape(batch * seq, D_MODEL), batch, seq, g_mix, w_in_b, ret_gain, hg_gain, hg_lb,
        w_out_b, proj_s, state_ret[0], state_hgrn[0], dec_len, w_xo[0], w_gate[0], w_up[0], w_down[0])
    x1_s, q_s, w_xq_b = _outq_sample(x_sample, omix_s, w_out_b, g_xa, w_xq[0])

    y_p, ox_s = _tail_prompt(x1_p, batch, seq, mkb.reshape(batch, N_MEM, D_MODEL),
                             mvb.reshape(batch, N_MEM, D_MODEL), g_xa, w_xq_b, w_xo_b, g_ffn,
                             w_gate_b, w_up_b, w_down_b, g_final2,
                             q_s, cache_mem_k[0], cache_mem_v[0], dec_len)
    y_s = _post_sample(x1_s, ox_s, dec_len, w_xo_b, g_ffn, w_gate_b, w_up_b, w_down_b, g_final2)

    kv_shape = (depth, batch, N_MEM, XA_HEADS, XA_HD)
    return (y_p.reshape(batch, seq, D_MODEL), y_s,
            ret_p[None], hg_p[None], mk.reshape(kv_shape), mv.reshape(kv_shape),
            ret_s[None], hg_s[None])
```
